```python
import jax, jax.numpy as jnp
from jax import lax
import numpy as np

D_MODEL = 2048
BATCH = 1
SEQ = 8192
DEPTH = 1
DEC_BATCH = 128
DEC_SEQ = 1
PAST_LEN = 8192
PAGE_SIZE = 128

ATT_HEADS = 16
ATT_KV_HEADS = 2
ATT_HEAD_DIM = 64
ATT_GROUP = ATT_HEADS // ATT_KV_HEADS
WINDOW = 128
BLOCK = WINDOW
RET_HEADS = 4
RET_DK = 256
RET_DV = 256
RET_CHUNK = 128
ROPE_BASE = 10000.0
D_FF = 4 * D_MODEL
EPS = 1e-6

ATT_WIDTH = ATT_HEADS * ATT_HEAD_DIM
KV_WIDTH = ATT_KV_HEADS * ATT_HEAD_DIM
RET_QK_WIDTH = RET_HEADS * RET_DK
RET_V_WIDTH = RET_HEADS * RET_DV
MIX_WIDTH = ATT_WIDTH + RET_V_WIDTH
IN_WIDTH = ATT_WIDTH + 2 * KV_WIDTH + 2 * RET_QK_WIDTH + 2 * RET_V_WIDTH
IN_SPLITS = (ATT_WIDTH,
             ATT_WIDTH + KV_WIDTH,
             ATT_WIDTH + 2 * KV_WIDTH,
             ATT_WIDTH + 2 * KV_WIDTH + RET_QK_WIDTH,
             ATT_WIDTH + 2 * KV_WIDTH + 2 * RET_QK_WIDTH,
             ATT_WIDTH + 2 * KV_WIDTH + 2 * RET_QK_WIDTH + RET_V_WIDTH)

kernel_name = 'hymba_swa_sink_retention_decoder_step'


def _rmsnorm(x, g):
    xf = x.astype(jnp.float32)
    var = jnp.mean(jnp.square(xf), axis=-1, keepdims=True)
    return (xf * lax.rsqrt(var + EPS) * g.astype(jnp.float32)).astype(x.dtype)


def _rotary(x, pos):
    half = x.shape[-1] // 2
    inv = ROPE_BASE ** (-jnp.arange(half, dtype=jnp.float32) / half)
    ang = pos.astype(jnp.float32)[:, None] * inv[None, :]
    cos = jnp.cos(ang)[None, :, None, :]
    sin = jnp.sin(ang)[None, :, None, :]
    xf = x.astype(jnp.float32)
    x1, x2 = xf[..., :half], xf[..., half:]
    return jnp.concatenate([x1 * cos - x2 * sin, x2 * cos + x1 * sin], axis=-1).astype(x.dtype)


def _sink_softmax(s, mask, sinks):
    s = jnp.where(mask, s, -jnp.inf)
    sk = sinks.astype(jnp.float32).reshape(ATT_KV_HEADS, ATT_GROUP)[:, :, None, None]
    m = jnp.maximum(jnp.max(s, axis=-1, keepdims=True), sk)
    p = jnp.exp(s - m)
    return p / (jnp.sum(p, axis=-1, keepdims=True) + jnp.exp(sk - m))


def _swa_prompt(q, k, v, sinks):
    B, L = q.shape[:2]
    nb = L // BLOCK
    qb = q.reshape(B, nb, BLOCK, ATT_KV_HEADS, ATT_GROUP, ATT_HEAD_DIM)
    kb = k.reshape(B, nb, BLOCK, ATT_KV_HEADS, ATT_HEAD_DIM)
    vb = v.reshape(B, nb, BLOCK, ATT_KV_HEADS, ATT_HEAD_DIM)
    shift = ((0, 0), (1, 0), (0, 0), (0, 0), (0, 0))
    kk = jnp.concatenate([jnp.pad(kb[:, :-1], shift), kb], axis=2)
    vv = jnp.concatenate([jnp.pad(vb[:, :-1], shift), vb], axis=2)
    s = jnp.einsum('bnikgd,bnjkd->bnkgij', qb, kk,
                   preferred_element_type=jnp.float32) * (ATT_HEAD_DIM ** -0.5)
    blk = jnp.arange(nb)[:, None]
    qpos = blk * BLOCK + jnp.arange(BLOCK)[None, :]
    kpos = (blk - 1) * BLOCK + jnp.arange(2 * BLOCK)[None, :]
    rel = qpos[:, :, None] - kpos[:, None, :]
    mask = (rel >= 0) & (rel <= WINDOW) & (kpos[:, None, :] >= 0)
    p = _sink_softmax(s, mask[None, :, None, None], sinks)
    o = jnp.einsum('bnkgij,bnjkd->bnikgd', p.astype(v.dtype), vv)
    wb = min(WINDOW, L)
    return o.reshape(B, L, ATT_WIDTH), k[:, L - wb:], v[:, L - wb:]


def _swa_decode(q, k, v, k_buf, v_buf, sinks):
    B, T = q.shape[:2]
    W = k_buf.shape[1]
    kk = jnp.concatenate([k_buf, k], axis=1)
    vv = jnp.concatenate([v_buf, v], axis=1)
    qg = q.reshape(B, T, ATT_KV_HEADS, ATT_GROUP, ATT_HEAD_DIM)
    s = jnp.einsum('btkgd,bjkd->bkgtj', qg, kk,
                   preferred_element_type=jnp.float32) * (ATT_HEAD_DIM ** -0.5)
    qpos = PAST_LEN + jnp.arange(T)
    kpos = jnp.concatenate([PAST_LEN - W + jnp.arange(W), PAST_LEN + jnp.arange(T)])
    rel = qpos[:, None] - kpos[None, :]
    mask = (rel >= 0) & (rel <= WINDOW)
    p = _sink_softmax(s, mask[None, None, None], sinks)
    o = jnp.einsum('bkgtj,bjkd->btkgd', p.astype(v.dtype), vv)
    return o.reshape(B, T, ATT_WIDTH), kk[:, T:], vv[:, T:]


def _retention(q, k, v, s0):
    B, L = q.shape[:2]
    C = RET_CHUNK if L % RET_CHUNK == 0 else L
    n = L // C
    log_g = jnp.log1p(-jnp.exp2(-5.0 - jnp.arange(RET_HEADS, dtype=jnp.float32)))
    idx = jnp.arange(C, dtype=jnp.float32)
    rel = idx[:, None] - idx[None, :]
    dmask = jnp.where(rel[None] >= 0,
                      jnp.exp(log_g[:, None, None] * jnp.maximum(rel, 0.0)[None]), 0.0)
    qc = q.reshape(B, n, C, RET_HEADS, RET_DK)
    kc = k.reshape(B, n, C, RET_HEADS, RET_DK)
    vc = v.reshape(B, n, C, RET_HEADS, RET_DV)
    att = jnp.einsum('bnihd,bnjhd->bnhij', qc, kc) * dmask
    o_intra = jnp.einsum('bnhij,bnjhv->bnihv', att, vc)
    k_dec = kc * jnp.exp(log_g[None, :] * (C - 1.0 - idx)[:, None])[None, None, :, :, None]
    kv = jnp.einsum('bnjhd,bnjhv->bnhdv', k_dec, vc)
    chunk_decay = jnp.exp(log_g * C)[:, None, None]

    def step(s, kv_n):
        return chunk_decay * s + kv_n, s

    s_final, s_prev = lax.scan(step, s0, jnp.moveaxis(kv, 1, 0))
    s_prev = jnp.moveaxis(s_prev, 0, 1)
    q_dec = qc * jnp.exp(log_g[None, :] * (idx + 1.0)[:, None])[None, None, :, :, None]
    o_inter = jnp.einsum('bnihd,bnhdv->bnihv', q_dec, s_prev)
    return (o_intra + o_inter).reshape(B, L, RET_HEADS, RET_DV), s_final


def _layer(x, pos, k_buf, v_buf, s0, ln1_g, w_in, q_norm_g, k_norm_g, attn_sinks,
           ret_norm_g, w_out, ln2_g, w_up, w_down):
    B, L, _ = x.shape
    h = _rmsnorm(x, ln1_g)
    z = h @ w_in
    aq, ak, av, rq, rk, rv, rg = jnp.split(z, IN_SPLITS, axis=-1)
    aq = _rmsnorm(aq.reshape(B, L, ATT_HEADS, ATT_HEAD_DIM), q_norm_g)
    ak = _rmsnorm(ak.reshape(B, L, ATT_KV_HEADS, ATT_HEAD_DIM), k_norm_g)
    av = av.reshape(B, L, ATT_KV_HEADS, ATT_HEAD_DIM)
    if k_buf is None:
        a_out, k_new, v_new = _swa_prompt(aq, ak, av, attn_sinks)
    else:
        a_out, k_new, v_new = _swa_decode(aq, ak, av, k_buf, v_buf, attn_sinks)
    rq = _rotary(rq.reshape(B, L, RET_HEADS, RET_DK), pos)
    rk = _rotary(rk.reshape(B, L, RET_HEADS, RET_DK), pos)
    rv = rv.reshape(B, L, RET_HEADS, RET_DV)
    r_o, s_new = _retention(rq.astype(jnp.float32),
                            rk.astype(jnp.float32) * (RET_DK ** -0.5),
                            rv.astype(jnp.float32), s0.astype(jnp.float32))
    r_o = _rmsnorm(r_o, ret_norm_g.reshape(RET_HEADS, RET_DV)).reshape(B, L, RET_V_WIDTH)
    r_out = r_o.astype(x.dtype) * jax.nn.silu(rg)
    x = x + jnp.concatenate([a_out, r_out], axis=-1) @ w_out
    h2 = _rmsnorm(x, ln2_g)
    x = x + jnp.square(jax.nn.relu(h2 @ w_up)) @ w_down
    return x, k_new, v_new, s_new.astype(x.dtype)


def setup_inputs(seed: int = 0) -> dict:
    key = jax.random.key(seed)
    ks = jax.random.split(key, 16)
    wb = min(WINDOW, PAST_LEN)
    f32 = jnp.float32
    nrm = lambda k, shape, scale: jax.random.normal(k, shape, f32) * scale
    return {
        'x_prompt': nrm(ks[0], (BATCH, SEQ, D_MODEL), 1.0),
        'x_sample': nrm(ks[1], (DEC_BATCH, DEC_SEQ, D_MODEL), 1.0),
        'cache_k_win': nrm(ks[2], (DEPTH, DEC_BATCH, wb, ATT_KV_HEADS, ATT_HEAD_DIM), 1.0),
        'cache_v_win': nrm(ks[3], (DEPTH, DEC_BATCH, wb, ATT_KV_HEADS, ATT_HEAD_DIM), 1.0),
        'state_ret': nrm(ks[4], (DEPTH, DEC_BATCH, RET_HEADS, RET_DK, RET_DV), 0.3),
        'ln1_g': 1.0 + nrm(ks[5], (DEPTH, D_MODEL), 0.02),
        'w_in': nrm(ks[6], (DEPTH, D_MODEL, IN_WIDTH), D_MODEL ** -0.5),
        'q_norm_g': 1.0 + nrm(ks[7], (DEPTH, ATT_HEAD_DIM), 0.02),
        'k_norm_g': 1.0 + nrm(ks[8], (DEPTH, ATT_HEAD_DIM), 0.02),
        'attn_sinks': nrm(ks[9], (DEPTH, ATT_HEADS), 0.5),
        'ret_norm_g': 1.0 + nrm(ks[10], (DEPTH, RET_V_WIDTH), 0.02),
        'w_out': nrm(ks[11], (DEPTH, MIX_WIDTH, D_MODEL), MIX_WIDTH ** -0.5),
        'ln2_g': 1.0 + nrm(ks[12], (DEPTH, D_MODEL), 0.02),
        'w_up': nrm(ks[13], (DEPTH, D_MODEL, D_FF), D_MODEL ** -0.5),
        'w_down': nrm(ks[14], (DEPTH, D_FF, D_MODEL), D_FF ** -0.5),
    }


def reference(x_prompt, x_sample, cache_k_win, cache_v_win, state_ret, ln1_g, w_in, q_norm_g,
              k_norm_g, attn_sinks, ret_norm_g, w_out, ln2_g, w_up, w_down):
    pos_p = jnp.arange(x_prompt.shape[1], dtype=jnp.int32)
    pos_s = PAST_LEN + jnp.arange(x_sample.shape[1], dtype=jnp.int32)
    yp, ys = x_prompt, x_sample
    kp_l, vp_l, sp_l, ks_l, vs_l, ss_l = [], [], [], [], [], []
    s0_p = jnp.zeros((x_prompt.shape[0], RET_HEADS, RET_DK, RET_DV), jnp.float32)
    for l in range(DEPTH):
        w = (ln1_g[l], w_in[l], q_norm_g[l], k_norm_g[l], attn_sinks[l], ret_norm_g[l],
             w_out[l], ln2_g[l], w_up[l], w_down[l])
        yp, kp, vp, sp = _layer(yp, pos_p, None, None, s0_p, *w)
        ys, kn, vn, sn = _layer(ys, pos_s, cache_k_win[l], cache_v_win[l], state_ret[l], *w)
        kp_l.append(kp); vp_l.append(vp); sp_l.append(sp)
        ks_l.append(kn); vs_l.append(vn); ss_l.append(sn)
    return (yp, ys, jnp.stack(kp_l), jnp.stack(vp_l), jnp.stack(sp_l),
            jnp.stack(ks_l), jnp.stack(vs_l), jnp.stack(ss_l))
```

```python
import functools
import math

import jax
import jax.numpy as jnp
from jax import lax
from jax.experimental import pallas as pl
from jax.experimental.pallas import tpu as pltpu

D_MODEL = 2048
ATT_HEADS = 16
ATT_KV_HEADS = 2
ATT_HEAD_DIM = 64
WINDOW = 128
RET_HEADS = 4
RET_DK = 256
RET_DV = 256
RET_CHUNK = 128
ROPE_BASE = 10000.0
D_FF = 4 * D_MODEL
EPS = 1e-6
PAST_LEN = 8192

ATT_WIDTH = ATT_HEADS * ATT_HEAD_DIM
KV_WIDTH = ATT_KV_HEADS * ATT_HEAD_DIM
RET_WIDTH = RET_HEADS * RET_DK
LANES = 128
HALF = RET_DK // 2

F32 = jnp.float32
BF16 = jnp.bfloat16
VMEM_LIMIT = 56 * 1024 * 1024

LOG_G = tuple(math.log1p(-(2.0 ** (-5.0 - h))) for h in range(RET_HEADS))


def _dot(a, b):
    return jnp.dot(a, b, preferred_element_type=F32)


def _dot_nt(a, b):
    return lax.dot_general(a, b, (((1,), (1,)), ((), ())), preferred_element_type=F32)


def _dot_tn(a, b):
    return lax.dot_general(a, b, (((0,), (0,)), ((), ())), preferred_element_type=F32)


def _const_spec(shape):
    n = len(shape)
    return pl.BlockSpec(shape, lambda *_: (0,) * n, pipeline_mode=pl.Buffered(1))


def _in_proj_body(x_ref, g_ref, waq, wkv, wrq, wrk, wrv, wrg, qg_ref, kg_ref,
                  cos_ref, sin_ref, ones_ref,
                  q_out, kv_out, rq_out, rk_out, rv_out, rg_out):
    x = x_ref[...]
    var = jnp.mean(x * x, axis=-1, keepdims=True)
    h = (x * lax.rsqrt(var + EPS) * g_ref[...]).astype(BF16)
    ones = ones_ref[...]
    inv_hd = 1.0 / ATT_HEAD_DIM

    aq = _dot(h, waq[...])
    for t in range(ATT_WIDTH // 256):
        blk = aq[:, 256 * t:256 * (t + 1)]
        ssq = _dot((blk * blk).astype(BF16), ones)
        q_out[:, 256 * t:256 * (t + 1)] = (
            blk * lax.rsqrt(ssq * inv_hd + EPS) * qg_ref[...]).astype(BF16)

    kvr = _dot(h, wkv[...])
    k = kvr[:, :KV_WIDTH]
    ssq = _dot((k * k).astype(BF16), ones[:KV_WIDTH, :KV_WIDTH])
    kv_out[:, :KV_WIDTH] = k * lax.rsqrt(ssq * inv_hd + EPS) * kg_ref[...]
    kv_out[:, KV_WIDTH:] = kvr[:, KV_WIDTH:]

    cos = cos_ref[...]
    sin = sin_ref[...]
    for w_ref, o_ref, scale in ((wrq, rq_out, 1.0), (wrk, rk_out, RET_DK ** -0.5)):
        r = _dot(h, w_ref[...])
        for hh in range(RET_HEADS):
            x1 = r[:, RET_DK * hh:RET_DK * hh + HALF]
            x2 = r[:, RET_DK * hh + HALF:RET_DK * (hh + 1)]
            o1 = x1 * cos - x2 * sin
            o2 = x2 * cos + x1 * sin
            if scale != 1.0:
                o1 = o1 * scale
                o2 = o2 * scale
            o_ref[:, RET_DK * hh:RET_DK * hh + HALF] = o1.astype(BF16)
            o_ref[:, RET_DK * hh + HALF:RET_DK * (hh + 1)] = o2.astype(BF16)

    rv_out[...] = _dot(h, wrv[...]).astype(BF16)
    rg = _dot(h, wrg[...])
    rg_out[...] = (rg / (1.0 + jnp.exp(-rg))).astype(BF16)


def _in_proj(x, ln_g, w_segs, qg, kg, cos, sin, ones, tm):
    m = x.shape[0]
    row = lambda w: pl.BlockSpec((tm, w), lambda i: (i, 0))
    waq, wkv, wrq, wrk, wrv, wrg = w_segs
    out_shapes = (
        jax.ShapeDtypeStruct((m, ATT_WIDTH), BF16),
        jax.ShapeDtypeStruct((m, 2 * KV_WIDTH), F32),
        jax.ShapeDtypeStruct((m, RET_WIDTH), BF16),
        jax.ShapeDtypeStruct((m, RET_WIDTH), BF16),
        jax.ShapeDtypeStruct((m, RET_WIDTH), BF16),
        jax.ShapeDtypeStruct((m, RET_WIDTH), BF16),
    )
    return pl.pallas_call(
        _in_proj_body,
        grid=(m // tm,),
        in_specs=[row(D_MODEL), _const_spec((1, D_MODEL))]
        + [_const_spec(w.shape) for w in w_segs]
        + [_const_spec((1, 256)), _const_spec((1, KV_WIDTH)),
           row(HALF), row(HALF), _const_spec((256, 256))],
        out_specs=[row(ATT_WIDTH), row(2 * KV_WIDTH), row(RET_WIDTH), row(RET_WIDTH),
                   row(RET_WIDTH), row(RET_WIDTH)],
        out_shape=out_shapes,
        compiler_params=pltpu.CompilerParams(
            dimension_semantics=("arbitrary",), vmem_limit_bytes=VMEM_LIMIT),
        name="in_proj",
    )(x, ln_g, waq, wkv, wrq, wrk, wrv, wrg, qg, kg, cos, sin, ones)


def _half_split(a):
    lane = lax.broadcasted_iota(jnp.int32, a.shape, 1)
    lo = lane < ATT_HEAD_DIM
    sw = pltpu.roll(a, ATT_HEAD_DIM, 1)
    zero = jnp.zeros_like(a)
    h0 = (jnp.where(lo, a, zero).astype(BF16), jnp.where(lo, zero, sw).astype(BF16))
    h1 = (jnp.where(lo, sw, zero).astype(BF16), jnp.where(lo, zero, a).astype(BF16))
    return h0, h1


def _swa_body(sink_ref, q_ref, kvc_ref, kvp_ref, o_ref, *, nsub):
    i = pl.program_id(0)
    row = lax.broadcasted_iota(jnp.int32, (WINDOW, 2 * WINDOW), 0)
    col = lax.broadcasted_iota(jnp.int32, (WINDOW, 2 * WINDOW), 1)
    band_cur = (col >= WINDOW) & (col - WINDOW <= row)
    first_col = jnp.where(i > 0, 0, WINDOW)
    lane_lo = lax.broadcasted_iota(jnp.int32, (WINDOW, LANES), 1) < ATT_HEAD_DIM
    for j in range(nsub):
        rows = slice(WINDOW * j, WINDOW * (j + 1))
        prev = kvp_ref[...] if j == 0 else kvc_ref[WINDOW * (j - 1):WINDOW * j, :]
        cur = kvc_ref[rows, :]
        kk = jnp.concatenate([prev[:, :KV_WIDTH], cur[:, :KV_WIDTH]], axis=0)
        vv = jnp.concatenate([prev[:, KV_WIDTH:], cur[:, KV_WIDTH:]], axis=0)
        ksp = _half_split(kk)
        vsp = _half_split(vv)
        lo_col = first_col if j == 0 else 0
        mask = band_cur | ((col < WINDOW) & (col >= row) & (col >= lo_col))
        for t in range(ATT_HEADS // 2):
            kh = t // (ATT_HEADS // 2 // ATT_KV_HEADS)
            qp = q_ref[rows, LANES * t:LANES * (t + 1)]
            acc = None
            inv = []
            for par in range(2):
                sink = sink_ref[2 * t + par]
                s = jnp.where(mask, _dot_nt(qp, ksp[kh][par]), -jnp.inf)
                mx = jnp.maximum(jnp.max(s, axis=-1, keepdims=True), sink)
                p = jnp.exp(s - mx)
                den = jnp.sum(p, axis=-1, keepdims=True) + jnp.exp(sink - mx)
                inv.append(1.0 / den)
                pv = _dot(p.astype(BF16), vsp[kh][par])
                acc = pv if acc is None else acc + pv
            o_ref[rows, LANES * t:LANES * (t + 1)] = (
                acc * jnp.where(lane_lo, inv[0], inv[1])).astype(BF16)


def _swa(q, kv, sinks, tq):
    m = q.shape[0]
    nsub = tq // WINDOW
    return pl.pallas_call(
        functools.partial(_swa_body, nsub=nsub),
        grid=(m // tq,),
        in_specs=[
            pl.BlockSpec(memory_space=pltpu.SMEM),
            pl.BlockSpec((tq, ATT_WIDTH), lambda i: (i, 0)),
            pl.BlockSpec((tq, 2 * KV_WIDTH), lambda i: (i, 0)),
            pl.BlockSpec((WINDOW, 2 * KV_WIDTH), lambda i: (jnp.maximum(i * nsub - 1, 0), 0)),
        ],
        out_specs=pl.BlockSpec((tq, ATT_WIDTH), lambda i: (i, 0)),
        out_shape=jax.ShapeDtypeStruct((m, ATT_WIDTH), BF16),
        compiler_params=pltpu.CompilerParams(
            dimension_semantics=("arbitrary",), vmem_limit_bytes=VMEM_LIMIT),
        name="swa",
    )(sinks, q, kv, kv)


def _ret_body(rq_ref, rk_ref, rv_ref, rg_ref, g_ref, o_ref, s_out, s_scr, *, nsub):
    i = pl.program_id(0)

    @pl.when(i == 0)
    def _():
        s_scr[...] = jnp.zeros_like(s_scr)

    c = RET_CHUNK
    ri = lax.broadcasted_iota(jnp.int32, (c, c), 0)
    ci = lax.broadcasted_iota(jnp.int32, (c, c), 1)
    rel = (ri - ci).astype(F32)
    idx = lax.broadcasted_iota(jnp.int32, (c, 1), 0).astype(F32)
    for hh in range(RET_HEADS):
        lg = LOG_G[hh]
        dmask = jnp.where(rel >= 0, jnp.exp(lg * jnp.maximum(rel, 0.0)), 0.0)
        qdec = jnp.exp(lg * (idx + 1.0))
        kdec = jnp.exp(lg * (c - 1.0 - idx))
        cdec = math.exp(lg * c)
        cols = slice(RET_DK * hh, RET_DK * (hh + 1))
        for j in range(nsub):
            rows = slice(c * j, c * (j + 1))
            q = rq_ref[rows, cols]
            k = rk_ref[rows, cols]
            v = rv_ref[rows, cols]
            att = _dot_nt(q, k) * dmask
            o = _dot(att.astype(BF16), v)
            s_prev = s_scr[hh]
            qd = (q.astype(F32) * qdec).astype(BF16)
            o = o + _dot(qd, s_prev.astype(BF16))
            kd = (k.astype(F32) * kdec).astype(BF16)
            s_scr[hh] = cdec * s_prev + _dot_tn(kd, v)
            var = jnp.mean(o * o, axis=-1, keepdims=True)
            on = o * lax.rsqrt(var + EPS) * g_ref[:, cols]
            o_ref[rows, cols] = (on * rg_ref[rows, cols].astype(F32)).astype(BF16)

    @pl.when(i == pl.num_programs(0) - 1)
    def _():
        s_out[...] = s_scr[...]


def _retention(rq, rk, rv, rg, g, tr):
    m = rq.shape[0]
    nsub = tr // RET_CHUNK
    row = pl.BlockSpec((tr, RET_WIDTH), lambda i: (i, 0))
    return pl.pallas_call(
        functools.partial(_ret_body, nsub=nsub),
        grid=(m // tr,),
        in_specs=[row, row, row, row, _const_spec((1, RET_WIDTH))],
        out_specs=[row, pl.BlockSpec((RET_HEADS, RET_DK, RET_DV), lambda i: (0, 0, 0))],
        out_shape=(jax.ShapeDtypeStruct((m, RET_WIDTH), BF16),
                   jax.ShapeDtypeStruct((RET_HEADS, RET_DK, RET_DV), F32)),
        scratch_shapes=[pltpu.VMEM((RET_HEADS, RET_DK, RET_DV), F32)],
        compiler_params=pltpu.CompilerParams(
            dimension_semantics=("arbitrary",), vmem_limit_bytes=VMEM_LIMIT),
        name="retention",
    )(rq, rk, rv, rg, g)


def _swa_dec_body(q_ref, kvn_ref, ck_ref, cv_ref, sink_ref, o_ref, nk_ref, nv_ref):
    bb = q_ref.shape[0]
    npair = ATT_HEADS // 2
    q8 = q_ref[...].astype(F32)
    q8r = pltpu.roll(q8, ATT_HEAD_DIM, 2)
    lane = lax.broadcasted_iota(jnp.int32, q8.shape, 2)
    pair = lax.broadcasted_iota(jnp.int32, q8.shape, 1)
    lo = lane < ATT_HEAD_DIM
    kv0 = pair < npair // ATT_KV_HEADS
    own = lo == kv0
    zero = jnp.zeros_like(q8)
    qe = jnp.where(own, jnp.where(kv0, q8, q8r), zero)
    qo = jnp.where(own, jnp.where(kv0, q8r, q8), zero)
    qb = jnp.concatenate([qe, qo], axis=1)

    ck = ck_ref[...]
    cv = cv_ref[...]
    kn = kvn_ref[:, :KV_WIDTH]
    vn = kvn_ref[:, KV_WIDTH:]
    s = lax.dot_general(qb.astype(BF16), ck.astype(BF16), (((2,), (2,)), ((0,), (0,))),
                        preferred_element_type=F32)
    s_new = jnp.sum(qb * kn[:, None, :], axis=-1, keepdims=True)
    sink = sink_ref[...][None, :, :]
    mx = jnp.maximum(jnp.maximum(jnp.max(s, axis=-1, keepdims=True), s_new), sink)
    p = jnp.exp(s - mx)
    p_new = jnp.exp(s_new - mx)
    den = jnp.sum(p, axis=-1, keepdims=True) + p_new + jnp.exp(sink - mx)
    o = lax.dot_general(p.astype(BF16), cv.astype(BF16), (((2,), (1,)), ((0,), (0,))),
                        preferred_element_type=F32)
    o = (o + p_new * vn[:, None, :]) / den
    oe = o[:, :npair, :]
    oo = o[:, npair:, :]
    oer = pltpu.roll(oe, ATT_HEAD_DIM, 2)
    oor = pltpu.roll(oo, ATT_HEAD_DIM, 2)
    o_ref[...] = jnp.where(lo, jnp.where(kv0, oe, oer), jnp.where(kv0, oor, oo)).astype(BF16)

    nk_ref[:, :WINDOW - 1, :] = ck[:, 1:, :]
    nk_ref[:, WINDOW - 1:, :] = kn[:, None, :]
    nv_ref[:, :WINDOW - 1, :] = cv[:, 1:, :]
    nv_ref[:, WINDOW - 1:, :] = vn[:, None, :]


def _swa_dec(q8, kvn, ck, cv, sink_col, bb):
    b = q8.shape[0]
    cache = pl.BlockSpec((bb, WINDOW, KV_WIDTH), lambda i: (i, 0, 0))
    return pl.pallas_call(
        _swa_dec_body,
        grid=(b // bb,),
        in_specs=[pl.BlockSpec((bb, ATT_HEADS // 2, LANES), lambda i: (i, 0, 0)),
                  pl.BlockSpec((bb, 2 * KV_WIDTH), lambda i: (i, 0)),
                  cache, cache, _const_spec((ATT_HEADS, 1))],
        out_specs=[pl.BlockSpec((bb, ATT_HEADS // 2, LANES), lambda i: (i, 0, 0)), cache, cache],
        out_shape=(jax.ShapeDtypeStruct((b, ATT_HEADS // 2, LANES), BF16),
                   jax.ShapeDtypeStruct(ck.shape, F32),
                   jax.ShapeDtypeStruct(cv.shape, F32)),
        compiler_params=pltpu.CompilerParams(
            dimension_semantics=("arbitrary",), vmem_limit_bytes=VMEM_LIMIT),
        name="swa_dec",
    )(q8, kvn, ck, cv, sink_col)


def _ret_dec_body(qt_ref, kt_ref, rq_ref, rk_ref, rv_ref, rg_ref, g_ref, s_ref,
                  o_ref, ns_ref, o_scr):
    bb = s_ref.shape[0]
    for hh in range(RET_HEADS):
        g1 = math.exp(LOG_G[hh])
        cols = slice(RET_DK * hh, RET_DK * (hh + 1))
        for jb in range(bb):
            s0 = s_ref[jb, hh]
            qc = qt_ref[0, cols, jb:jb + 1] * g1
            kc = kt_ref[0, cols, jb:jb + 1]
            v = rv_ref[jb:jb + 1, cols].astype(F32)
            o_scr[jb:jb + 1, cols] = jnp.sum(qc * s0, axis=0, keepdims=True)
            ns_ref[jb, hh] = g1 * s0 + kc * v
    for hh in range(RET_HEADS):
        cols = slice(RET_DK * hh, RET_DK * (hh + 1))
        qk = jnp.sum(rq_ref[:, cols].astype(F32) * rk_ref[:, cols].astype(F32),
                     axis=-1, keepdims=True)
        o = o_scr[:, cols] + qk * rv_ref[:, cols].astype(F32)
        var = jnp.mean(o * o, axis=-1, keepdims=True)
        on = o * lax.rsqrt(var + EPS) * g_ref[:, cols]
        o_ref[:, cols] = (on * rg_ref[:, cols].astype(F32)).astype(BF16)


def _ret_dec(qt, kt, rq, rk, rv, rg, g, state, bb):
    b = rv.shape[0]
    row = pl.BlockSpec((bb, RET_WIDTH), lambda i: (i, 0))
    col = pl.BlockSpec((1, RET_WIDTH, bb), lambda i: (i, 0, 0))
    st = pl.BlockSpec((bb, RET_HEADS, RET_DK, RET_DV), lambda i: (i, 0, 0, 0))
    return pl.pallas_call(
        _ret_dec_body,
        grid=(b // bb,),
        in_specs=[col, col, row, row, row, row, _const_spec((1, RET_WIDTH)), st],
        out_specs=[row, st],
        out_shape=(jax.ShapeDtypeStruct((b, RET_WIDTH), BF16),
                   jax.ShapeDtypeStruct(state.shape, F32)),
        scratch_shapes=[pltpu.VMEM((bb, RET_WIDTH), F32)],
        compiler_params=pltpu.CompilerParams(
            dimension_semantics=("arbitrary",), vmem_limit_bytes=VMEM_LIMIT),
        name="ret_dec",
    )(qt, kt, rq, rk, rv, rg, g, state)


def _out_proj_body(x_ref, a_ref, r_ref, wa_ref, wr_ref, g_ref, x1_ref, h2_ref):
    x1 = x_ref[...] + _dot(a_ref[...], wa_ref[...]) + _dot(r_ref[...], wr_ref[...])
    x1_ref[...] = x1
    var = jnp.mean(x1 * x1, axis=-1, keepdims=True)
    h2_ref[...] = (x1 * lax.rsqrt(var + EPS) * g_ref[...]).astype(BF16)


def _out_proj(x, a, r, wa, wr, g, tm):
    m = x.shape[0]
    row = lambda w: pl.BlockSpec((tm, w), lambda i: (i, 0))
    return pl.pallas_call(
        _out_proj_body,
        grid=(m // tm,),
        in_specs=[row(D_MODEL), row(ATT_WIDTH), row(RET_WIDTH),
                  _const_spec(wa.shape), _const_spec(wr.shape), _const_spec((1, D_MODEL))],
        out_specs=[row(D_MODEL), row(D_MODEL)],
        out_shape=(jax.ShapeDtypeStruct((m, D_MODEL), F32),
                   jax.ShapeDtypeStruct((m, D_MODEL), BF16)),
        compiler_params=pltpu.CompilerParams(
            dimension_semantics=("arbitrary",), vmem_limit_bytes=VMEM_LIMIT),
        name="out_proj",
    )(x, a, r, wa, wr, g)


def _mlp_body(x1_ref, h2_ref, wu_ref, wd_ref, o_ref):
    f = pl.program_id(1)

    @pl.when(f == 0)
    def _():
        o_ref[...] = x1_ref[...]

    u = jnp.maximum(_dot(h2_ref[...], wu_ref[...]), 0.0)
    o_ref[...] += _dot((u * u).astype(BF16), wd_ref[...])


def _mlp(x1, h2, wu, wd, tm, tf):
    m = x1.shape[0]
    return pl.pallas_call(
        _mlp_body,
        grid=(m // tm, D_FF // tf),
        in_specs=[pl.BlockSpec((tm, D_MODEL), lambda i, f: (i, 0)),
                  pl.BlockSpec((tm, D_MODEL), lambda i, f: (i, 0)),
                  pl.BlockSpec((D_MODEL, tf), lambda i, f: (0, f)),
                  pl.BlockSpec((tf, D_MODEL), lambda i, f: (f, 0))],
        out_specs=pl.BlockSpec((tm, D_MODEL), lambda i, f: (i, 0)),
        out_shape=jax.ShapeDtypeStruct((m, D_MODEL), F32),
        compiler_params=pltpu.CompilerParams(
            dimension_semantics=("arbitrary", "arbitrary"), vmem_limit_bytes=VMEM_LIMIT),
        name="mlp",
    )(x1, h2, wu, wd)


def _rope_tables(pos):
    inv = ROPE_BASE ** (-jnp.arange(HALF, dtype=F32) / HALF)
    ang = pos.astype(F32)[:, None] * inv[None, :]
    return jnp.cos(ang), jnp.sin(ang)


def kernel(x_prompt, x_sample, cache_k_win, cache_v_win, state_ret, ln1_g, w_in, q_norm_g,
           k_norm_g, attn_sinks, ret_norm_g, w_out, ln2_g, w_up, w_down):
    seq = x_prompt.shape[1]
    nb = x_sample.shape[0]
    assert x_prompt.shape[0] == 1 and x_sample.shape[1] == 1 and w_in.shape[0] == 1

    wi = w_in[0]
    bounds = (0, ATT_WIDTH, ATT_WIDTH + 2 * KV_WIDTH)
    bounds = bounds + tuple(bounds[-1] + RET_WIDTH * n for n in range(1, 5))
    w_segs = tuple(wi[:, a:b].astype(BF16) for a, b in zip(bounds[:-1], bounds[1:]))
    wo_a = w_out[0][:ATT_WIDTH].astype(BF16)
    wo_r = w_out[0][ATT_WIDTH:].astype(BF16)
    wu = w_up[0].astype(BF16)
    wd = w_down[0].astype(BF16)

    ln1 = ln1_g.reshape(1, D_MODEL)
    ln2 = ln2_g.reshape(1, D_MODEL)
    qg = jnp.tile(q_norm_g.reshape(1, ATT_HEAD_DIM) * (ATT_HEAD_DIM ** -0.5), (1, 256 // ATT_HEAD_DIM))
    kg = jnp.tile(k_norm_g.reshape(1, ATT_HEAD_DIM), (1, KV_WIDTH // ATT_HEAD_DIM))
    rg_g = ret_norm_g.reshape(1, RET_WIDTH)
    sinks = attn_sinks.reshape(ATT_HEADS)
    blk = jnp.arange(256) // ATT_HEAD_DIM
    ones = (blk[:, None] == blk[None, :]).astype(BF16)

    xp = x_prompt[0]
    cos_p, sin_p = _rope_tables(jnp.arange(seq, dtype=jnp.int32))
    q, kv, rq, rk, rv, rg = _in_proj(xp, ln1, w_segs, qg, kg, cos_p, sin_p, ones, tm=512)
    a_out = _swa(q, kv, sinks, tq=512)
    r_out, s_fin = _retention(rq, rk, rv, rg, rg_g, tr=512)
    x1, h2 = _out_proj(xp, a_out, r_out, wo_a, wo_r, ln2, tm=512)
    yp = _mlp(x1, h2, wu, wd, tm=512, tf=1024)

    wb = min(WINDOW, seq)
    kp = kv[seq - wb:, :KV_WIDTH].reshape(1, 1, wb, ATT_KV_HEADS, ATT_HEAD_DIM)
    vp = kv[seq - wb:, KV_WIDTH:].reshape(1, 1, wb, ATT_KV_HEADS, ATT_HEAD_DIM)
    sp = s_fin.reshape(1, 1, RET_HEADS, RET_DK, RET_DV)

    xs = x_sample[:, 0]
    cos_s, sin_s = _rope_tables(jnp.full((nb,), PAST_LEN, dtype=jnp.int32))
    qs, kvs, rqs, rks, rvs, rgs = _in_proj(xs, ln1, w_segs, qg, kg, cos_s, sin_s, ones, tm=nb)
    sink_col = jnp.concatenate([sinks[0::2], sinks[1::2]]).reshape(ATT_HEADS, 1)
    ck = cache_k_win[0].reshape(nb, WINDOW, KV_WIDTH)
    cv = cache_v_win[0].reshape(nb, WINDOW, KV_WIDTH)
    a8, nk, nv = _swa_dec(qs.reshape(nb, ATT_HEADS // 2, LANES), kvs, ck, cv, sink_col, bb=16)
    a_s = a8.reshape(nb, ATT_WIDTH)

    bb = 8
    rq32 = rqs.astype(F32)
    rk32 = rks.astype(F32)
    to_cols = lambda a: a.reshape(nb // bb, bb, RET_WIDTH).transpose(0, 2, 1)
    r_s, ns = _ret_dec(to_cols(rq32), to_cols(rk32), rqs, rks, rvs, rgs, rg_g, state_ret[0], bb=bb)
    x1s, h2s = _out_proj(xs, a_s, r_s, wo_a, wo_r, ln2, tm=nb)
    ys = _mlp(x1s, h2s, wu, wd, tm=nb, tf=1024)

    shape_kv = (1, nb, WINDOW, ATT_KV_HEADS, ATT_HEAD_DIM)
    return (yp[None], ys[:, None, :], kp, vp, sp,
            nk.reshape(shape_kv), nv.reshape(shape_kv), ns[None])
```

```python
import functools
import math

import jax
import jax.numpy as jnp
from jax import lax
from jax.experimental import pallas as pl
from jax.experimental.pallas import tpu as pltpu

D_MODEL = 2048
ATT_HEADS = 16
ATT_KV_HEADS = 2
ATT_HEAD_DIM = 64
WINDOW = 128
RET_HEADS = 4
RET_DK = 256
RET_DV = 256
RET_CHUNK = 128
ROPE_BASE = 10000.0
D_FF = 4 * D_MODEL
EPS = 1e-6
PAST_LEN = 8192

ATT_WIDTH = ATT_HEADS * ATT_HEAD_DIM
KV_WIDTH = ATT_KV_HEADS * ATT_HEAD_DIM
RET_WIDTH = RET_HEADS * RET_DK
LANES = 128
HALF = RET_DK // 2
IN_BOUNDS = (0, ATT_WIDTH, ATT_WIDTH + 2 * KV_WIDTH) + tuple(
    ATT_WIDTH + 2 * KV_WIDTH + RET_WIDTH * n for n in range(1, 5))
IN_WIDTH = IN_BOUNDS[-1]

F32 = jnp.float32
BF16 = jnp.bfloat16
VMEM_LIMIT = 56 * 1024 * 1024

LOG_G = tuple(math.log1p(-(2.0 ** (-5.0 - h))) for h in range(RET_HEADS))


def _dot(a, b):
    return jnp.dot(a, b, preferred_element_type=F32)


def _dot_nt(a, b):
    return lax.dot_general(a, b, (((1,), (1,)), ((), ())), preferred_element_type=F32)


def _dot_tn(a, b):
    return lax.dot_general(a, b, (((0,), (0,)), ((), ())), preferred_element_type=F32)


def _const_spec(shape):
    n = len(shape)
    return pl.BlockSpec(shape, lambda *_: (0,) * n, pipeline_mode=pl.Buffered(1))


def _in_proj_body(x_ref, g_ref, w_ref, qg_ref, kg_ref, inv_ref, ones_ref,
                  q_out, kv_out, rq_out, rk_out, rv_out, rg_out, cr_scr, sr_scr,
                  *, pos0, pos_step):
    tm = x_ref.shape[0]

    @pl.when(pl.program_id(0) == 0)
    def _():
        row = lax.broadcasted_iota(jnp.int32, (tm, 1), 0)
        ang_r = (pos_step * row).astype(F32) * inv_ref[...]
        cr_scr[...] = jnp.cos(ang_r)
        sr_scr[...] = jnp.sin(ang_r)

    waq, wkv, wrq, wrk, wrv, wrg = (
        w_ref.at[:, a:b] for a, b in zip(IN_BOUNDS[:-1], IN_BOUNDS[1:]))
    x = x_ref[...]
    var = jnp.mean(x * x, axis=-1, keepdims=True)
    h = (x * lax.rsqrt(var + EPS) * g_ref[...]).astype(BF16)
    ones = ones_ref[...]
    inv_hd = 1.0 / ATT_HEAD_DIM

    aq = _dot(h, waq[...])
    for t in range(ATT_WIDTH // 256):
        blk = aq[:, 256 * t:256 * (t + 1)]
        ssq = _dot((blk * blk).astype(BF16), ones)
        q_out[:, 256 * t:256 * (t + 1)] = (
            blk * lax.rsqrt(ssq * inv_hd + EPS) * qg_ref[...]).astype(BF16)

    kvr = _dot(h, wkv[...])
    k = kvr[:, :KV_WIDTH]
    ssq = _dot((k * k).astype(BF16), ones[:KV_WIDTH, :KV_WIDTH])
    kv_out[:, :KV_WIDTH] = k * lax.rsqrt(ssq * inv_hd + EPS) * kg_ref[...]
    kv_out[:, KV_WIDTH:] = kvr[:, KV_WIDTH:]

    base = jnp.full((8, 1), pos0, jnp.int32) + pos_step * tm * pl.program_id(0)
    ang_b = base.astype(F32) * inv_ref[...]
    cb = jnp.cos(ang_b)[:1]
    sb = jnp.sin(ang_b)[:1]
    cr = cr_scr[...]
    sr = sr_scr[...]
    cos = cb * cr - sb * sr
    sin = sb * cr + cb * sr
    for wseg, o_ref, scale in ((wrq, rq_out, 1.0), (wrk, rk_out, RET_DK ** -0.5)):
        r = _dot(h, wseg[...])
        for hh in range(RET_HEADS):
            x1 = r[:, RET_DK * hh:RET_DK * hh + HALF]
            x2 = r[:, RET_DK * hh + HALF:RET_DK * (hh + 1)]
            o1 = x1 * cos - x2 * sin
            o2 = x2 * cos + x1 * sin
            if scale != 1.0:
                o1 = o1 * scale
                o2 = o2 * scale
            o_ref[:, RET_DK * hh:RET_DK * hh + HALF] = o1.astype(BF16)
            o_ref[:, RET_DK * hh + HALF:RET_DK * (hh + 1)] = o2.astype(BF16)

    rv_out[...] = _dot(h, wrv[...]).astype(BF16)
    rg = _dot(h, wrg[...])
    rg_out[...] = (rg / (1.0 + jnp.exp(-rg))).astype(BF16)


def _in_proj(x, ln_g, w, qg, kg, inv, ones, tm, pos0, pos_step):
    m = x.shape[0]
    row = lambda w: pl.BlockSpec((tm, w), lambda i: (i, 0))
    out_shapes = (
        jax.ShapeDtypeStruct((m, ATT_WIDTH), BF16),
        jax.ShapeDtypeStruct((m, 2 * KV_WIDTH), F32),
        jax.ShapeDtypeStruct((m, RET_WIDTH), BF16),
        jax.ShapeDtypeStruct((m, RET_WIDTH), BF16),
        jax.ShapeDtypeStruct((m, RET_WIDTH), BF16),
        jax.ShapeDtypeStruct((m, RET_WIDTH), BF16),
    )
    return pl.pallas_call(
        functools.partial(_in_proj_body, pos0=pos0, pos_step=pos_step),
        grid=(m // tm,),
        in_specs=[row(D_MODEL), _const_spec((1, D_MODEL)), _const_spec(w.shape),
                  _const_spec((1, 256)), _const_spec((1, KV_WIDTH)),
                  _const_spec((1, HALF)), _const_spec((256, 256))],
        out_specs=[row(ATT_WIDTH), row(2 * KV_WIDTH), row(RET_WIDTH), row(RET_WIDTH),
                   row(RET_WIDTH), row(RET_WIDTH)],
        out_shape=out_shapes,
        scratch_shapes=[pltpu.VMEM((tm, HALF), F32), pltpu.VMEM((tm, HALF), F32)],
        compiler_params=pltpu.CompilerParams(
            dimension_semantics=("arbitrary",), vmem_limit_bytes=VMEM_LIMIT),
        name="in_proj",
    )(x, ln_g, w, qg, kg, inv, ones)


def _half_split(a):
    lane = lax.broadcasted_iota(jnp.int32, a.shape, 1)
    lo = lane < ATT_HEAD_DIM
    sw = pltpu.roll(a, ATT_HEAD_DIM, 1)
    zero = jnp.zeros_like(a)
    h0 = (jnp.where(lo, a, zero).astype(BF16), jnp.where(lo, zero, sw).astype(BF16))
    h1 = (jnp.where(lo, sw, zero).astype(BF16), jnp.where(lo, zero, a).astype(BF16))
    return h0, h1


def _swa_body(sink_ref, q_ref, kvc_ref, kvp_ref, wo_ref, wu_ref, wd_ref,
              o_ref, wo_bf, wu_bf, wd_bf, *, nsub):
    wo_bf[...] = wo_ref[...].astype(BF16)
    wu_bf[...] = wu_ref[...].astype(BF16)
    wd_bf[...] = wd_ref[...].astype(BF16)

    i = pl.program_id(0)
    row = lax.broadcasted_iota(jnp.int32, (WINDOW, 2 * WINDOW), 0)
    col = lax.broadcasted_iota(jnp.int32, (WINDOW, 2 * WINDOW), 1)
    band_cur = (col >= WINDOW) & (col - WINDOW <= row)
    first_col = jnp.where(i > 0, 0, WINDOW)
    lane_lo = lax.broadcasted_iota(jnp.int32, (WINDOW, LANES), 1) < ATT_HEAD_DIM
    for j in range(nsub):
        rows = slice(WINDOW * j, WINDOW * (j + 1))
        prev = kvp_ref[...] if j == 0 else kvc_ref[WINDOW * (j - 1):WINDOW * j, :]
        cur = kvc_ref[rows, :]
        kk = jnp.concatenate([prev[:, :KV_WIDTH], cur[:, :KV_WIDTH]], axis=0)
        vv = jnp.concatenate([prev[:, KV_WIDTH:], cur[:, KV_WIDTH:]], axis=0)
        ksp = _half_split(kk)
        vsp = _half_split(vv)
        lo_col = first_col if j == 0 else 0
        mask = band_cur | ((col < WINDOW) & (col >= row) & (col >= lo_col))
        for t in range(ATT_HEADS // 2):
            kh = t // (ATT_HEADS // 2 // ATT_KV_HEADS)
            qp = q_ref[rows, LANES * t:LANES * (t + 1)]
            acc = None
            inv = []
            for par in range(2):
                sink = sink_ref[2 * t + par]
                s = jnp.where(mask, _dot_nt(qp, ksp[kh][par]), -jnp.inf)
                mx = jnp.maximum(jnp.max(s, axis=-1, keepdims=True), sink)
                p = jnp.exp(s - mx)
                den = jnp.sum(p, axis=-1, keepdims=True) + jnp.exp(sink - mx)
                inv.append(1.0 / den)
                pv = _dot(p.astype(BF16), vsp[kh][par])
                acc = pv if acc is None else acc + pv
            o_ref[rows, LANES * t:LANES * (t + 1)] = (
                acc * jnp.where(lane_lo, inv[0], inv[1])).astype(BF16)


def _swa(q, kv, sinks, w_out, w_up, w_down, tq):
    m = q.shape[0]
    nsub = tq // WINDOW
    steps = m // tq
    wo_spec = pl.BlockSpec((w_out.shape[0] // steps, w_out.shape[1]), lambda i: (i, 0))
    wu_spec = pl.BlockSpec((w_up.shape[0], w_up.shape[1] // steps), lambda i: (0, i))
    wd_spec = pl.BlockSpec((w_down.shape[0] // steps, w_down.shape[1]), lambda i: (i, 0))
    return pl.pallas_call(
        functools.partial(_swa_body, nsub=nsub),
        grid=(steps,),
        in_specs=[
            pl.BlockSpec(memory_space=pltpu.SMEM),
            pl.BlockSpec((tq, ATT_WIDTH), lambda i: (i, 0)),
            pl.BlockSpec((tq, 2 * KV_WIDTH), lambda i: (i, 0)),
            pl.BlockSpec((WINDOW, 2 * KV_WIDTH), lambda i: (jnp.maximum(i * nsub - 1, 0), 0)),
            wo_spec, wu_spec, wd_spec,
        ],
        out_specs=[pl.BlockSpec((tq, ATT_WIDTH), lambda i: (i, 0)), wo_spec, wu_spec, wd_spec],
        out_shape=(jax.ShapeDtypeStruct((m, ATT_WIDTH), BF16),
                   jax.ShapeDtypeStruct(w_out.shape, BF16),
                   jax.ShapeDtypeStruct(w_up.shape, BF16),
                   jax.ShapeDtypeStruct(w_down.shape, BF16)),
        compiler_params=pltpu.CompilerParams(
            dimension_semantics=("arbitrary",), vmem_limit_bytes=VMEM_LIMIT),
        name="swa",
    )(sinks, q, kv, kv, w_out, w_up, w_down)


def _ret_body(rq_ref, rk_ref, rv_ref, rg_ref, g_ref, o_ref, s_out, s_scr, *, nsub):
    i = pl.program_id(0)

    @pl.when(i == 0)
    def _():
        s_scr[...] = jnp.zeros_like(s_scr)

    c = RET_CHUNK
    ri = lax.broadcasted_iota(jnp.int32, (c, c), 0)
    ci = lax.broadcasted_iota(jnp.int32, (c, c), 1)
    rel = (ri - ci).astype(F32)
    idx = lax.broadcasted_iota(jnp.int32, (c, 1), 0).astype(F32)
    for hh in range(RET_HEADS):
        lg = LOG_G[hh]
        dmask = jnp.where(rel >= 0, jnp.exp(lg * jnp.maximum(rel, 0.0)), 0.0)
        qdec = jnp.exp(lg * (idx + 1.0))
        kdec = jnp.exp(lg * (c - 1.0 - idx))
        cdec = math.exp(lg * c)
        cols = slice(RET_DK * hh, RET_DK * (hh + 1))
        for j in range(nsub):
            rows = slice(c * j, c * (j + 1))
            q = rq_ref[rows, cols]
            k = rk_ref[rows, cols]
            v = rv_ref[rows, cols]
            att = _dot_nt(q, k) * dmask
            o = _dot(att.astype(BF16), v)
            s_prev = s_scr[hh]
            qd = (q.astype(F32) * qdec).astype(BF16)
            o = o + _dot(qd, s_prev.astype(BF16))
            kd = (k.astype(F32) * kdec).astype(BF16)
            s_scr[hh] = cdec * s_prev + _dot_tn(kd, v)
            var = jnp.mean(o * o, axis=-1, keepdims=True)
            on = o * lax.rsqrt(var + EPS) * g_ref[:, cols]
            o_ref[rows, cols] = (on * rg_ref[rows, cols].astype(F32)).astype(BF16)

    @pl.when(i == pl.num_programs(0) - 1)
    def _():
        s_out[...] = s_scr[...]


def _retention(rq, rk, rv, rg, g, tr):
    m = rq.shape[0]
    nsub = tr // RET_CHUNK
    row = pl.BlockSpec((tr, RET_WIDTH), lambda i: (i, 0))
    return pl.pallas_call(
        functools.partial(_ret_body, nsub=nsub),
        grid=(m // tr,),
        in_specs=[row, row, row, row, _const_spec((1, RET_WIDTH))],
        out_specs=[row, pl.BlockSpec((RET_HEADS, RET_DK, RET_DV), lambda i: (0, 0, 0))],
        out_shape=(jax.ShapeDtypeStruct((m, RET_WIDTH), BF16),
                   jax.ShapeDtypeStruct((RET_HEADS, RET_DK, RET_DV), F32)),
        scratch_shapes=[pltpu.VMEM((RET_HEADS, RET_DK, RET_DV), F32)],
        compiler_params=pltpu.CompilerParams(
            dimension_semantics=("arbitrary",), vmem_limit_bytes=VMEM_LIMIT),
        name="retention",
    )(rq, rk, rv, rg, g)


def _swa_dec_body(q_ref, kvn_ref, ck_ref, cv_ref, sink_ref, o_ref, nk_ref, nv_ref):
    bb = q_ref.shape[0]
    npair = ATT_HEADS // 2
    q8 = q_ref[...].astype(F32)
    q8r = pltpu.roll(q8, ATT_HEAD_DIM, 2)
    lane = lax.broadcasted_iota(jnp.int32, q8.shape, 2)
    pair = lax.broadcasted_iota(jnp.int32, q8.shape, 1)
    lo = lane < ATT_HEAD_DIM
    kv0 = pair < npair // ATT_KV_HEADS
    own = lo == kv0
    zero = jnp.zeros_like(q8)
    qe = jnp.where(own, jnp.where(kv0, q8, q8r), zero)
    qo = jnp.where(own, jnp.where(kv0, q8r, q8), zero)
    qb = jnp.concatenate([qe, qo], axis=1)

    ck = ck_ref[...]
    cv = cv_ref[...]
    kn = kvn_ref[:, :KV_WIDTH]
    vn = kvn_ref[:, KV_WIDTH:]
    s = lax.dot_general(qb.astype(BF16), ck.astype(BF16), (((2,), (2,)), ((0,), (0,))),
                        preferred_element_type=F32)
    s_new = jnp.sum(qb * kn[:, None, :], axis=-1, keepdims=True)
    sink = sink_ref[...][None, :, :]
    mx = jnp.maximum(jnp.maximum(jnp.max(s, axis=-1, keepdims=True), s_new), sink)
    p = jnp.exp(s - mx)
    p_new = jnp.exp(s_new - mx)
    den = jnp.sum(p, axis=-1, keepdims=True) + p_new + jnp.exp(sink - mx)
    o = lax.dot_general(p.astype(BF16), cv.astype(BF16), (((2,), (1,)), ((0,), (0,))),
                        preferred_element_type=F32)
    o = (o + p_new * vn[:, None, :]) / den
    oe = o[:, :npair, :]
    oo = o[:, npair:, :]
    oer = pltpu.roll(oe, ATT_HEAD_DIM, 2)
    oor = pltpu.roll(oo, ATT_HEAD_DIM, 2)
    o_ref[...] = jnp.where(lo, jnp.where(kv0, oe, oer), jnp.where(kv0, oor, oo)).astype(BF16)

    nk_ref[:, :WINDOW - 1, :] = ck[:, 1:, :]
    nk_ref[:, WINDOW - 1:, :] = kn[:, None, :]
    nv_ref[:, :WINDOW - 1, :] = cv[:, 1:, :]
    nv_ref[:, WINDOW - 1:, :] = vn[:, None, :]


def _swa_dec(q8, kvn, ck, cv, sink_col, bb):
    b = q8.shape[0]
    cache = pl.BlockSpec((bb, WINDOW, KV_WIDTH), lambda i: (i, 0, 0))
    return pl.pallas_call(
        _swa_dec_body,
        grid=(b // bb,),
        in_specs=[pl.BlockSpec((bb, ATT_HEADS // 2, LANES), lambda i: (i, 0, 0)),
                  pl.BlockSpec((bb, 2 * KV_WIDTH), lambda i: (i, 0)),
                  cache, cache, _const_spec((ATT_HEADS, 1))],
        out_specs=[pl.BlockSpec((bb, ATT_HEADS // 2, LANES), lambda i: (i, 0, 0)), cache, cache],
        out_shape=(jax.ShapeDtypeStruct((b, ATT_HEADS // 2, LANES), BF16),
                   jax.ShapeDtypeStruct(ck.shape, F32),
                   jax.ShapeDtypeStruct(cv.shape, F32)),
        compiler_params=pltpu.CompilerParams(
            dimension_semantics=("arbitrary",), vmem_limit_bytes=VMEM_LIMIT),
        name="swa_dec",
    )(q8, kvn, ck, cv, sink_col)


def _ret_dec_body(qt_ref, kt_ref, rq_ref, rk_ref, rv_ref, rg_ref, g_ref, s_ref,
                  o_ref, ns_ref, o_scr):
    bb = s_ref.shape[0]
    for hh in range(RET_HEADS):
        g1 = math.exp(LOG_G[hh])
        cols = slice(RET_DK * hh, RET_DK * (hh + 1))
        for jb in range(bb):
            s0 = s_ref[jb, hh]
            qc = qt_ref[0, cols, jb:jb + 1] * g1
            kc = kt_ref[0, cols, jb:jb + 1]
            v = rv_ref[jb:jb + 1, cols].astype(F32)
            o_scr[jb:jb + 1, cols] = jnp.sum(qc * s0, axis=0, keepdims=True)
            ns_ref[jb, hh] = g1 * s0 + kc * v
    for hh in range(RET_HEADS):
        cols = slice(RET_DK * hh, RET_DK * (hh + 1))
        qk = jnp.sum(rq_ref[:, cols].astype(F32) * rk_ref[:, cols].astype(F32),
                     axis=-1, keepdims=True)
        o = o_scr[:, cols] + qk * rv_ref[:, cols].astype(F32)
        var = jnp.mean(o * o, axis=-1, keepdims=True)
        on = o * lax.rsqrt(var + EPS) * g_ref[:, cols]
        o_ref[:, cols] = (on * rg_ref[:, cols].astype(F32)).astype(BF16)


def _ret_dec(qt, kt, rq, rk, rv, rg, g, state, bb):
    b = rv.shape[0]
    row = pl.BlockSpec((bb, RET_WIDTH), lambda i: (i, 0))
    col = pl.BlockSpec((1, RET_WIDTH, bb), lambda i: (i, 0, 0))
    st = pl.BlockSpec((bb, RET_HEADS, RET_DK, RET_DV), lambda i: (i, 0, 0, 0))
    return pl.pallas_call(
        _ret_dec_body,
        grid=(b // bb,),
        in_specs=[col, col, row, row, row, row, _const_spec((1, RET_WIDTH)), st],
        out_specs=[row, st],
        out_shape=(jax.ShapeDtypeStruct((b, RET_WIDTH), BF16),
                   jax.ShapeDtypeStruct(state.shape, F32)),
        scratch_shapes=[pltpu.VMEM((bb, RET_WIDTH), F32)],
        compiler_params=pltpu.CompilerParams(
            dimension_semantics=("arbitrary",), vmem_limit_bytes=VMEM_LIMIT),
        name="ret_dec",
    )(qt, kt, rq, rk, rv, rg, g, state)


def _out_proj_body(x_ref, a_ref, r_ref, w_ref, g_ref, x1_ref, h2_ref):
    x1 = (x_ref[...] + _dot(a_ref[...], w_ref[:ATT_WIDTH, :])
          + _dot(r_ref[...], w_ref[ATT_WIDTH:, :]))
    x1_ref[...] = x1
    var = jnp.mean(x1 * x1, axis=-1, keepdims=True)
    h2_ref[...] = (x1 * lax.rsqrt(var + EPS) * g_ref[...]).astype(BF16)


def _out_proj(x, a, r, w, g, tm):
    m = x.shape[0]
    row = lambda w: pl.BlockSpec((tm, w), lambda i: (i, 0))
    return pl.pallas_call(
        _out_proj_body,
        grid=(m // tm,),
        in_specs=[row(D_MODEL), row(ATT_WIDTH), row(RET_WIDTH),
                  _const_spec(w.shape), _const_spec((1, D_MODEL))],
        out_specs=[row(D_MODEL), row(D_MODEL)],
        out_shape=(jax.ShapeDtypeStruct((m, D_MODEL), F32),
                   jax.ShapeDtypeStruct((m, D_MODEL), BF16)),
        compiler_params=pltpu.CompilerParams(
            dimension_semantics=("arbitrary",), vmem_limit_bytes=VMEM_LIMIT),
        name="out_proj",
    )(x, a, r, w, g)


def _mlp_body(x1_ref, h2_ref, wu_ref, wd_ref, o_ref):
    f = pl.program_id(1)

    @pl.when(f == 0)
    def _():
        o_ref[...] = x1_ref[...]

    u = jnp.maximum(_dot(h2_ref[...], wu_ref[...]), 0.0)
    o_ref[...] += _dot((u * u).astype(BF16), wd_ref[...])


def _mlp(x1, h2, wu, wd, tm, tf):
    m = x1.shape[0]
    return pl.pallas_call(
        _mlp_body,
        grid=(m // tm, D_FF // tf),
        in_specs=[pl.BlockSpec((tm, D_MODEL), lambda i, f: (i, 0)),
                  pl.BlockSpec((tm, D_MODEL), lambda i, f: (i, 0)),
                  pl.BlockSpec((D_MODEL, tf), lambda i, f: (0, f)),
                  pl.BlockSpec((tf, D_MODEL), lambda i, f: (f, 0))],
        out_specs=pl.BlockSpec((tm, D_MODEL), lambda i, f: (i, 0)),
        out_shape=jax.ShapeDtypeStruct((m, D_MODEL), F32),
        compiler_params=pltpu.CompilerParams(
            dimension_semantics=("arbitrary", "arbitrary"), vmem_limit_bytes=VMEM_LIMIT),
        name="mlp",
    )(x1, h2, wu, wd)


def kernel(x_prompt, x_sample, cache_k_win, cache_v_win, state_ret, ln1_g, w_in, q_norm_g,
           k_norm_g, attn_sinks, ret_norm_g, w_out, ln2_g, w_up, w_down):
    seq = x_prompt.shape[1]
    nb = x_sample.shape[0]
    assert x_prompt.shape[0] == 1 and x_sample.shape[1] == 1 and w_in.shape[0] == 1

    wi = w_in[0].astype(BF16)
    inv = (ROPE_BASE ** (-jnp.arange(HALF, dtype=F32) / HALF)).reshape(1, HALF)

    ln1 = ln1_g.reshape(1, D_MODEL)
    ln2 = ln2_g.reshape(1, D_MODEL)
    qg = jnp.tile(q_norm_g.reshape(1, ATT_HEAD_DIM) * (ATT_HEAD_DIM ** -0.5), (1, 256 // ATT_HEAD_DIM))
    kg = jnp.tile(k_norm_g.reshape(1, ATT_HEAD_DIM), (1, KV_WIDTH // ATT_HEAD_DIM))
    rg_g = ret_norm_g.reshape(1, RET_WIDTH)
    sinks = attn_sinks.reshape(ATT_HEADS)
    blk = jnp.arange(256) // ATT_HEAD_DIM
    ones = (blk[:, None] == blk[None, :]).astype(BF16)

    xp = x_prompt[0]
    q, kv, rq, rk, rv, rg = _in_proj(xp, ln1, wi, qg, kg, inv, ones, tm=512, pos0=0, pos_step=1)
    a_out, wo, wu, wd = _swa(q, kv, sinks, w_out[0], w_up[0], w_down[0], tq=512)
    r_out, s_fin = _retention(rq, rk, rv, rg, rg_g, tr=512)
    x1, h2 = _out_proj(xp, a_out, r_out, wo, ln2, tm=512)
    yp = _mlp(x1, h2, wu, wd, tm=512, tf=1024)

    wb = min(WINDOW, seq)
    kp = kv[seq - wb:, :KV_WIDTH].reshape(1, 1, wb, ATT_KV_HEADS, ATT_HEAD_DIM)
    vp = kv[seq - wb:, KV_WIDTH:].reshape(1, 1, wb, ATT_KV_HEADS, ATT_HEAD_DIM)
    sp = s_fin.reshape(1, 1, RET_HEADS, RET_DK, RET_DV)

    xs = x_sample[:, 0]
    qs, kvs, rqs, rks, rvs, rgs = _in_proj(xs, ln1, wi, qg, kg, inv, ones, tm=nb,
                                           pos0=PAST_LEN, pos_step=0)
    sink_col = jnp.concatenate([sinks[0::2], sinks[1::2]]).reshape(ATT_HEADS, 1)
    ck = cache_k_win[0].reshape(nb, WINDOW, KV_WIDTH)
    cv = cache_v_win[0].reshape(nb, WINDOW, KV_WIDTH)
    a8, nk, nv = _swa_dec(qs.reshape(nb, ATT_HEADS // 2, LANES), kvs, ck, cv, sink_col, bb=16)
    a_s = a8.reshape(nb, ATT_WIDTH)

    bb = 8
    rq32 = rqs.astype(F32)
    rk32 = rks.astype(F32)
    to_cols = lambda a: a.reshape(nb // bb, bb, RET_WIDTH).transpose(0, 2, 1)
    r_s, ns = _ret_dec(to_cols(rq32), to_cols(rk32), rqs, rks, rvs, rgs, rg_g, state_ret[0], bb=bb)
    x1s, h2s = _out_proj(xs, a_s, r_s, wo, ln2, tm=nb)
    ys = _mlp(x1s, h2s, wu, wd, tm=nb, tf=1024)

    shape_kv = (1, nb, WINDOW, ATT_KV_HEADS, ATT_HEAD_DIM)
    return (yp[None], ys[:, None, :], kp, vp, sp,
            nk.reshape(shape_kv), nv.reshape(shape_kv), ns[None])
```

```python
import functools
import math

import jax
import jax.numpy as jnp
from jax import lax
from jax.experimental import pallas as pl
from jax.experimental.pallas import tpu as pltpu

D_MODEL = 2048
ATT_HEADS = 16
ATT_KV_HEADS = 2
ATT_HEAD_DIM = 64
WINDOW = 128
RET_HEADS = 4
RET_DK = 256
RET_DV = 256
RET_CHUNK = 128
ROPE_BASE = 10000.0
D_FF = 4 * D_MODEL
EPS = 1e-6
PAST_LEN = 8192

ATT_WIDTH = ATT_HEADS * ATT_HEAD_DIM
KV_WIDTH = ATT_KV_HEADS * ATT_HEAD_DIM
RET_WIDTH = RET_HEADS * RET_DK
LANES = 128
HALF = RET_DK // 2
SPLIT_A = ATT_WIDTH + 2 * KV_WIDTH + RET_WIDTH
SPLIT_B = 3 * RET_WIDTH

F32 = jnp.float32
BF16 = jnp.bfloat16
VMEM_LIMIT = 60 * 1024 * 1024

TM = 512
TF = 1024
BB_ATT = 16
BB_RET = 8

LOG_G = tuple(math.log1p(-(2.0 ** (-5.0 - h))) for h in range(RET_HEADS))


def _dot(a, b):
    return jnp.dot(a, b, preferred_element_type=F32)


def _dot_nt(a, b):
    return lax.dot_general(a, b, (((1,), (1,)), ((), ())), preferred_element_type=F32)


def _dot_tn(a, b):
    return lax.dot_general(a, b, (((0,), (0,)), ((), ())), preferred_element_type=F32)


def _const_spec(shape):
    n = len(shape)
    return pl.BlockSpec(shape, lambda *_: (0,) * n, pipeline_mode=pl.Buffered(1))


def _params(n_axes=1):
    return pltpu.CompilerParams(dimension_semantics=("arbitrary",) * n_axes,
                                vmem_limit_bytes=VMEM_LIMIT)


def _norm_rows(x_ref, g_ref):
    x = x_ref[...]
    var = jnp.mean(x * x, axis=-1, keepdims=True)
    return (x * lax.rsqrt(var + EPS) * g_ref[...]).astype(BF16)


def _attn_proj(h, w_ref, qg_ref, kg_ref, ones_ref):
    ones = ones_ref[...]
    inv_hd = 1.0 / ATT_HEAD_DIM
    aq = _dot(h, w_ref[:, :ATT_WIDTH])
    q_tiles = []
    for t in range(ATT_WIDTH // 256):
        blk = aq[:, 256 * t:256 * (t + 1)]
        ssq = _dot((blk * blk).astype(BF16), ones)
        q_tiles.append((blk * lax.rsqrt(ssq * inv_hd + EPS) * qg_ref[...]).astype(BF16))
    kvr = _dot(h, w_ref[:, ATT_WIDTH:ATT_WIDTH + 2 * KV_WIDTH])
    k = kvr[:, :KV_WIDTH]
    ssq = _dot((k * k).astype(BF16), ones[:KV_WIDTH, :KV_WIDTH])
    kn = k * lax.rsqrt(ssq * inv_hd + EPS) * kg_ref[...]
    return q_tiles, kn, kvr[:, KV_WIDTH:]


def _rope_init(inv_ref, cr_scr, sr_scr, pos_step):
    tm = cr_scr.shape[0]
    row = lax.broadcasted_iota(jnp.int32, (tm, 1), 0)
    ang_r = (pos_step * row).astype(F32) * inv_ref[...]
    cr_scr[...] = jnp.cos(ang_r)
    sr_scr[...] = jnp.sin(ang_r)


def _rope_tables(inv_ref, cr_scr, sr_scr, base_pos):
    base = jnp.zeros((8, 1), jnp.int32) + base_pos
    ang_b = base.astype(F32) * inv_ref[...]
    cb = jnp.cos(ang_b)[:1]
    sb = jnp.sin(ang_b)[:1]
    cr = cr_scr[...]
    sr = sr_scr[...]
    return cb * cr - sb * sr, sb * cr + cb * sr


def _rope_head(r, cos, sin, scale, o_ref, hh):
    x1 = r[:, :HALF]
    x2 = r[:, HALF:]
    o1 = x1 * cos - x2 * sin
    o2 = x2 * cos + x1 * sin
    if scale != 1.0:
        o1 = o1 * scale
        o2 = o2 * scale
    o_ref[:, RET_DK * hh:RET_DK * hh + HALF] = o1.astype(BF16)
    o_ref[:, RET_DK * hh + HALF:RET_DK * (hh + 1)] = o2.astype(BF16)


def _rope_proj(h, w, cos, sin, scale, o_ref):
    r = _dot(h, w)
    for hh in range(RET_HEADS):
        _rope_head(r[:, RET_DK * hh:RET_DK * (hh + 1)], cos, sin, scale, o_ref, hh)


def _gate_proj(h, w, o_ref):
    rg = _dot(h, w)
    o_ref[...] = (rg / (1.0 + jnp.exp(-rg))).astype(BF16)


def _half_split(a):
    lane = lax.broadcasted_iota(jnp.int32, a.shape, 1)
    lo = lane < ATT_HEAD_DIM
    sw = pltpu.roll(a, ATT_HEAD_DIM, 1)
    zero = jnp.zeros_like(a)
    h0 = (jnp.where(lo, a, zero).astype(BF16), jnp.where(lo, zero, sw).astype(BF16))
    h1 = (jnp.where(lo, sw, zero).astype(BF16), jnp.where(lo, zero, a).astype(BF16))
    return h0, h1


SWA_GROUP = 4


def _swa_phases(sink_ref, q_ref, kvc_ref, kvp_ref, o_ref, has_prev):
    nsub = q_ref.shape[0] // WINDOW
    npair = ATT_HEADS // 2
    row = lax.broadcasted_iota(jnp.int32, (WINDOW, 2 * WINDOW), 0)
    col = lax.broadcasted_iota(jnp.int32, (WINDOW, 2 * WINDOW), 1)
    band_cur = (col >= WINDOW) & (col - WINDOW <= row)
    first_col = jnp.where(has_prev, 0, WINDOW)
    lane_lo = lax.broadcasted_iota(jnp.int32, (WINDOW, LANES), 1) < ATT_HEAD_DIM
    ctx = [{} for _ in range(nsub)]
    st = {}

    def prep(g):
        j, first = divmod(g * SWA_GROUP, npair)
        if first:
            return
        prev = kvp_ref[...] if j == 0 else kvc_ref[WINDOW * (j - 1):WINDOW * j, :]
        cur = kvc_ref[WINDOW * j:WINDOW * (j + 1), :]
        kk = jnp.concatenate([prev[:, :KV_WIDTH], cur[:, :KV_WIDTH]], axis=0)
        vv = jnp.concatenate([prev[:, KV_WIDTH:], cur[:, KV_WIDTH:]], axis=0)
        ctx[j]["k"] = _half_split(kk)
        ctx[j]["v"] = _half_split(vv)
        lo_col = first_col if j == 0 else 0
        ctx[j]["mask"] = band_cur | ((col < WINDOW) & (col >= row) & (col >= lo_col))

    def units(g):
        for u in range(g * SWA_GROUP, (g + 1) * SWA_GROUP):
            j, t = divmod(u, npair)
            yield j, t, t // (npair // ATT_KV_HEADS), slice(WINDOW * j, WINDOW * (j + 1))

    def qk(g):
        us = list(units(g))
        j, _, kh, rows = us[0]
        qs = jnp.concatenate([q_ref[rows, LANES * t:LANES * (t + 1)] for _, t, _, _ in us], axis=0)
        s = [_dot_nt(qs, ctx[j]["k"][kh][par]) for par in range(2)]
        for n, (_, t, _, _) in enumerate(us):
            st[j, t] = {"s": [sp[WINDOW * n:WINDOW * (n + 1), :] for sp in s]}

    def softmax(g):
        for j, t, kh, rows in units(g):
            p, inv = [], []
            for par in range(2):
                sink = sink_ref[2 * t + par]
                s = jnp.where(ctx[j]["mask"], st[j, t]["s"][par], -jnp.inf)
                mx = jnp.maximum(jnp.max(s, axis=-1, keepdims=True), sink)
                e = jnp.exp(s - mx)
                den = jnp.sum(e, axis=-1, keepdims=True) + jnp.exp(sink - mx)
                p.append(e.astype(BF16))
                inv.append(1.0 / den)
            st[j, t] = {"p": p, "inv": jnp.where(lane_lo, inv[0], inv[1])}

    def pv(g):
        us = list(units(g))
        j, _, kh, rows = us[0]
        got = [st.pop((j, t)) for _, t, _, _ in us]
        acc = sum(_dot(jnp.concatenate([u["p"][par] for u in got], axis=0), ctx[j]["v"][kh][par])
                  for par in range(2))
        for n, (_, t, _, _) in enumerate(us):
            o_ref[rows, LANES * t:LANES * (t + 1)] = (
                acc[WINDOW * n:WINDOW * (n + 1), :] * got[n]["inv"]).astype(BF16)

    ngroup = nsub * npair // SWA_GROUP
    return ngroup, prep, qk, softmax, pv


def _emit_pipelined(main, ngroup, prep, first, middle, last):
    nstage = ngroup + 2
    done = 0
    prep(0)
    for k in range(nstage):
        if k < ngroup:
            first(k)
        if 0 <= k - 2 < ngroup:
            last(k - 2)
        if 0 <= k - 1 < ngroup:
            middle(k - 1)
        if k + 1 < ngroup:
            prep(k + 1)
        upto = -(-len(main) * (k + 1) // nstage)
        for piece in main[done:upto]:
            piece()
        done = upto


def _ret_block(rq_ref, rk_ref, rv_ref, gt_ref, g_ref, o_ref, s_scr):
    c = RET_CHUNK
    nsub = rq_ref.shape[0] // c
    ri = lax.broadcasted_iota(jnp.int32, (c, c), 0)
    ci = lax.broadcasted_iota(jnp.int32, (c, c), 1)
    rel = (ri - ci).astype(F32)
    idx = lax.broadcasted_iota(jnp.int32, (c, 1), 0).astype(F32)
    for hh in range(RET_HEADS):
        lg = LOG_G[hh]
        dmask = jnp.where(rel >= 0, jnp.exp(lg * jnp.maximum(rel, 0.0)), 0.0)
        qdec = jnp.exp(lg * (idx + 1.0))
        kdec = jnp.exp(lg * (c - 1.0 - idx))
        cdec = math.exp(lg * c)
        cols = slice(RET_DK * hh, RET_DK * (hh + 1))
        for j in range(nsub):
            rows = slice(c * j, c * (j + 1))
            q = rq_ref[rows, cols]
            k = rk_ref[rows, cols]
            v = rv_ref[rows, cols]
            att = _dot_nt(q, k) * dmask
            o = _dot(att.astype(BF16), v)
            s_prev = s_scr[hh]
            qd = (q.astype(F32) * qdec).astype(BF16)
            o = o + _dot(qd, s_prev.astype(BF16))
            kd = (k.astype(F32) * kdec).astype(BF16)
            s_scr[hh] = cdec * s_prev + _dot_tn(kd, v)
            var = jnp.mean(o * o, axis=-1, keepdims=True)
            on = o * lax.rsqrt(var + EPS) * g_ref[:, cols]
            o_ref[rows, cols] = (on * gt_ref[rows, cols].astype(F32)).astype(BF16)


def _proj_swa_body(sink_ref, x_ref, g_ref, w_ref, qg_ref, kg_ref, inv_ref, ones_ref,
                   wo_ref, wu_ref,
                   a_out, kv_out, rq_out, wo_bf, wu_bf,
                   q_scr, kv_scr, kvp_scr, cr_scr, sr_scr, *, nblk):
    i = pl.program_id(0)
    tm = x_ref.shape[0]

    @pl.when(i == 0)
    def _():
        _rope_init(inv_ref, cr_scr, sr_scr, 1)
        q_scr[...] = jnp.zeros_like(q_scr)
        kv_scr[...] = jnp.zeros_like(kv_scr)
        kvp_scr[...] = jnp.zeros_like(kvp_scr)

    wo_bf[...] = wo_ref[...].astype(BF16)
    wu_bf[...] = wu_ref[...].astype(BF16)

    cur = lax.rem(i, 2)
    prv = 1 - cur
    blk = jnp.minimum(i, nblk - 1)

    h = _norm_rows(x_ref, g_ref)
    ones = ones_ref[...]
    inv_hd = 1.0 / ATT_HEAD_DIM
    cos, sin = _rope_tables(inv_ref, cr_scr, sr_scr, blk * tm)


    def q_piece(n):
        y = _dot(h, w_ref[:, 512 * n:512 * (n + 1)])
        for t in range(2):
            yt = y[:, 256 * t:256 * (t + 1)]
            ssq = _dot((yt * yt).astype(BF16), ones)
            c0 = 512 * n + 256 * t
            q_scr[cur, :, c0:c0 + 256] = (
                yt * lax.rsqrt(ssq * inv_hd + EPS) * qg_ref[...]).astype(BF16)

    kv0 = ATT_WIDTH
    rq0 = ATT_WIDTH + 2 * KV_WIDTH

    def kv_rq_piece():
        y = _dot(h, w_ref[:, kv0:rq0 + RET_DK])
        k = y[:, :KV_WIDTH]
        ssq = _dot((k * k).astype(BF16), ones[:KV_WIDTH, :KV_WIDTH])
        kn = k * lax.rsqrt(ssq * inv_hd + EPS) * kg_ref[...]
        v = y[:, KV_WIDTH:2 * KV_WIDTH]
        kv_scr[cur, :, :KV_WIDTH] = kn
        kv_scr[cur, :, KV_WIDTH:] = v
        kv_out[:, :KV_WIDTH] = kn
        kv_out[:, KV_WIDTH:] = v
        _rope_head(y[:, 2 * KV_WIDTH:], cos, sin, 1.0, rq_out, 0)

    def rq_piece():
        y = _dot(h, w_ref[:, rq0 + RET_DK:])
        for hh in range(1, RET_HEADS):
            _rope_head(y[:, RET_DK * (hh - 1):RET_DK * hh], cos, sin, 1.0, rq_out, hh)

    main = [functools.partial(q_piece, 0), functools.partial(q_piece, 1), kv_rq_piece, rq_piece]
    _emit_pipelined(main, *_swa_phases(sink_ref, q_scr.at[prv], kv_scr.at[prv], kvp_scr, a_out,
                                       has_prev=i > 1))
    kvp_scr[...] = kv_scr[prv, tm - WINDOW:, :]


def _proj_swa(x, ln_g, w_a, qg, kg, inv, ones, sinks, w_out, w_up):
    m = x.shape[0]
    nblk = m // TM
    cl = lambda i: jnp.minimum(i, nblk - 1)
    row = lambda w: pl.BlockSpec((TM, w), lambda i: (cl(i), 0))
    wo_spec = pl.BlockSpec((w_out.shape[0] // nblk, w_out.shape[1]), lambda i: (cl(i), 0))
    wu_spec = pl.BlockSpec((w_up.shape[0], w_up.shape[1] // nblk), lambda i: (0, cl(i)))
    return pl.pallas_call(
        functools.partial(_proj_swa_body, nblk=nblk),
        grid=(nblk + 1,),
        in_specs=[pl.BlockSpec(memory_space=pltpu.SMEM),
                  row(D_MODEL), _const_spec((1, D_MODEL)), _const_spec(w_a.shape),
                  _const_spec((1, 256)), _const_spec((1, KV_WIDTH)),
                  _const_spec((1, HALF)), _const_spec((256, 256)), wo_spec, wu_spec],
        out_specs=[pl.BlockSpec((TM, ATT_WIDTH), lambda i: (jnp.maximum(i - 1, 0), 0)),
                   row(2 * KV_WIDTH), row(RET_WIDTH), wo_spec, wu_spec],
        out_shape=(jax.ShapeDtypeStruct((m, ATT_WIDTH), BF16),
                   jax.ShapeDtypeStruct((m, 2 * KV_WIDTH), F32),
                   jax.ShapeDtypeStruct((m, RET_WIDTH), BF16),
                   jax.ShapeDtypeStruct(w_out.shape, BF16),
                   jax.ShapeDtypeStruct(w_up.shape, BF16)),
        scratch_shapes=[pltpu.VMEM((2, TM, ATT_WIDTH), BF16),
                        pltpu.VMEM((2, TM, 2 * KV_WIDTH), F32),
                        pltpu.VMEM((WINDOW, 2 * KV_WIDTH), F32),
                        pltpu.VMEM((TM, HALF), F32), pltpu.VMEM((TM, HALF), F32)],
        compiler_params=_params(),
        name="proj_swa",
    )(sinks, x, ln_g, w_a, qg, kg, inv, ones, w_out, w_up)


def _proj_ret_body(x_ref, g_ref, w_ref, inv_ref, rq_ref, rg_ref, wd_ref,
                   r_out, s_out, wd_bf,
                   rk_scr, rv_scr, gt_scr, s_scr, cr_scr, sr_scr, *, nblk):
    i = pl.program_id(0)
    tm = x_ref.shape[0]

    @pl.when(i == 0)
    def _():
        _rope_init(inv_ref, cr_scr, sr_scr, 1)
        rk_scr[...] = jnp.zeros_like(rk_scr)
        rv_scr[...] = jnp.zeros_like(rv_scr)
        gt_scr[...] = jnp.zeros_like(gt_scr)
        s_scr[...] = jnp.zeros_like(s_scr)

    wd_bf[...] = wd_ref[...].astype(BF16)

    cur = lax.rem(i, 2)
    prv = 1 - cur
    blk = jnp.minimum(i, nblk - 1)

    h = _norm_rows(x_ref, g_ref)
    cos, sin = _rope_tables(inv_ref, cr_scr, sr_scr, blk * tm)
    _rope_proj(h, w_ref[:, :RET_WIDTH], cos, sin, RET_DK ** -0.5, rk_scr.at[cur])
    rv_scr[cur] = _dot(h, w_ref[:, RET_WIDTH:2 * RET_WIDTH]).astype(BF16)
    _gate_proj(h, w_ref[:, 2 * RET_WIDTH:], gt_scr.at[cur])

    _ret_block(rq_ref, rk_scr.at[prv], rv_scr.at[prv], gt_scr.at[prv], rg_ref, r_out, s_scr)

    @pl.when(i == nblk)
    def _():
        s_out[...] = s_scr[...]


def _proj_ret(x, ln_g, w_b, inv, rq, rg_g, w_down):
    m = x.shape[0]
    nblk = m // TM
    cl = lambda i: jnp.minimum(i, nblk - 1)
    prev = lambda w: pl.BlockSpec((TM, w), lambda i: (jnp.maximum(i - 1, 0), 0))
    wd_spec = pl.BlockSpec((w_down.shape[0] // nblk, w_down.shape[1]), lambda i: (cl(i), 0))
    state = (RET_HEADS, RET_DK, RET_DV)
    slot = pltpu.VMEM((2, TM, RET_WIDTH), BF16)
    return pl.pallas_call(
        functools.partial(_proj_ret_body, nblk=nblk),
        grid=(nblk + 1,),
        in_specs=[pl.BlockSpec((TM, D_MODEL), lambda i: (cl(i), 0)),
                  _const_spec((1, D_MODEL)), _const_spec(w_b.shape), _const_spec((1, HALF)),
                  prev(RET_WIDTH), _const_spec((1, RET_WIDTH)), wd_spec],
        out_specs=[prev(RET_WIDTH), pl.BlockSpec(state, lambda i: (0, 0, 0)), wd_spec],
        out_shape=(jax.ShapeDtypeStruct((m, RET_WIDTH), BF16),
                   jax.ShapeDtypeStruct(state, F32),
                   jax.ShapeDtypeStruct(w_down.shape, BF16)),
        scratch_shapes=[slot, slot, slot, pltpu.VMEM(state, F32),
                        pltpu.VMEM((TM, HALF), F32), pltpu.VMEM((TM, HALF), F32)],
        compiler_params=_params(),
        name="proj_ret",
    )(x, ln_g, w_b, inv, rq, rg_g, w_down)


def _in_proj_body(x_ref, g_ref, wa_ref, wb_ref, qg_ref, kg_ref, inv_ref, ones_ref,
                  q_out, kv_out, rq_out, rk_out, rv_out, gt_out, cr_scr, sr_scr, *, pos):
    _rope_init(inv_ref, cr_scr, sr_scr, 0)
    h = _norm_rows(x_ref, g_ref)
    q_tiles, kn, v = _attn_proj(h, wa_ref, qg_ref, kg_ref, ones_ref)
    for t, qt in enumerate(q_tiles):
        q_out[:, 256 * t:256 * (t + 1)] = qt
    kv_out[:, :KV_WIDTH] = kn
    kv_out[:, KV_WIDTH:] = v
    cos, sin = _rope_tables(inv_ref, cr_scr, sr_scr, pos)
    _rope_proj(h, wa_ref[:, ATT_WIDTH + 2 * KV_WIDTH:], cos, sin, 1.0, rq_out)
    _rope_proj(h, wb_ref[:, :RET_WIDTH], cos, sin, RET_DK ** -0.5, rk_out)
    rv_out[...] = _dot(h, wb_ref[:, RET_WIDTH:2 * RET_WIDTH]).astype(BF16)
    _gate_proj(h, wb_ref[:, 2 * RET_WIDTH:], gt_out)


def _in_proj(x, ln_g, w_a, w_b, qg, kg, inv, ones, pos):
    m = x.shape[0]
    full = lambda w: pl.BlockSpec((m, w), lambda i: (0, 0))
    ret = jax.ShapeDtypeStruct((m, RET_WIDTH), BF16)
    return pl.pallas_call(
        functools.partial(_in_proj_body, pos=pos),
        grid=(1,),
        in_specs=[full(D_MODEL), _const_spec((1, D_MODEL)), _const_spec(w_a.shape),
                  _const_spec(w_b.shape), _const_spec((1, 256)), _const_spec((1, KV_WIDTH)),
                  _const_spec((1, HALF)), _const_spec((256, 256))],
        out_specs=[full(ATT_WIDTH), full(2 * KV_WIDTH)] + [full(RET_WIDTH)] * 4,
        out_shape=(jax.ShapeDtypeStruct((m, ATT_WIDTH), BF16),
                   jax.ShapeDtypeStruct((m, 2 * KV_WIDTH), F32), ret, ret, ret, ret),
        scratch_shapes=[pltpu.VMEM((m, HALF), F32), pltpu.VMEM((m, HALF), F32)],
        compiler_params=_params(),
        name="in_proj",
    )(x, ln_g, w_a, w_b, qg, kg, inv, ones)


def _swa_dec_body(q_ref, kvn_ref, ck_ref, cv_ref, sink_ref, o_ref, nk_ref, nv_ref):
    npair = ATT_HEADS // 2
    q8 = q_ref[...].astype(F32)
    q8r = pltpu.roll(q8, ATT_HEAD_DIM, 2)
    lane = lax.broadcasted_iota(jnp.int32, q8.shape, 2)
    pair = lax.broadcasted_iota(jnp.int32, q8.shape, 1)
    lo = lane < ATT_HEAD_DIM
    kv0 = pair < npair // ATT_KV_HEADS
    own = lo == kv0
    zero = jnp.zeros_like(q8)
    qe = jnp.where(own, jnp.where(kv0, q8, q8r), zero)
    qo = jnp.where(own, jnp.where(kv0, q8r, q8), zero)
    qb = jnp.concatenate([qe, qo], axis=1)

    ck = ck_ref[...]
    cv = cv_ref[...]
    kn = kvn_ref[:, :KV_WIDTH]
    vn = kvn_ref[:, KV_WIDTH:]
    s = lax.dot_general(qb.astype(BF16), ck.astype(BF16), (((2,), (2,)), ((0,), (0,))),
                        preferred_element_type=F32)
    s_new = jnp.sum(qb * kn[:, None, :], axis=-1, keepdims=True)
    sink = sink_ref[...][None, :, :]
    mx = jnp.maximum(jnp.maximum(jnp.max(s, axis=-1, keepdims=True), s_new), sink)
    p = jnp.exp(s - mx)
    p_new = jnp.exp(s_new - mx)
    den = jnp.sum(p, axis=-1, keepdims=True) + p_new + jnp.exp(sink - mx)
    o = lax.dot_general(p.astype(BF16), cv.astype(BF16), (((2,), (1,)), ((0,), (0,))),
                        preferred_element_type=F32)
    o = (o + p_new * vn[:, None, :]) / den
    oe = o[:, :npair, :]
    oo = o[:, npair:, :]
    oer = pltpu.roll(oe, ATT_HEAD_DIM, 2)
    oor = pltpu.roll(oo, ATT_HEAD_DIM, 2)
    o_ref[...] = jnp.where(lo, jnp.where(kv0, oe, oer), jnp.where(kv0, oor, oo)).astype(BF16)

    nk_ref[:, :WINDOW - 1, :] = ck[:, 1:, :]
    nk_ref[:, WINDOW - 1:, :] = kn[:, None, :]
    nv_ref[:, :WINDOW - 1, :] = cv[:, 1:, :]
    nv_ref[:, WINDOW - 1:, :] = vn[:, None, :]


def _swa_dec(q8, kvn, ck, cv, sink_col):
    b = q8.shape[0]
    bb = BB_ATT
    cache = pl.BlockSpec((bb, WINDOW, KV_WIDTH), lambda i: (i, 0, 0))
    return pl.pallas_call(
        _swa_dec_body,
        grid=(b // bb,),
        in_specs=[pl.BlockSpec((bb, ATT_HEADS // 2, LANES), lambda i: (i, 0, 0)),
                  pl.BlockSpec((bb, 2 * KV_WIDTH), lambda i: (i, 0)),
                  cache, cache, _const_spec((ATT_HEADS, 1))],
        out_specs=[pl.BlockSpec((bb, ATT_HEADS // 2, LANES), lambda i: (i, 0, 0)), cache, cache],
        out_shape=(jax.ShapeDtypeStruct((b, ATT_HEADS // 2, LANES), BF16),
                   jax.ShapeDtypeStruct(ck.shape, F32),
                   jax.ShapeDtypeStruct(cv.shape, F32)),
        compiler_params=_params(),
        name="swa_dec",
    )(q8, kvn, ck, cv, sink_col)


def _ret_dec_body(qt_ref, kt_ref, rq_ref, rk_ref, rv_ref, rg_ref, g_ref, s_ref,
                  o_ref, ns_ref, o_scr):
    bb = s_ref.shape[0]
    for hh in range(RET_HEADS):
        g1 = math.exp(LOG_G[hh])
        cols = slice(RET_DK * hh, RET_DK * (hh + 1))
        for jb in range(bb):
            s0 = s_ref[jb, hh]
            qc = qt_ref[0, cols, jb:jb + 1] * g1
            kc = kt_ref[0, cols, jb:jb + 1]
            v = rv_ref[jb:jb + 1, cols].astype(F32)
            o_scr[jb:jb + 1, cols] = jnp.sum(qc * s0, axis=0, keepdims=True)
            ns_ref[jb, hh] = g1 * s0 + kc * v
    for hh in range(RET_HEADS):
        cols = slice(RET_DK * hh, RET_DK * (hh + 1))
        qk = jnp.sum(rq_ref[:, cols].astype(F32) * rk_ref[:, cols].astype(F32),
                     axis=-1, keepdims=True)
        o = o_scr[:, cols] + qk * rv_ref[:, cols].astype(F32)
        var = jnp.mean(o * o, axis=-1, keepdims=True)
        on = o * lax.rsqrt(var + EPS) * g_ref[:, cols]
        o_ref[:, cols] = (on * rg_ref[:, cols].astype(F32)).astype(BF16)


def _ret_dec(qt, kt, rq, rk, rv, rg, g, state):
    b = rv.shape[0]
    bb = BB_RET
    row = pl.BlockSpec((bb, RET_WIDTH), lambda i: (i, 0))
    col = pl.BlockSpec((1, RET_WIDTH, bb), lambda i: (i, 0, 0))
    st = pl.BlockSpec((bb, RET_HEADS, RET_DK, RET_DV), lambda i: (i, 0, 0, 0))
    return pl.pallas_call(
        _ret_dec_body,
        grid=(b // bb,),
        in_specs=[col, col, row, row, row, row, _const_spec((1, RET_WIDTH)), st],
        out_specs=[row, st],
        out_shape=(jax.ShapeDtypeStruct((b, RET_WIDTH), BF16),
                   jax.ShapeDtypeStruct(state.shape, F32)),
        scratch_shapes=[pltpu.VMEM((bb, RET_WIDTH), F32)],
        compiler_params=_params(),
        name="ret_dec",
    )(qt, kt, rq, rk, rv, rg, g, state)


def _out_proj_body(x_ref, a_ref, r_ref, w_ref, g_ref, x1_ref, h2_ref):
    x1 = (x_ref[...] + _dot(a_ref[...], w_ref[:ATT_WIDTH, :])
          + _dot(r_ref[...], w_ref[ATT_WIDTH:, :]))
    x1_ref[...] = x1
    var = jnp.mean(x1 * x1, axis=-1, keepdims=True)
    h2_ref[...] = (x1 * lax.rsqrt(var + EPS) * g_ref[...]).astype(BF16)


def _out_proj(x, a, r, w, g, tm):
    m = x.shape[0]
    row = lambda w: pl.BlockSpec((tm, w), lambda i: (i, 0))
    return pl.pallas_call(
        _out_proj_body,
        grid=(m // tm,),
        in_specs=[row(D_MODEL), row(ATT_WIDTH), row(RET_WIDTH),
                  _const_spec(w.shape), _const_spec((1, D_MODEL))],
        out_specs=[row(D_MODEL), row(D_MODEL)],
        out_shape=(jax.ShapeDtypeStruct((m, D_MODEL), F32),
                   jax.ShapeDtypeStruct((m, D_MODEL), BF16)),
        compiler_params=_params(),
        name="out_proj",
    )(x, a, r, w, g)


def _mlp_body(x1_ref, h2_ref, wu_ref, wd_ref, o_ref):
    f = pl.program_id(1)

    @pl.when(f == 0)
    def _():
        o_ref[...] = x1_ref[...]

    u = jnp.maximum(_dot(h2_ref[...], wu_ref[...]), 0.0)
    o_ref[...] += _dot((u * u).astype(BF16), wd_ref[...])


def _mlp(x1, h2, wu, wd, tm):
    m = x1.shape[0]
    return pl.pallas_call(
        _mlp_body,
        grid=(m // tm, D_FF // TF),
        in_specs=[pl.BlockSpec((tm, D_MODEL), lambda i, f: (i, 0)),
                  pl.BlockSpec((tm, D_MODEL), lambda i, f: (i, 0)),
                  pl.BlockSpec((D_MODEL, TF), lambda i, f: (0, f)),
                  pl.BlockSpec((TF, D_MODEL), lambda i, f: (f, 0))],
        out_specs=pl.BlockSpec((tm, D_MODEL), lambda i, f: (i, 0)),
        out_shape=jax.ShapeDtypeStruct((m, D_MODEL), F32),
        compiler_params=_params(2),
        name="mlp",
    )(x1, h2, wu, wd)


def kernel(x_prompt, x_sample, cache_k_win, cache_v_win, state_ret, ln1_g, w_in, q_norm_g,
           k_norm_g, attn_sinks, ret_norm_g, w_out, ln2_g, w_up, w_down):
    seq = x_prompt.shape[1]
    nb = x_sample.shape[0]
    assert x_prompt.shape[0] == 1 and x_sample.shape[1] == 1 and w_in.shape[0] == 1
    assert seq % TM == 0 and w_in.shape[2] == SPLIT_A + SPLIT_B

    w_a = w_in[0][:, :SPLIT_A].astype(BF16)
    w_b = w_in[0][:, SPLIT_A:].astype(BF16)
    inv = (ROPE_BASE ** (-jnp.arange(HALF, dtype=F32) / HALF)).reshape(1, HALF)

    ln1 = ln1_g.reshape(1, D_MODEL)
    ln2 = ln2_g.reshape(1, D_MODEL)
    qg = jnp.tile(q_norm_g.reshape(1, ATT_HEAD_DIM) * (ATT_HEAD_DIM ** -0.5), (1, 256 // ATT_HEAD_DIM))
    kg = jnp.tile(k_norm_g.reshape(1, ATT_HEAD_DIM), (1, KV_WIDTH // ATT_HEAD_DIM))
    rg_g = ret_norm_g.reshape(1, RET_WIDTH)
    sinks = attn_sinks.reshape(ATT_HEADS)
    blk = jnp.arange(256) // ATT_HEAD_DIM
    ones = (blk[:, None] == blk[None, :]).astype(BF16)

    xp = x_prompt[0]
    a_out, kv, rq, wo, wu = _proj_swa(xp, ln1, w_a, qg, kg, inv, ones, sinks, w_out[0], w_up[0])
    r_out, s_fin, wd = _proj_ret(xp, ln1, w_b, inv, rq, rg_g, w_down[0])
    x1, h2 = _out_proj(xp, a_out, r_out, wo, ln2, tm=TM)
    yp = _mlp(x1, h2, wu, wd, tm=TM)

    wb = min(WINDOW, seq)
    kp = kv[seq - wb:, :KV_WIDTH].reshape(1, 1, wb, ATT_KV_HEADS, ATT_HEAD_DIM)
    vp = kv[seq - wb:, KV_WIDTH:].reshape(1, 1, wb, ATT_KV_HEADS, ATT_HEAD_DIM)
    sp = s_fin.reshape(1, 1, RET_HEADS, RET_DK, RET_DV)

    xs = x_sample[:, 0]
    qs, kvs, rqs, rks, rvs, rgs = _in_proj(xs, ln1, w_a, w_b, qg, kg, inv, ones, pos=PAST_LEN)
    sink_col = jnp.concatenate([sinks[0::2], sinks[1::2]]).reshape(ATT_HEADS, 1)
    ck = cache_k_win[0].reshape(nb, WINDOW, KV_WIDTH)
    cv = cache_v_win[0].reshape(nb, WINDOW, KV_WIDTH)
    a8, nk, nv = _swa_dec(qs.reshape(nb, ATT_HEADS // 2, LANES), kvs, ck, cv, sink_col)
    a_s = a8.reshape(nb, ATT_WIDTH)

    to_cols = lambda a: a.astype(F32).reshape(nb // BB_RET, BB_RET, RET_WIDTH).transpose(0, 2, 1)
    r_s, ns = _ret_dec(to_cols(rqs), to_cols(rks), rqs, rks, rvs, rgs, rg_g, state_ret[0])
    x1s, h2s = _out_proj(xs, a_s, r_s, wo, ln2, tm=nb)
    ys = _mlp(x1s, h2s, wu, wd, tm=nb)

    shape_kv = (1, nb, WINDOW, ATT_KV_HEADS, ATT_HEAD_DIM)
    return (yp[None], ys[:, None, :], kp, vp, sp,
            nk.reshape(shape_kv), nv.reshape(shape_kv), ns[None])
```

```python
import functools
import math

import jax
import jax.numpy as jnp
from jax import lax
from jax.experimental import pallas as pl
from jax.experimental.pallas import tpu as pltpu

D_MODEL = 2048
ATT_HEADS = 16
ATT_KV_HEADS = 2
ATT_HEAD_DIM = 64
WINDOW = 128
RET_HEADS = 4
RET_DK = 256
RET_DV = 256
RET_CHUNK = 128
ROPE_BASE = 10000.0
D_FF = 4 * D_MODEL
EPS = 1e-6
PAST_LEN = 8192

ATT_WIDTH = ATT_HEADS * ATT_HEAD_DIM
KV_WIDTH = ATT_KV_HEADS * ATT_HEAD_DIM
RET_WIDTH = RET_HEADS * RET_DK
LANES = 128
HALF = RET_DK // 2
SPLIT_A = ATT_WIDTH + 2 * KV_WIDTH + RET_WIDTH
SPLIT_B = 3 * RET_WIDTH

F32 = jnp.float32
BF16 = jnp.bfloat16
VMEM_LIMIT = 60 * 1024 * 1024

TM = 512
TF = 2048
BB_ATT = 16
BB_RET = 8

LOG_G = tuple(math.log1p(-(2.0 ** (-5.0 - h))) for h in range(RET_HEADS))


def _dot(a, b):
    return jnp.dot(a, b, preferred_element_type=F32)


def _dot_nt(a, b):
    return lax.dot_general(a, b, (((1,), (1,)), ((), ())), preferred_element_type=F32)


def _dot_tn(a, b):
    return lax.dot_general(a, b, (((0,), (0,)), ((), ())), preferred_element_type=F32)


def _const_spec(shape):
    n = len(shape)
    return pl.BlockSpec(shape, lambda *_: (0,) * n, pipeline_mode=pl.Buffered(1))


def _w_in_spec(part):
    width, start = ((SPLIT_A, 0), (SPLIT_B, SPLIT_A))[part]
    return pl.BlockSpec((pl.Element(D_MODEL), pl.Element(width)), lambda *_: (0, start),
                        pipeline_mode=pl.Buffered(1))


def _params(n_axes=1):
    return pltpu.CompilerParams(dimension_semantics=("arbitrary",) * n_axes,
                                vmem_limit_bytes=VMEM_LIMIT)


def _norm_rows(x_ref, g_ref):
    x = x_ref[...]
    var = jnp.mean(x * x, axis=-1, keepdims=True)
    return (x * lax.rsqrt(var + EPS) * g_ref[...]).astype(BF16)


def _attn_proj(h, w_ref, qg_ref, kg_ref, ones_ref):
    ones = ones_ref[...]
    inv_hd = 1.0 / ATT_HEAD_DIM
    aq = _dot(h, w_ref[:, :ATT_WIDTH])
    q_tiles = []
    for t in range(ATT_WIDTH // 256):
        blk = aq[:, 256 * t:256 * (t + 1)]
        ssq = _dot((blk * blk).astype(BF16), ones)
        q_tiles.append((blk * lax.rsqrt(ssq * inv_hd + EPS) * qg_ref[...]).astype(BF16))
    kvr = _dot(h, w_ref[:, ATT_WIDTH:ATT_WIDTH + 2 * KV_WIDTH])
    k = kvr[:, :KV_WIDTH]
    ssq = _dot((k * k).astype(BF16), ones[:KV_WIDTH, :KV_WIDTH])
    kn = k * lax.rsqrt(ssq * inv_hd + EPS) * kg_ref[...]
    return q_tiles, kn, kvr[:, KV_WIDTH:]


def _rope_init(inv_ref, cr_scr, sr_scr, pos_step):
    tm = cr_scr.shape[0]
    row = lax.broadcasted_iota(jnp.int32, (tm, 1), 0)
    ang_r = (pos_step * row).astype(F32) * inv_ref[...]
    cr_scr[...] = jnp.cos(ang_r)
    sr_scr[...] = jnp.sin(ang_r)


def _rope_tables(inv_ref, cr_scr, sr_scr, base_pos):
    base = jnp.zeros((8, 1), jnp.int32) + base_pos
    ang_b = base.astype(F32) * inv_ref[...]
    cb = jnp.cos(ang_b)[:1]
    sb = jnp.sin(ang_b)[:1]
    cr = cr_scr[...]
    sr = sr_scr[...]
    return cb * cr - sb * sr, sb * cr + cb * sr


def _rope_head(r, cos, sin, scale, o_ref, hh):
    x1 = r[:, :HALF]
    x2 = r[:, HALF:]
    o1 = x1 * cos - x2 * sin
    o2 = x2 * cos + x1 * sin
    if scale != 1.0:
        o1 = o1 * scale
        o2 = o2 * scale
    o_ref[:, RET_DK * hh:RET_DK * hh + HALF] = o1.astype(BF16)
    o_ref[:, RET_DK * hh + HALF:RET_DK * (hh + 1)] = o2.astype(BF16)


def _rope_proj(h, w, cos, sin, scale, o_ref):
    r = _dot(h, w)
    for hh in range(RET_HEADS):
        _rope_head(r[:, RET_DK * hh:RET_DK * (hh + 1)], cos, sin, scale, o_ref, hh)


def _gate_proj(h, w, o_ref):
    rg = _dot(h, w)
    o_ref[...] = (rg / (1.0 + jnp.exp(-rg))).astype(BF16)


def _half_split(a):
    lane = lax.broadcasted_iota(jnp.int32, a.shape, 1)
    lo = lane < ATT_HEAD_DIM
    sw = pltpu.roll(a, ATT_HEAD_DIM, 1)
    zero = jnp.zeros_like(a)
    h0 = (jnp.where(lo, a, zero).astype(BF16), jnp.where(lo, zero, sw).astype(BF16))
    h1 = (jnp.where(lo, sw, zero).astype(BF16), jnp.where(lo, zero, a).astype(BF16))
    return h0, h1


SWA_GROUP = 4


def _swa_phases(sink_ref, q_ref, kvc_ref, kvp_ref, o_ref, has_prev):
    nsub = q_ref.shape[0] // WINDOW
    npair = ATT_HEADS // 2
    row = lax.broadcasted_iota(jnp.int32, (WINDOW, 2 * WINDOW), 0)
    col = lax.broadcasted_iota(jnp.int32, (WINDOW, 2 * WINDOW), 1)
    band_cur = (col >= WINDOW) & (col - WINDOW <= row)
    first_col = jnp.where(has_prev, 0, WINDOW)
    lane_lo = lax.broadcasted_iota(jnp.int32, (WINDOW, LANES), 1) < ATT_HEAD_DIM
    ctx = [{} for _ in range(nsub)]
    st = {}

    def prep(g):
        j, first = divmod(g * SWA_GROUP, npair)
        if first:
            return
        prev = kvp_ref[...] if j == 0 else kvc_ref[WINDOW * (j - 1):WINDOW * j, :]
        cur = kvc_ref[WINDOW * j:WINDOW * (j + 1), :]
        kk = jnp.concatenate([prev[:, :KV_WIDTH], cur[:, :KV_WIDTH]], axis=0)
        vv = jnp.concatenate([prev[:, KV_WIDTH:], cur[:, KV_WIDTH:]], axis=0)
        ctx[j]["k"] = _half_split(kk)
        ctx[j]["v"] = _half_split(vv)
        lo_col = first_col if j == 0 else 0
        ctx[j]["mask"] = band_cur | ((col < WINDOW) & (col >= row) & (col >= lo_col))

    def units(g):
        for u in range(g * SWA_GROUP, (g + 1) * SWA_GROUP):
            j, t = divmod(u, npair)
            yield j, t, t // (npair // ATT_KV_HEADS), slice(WINDOW * j, WINDOW * (j + 1))

    def qk(g):
        us = list(units(g))
        j, _, kh, rows = us[0]
        qs = jnp.concatenate([q_ref[rows, LANES * t:LANES * (t + 1)] for _, t, _, _ in us], axis=0)
        s = [_dot_nt(qs, ctx[j]["k"][kh][par]) for par in range(2)]
        for n, (_, t, _, _) in enumerate(us):
            st[j, t] = {"s": [sp[WINDOW * n:WINDOW * (n + 1), :] for sp in s]}

    def softmax(g):
        for j, t, kh, rows in units(g):
            p, inv = [], []
            for par in range(2):
                sink = sink_ref[2 * t + par]
                s = jnp.where(ctx[j]["mask"], st[j, t]["s"][par], -jnp.inf)
                mx = jnp.maximum(jnp.max(s, axis=-1, keepdims=True), sink)
                e = jnp.exp(s - mx)
                den = jnp.sum(e, axis=-1, keepdims=True) + jnp.exp(sink - mx)
                p.append(e.astype(BF16))
                inv.append(1.0 / den)
            st[j, t] = {"p": p, "inv": jnp.where(lane_lo, inv[0], inv[1])}

    def pv(g):
        us = list(units(g))
        j, _, kh, rows = us[0]
        got = [st.pop((j, t)) for _, t, _, _ in us]
        acc = sum(_dot(jnp.concatenate([u["p"][par] for u in got], axis=0), ctx[j]["v"][kh][par])
                  for par in range(2))
        for n, (_, t, _, _) in enumerate(us):
            o_ref[rows, LANES * t:LANES * (t + 1)] = (
                acc[WINDOW * n:WINDOW * (n + 1), :] * got[n]["inv"]).astype(BF16)

    ngroup = nsub * npair // SWA_GROUP
    return ngroup, prep, qk, softmax, pv


def _emit_pipelined(main, ngroup, prep, first, middle, last, finish=None):
    nstage = ngroup + 2
    done = 0
    prep(0)
    for k in range(nstage):
        if k < ngroup:
            first(k)
        if 0 <= k - 2 < ngroup:
            last(k - 2)
        if 0 <= k - 1 < ngroup:
            middle(k - 1)
        if k + 1 < ngroup:
            prep(k + 1)
        upto = -(-len(main) * (k + 1) // nstage)
        for piece in main[done:upto]:
            piece()
        done = upto
    if finish is not None:
        finish()


def _ret_phases(rq_ref, rk_ref, rv_ref, gt_ref, g_ref, o_ref, s_scr):
    c = RET_CHUNK
    nsub = rq_ref.shape[0] // c
    ri = lax.broadcasted_iota(jnp.int32, (c, c), 0)
    ci = lax.broadcasted_iota(jnp.int32, (c, c), 1)
    rel = (ri - ci).astype(F32)
    idx = lax.broadcasted_iota(jnp.int32, (c, 1), 0).astype(F32)
    head = []
    for hh in range(RET_HEADS):
        lg = LOG_G[hh]
        head.append(dict(
            dmask=jnp.where(rel >= 0, jnp.exp(lg * jnp.maximum(rel, 0.0)), 0.0),
            qdec=jnp.exp(lg * (idx + 1.0)), kdec=jnp.exp(lg * (c - 1.0 - idx)),
            cdec=math.exp(lg * c), cols=slice(RET_DK * hh, RET_DK * (hh + 1))))
    st = {}

    def update_state(g):
        for hh, hd in enumerate(head):
            u = st[g, hh]
            s_scr[hh] = hd["cdec"] * u.pop("s_prev") + u.pop("kv")

    def prep(g):
        if g > 0:
            update_state(g - 1)
        rows = slice(c * g, c * (g + 1))
        for hh, hd in enumerate(head):
            q = rq_ref[rows, hd["cols"]]
            k = rk_ref[rows, hd["cols"]]
            s_prev = s_scr[hh]
            st[g, hh] = dict(q=q, k=k, v=rv_ref[rows, hd["cols"]], s_prev=s_prev,
                             qd=(q.astype(F32) * hd["qdec"]).astype(BF16),
                             kd=(k.astype(F32) * hd["kdec"]).astype(BF16),
                             s_bf=s_prev.astype(BF16))

    def first(g):
        for hh in range(RET_HEADS):
            u = st[g, hh]
            u["att"] = _dot_nt(u.pop("q"), u.pop("k"))
            u["inter"] = _dot(u.pop("qd"), u.pop("s_bf"))
            u["kv"] = _dot_tn(u.pop("kd"), u["v"])

    def middle(g):
        for hh, hd in enumerate(head):
            u = st[g, hh]
            u["att"] = (u["att"] * hd["dmask"]).astype(BF16)

    def last(g):
        rows = slice(c * g, c * (g + 1))
        for hh, hd in enumerate(head):
            u = st[g, hh]
            o = _dot(u.pop("att"), u.pop("v")) + u.pop("inter")
            var = jnp.mean(o * o, axis=-1, keepdims=True)
            on = o * lax.rsqrt(var + EPS) * g_ref[:, hd["cols"]]
            o_ref[rows, hd["cols"]] = (on * gt_ref[rows, hd["cols"]].astype(F32)).astype(BF16)

    return nsub, prep, first, middle, last, functools.partial(update_state, nsub - 1)


def _proj_swa_body(sink_ref, x_ref, g_ref, w_ref, qg_ref, kg_ref, inv_ref, ones_ref,
                   wo_ref, wu_ref,
                   a_out, kv_out, rq_out, wo_bf, wu_bf,
                   q_scr, kv_scr, kvp_scr, cr_scr, sr_scr, *, nblk):
    i = pl.program_id(0)
    tm = x_ref.shape[0]

    @pl.when(i == 0)
    def _():
        _rope_init(inv_ref, cr_scr, sr_scr, 1)
        q_scr[...] = jnp.zeros_like(q_scr)
        kv_scr[...] = jnp.zeros_like(kv_scr)
        kvp_scr[...] = jnp.zeros_like(kvp_scr)

    wo_bf[...] = wo_ref[...].astype(BF16)
    wu_bf[...] = wu_ref[...].astype(BF16)

    cur = lax.rem(i, 2)
    prv = 1 - cur
    blk = jnp.minimum(i, nblk - 1)

    h = _norm_rows(x_ref, g_ref)
    ones = ones_ref[...]
    inv_hd = 1.0 / ATT_HEAD_DIM
    cos, sin = _rope_tables(inv_ref, cr_scr, sr_scr, blk * tm)


    def q_piece(n):
        y = _dot(h, w_ref[:, 512 * n:512 * (n + 1)])
        for t in range(2):
            yt = y[:, 256 * t:256 * (t + 1)]
            ssq = _dot((yt * yt).astype(BF16), ones)
            c0 = 512 * n + 256 * t
            q_scr[cur, :, c0:c0 + 256] = (
                yt * lax.rsqrt(ssq * inv_hd + EPS) * qg_ref[...]).astype(BF16)

    kv0 = ATT_WIDTH
    rq0 = ATT_WIDTH + 2 * KV_WIDTH

    def kv_rq_piece():
        y = _dot(h, w_ref[:, kv0:rq0 + RET_DK])
        k = y[:, :KV_WIDTH]
        ssq = _dot((k * k).astype(BF16), ones[:KV_WIDTH, :KV_WIDTH])
        kn = k * lax.rsqrt(ssq * inv_hd + EPS) * kg_ref[...]
        v = y[:, KV_WIDTH:2 * KV_WIDTH]
        kv_scr[cur, :, :KV_WIDTH] = kn
        kv_scr[cur, :, KV_WIDTH:] = v
        kv_out[:, :KV_WIDTH] = kn
        kv_out[:, KV_WIDTH:] = v
        _rope_head(y[:, 2 * KV_WIDTH:], cos, sin, 1.0, rq_out, 0)

    def rq_piece():
        y = _dot(h, w_ref[:, rq0 + RET_DK:])
        for hh in range(1, RET_HEADS):
            _rope_head(y[:, RET_DK * (hh - 1):RET_DK * hh], cos, sin, 1.0, rq_out, hh)

    main = [functools.partial(q_piece, 0), functools.partial(q_piece, 1), kv_rq_piece, rq_piece]
    _emit_pipelined(main, *_swa_phases(sink_ref, q_scr.at[prv], kv_scr.at[prv], kvp_scr, a_out,
                                       has_prev=i > 1))
    kvp_scr[...] = kv_scr[prv, tm - WINDOW:, :]


def _proj_swa(x, ln_g, w_in, qg, kg, inv, ones, sinks, w_out, w_up):
    m = x.shape[0]
    nblk = m // TM
    cl = lambda i: jnp.minimum(i, nblk - 1)
    row = lambda w: pl.BlockSpec((TM, w), lambda i: (cl(i), 0))
    wo_spec = pl.BlockSpec((w_out.shape[0] // nblk, w_out.shape[1]), lambda i: (cl(i), 0))
    wu_spec = pl.BlockSpec((w_up.shape[0], w_up.shape[1] // nblk), lambda i: (0, cl(i)))
    return pl.pallas_call(
        functools.partial(_proj_swa_body, nblk=nblk),
        grid=(nblk + 1,),
        in_specs=[pl.BlockSpec(memory_space=pltpu.SMEM),
                  row(D_MODEL), _const_spec((1, D_MODEL)), _w_in_spec(0),
                  _const_spec((1, 256)), _const_spec((1, KV_WIDTH)),
                  _const_spec((1, HALF)), _const_spec((256, 256)), wo_spec, wu_spec],
        out_specs=[pl.BlockSpec((TM, ATT_WIDTH), lambda i: (jnp.maximum(i - 1, 0), 0)),
                   row(2 * KV_WIDTH), row(RET_WIDTH), wo_spec, wu_spec],
        out_shape=(jax.ShapeDtypeStruct((m, ATT_WIDTH), BF16),
                   jax.ShapeDtypeStruct((m, 2 * KV_WIDTH), F32),
                   jax.ShapeDtypeStruct((m, RET_WIDTH), BF16),
                   jax.ShapeDtypeStruct(w_out.shape, BF16),
                   jax.ShapeDtypeStruct(w_up.shape, BF16)),
        scratch_shapes=[pltpu.VMEM((2, TM, ATT_WIDTH), BF16),
                        pltpu.VMEM((2, TM, 2 * KV_WIDTH), F32),
                        pltpu.VMEM((WINDOW, 2 * KV_WIDTH), F32),
                        pltpu.VMEM((TM, HALF), F32), pltpu.VMEM((TM, HALF), F32)],
        compiler_params=_params(),
        name="proj_swa",
    )(sinks, x, ln_g, w_in, qg, kg, inv, ones, w_out, w_up)


def _proj_ret_body(x_ref, g_ref, w_ref, inv_ref, rq_ref, rg_ref, wd_ref,
                   r_out, s_out, wd_bf,
                   rk_scr, rv_scr, gt_scr, s_scr, cr_scr, sr_scr, *, nblk):
    i = pl.program_id(0)
    tm = x_ref.shape[0]

    @pl.when(i == 0)
    def _():
        _rope_init(inv_ref, cr_scr, sr_scr, 1)
        rk_scr[...] = jnp.zeros_like(rk_scr)
        rv_scr[...] = jnp.zeros_like(rv_scr)
        gt_scr[...] = jnp.zeros_like(gt_scr)
        s_scr[...] = jnp.zeros_like(s_scr)

    wd_bf[...] = wd_ref[...].astype(BF16)

    cur = lax.rem(i, 2)
    prv = 1 - cur
    blk = jnp.minimum(i, nblk - 1)

    h = _norm_rows(x_ref, g_ref)
    cos, sin = _rope_tables(inv_ref, cr_scr, sr_scr, blk * tm)

    def rk_piece(n):
        y = _dot(h, w_ref[:, 512 * n:512 * (n + 1)])
        for t in range(2):
            _rope_head(y[:, RET_DK * t:RET_DK * (t + 1)], cos, sin, RET_DK ** -0.5,
                       rk_scr.at[cur], 2 * n + t)

    def rv_piece(n):
        c = slice(512 * n, 512 * (n + 1))
        rv_scr[cur, :, c] = _dot(h, w_ref[:, RET_WIDTH + 512 * n:RET_WIDTH + 512 * (n + 1)]
                                 ).astype(BF16)

    def gate_piece(n):
        c0 = 2 * RET_WIDTH + 512 * n
        rg = _dot(h, w_ref[:, c0:c0 + 512])
        gt_scr[cur, :, 512 * n:512 * (n + 1)] = (rg / (1.0 + jnp.exp(-rg))).astype(BF16)

    main = [functools.partial(f, n) for f in (rk_piece, rv_piece, gate_piece) for n in range(2)]
    _emit_pipelined(main, *_ret_phases(rq_ref, rk_scr.at[prv], rv_scr.at[prv], gt_scr.at[prv],
                                       rg_ref, r_out, s_scr))

    @pl.when(i == nblk)
    def _():
        s_out[...] = s_scr[...]


def _proj_ret(x, ln_g, w_in, inv, rq, rg_g, w_down):
    m = x.shape[0]
    nblk = m // TM
    cl = lambda i: jnp.minimum(i, nblk - 1)
    prev = lambda w: pl.BlockSpec((TM, w), lambda i: (jnp.maximum(i - 1, 0), 0))
    wd_spec = pl.BlockSpec((w_down.shape[0] // nblk, w_down.shape[1]), lambda i: (cl(i), 0))
    state = (RET_HEADS, RET_DK, RET_DV)
    slot = pltpu.VMEM((2, TM, RET_WIDTH), BF16)
    return pl.pallas_call(
        functools.partial(_proj_ret_body, nblk=nblk),
        grid=(nblk + 1,),
        in_specs=[pl.BlockSpec((TM, D_MODEL), lambda i: (cl(i), 0)),
                  _const_spec((1, D_MODEL)), _w_in_spec(1), _const_spec((1, HALF)),
                  prev(RET_WIDTH), _const_spec((1, RET_WIDTH)), wd_spec],
        out_specs=[prev(RET_WIDTH), pl.BlockSpec(state, lambda i: (0, 0, 0)), wd_spec],
        out_shape=(jax.ShapeDtypeStruct((m, RET_WIDTH), BF16),
                   jax.ShapeDtypeStruct(state, F32),
                   jax.ShapeDtypeStruct(w_down.shape, BF16)),
        scratch_shapes=[slot, slot, slot, pltpu.VMEM(state, F32),
                        pltpu.VMEM((TM, HALF), F32), pltpu.VMEM((TM, HALF), F32)],
        compiler_params=_params(),
        name="proj_ret",
    )(x, ln_g, w_in, inv, rq, rg_g, w_down)


def _in_proj_body(x_ref, g_ref, wa_ref, wb_ref, qg_ref, kg_ref, inv_ref, ones_ref,
                  q_out, kv_out, rq_out, rk_out, rv_out, gt_out, cr_scr, sr_scr, *, pos):
    _rope_init(inv_ref, cr_scr, sr_scr, 0)
    h = _norm_rows(x_ref, g_ref)
    q_tiles, kn, v = _attn_proj(h, wa_ref, qg_ref, kg_ref, ones_ref)
    for t, qt in enumerate(q_tiles):
        q_out[:, 256 * t:256 * (t + 1)] = qt
    kv_out[:, :KV_WIDTH] = kn
    kv_out[:, KV_WIDTH:] = v
    cos, sin = _rope_tables(inv_ref, cr_scr, sr_scr, pos)
    _rope_proj(h, wa_ref[:, ATT_WIDTH + 2 * KV_WIDTH:], cos, sin, 1.0, rq_out)
    _rope_proj(h, wb_ref[:, :RET_WIDTH], cos, sin, RET_DK ** -0.5, rk_out)
    rv_out[...] = _dot(h, wb_ref[:, RET_WIDTH:2 * RET_WIDTH]).astype(BF16)
    _gate_proj(h, wb_ref[:, 2 * RET_WIDTH:], gt_out)


def _in_proj(x, ln_g, w_in, qg, kg, inv, ones, pos):
    m = x.shape[0]
    full = lambda w: pl.BlockSpec((m, w), lambda i: (0, 0))
    ret = jax.ShapeDtypeStruct((m, RET_WIDTH), BF16)
    return pl.pallas_call(
        functools.partial(_in_proj_body, pos=pos),
        grid=(1,),
        in_specs=[full(D_MODEL), _const_spec((1, D_MODEL)), _w_in_spec(0),
                  _w_in_spec(1), _const_spec((1, 256)), _const_spec((1, KV_WIDTH)),
                  _const_spec((1, HALF)), _const_spec((256, 256))],
        out_specs=[full(ATT_WIDTH), full(2 * KV_WIDTH)] + [full(RET_WIDTH)] * 4,
        out_shape=(jax.ShapeDtypeStruct((m, ATT_WIDTH), BF16),
                   jax.ShapeDtypeStruct((m, 2 * KV_WIDTH), F32), ret, ret, ret, ret),
        scratch_shapes=[pltpu.VMEM((m, HALF), F32), pltpu.VMEM((m, HALF), F32)],
        compiler_params=_params(),
        name="in_proj",
    )(x, ln_g, w_in, w_in, qg, kg, inv, ones)


def _swa_dec_body(q_ref, kvn_ref, ck_ref, cv_ref, sink_ref, o_ref, nk_ref, nv_ref):
    npair = ATT_HEADS // 2
    q8 = q_ref[...].astype(F32)
    q8r = pltpu.roll(q8, ATT_HEAD_DIM, 2)
    lane = lax.broadcasted_iota(jnp.int32, q8.shape, 2)
    pair = lax.broadcasted_iota(jnp.int32, q8.shape, 1)
    lo = lane < ATT_HEAD_DIM
    kv0 = pair < npair // ATT_KV_HEADS
    own = lo == kv0
    zero = jnp.zeros_like(q8)
    qe = jnp.where(own, jnp.where(kv0, q8, q8r), zero)
    qo = jnp.where(own, jnp.where(kv0, q8r, q8), zero)
    qb = jnp.concatenate([qe, qo], axis=1)

    ck = ck_ref[...]
    cv = cv_ref[...]
    kn = kvn_ref[:, :KV_WIDTH]
    vn = kvn_ref[:, KV_WIDTH:]
    s = lax.dot_general(qb.astype(BF16), ck.astype(BF16), (((2,), (2,)), ((0,), (0,))),
                        preferred_element_type=F32)
    s_new = jnp.sum(qb * kn[:, None, :], axis=-1, keepdims=True)
    sink = sink_ref[...][None, :, :]
    mx = jnp.maximum(jnp.maximum(jnp.max(s, axis=-1, keepdims=True), s_new), sink)
    p = jnp.exp(s - mx)
    p_new = jnp.exp(s_new - mx)
    den = jnp.sum(p, axis=-1, keepdims=True) + p_new + jnp.exp(sink - mx)
    o = lax.dot_general(p.astype(BF16), cv.astype(BF16), (((2,), (1,)), ((0,), (0,))),
                        preferred_element_type=F32)
    o = (o + p_new * vn[:, None, :]) / den
    oe = o[:, :npair, :]
    oo = o[:, npair:, :]
    oer = pltpu.roll(oe, ATT_HEAD_DIM, 2)
    oor = pltpu.roll(oo, ATT_HEAD_DIM, 2)
    o_ref[...] = jnp.where(lo, jnp.where(kv0, oe, oer), jnp.where(kv0, oor, oo)).astype(BF16)

    nk_ref[:, :WINDOW - 1, :] = ck[:, 1:, :]
    nk_ref[:, WINDOW - 1:, :] = kn[:, None, :]
    nv_ref[:, :WINDOW - 1, :] = cv[:, 1:, :]
    nv_ref[:, WINDOW - 1:, :] = vn[:, None, :]


def _swa_dec(q8, kvn, ck, cv, sink_col):
    b = q8.shape[0]
    bb = BB_ATT
    cache = pl.BlockSpec((bb, WINDOW, KV_WIDTH), lambda i: (i, 0, 0))
    return pl.pallas_call(
        _swa_dec_body,
        grid=(b // bb,),
        in_specs=[pl.BlockSpec((bb, ATT_HEADS // 2, LANES), lambda i: (i, 0, 0)),
                  pl.BlockSpec((bb, 2 * KV_WIDTH), lambda i: (i, 0)),
                  cache, cache, _const_spec((ATT_HEADS, 1))],
        out_specs=[pl.BlockSpec((bb, ATT_HEADS // 2, LANES), lambda i: (i, 0, 0)), cache, cache],
        out_shape=(jax.ShapeDtypeStruct((b, ATT_HEADS // 2, LANES), BF16),
                   jax.ShapeDtypeStruct(ck.shape, F32),
                   jax.ShapeDtypeStruct(cv.shape, F32)),
        compiler_params=_params(),
        name="swa_dec",
    )(q8, kvn, ck, cv, sink_col)


def _ret_dec_body(qt_ref, kt_ref, rq_ref, rk_ref, rv_ref, rg_ref, g_ref, s_ref,
                  o_ref, ns_ref, o_scr):
    bb = s_ref.shape[0]
    for hh in range(RET_HEADS):
        g1 = math.exp(LOG_G[hh])
        cols = slice(RET_DK * hh, RET_DK * (hh + 1))
        for jb in range(bb):
            s0 = s_ref[jb, hh]
            qc = qt_ref[0, cols, jb:jb + 1] * g1
            kc = kt_ref[0, cols, jb:jb + 1]
            v = rv_ref[jb:jb + 1, cols].astype(F32)
            o_scr[jb:jb + 1, cols] = jnp.sum(qc * s0, axis=0, keepdims=True)
            ns_ref[jb, hh] = g1 * s0 + kc * v
    for hh in range(RET_HEADS):
        cols = slice(RET_DK * hh, RET_DK * (hh + 1))
        qk = jnp.sum(rq_ref[:, cols].astype(F32) * rk_ref[:, cols].astype(F32),
                     axis=-1, keepdims=True)
        o = o_scr[:, cols] + qk * rv_ref[:, cols].astype(F32)
        var = jnp.mean(o * o, axis=-1, keepdims=True)
        on = o * lax.rsqrt(var + EPS) * g_ref[:, cols]
        o_ref[:, cols] = (on * rg_ref[:, cols].astype(F32)).astype(BF16)


def _ret_dec(qt, kt, rq, rk, rv, rg, g, state):
    b = rv.shape[0]
    bb = BB_RET
    row = pl.BlockSpec((bb, RET_WIDTH), lambda i: (i, 0))
    col = pl.BlockSpec((1, RET_WIDTH, bb), lambda i: (i, 0, 0))
    st = pl.BlockSpec((bb, RET_HEADS, RET_DK, RET_DV), lambda i: (i, 0, 0, 0))
    return pl.pallas_call(
        _ret_dec_body,
        grid=(b // bb,),
        in_specs=[col, col, row, row, row, row, _const_spec((1, RET_WIDTH)), st],
        out_specs=[row, st],
        out_shape=(jax.ShapeDtypeStruct((b, RET_WIDTH), BF16),
                   jax.ShapeDtypeStruct(state.shape, F32)),
        scratch_shapes=[pltpu.VMEM((bb, RET_WIDTH), F32)],
        compiler_params=_params(),
        name="ret_dec",
    )(qt, kt, rq, rk, rv, rg, g, state)


def _out_proj_body(x_ref, a_ref, r_ref, w_ref, g_ref, x1_ref, h2_ref):
    x1 = (x_ref[...] + _dot(a_ref[...], w_ref[:ATT_WIDTH, :])
          + _dot(r_ref[...], w_ref[ATT_WIDTH:, :]))
    x1_ref[...] = x1
    var = jnp.mean(x1 * x1, axis=-1, keepdims=True)
    h2_ref[...] = (x1 * lax.rsqrt(var + EPS) * g_ref[...]).astype(BF16)


def _out_proj(x, a, r, w, g, tm):
    m = x.shape[0]
    row = lambda w: pl.BlockSpec((tm, w), lambda i: (i, 0))
    return pl.pallas_call(
        _out_proj_body,
        grid=(m // tm,),
        in_specs=[row(D_MODEL), row(ATT_WIDTH), row(RET_WIDTH),
                  _const_spec(w.shape), _const_spec((1, D_MODEL))],
        out_specs=[row(D_MODEL), row(D_MODEL)],
        out_shape=(jax.ShapeDtypeStruct((m, D_MODEL), F32),
                   jax.ShapeDtypeStruct((m, D_MODEL), BF16)),
        compiler_params=_params(),
        name="out_proj",
    )(x, a, r, w, g)


def _mlp_body(x1_ref, h2_ref, wu_ref, wd_ref, o_ref):
    f = pl.program_id(1)

    @pl.when(f == 0)
    def _():
        o_ref[...] = x1_ref[...]

    u = jnp.maximum(_dot(h2_ref[...], wu_ref[...]), 0.0)
    o_ref[...] += _dot((u * u).astype(BF16), wd_ref[...])


def _mlp(x1, h2, wu, wd, tm):
    m = x1.shape[0]
    return pl.pallas_call(
        _mlp_body,
        grid=(m // tm, D_FF // TF),
        in_specs=[pl.BlockSpec((tm, D_MODEL), lambda i, f: (i, 0)),
                  pl.BlockSpec((tm, D_MODEL), lambda i, f: (i, 0)),
                  pl.BlockSpec((D_MODEL, TF), lambda i, f: (0, f)),
                  pl.BlockSpec((TF, D_MODEL), lambda i, f: (f, 0))],
        out_specs=pl.BlockSpec((tm, D_MODEL), lambda i, f: (i, 0)),
        out_shape=jax.ShapeDtypeStruct((m, D_MODEL), F32),
        compiler_params=_params(2),
        name="mlp",
    )(x1, h2, wu, wd)


def kernel(x_prompt, x_sample, cache_k_win, cache_v_win, state_ret, ln1_g, w_in, q_norm_g,
           k_norm_g, attn_sinks, ret_norm_g, w_out, ln2_g, w_up, w_down):
    seq = x_prompt.shape[1]
    nb = x_sample.shape[0]
    assert x_prompt.shape[0] == 1 and x_sample.shape[1] == 1 and w_in.shape[0] == 1
    assert seq % TM == 0 and w_in.shape[2] == SPLIT_A + SPLIT_B

    wi = w_in[0].astype(BF16)
    inv = (ROPE_BASE ** (-jnp.arange(HALF, dtype=F32) / HALF)).reshape(1, HALF)

    ln1 = ln1_g.reshape(1, D_MODEL)
    ln2 = ln2_g.reshape(1, D_MODEL)
    qg = jnp.tile(q_norm_g.reshape(1, ATT_HEAD_DIM) * (ATT_HEAD_DIM ** -0.5), (1, 256 // ATT_HEAD_DIM))
    kg = jnp.tile(k_norm_g.reshape(1, ATT_HEAD_DIM), (1, KV_WIDTH // ATT_HEAD_DIM))
    rg_g = ret_norm_g.reshape(1, RET_WIDTH)
    sinks = attn_sinks.reshape(ATT_HEADS)
    blk = jnp.arange(256) // ATT_HEAD_DIM
    ones = (blk[:, None] == blk[None, :]).astype(BF16)

    xp = x_prompt[0]
    a_out, kv, rq, wo, wu = _proj_swa(xp, ln1, wi, qg, kg, inv, ones, sinks, w_out[0], w_up[0])
    r_out, s_fin, wd = _proj_ret(xp, ln1, wi, inv, rq, rg_g, w_down[0])
    x1, h2 = _out_proj(xp, a_out, r_out, wo, ln2, tm=TM)
    yp = _mlp(x1, h2, wu, wd, tm=TM)

    wb = min(WINDOW, seq)
    kp = kv[seq - wb:, :KV_WIDTH].reshape(1, 1, wb, ATT_KV_HEADS, ATT_HEAD_DIM)
    vp = kv[seq - wb:, KV_WIDTH:].reshape(1, 1, wb, ATT_KV_HEADS, ATT_HEAD_DIM)
    sp = s_fin.reshape(1, 1, RET_HEADS, RET_DK, RET_DV)

    xs = x_sample[:, 0]
    qs, kvs, rqs, rks, rvs, rgs = _in_proj(xs, ln1, wi, qg, kg, inv, ones, pos=PAST_LEN)
    sink_col = jnp.concatenate([sinks[0::2], sinks[1::2]]).reshape(ATT_HEADS, 1)
    ck = cache_k_win[0].reshape(nb, WINDOW, KV_WIDTH)
    cv = cache_v_win[0].reshape(nb, WINDOW, KV_WIDTH)
    a8, nk, nv = _swa_dec(qs.reshape(nb, ATT_HEADS // 2, LANES), kvs, ck, cv, sink_col)
    a_s = a8.reshape(nb, ATT_WIDTH)

    to_cols = lambda a: a.astype(F32).reshape(nb // BB_RET, BB_RET, RET_WIDTH).transpose(0, 2, 1)
    r_s, ns = _ret_dec(to_cols(rqs), to_cols(rks), rqs, rks, rvs, rgs, rg_g, state_ret[0])
    x1s, h2s = _out_proj(xs, a_s, r_s, wo, ln2, tm=nb)
    ys = _mlp(x1s, h2s, wu, wd, tm=nb)

    shape_kv = (1, nb, WINDOW, ATT_KV_HEADS, ATT_HEAD_DIM)
    return (yp[None], ys[:, None, :], kp, vp, sp,
            nk.reshape(shape_kv), nv.reshape(shape_kv), ns[None])
```

```python
import functools
import math

import jax
import jax.numpy as jnp
from jax import lax
from jax.experimental import pallas as pl
from jax.experimental.pallas import tpu as pltpu

D_MODEL = 2048
ATT_HEADS = 16
ATT_KV_HEADS = 2
ATT_HEAD_DIM = 64
WINDOW = 128
RET_HEADS = 4
RET_DK = 256
RET_DV = 256
RET_CHUNK = 128
ROPE_BASE = 10000.0
D_FF = 4 * D_MODEL
EPS = 1e-6
PAST_LEN = 8192

ATT_WIDTH = ATT_HEADS * ATT_HEAD_DIM
KV_WIDTH = ATT_KV_HEADS * ATT_HEAD_DIM
RET_WIDTH = RET_HEADS * RET_DK
LANES = 128
HALF = RET_DK // 2
SPLIT_A = ATT_WIDTH + 2 * KV_WIDTH + RET_WIDTH
SPLIT_B = 3 * RET_WIDTH

F32 = jnp.float32
BF16 = jnp.bfloat16
VMEM_LIMIT = 60 * 1024 * 1024

TM = 512
TF = 2048
BB_ATT = 16
BB_RET = 8

LOG_G = tuple(math.log1p(-(2.0 ** (-5.0 - h))) for h in range(RET_HEADS))


def _dot(a, b):
    return jnp.dot(a, b, preferred_element_type=F32)


def _dot_nt(a, b):
    return lax.dot_general(a, b, (((1,), (1,)), ((), ())), preferred_element_type=F32)


def _dot_tn(a, b):
    return lax.dot_general(a, b, (((0,), (0,)), ((), ())), preferred_element_type=F32)


def _const_spec(shape):
    n = len(shape)
    return pl.BlockSpec(shape, lambda *_: (0,) * n, pipeline_mode=pl.Buffered(1))


def _w_in_spec(part):
    width, start = ((SPLIT_A, 0), (SPLIT_B, SPLIT_A))[part]
    return pl.BlockSpec((pl.Element(D_MODEL), pl.Element(width)), lambda *_: (0, start),
                        pipeline_mode=pl.Buffered(1))


def _params(n_axes=1):
    return pltpu.CompilerParams(dimension_semantics=("arbitrary",) * n_axes,
                                vmem_limit_bytes=VMEM_LIMIT)


def _norm_rows(x_ref, g_ref):
    x = x_ref[...]
    var = jnp.mean(x * x, axis=-1, keepdims=True)
    return (x * lax.rsqrt(var + EPS) * g_ref[...]).astype(BF16)


def _attn_proj(h, w_ref, qg_ref, kg_ref, ones_ref):
    ones = ones_ref[...]
    inv_hd = 1.0 / ATT_HEAD_DIM
    aq = _dot(h, w_ref[:, :ATT_WIDTH])
    q_tiles = []
    for t in range(ATT_WIDTH // 256):
        blk = aq[:, 256 * t:256 * (t + 1)]
        ssq = _dot((blk * blk).astype(BF16), ones)
        q_tiles.append((blk * lax.rsqrt(ssq * inv_hd + EPS) * qg_ref[...]).astype(BF16))
    kvr = _dot(h, w_ref[:, ATT_WIDTH:ATT_WIDTH + 2 * KV_WIDTH])
    k = kvr[:, :KV_WIDTH]
    ssq = _dot((k * k).astype(BF16), ones[:KV_WIDTH, :KV_WIDTH])
    kn = k * lax.rsqrt(ssq * inv_hd + EPS) * kg_ref[...]
    return q_tiles, kn, kvr[:, KV_WIDTH:]


def _rope_init(inv_ref, cr_scr, sr_scr, pos_step):
    tm = cr_scr.shape[0]
    row = lax.broadcasted_iota(jnp.int32, (tm, 1), 0)
    ang_r = (pos_step * row).astype(F32) * inv_ref[...]
    cr_scr[...] = jnp.cos(ang_r)
    sr_scr[...] = jnp.sin(ang_r)


def _rope_tables(inv_ref, cr_scr, sr_scr, base_pos):
    base = jnp.zeros((8, 1), jnp.int32) + base_pos
    ang_b = base.astype(F32) * inv_ref[...]
    cb = jnp.cos(ang_b)[:1]
    sb = jnp.sin(ang_b)[:1]
    cr = cr_scr[...]
    sr = sr_scr[...]
    return cb * cr - sb * sr, sb * cr + cb * sr


def _rope_head(r, cos, sin, scale, o_ref, hh):
    x1 = r[:, :HALF]
    x2 = r[:, HALF:]
    o1 = x1 * cos - x2 * sin
    o2 = x2 * cos + x1 * sin
    if scale != 1.0:
        o1 = o1 * scale
        o2 = o2 * scale
    o_ref[:, RET_DK * hh:RET_DK * hh + HALF] = o1.astype(BF16)
    o_ref[:, RET_DK * hh + HALF:RET_DK * (hh + 1)] = o2.astype(BF16)


def _rope_proj(h, w, cos, sin, scale, o_ref):
    r = _dot(h, w)
    for hh in range(RET_HEADS):
        _rope_head(r[:, RET_DK * hh:RET_DK * (hh + 1)], cos, sin, scale, o_ref, hh)


def _gate_proj(h, w, o_ref):
    rg = _dot(h, w)
    o_ref[...] = (rg / (1.0 + jnp.exp(-rg))).astype(BF16)


def _half_split(a):
    lane = lax.broadcasted_iota(jnp.int32, a.shape, 1)
    lo = lane < ATT_HEAD_DIM
    sw = pltpu.roll(a, ATT_HEAD_DIM, 1)
    zero = jnp.zeros_like(a)
    h0 = (jnp.where(lo, a, zero).astype(BF16), jnp.where(lo, zero, sw).astype(BF16))
    h1 = (jnp.where(lo, sw, zero).astype(BF16), jnp.where(lo, zero, a).astype(BF16))
    return h0, h1


SWA_GROUP = 4


def _swa_phases(sink_ref, q_ref, kvc_ref, kvp_ref, o_ref, has_prev):
    nsub = q_ref.shape[0] // WINDOW
    npair = ATT_HEADS // 2
    row = lax.broadcasted_iota(jnp.int32, (WINDOW, 2 * WINDOW), 0)
    col = lax.broadcasted_iota(jnp.int32, (WINDOW, 2 * WINDOW), 1)
    band_cur = (col >= WINDOW) & (col - WINDOW <= row)
    first_col = jnp.where(has_prev, 0, WINDOW)
    lane_lo = lax.broadcasted_iota(jnp.int32, (WINDOW, LANES), 1) < ATT_HEAD_DIM
    ctx = [{} for _ in range(nsub)]
    st = {}

    def prep(g):
        j, first = divmod(g * SWA_GROUP, npair)
        if first:
            return
        prev = kvp_ref[...] if j == 0 else kvc_ref[WINDOW * (j - 1):WINDOW * j, :]
        cur = kvc_ref[WINDOW * j:WINDOW * (j + 1), :]
        kk = jnp.concatenate([prev[:, :KV_WIDTH], cur[:, :KV_WIDTH]], axis=0)
        vv = jnp.concatenate([prev[:, KV_WIDTH:], cur[:, KV_WIDTH:]], axis=0)
        ctx[j]["k"] = _half_split(kk)
        ctx[j]["v"] = _half_split(vv)
        lo_col = first_col if j == 0 else 0
        ctx[j]["mask"] = band_cur | ((col < WINDOW) & (col >= row) & (col >= lo_col))

    def units(g):
        for u in range(g * SWA_GROUP, (g + 1) * SWA_GROUP):
            j, t = divmod(u, npair)
            yield j, t, t // (npair // ATT_KV_HEADS), slice(WINDOW * j, WINDOW * (j + 1))

    def qk(g):
        us = list(units(g))
        j, _, kh, rows = us[0]
        qs = jnp.concatenate([q_ref[rows, LANES * t:LANES * (t + 1)] for _, t, _, _ in us], axis=0)
        s = [_dot_nt(qs, ctx[j]["k"][kh][par]) for par in range(2)]
        for n, (_, t, _, _) in enumerate(us):
            st[j, t] = {"s": [sp[WINDOW * n:WINDOW * (n + 1), :] for sp in s]}

    def softmax(g):
        for j, t, kh, rows in units(g):
            p, inv = [], []
            for par in range(2):
                sink = sink_ref[2 * t + par]
                s = jnp.where(ctx[j]["mask"], st[j, t]["s"][par], -jnp.inf)
                mx = jnp.maximum(jnp.max(s, axis=-1, keepdims=True), sink)
                e = jnp.exp(s - mx)
                den = jnp.sum(e, axis=-1, keepdims=True) + jnp.exp(sink - mx)
                p.append(e.astype(BF16))
                inv.append(1.0 / den)
            st[j, t] = {"p": p, "inv": jnp.where(lane_lo, inv[0], inv[1])}

    def pv(g):
        us = list(units(g))
        j, _, kh, rows = us[0]
        got = [st.pop((j, t)) for _, t, _, _ in us]
        acc = sum(_dot(jnp.concatenate([u["p"][par] for u in got], axis=0), ctx[j]["v"][kh][par])
                  for par in range(2))
        for n, (_, t, _, _) in enumerate(us):
            o_ref[rows, LANES * t:LANES * (t + 1)] = (
                acc[WINDOW * n:WINDOW * (n + 1), :] * got[n]["inv"]).astype(BF16)

    ngroup = nsub * npair // SWA_GROUP
    return ngroup, prep, qk, softmax, pv


def _emit_pipelined(main, ngroup, prep, first, middle, last, finish=None):
    nstage = ngroup + 2
    done = 0
    prep(0)
    for k in range(nstage):
        if k < ngroup:
            first(k)
        if 0 <= k - 2 < ngroup:
            last(k - 2)
        if 0 <= k - 1 < ngroup:
            middle(k - 1)
        if k + 1 < ngroup:
            prep(k + 1)
        upto = -(-len(main) * (k + 1) // nstage)
        for piece in main[done:upto]:
            piece()
        done = upto
    if finish is not None:
        finish()


def _ret_phases(rq_ref, rk_ref, rv_ref, gt_ref, g_ref, o_ref, s_scr):
    c = RET_CHUNK
    nsub = rq_ref.shape[0] // c
    ri = lax.broadcasted_iota(jnp.int32, (c, c), 0)
    ci = lax.broadcasted_iota(jnp.int32, (c, c), 1)
    rel = (ri - ci).astype(F32)
    idx = lax.broadcasted_iota(jnp.int32, (c, 1), 0).astype(F32)
    head = []
    for hh in range(RET_HEADS):
        lg = LOG_G[hh]
        head.append(dict(
            dmask=jnp.where(rel >= 0, jnp.exp(lg * jnp.maximum(rel, 0.0)), 0.0),
            qdec=jnp.exp(lg * (idx + 1.0)), kdec=jnp.exp(lg * (c - 1.0 - idx)),
            cdec=math.exp(lg * c), cols=slice(RET_DK * hh, RET_DK * (hh + 1))))
    st = {}

    def update_state(g):
        for hh, hd in enumerate(head):
            u = st[g, hh]
            s_scr[hh] = hd["cdec"] * u.pop("s_prev") + u.pop("kv")

    def prep(g):
        if g > 0:
            update_state(g - 1)
        rows = slice(c * g, c * (g + 1))
        for hh, hd in enumerate(head):
            q = rq_ref[rows, hd["cols"]]
            k = rk_ref[rows, hd["cols"]]
            s_prev = s_scr[hh]
            st[g, hh] = dict(q=q, k=k, v=rv_ref[rows, hd["cols"]], s_prev=s_prev,
                             qd=(q.astype(F32) * hd["qdec"]).astype(BF16),
                             kd=(k.astype(F32) * hd["kdec"]).astype(BF16),
                             s_bf=s_prev.astype(BF16))

    def first(g):
        for hh in range(RET_HEADS):
            u = st[g, hh]
            u["att"] = _dot_nt(u.pop("q"), u.pop("k"))
            u["inter"] = _dot(u.pop("qd"), u.pop("s_bf"))
            u["kv"] = _dot_tn(u.pop("kd"), u["v"])

    def middle(g):
        for hh, hd in enumerate(head):
            u = st[g, hh]
            u["att"] = (u["att"] * hd["dmask"]).astype(BF16)

    def last(g):
        rows = slice(c * g, c * (g + 1))
        for hh, hd in enumerate(head):
            u = st[g, hh]
            o = _dot(u.pop("att"), u.pop("v")) + u.pop("inter")
            var = jnp.mean(o * o, axis=-1, keepdims=True)
            on = o * lax.rsqrt(var + EPS) * g_ref[:, hd["cols"]]
            o_ref[rows, hd["cols"]] = (on * gt_ref[rows, hd["cols"]].astype(F32)).astype(BF16)

    return nsub, prep, first, middle, last, functools.partial(update_state, nsub - 1)


def _proj_swa_body(sink_ref, x_ref, g_ref, w_ref, qg_ref, kg_ref, inv_ref, ones_ref,
                   wo_ref, wu_ref,
                   a_out, kv_out, rq_out, wo_bf, wu_bf,
                   q_scr, kv_scr, kvp_scr, cr_scr, sr_scr, *, nblk):
    i = pl.program_id(0)
    tm = x_ref.shape[0]

    @pl.when(i == 0)
    def _():
        _rope_init(inv_ref, cr_scr, sr_scr, 1)
        q_scr[...] = jnp.zeros_like(q_scr)
        kv_scr[...] = jnp.zeros_like(kv_scr)
        kvp_scr[...] = jnp.zeros_like(kvp_scr)

    wo_bf[...] = wo_ref[...].astype(BF16)
    wu_bf[...] = wu_ref[...].astype(BF16)

    cur = lax.rem(i, 2)
    prv = 1 - cur
    blk = jnp.minimum(i, nblk - 1)

    h = _norm_rows(x_ref, g_ref)
    ones = ones_ref[...]
    inv_hd = 1.0 / ATT_HEAD_DIM
    cos, sin = _rope_tables(inv_ref, cr_scr, sr_scr, blk * tm)


    def q_piece(n):
        y = _dot(h, w_ref[:, 512 * n:512 * (n + 1)])
        for t in range(2):
            yt = y[:, 256 * t:256 * (t + 1)]
            ssq = _dot((yt * yt).astype(BF16), ones)
            c0 = 512 * n + 256 * t
            q_scr[cur, :, c0:c0 + 256] = (
                yt * lax.rsqrt(ssq * inv_hd + EPS) * qg_ref[...]).astype(BF16)

    kv0 = ATT_WIDTH
    rq0 = ATT_WIDTH + 2 * KV_WIDTH

    def kv_rq_piece():
        y = _dot(h, w_ref[:, kv0:rq0 + RET_DK])
        k = y[:, :KV_WIDTH]
        ssq = _dot((k * k).astype(BF16), ones[:KV_WIDTH, :KV_WIDTH])
        kn = k * lax.rsqrt(ssq * inv_hd + EPS) * kg_ref[...]
        v = y[:, KV_WIDTH:2 * KV_WIDTH]
        kv_scr[cur, :, :KV_WIDTH] = kn
        kv_scr[cur, :, KV_WIDTH:] = v
        kv_out[:, :KV_WIDTH] = kn
        kv_out[:, KV_WIDTH:] = v
        _rope_head(y[:, 2 * KV_WIDTH:], cos, sin, 1.0, rq_out, 0)

    def rq_piece():
        y = _dot(h, w_ref[:, rq0 + RET_DK:])
        for hh in range(1, RET_HEADS):
            _rope_head(y[:, RET_DK * (hh - 1):RET_DK * hh], cos, sin, 1.0, rq_out, hh)

    main = [functools.partial(q_piece, 0), functools.partial(q_piece, 1), kv_rq_piece, rq_piece]
    _emit_pipelined(main, *_swa_phases(sink_ref, q_scr.at[prv], kv_scr.at[prv], kvp_scr, a_out,
                                       has_prev=i > 1))
    kvp_scr[...] = kv_scr[prv, tm - WINDOW:, :]


def _proj_swa(x, ln_g, w_in, qg, kg, inv, ones, sinks, w_out, w_up):
    m = x.shape[0]
    nblk = m // TM
    cl = lambda i: jnp.minimum(i, nblk - 1)
    row = lambda w: pl.BlockSpec((TM, w), lambda i: (cl(i), 0))
    wo_spec = pl.BlockSpec((w_out.shape[0] // nblk, w_out.shape[1]), lambda i: (cl(i), 0))
    wu_spec = pl.BlockSpec((w_up.shape[0], w_up.shape[1] // nblk), lambda i: (0, cl(i)))
    return pl.pallas_call(
        functools.partial(_proj_swa_body, nblk=nblk),
        grid=(nblk + 1,),
        in_specs=[pl.BlockSpec(memory_space=pltpu.SMEM),
                  row(D_MODEL), _const_spec((1, D_MODEL)), _w_in_spec(0),
                  _const_spec((1, 256)), _const_spec((1, KV_WIDTH)),
                  _const_spec((1, HALF)), _const_spec((256, 256)), wo_spec, wu_spec],
        out_specs=[pl.BlockSpec((TM, ATT_WIDTH), lambda i: (jnp.maximum(i - 1, 0), 0)),
                   row(2 * KV_WIDTH), row(RET_WIDTH), wo_spec, wu_spec],
        out_shape=(jax.ShapeDtypeStruct((m, ATT_WIDTH), BF16),
                   jax.ShapeDtypeStruct((m, 2 * KV_WIDTH), F32),
                   jax.ShapeDtypeStruct((m, RET_WIDTH), BF16),
                   jax.ShapeDtypeStruct(w_out.shape, BF16),
                   jax.ShapeDtypeStruct(w_up.shape, BF16)),
        scratch_shapes=[pltpu.VMEM((2, TM, ATT_WIDTH), BF16),
                        pltpu.VMEM((2, TM, 2 * KV_WIDTH), F32),
                        pltpu.VMEM((WINDOW, 2 * KV_WIDTH), F32),
                        pltpu.VMEM((TM, HALF), F32), pltpu.VMEM((TM, HALF), F32)],
        compiler_params=_params(),
        name="proj_swa",
    )(sinks, x, ln_g, w_in, qg, kg, inv, ones, w_out, w_up)


def _proj_ret_body(x_ref, g_ref, w_ref, inv_ref, rq_ref, rg_ref, wd_ref,
                   r_out, s_out, wd_bf,
                   rk_scr, rv_scr, gt_scr, s_scr, cr_scr, sr_scr, *, nblk):
    i = pl.program_id(0)
    tm = x_ref.shape[0]

    @pl.when(i == 0)
    def _():
        _rope_init(inv_ref, cr_scr, sr_scr, 1)
        rk_scr[...] = jnp.zeros_like(rk_scr)
        rv_scr[...] = jnp.zeros_like(rv_scr)
        gt_scr[...] = jnp.zeros_like(gt_scr)
        s_scr[...] = jnp.zeros_like(s_scr)

    wd_bf[...] = wd_ref[...].astype(BF16)

    cur = lax.rem(i, 2)
    prv = 1 - cur
    blk = jnp.minimum(i, nblk - 1)

    h = _norm_rows(x_ref, g_ref)
    cos, sin = _rope_tables(inv_ref, cr_scr, sr_scr, blk * tm)

    def rk_piece(n):
        y = _dot(h, w_ref[:, 512 * n:512 * (n + 1)])
        for t in range(2):
            _rope_head(y[:, RET_DK * t:RET_DK * (t + 1)], cos, sin, RET_DK ** -0.5,
                       rk_scr.at[cur], 2 * n + t)

    def rv_piece(n):
        c = slice(512 * n, 512 * (n + 1))
        rv_scr[cur, :, c] = _dot(h, w_ref[:, RET_WIDTH + 512 * n:RET_WIDTH + 512 * (n + 1)]
                                 ).astype(BF16)

    def gate_piece(n):
        c0 = 2 * RET_WIDTH + 512 * n
        rg = _dot(h, w_ref[:, c0:c0 + 512])
        gt_scr[cur, :, 512 * n:512 * (n + 1)] = (rg / (1.0 + jnp.exp(-rg))).astype(BF16)

    main = [functools.partial(f, n) for f in (rk_piece, rv_piece, gate_piece) for n in range(2)]
    _emit_pipelined(main, *_ret_phases(rq_ref, rk_scr.at[prv], rv_scr.at[prv], gt_scr.at[prv],
                                       rg_ref, r_out, s_scr))

    @pl.when(i == nblk)
    def _():
        s_out[...] = s_scr[...]


def _proj_ret(x, ln_g, w_in, inv, rq, rg_g, w_down):
    m = x.shape[0]
    nblk = m // TM
    cl = lambda i: jnp.minimum(i, nblk - 1)
    prev = lambda w: pl.BlockSpec((TM, w), lambda i: (jnp.maximum(i - 1, 0), 0))
    wd_spec = pl.BlockSpec((w_down.shape[0] // nblk, w_down.shape[1]), lambda i: (cl(i), 0))
    state = (RET_HEADS, RET_DK, RET_DV)
    slot = pltpu.VMEM((2, TM, RET_WIDTH), BF16)
    return pl.pallas_call(
        functools.partial(_proj_ret_body, nblk=nblk),
        grid=(nblk + 1,),
        in_specs=[pl.BlockSpec((TM, D_MODEL), lambda i: (cl(i), 0)),
                  _const_spec((1, D_MODEL)), _w_in_spec(1), _const_spec((1, HALF)),
                  prev(RET_WIDTH), _const_spec((1, RET_WIDTH)), wd_spec],
        out_specs=[prev(RET_WIDTH), pl.BlockSpec(state, lambda i: (0, 0, 0)), wd_spec],
        out_shape=(jax.ShapeDtypeStruct((m, RET_WIDTH), BF16),
                   jax.ShapeDtypeStruct(state, F32),
                   jax.ShapeDtypeStruct(w_down.shape, BF16)),
        scratch_shapes=[slot, slot, slot, pltpu.VMEM(state, F32),
                        pltpu.VMEM((TM, HALF), F32), pltpu.VMEM((TM, HALF), F32)],
        compiler_params=_params(),
        name="proj_ret",
    )(x, ln_g, w_in, inv, rq, rg_g, w_down)


def _in_proj_body(x_ref, g_ref, wa_ref, wb_ref, qg_ref, kg_ref, inv_ref, ones_ref,
                  q_out, kv_out, rq_out, rk_out, rv_out, gt_out, kvt_out, rqt_out, rkt_out,
                  cr_scr, sr_scr, *, pos):
    _rope_init(inv_ref, cr_scr, sr_scr, 0)
    h = _norm_rows(x_ref, g_ref)
    q_tiles, kn, v = _attn_proj(h, wa_ref, qg_ref, kg_ref, ones_ref)
    for t, qt in enumerate(q_tiles):
        q_out[:, 256 * t:256 * (t + 1)] = qt
    kv_out[:, :KV_WIDTH] = kn
    kv_out[:, KV_WIDTH:] = v
    cos, sin = _rope_tables(inv_ref, cr_scr, sr_scr, pos)
    _rope_proj(h, wa_ref[:, ATT_WIDTH + 2 * KV_WIDTH:], cos, sin, 1.0, rq_out)
    _rope_proj(h, wb_ref[:, :RET_WIDTH], cos, sin, RET_DK ** -0.5, rk_out)
    rv_out[...] = _dot(h, wb_ref[:, RET_WIDTH:2 * RET_WIDTH]).astype(BF16)
    _gate_proj(h, wb_ref[:, 2 * RET_WIDTH:], gt_out)
    kvt_out[...] = kv_out[...].T
    rqt_out[...] = rq_out[...].astype(F32).T
    rkt_out[...] = rk_out[...].astype(F32).T


def _in_proj(x, ln_g, w_in, qg, kg, inv, ones, pos):
    m = x.shape[0]
    assert m == LANES
    full = lambda w: pl.BlockSpec((m, w), lambda i: (0, 0))
    colm = lambda w: pl.BlockSpec((w, m), lambda i: (0, 0))
    ret = jax.ShapeDtypeStruct((m, RET_WIDTH), BF16)
    ret_t = jax.ShapeDtypeStruct((RET_WIDTH, m), F32)
    return pl.pallas_call(
        functools.partial(_in_proj_body, pos=pos),
        grid=(1,),
        in_specs=[full(D_MODEL), _const_spec((1, D_MODEL)), _w_in_spec(0),
                  _w_in_spec(1), _const_spec((1, 256)), _const_spec((1, KV_WIDTH)),
                  _const_spec((1, HALF)), _const_spec((256, 256))],
        out_specs=[full(ATT_WIDTH), full(2 * KV_WIDTH)] + [full(RET_WIDTH)] * 4
        + [colm(2 * KV_WIDTH), colm(RET_WIDTH), colm(RET_WIDTH)],
        out_shape=(jax.ShapeDtypeStruct((m, ATT_WIDTH), BF16),
                   jax.ShapeDtypeStruct((m, 2 * KV_WIDTH), F32), ret, ret, ret, ret,
                   jax.ShapeDtypeStruct((2 * KV_WIDTH, m), F32), ret_t, ret_t),
        scratch_shapes=[pltpu.VMEM((m, HALF), F32), pltpu.VMEM((m, HALF), F32)],
        compiler_params=_params(),
        name="in_proj",
    )(x, ln_g, w_in, w_in, qg, kg, inv, ones)


def _this_steps_columns(t_ref, rows, bb):
    shift = lax.rem(LANES - bb * pl.program_id(0), LANES)
    return pltpu.roll(t_ref[rows, :], shift, 1)


def _swa_dec_body(q_ref, kvn_ref, kvt_ref, ck_ref, cv_ref, sink_ref, o_ref, nk_ref, nv_ref):
    bb = q_ref.shape[0]
    npair = ATT_HEADS // 2
    q8 = q_ref[...].astype(F32)
    q8r = pltpu.roll(q8, ATT_HEAD_DIM, 2)
    lane = lax.broadcasted_iota(jnp.int32, q8.shape, 2)
    pair = lax.broadcasted_iota(jnp.int32, q8.shape, 1)
    lo = lane < ATT_HEAD_DIM
    kv0 = pair < npair // ATT_KV_HEADS
    own = lo == kv0
    zero = jnp.zeros_like(q8)
    qe = jnp.where(own, jnp.where(kv0, q8, q8r), zero)
    qo = jnp.where(own, jnp.where(kv0, q8r, q8), zero)
    qb = jnp.concatenate([qe, qo], axis=1)

    ck = ck_ref[...]
    cv = cv_ref[...]
    kn = kvn_ref[:, :KV_WIDTH]
    vn = kvn_ref[:, KV_WIDTH:]
    s = lax.dot_general(qb.astype(BF16), ck.astype(BF16), (((2,), (1,)), ((0,), (0,))),
                        preferred_element_type=F32)
    s_new = jnp.sum(qb * kn[:, None, :], axis=-1, keepdims=True)
    sink = sink_ref[...][None, :, :]
    mx = jnp.maximum(jnp.maximum(jnp.max(s, axis=-1, keepdims=True), s_new), sink)
    p = jnp.exp(s - mx)
    p_new = jnp.exp(s_new - mx)
    den = jnp.sum(p, axis=-1, keepdims=True) + p_new + jnp.exp(sink - mx)
    o = lax.dot_general(p.astype(BF16), cv.astype(BF16), (((2,), (2,)), ((0,), (0,))),
                        preferred_element_type=F32)
    o = (o + p_new * vn[:, None, :]) / den
    oe = o[:, :npair, :]
    oo = o[:, npair:, :]
    oer = pltpu.roll(oe, ATT_HEAD_DIM, 2)
    oor = pltpu.roll(oo, ATT_HEAD_DIM, 2)
    o_ref[...] = jnp.where(lo, jnp.where(kv0, oe, oer), jnp.where(kv0, oor, oo)).astype(BF16)

    newcol = _this_steps_columns(kvt_ref, slice(None), bb)
    last = lax.broadcasted_iota(jnp.int32, (KV_WIDTH, WINDOW), 1) == WINDOW - 1
    for jb in range(bb):
        nk_ref[jb] = jnp.where(last, newcol[:KV_WIDTH, jb:jb + 1],
                               pltpu.roll(ck[jb], WINDOW - 1, 1))
        nv_ref[jb] = jnp.where(last, newcol[KV_WIDTH:, jb:jb + 1],
                               pltpu.roll(cv[jb], WINDOW - 1, 1))


def _swa_dec(q8, kvn, kvt, ck, cv, sink_col):
    b = q8.shape[0]
    bb = BB_ATT
    cache = pl.BlockSpec((bb, KV_WIDTH, WINDOW), lambda i: (i, 0, 0))
    return pl.pallas_call(
        _swa_dec_body,
        grid=(b // bb,),
        in_specs=[pl.BlockSpec((bb, ATT_HEADS // 2, LANES), lambda i: (i, 0, 0)),
                  pl.BlockSpec((bb, 2 * KV_WIDTH), lambda i: (i, 0)),
                  _const_spec(kvt.shape), cache, cache, _const_spec((ATT_HEADS, 1))],
        out_specs=[pl.BlockSpec((bb, ATT_HEADS // 2, LANES), lambda i: (i, 0, 0)), cache, cache],
        out_shape=(jax.ShapeDtypeStruct((b, ATT_HEADS // 2, LANES), BF16),
                   jax.ShapeDtypeStruct(ck.shape, F32),
                   jax.ShapeDtypeStruct(cv.shape, F32)),
        compiler_params=_params(),
        name="swa_dec",
    )(q8, kvn, kvt, ck, cv, sink_col)


def _ret_dec_body(qt_ref, kt_ref, rq_ref, rk_ref, rv_ref, rg_ref, g_ref, s_ref,
                  o_ref, ns_ref, o_scr):
    bb = s_ref.shape[0]
    for hh in range(RET_HEADS):
        g1 = math.exp(LOG_G[hh])
        cols = slice(RET_DK * hh, RET_DK * (hh + 1))
        qcols = _this_steps_columns(qt_ref, cols, bb) * g1
        kcols = _this_steps_columns(kt_ref, cols, bb)
        for jb in range(bb):
            s0 = s_ref[jb, hh]
            qc = qcols[:, jb:jb + 1]
            kc = kcols[:, jb:jb + 1]
            v = rv_ref[jb:jb + 1, cols].astype(F32)
            o_scr[jb:jb + 1, cols] = jnp.sum(qc * s0, axis=0, keepdims=True)
            ns_ref[jb, hh] = g1 * s0 + kc * v
    for hh in range(RET_HEADS):
        cols = slice(RET_DK * hh, RET_DK * (hh + 1))
        qk = jnp.sum(rq_ref[:, cols].astype(F32) * rk_ref[:, cols].astype(F32),
                     axis=-1, keepdims=True)
        o = o_scr[:, cols] + qk * rv_ref[:, cols].astype(F32)
        var = jnp.mean(o * o, axis=-1, keepdims=True)
        on = o * lax.rsqrt(var + EPS) * g_ref[:, cols]
        o_ref[:, cols] = (on * rg_ref[:, cols].astype(F32)).astype(BF16)


def _ret_dec(qt, kt, rq, rk, rv, rg, g, state):
    b = rv.shape[0]
    bb = BB_RET
    row = pl.BlockSpec((bb, RET_WIDTH), lambda i: (i, 0))
    col = _const_spec(qt.shape)
    st = pl.BlockSpec((bb, RET_HEADS, RET_DK, RET_DV), lambda i: (i, 0, 0, 0))
    return pl.pallas_call(
        _ret_dec_body,
        grid=(b // bb,),
        in_specs=[col, col, row, row, row, row, _const_spec((1, RET_WIDTH)), st],
        out_specs=[row, st],
        out_shape=(jax.ShapeDtypeStruct((b, RET_WIDTH), BF16),
                   jax.ShapeDtypeStruct(state.shape, F32)),
        scratch_shapes=[pltpu.VMEM((bb, RET_WIDTH), F32)],
        compiler_params=_params(),
        name="ret_dec",
    )(qt, kt, rq, rk, rv, rg, g, state)


def _out_proj_body(x_ref, a_ref, r_ref, w_ref, g_ref, x1_ref, h2_ref):
    x1 = (x_ref[...] + _dot(a_ref[...], w_ref[:ATT_WIDTH, :])
          + _dot(r_ref[...], w_ref[ATT_WIDTH:, :]))
    x1_ref[...] = x1
    var = jnp.mean(x1 * x1, axis=-1, keepdims=True)
    h2_ref[...] = (x1 * lax.rsqrt(var + EPS) * g_ref[...]).astype(BF16)


def _out_proj(x, a, r, w, g, tm):
    m = x.shape[0]
    row = lambda w: pl.BlockSpec((tm, w), lambda i: (i, 0))
    return pl.pallas_call(
        _out_proj_body,
        grid=(m // tm,),
        in_specs=[row(D_MODEL), row(ATT_WIDTH), row(RET_WIDTH),
                  _const_spec(w.shape), _const_spec((1, D_MODEL))],
        out_specs=[row(D_MODEL), row(D_MODEL)],
        out_shape=(jax.ShapeDtypeStruct((m, D_MODEL), F32),
                   jax.ShapeDtypeStruct((m, D_MODEL), BF16)),
        compiler_params=_params(),
        name="out_proj",
    )(x, a, r, w, g)


def _mlp_body(x1_ref, h2_ref, wu_ref, wd_ref, o_ref):
    f = pl.program_id(1)

    @pl.when(f == 0)
    def _():
        o_ref[...] = x1_ref[...]

    u = jnp.maximum(_dot(h2_ref[...], wu_ref[...]), 0.0)
    o_ref[...] += _dot((u * u).astype(BF16), wd_ref[...])


def _mlp(x1, h2, wu, wd, tm):
    m = x1.shape[0]
    return pl.pallas_call(
        _mlp_body,
        grid=(m // tm, D_FF // TF),
        in_specs=[pl.BlockSpec((tm, D_MODEL), lambda i, f: (i, 0)),
                  pl.BlockSpec((tm, D_MODEL), lambda i, f: (i, 0)),
                  pl.BlockSpec((D_MODEL, TF), lambda i, f: (0, f)),
                  pl.BlockSpec((TF, D_MODEL), lambda i, f: (f, 0))],
        out_specs=pl.BlockSpec((tm, D_MODEL), lambda i, f: (i, 0)),
        out_shape=jax.ShapeDtypeStruct((m, D_MODEL), F32),
        compiler_params=_params(2),
        name="mlp",
    )(x1, h2, wu, wd)


def kernel(x_prompt, x_sample, cache_k_win, cache_v_win, state_ret, ln1_g, w_in, q_norm_g,
           k_norm_g, attn_sinks, ret_norm_g, w_out, ln2_g, w_up, w_down):
    seq = x_prompt.shape[1]
    nb = x_sample.shape[0]
    assert x_prompt.shape[0] == 1 and x_sample.shape[1] == 1 and w_in.shape[0] == 1
    assert seq % TM == 0 and w_in.shape[2] == SPLIT_A + SPLIT_B

    wi = w_in[0].astype(BF16)
    inv = (ROPE_BASE ** (-jnp.arange(HALF, dtype=F32) / HALF)).reshape(1, HALF)

    ln1 = ln1_g.reshape(1, D_MODEL)
    ln2 = ln2_g.reshape(1, D_MODEL)
    qg = jnp.tile(q_norm_g.reshape(1, ATT_HEAD_DIM) * (ATT_HEAD_DIM ** -0.5), (1, 256 // ATT_HEAD_DIM))
    kg = jnp.tile(k_norm_g.reshape(1, ATT_HEAD_DIM), (1, KV_WIDTH // ATT_HEAD_DIM))
    rg_g = ret_norm_g.reshape(1, RET_WIDTH)
    sinks = attn_sinks.reshape(ATT_HEADS)
    blk = jnp.arange(256) // ATT_HEAD_DIM
    ones = (blk[:, None] == blk[None, :]).astype(BF16)

    xp = x_prompt[0]
    a_out, kv, rq, wo, wu = _proj_swa(xp, ln1, wi, qg, kg, inv, ones, sinks, w_out[0], w_up[0])
    r_out, s_fin, wd = _proj_ret(xp, ln1, wi, inv, rq, rg_g, w_down[0])
    x1, h2 = _out_proj(xp, a_out, r_out, wo, ln2, tm=TM)
    yp = _mlp(x1, h2, wu, wd, tm=TM)

    wb = min(WINDOW, seq)
    kp = kv[seq - wb:, :KV_WIDTH].reshape(1, 1, wb, ATT_KV_HEADS, ATT_HEAD_DIM)
    vp = kv[seq - wb:, KV_WIDTH:].reshape(1, 1, wb, ATT_KV_HEADS, ATT_HEAD_DIM)
    sp = s_fin.reshape(1, 1, RET_HEADS, RET_DK, RET_DV)

    xs = x_sample[:, 0]
    qs, kvs, rqs, rks, rvs, rgs, kvt, rqt, rkt = _in_proj(xs, ln1, wi, qg, kg, inv, ones,
                                                          pos=PAST_LEN)
    sink_col = jnp.concatenate([sinks[0::2], sinks[1::2]]).reshape(ATT_HEADS, 1)
    to_fm = lambda c: c[0].transpose(0, 2, 3, 1).reshape(nb, KV_WIDTH, WINDOW)
    from_fm = lambda c: c.reshape(nb, ATT_KV_HEADS, ATT_HEAD_DIM, WINDOW).transpose(0, 3, 1, 2)[None]
    a8, nk, nv = _swa_dec(qs.reshape(nb, ATT_HEADS // 2, LANES), kvs, kvt,
                          to_fm(cache_k_win), to_fm(cache_v_win), sink_col)
    a_s = a8.reshape(nb, ATT_WIDTH)

    r_s, ns = _ret_dec(rqt, rkt, rqs, rks, rvs, rgs, rg_g, state_ret[0])
    x1s, h2s = _out_proj(xs, a_s, r_s, wo, ln2, tm=nb)
    ys = _mlp(x1s, h2s, wu, wd, tm=nb)

    return (yp[None], ys[:, None, :], kp, vp, sp, from_fm(nk), from_fm(nv), ns[None])
```

```python
import functools
import math

import jax
import jax.numpy as jnp
from jax import lax
from jax.experimental import pallas as pl
from jax.experimental.pallas import tpu as pltpu

D_MODEL = 2048
ATT_HEADS = 16
ATT_KV_HEADS = 2
ATT_HEAD_DIM = 64
WINDOW = 128
RET_HEADS = 4
RET_DK = 256
RET_DV = 256
RET_CHUNK = 128
ROPE_BASE = 10000.0
D_FF = 4 * D_MODEL
EPS = 1e-6
PAST_LEN = 8192

ATT_WIDTH = ATT_HEADS * ATT_HEAD_DIM
KV_WIDTH = ATT_KV_HEADS * ATT_HEAD_DIM
RET_WIDTH = RET_HEADS * RET_DK
LANES = 128
HALF = RET_DK // 2
SPLIT_A = ATT_WIDTH + 2 * KV_WIDTH + RET_WIDTH
SPLIT_B = 3 * RET_WIDTH

F32 = jnp.float32
BF16 = jnp.bfloat16
VMEM_LIMIT = 60 * 1024 * 1024

TM = 512
TF = 2048
TF_RET = 1024
BB_ATT = 16

LOG_G = tuple(math.log1p(-(2.0 ** (-5.0 - h))) for h in range(RET_HEADS))


def _dot(a, b):
    return jnp.dot(a, b, preferred_element_type=F32)


def _dot_nt(a, b):
    return lax.dot_general(a, b, (((1,), (1,)), ((), ())), preferred_element_type=F32)


def _dot_tn(a, b):
    return lax.dot_general(a, b, (((0,), (0,)), ((), ())), preferred_element_type=F32)


def _const_spec(shape):
    n = len(shape)
    return pl.BlockSpec(shape, lambda *_: (0,) * n, pipeline_mode=pl.Buffered(1))


def _w_in_spec(part):
    width, start = ((SPLIT_A, 0), (SPLIT_B, SPLIT_A))[part]
    return pl.BlockSpec((pl.Element(D_MODEL), pl.Element(width)), lambda *_: (0, start),
                        pipeline_mode=pl.Buffered(1))


def _params(n_axes=1):
    return pltpu.CompilerParams(dimension_semantics=("arbitrary",) * n_axes,
                                vmem_limit_bytes=VMEM_LIMIT)


def _norm_rows(x_ref, g_ref):
    x = x_ref[...]
    var = jnp.mean(x * x, axis=-1, keepdims=True)
    return (x * lax.rsqrt(var + EPS) * g_ref[...]).astype(BF16)


def _attn_proj(h, w_ref, qg_ref, kg_ref, ones_ref):
    ones = ones_ref[...]
    inv_hd = 1.0 / ATT_HEAD_DIM
    aq = _dot(h, w_ref[:, :ATT_WIDTH])
    q_tiles = []
    for t in range(ATT_WIDTH // 256):
        blk = aq[:, 256 * t:256 * (t + 1)]
        ssq = _dot((blk * blk).astype(BF16), ones)
        q_tiles.append((blk * lax.rsqrt(ssq * inv_hd + EPS) * qg_ref[...]).astype(BF16))
    kvr = _dot(h, w_ref[:, ATT_WIDTH:ATT_WIDTH + 2 * KV_WIDTH])
    k = kvr[:, :KV_WIDTH]
    ssq = _dot((k * k).astype(BF16), ones[:KV_WIDTH, :KV_WIDTH])
    kn = k * lax.rsqrt(ssq * inv_hd + EPS) * kg_ref[...]
    return q_tiles, kn, kvr[:, KV_WIDTH:]


def _rope_init(inv_ref, cr_scr, sr_scr, pos_step):
    tm = cr_scr.shape[0]
    row = lax.broadcasted_iota(jnp.int32, (tm, 1), 0)
    ang_r = (pos_step * row).astype(F32) * inv_ref[...]
    cr_scr[...] = jnp.cos(ang_r)
    sr_scr[...] = jnp.sin(ang_r)


def _rope_tables(inv_ref, cr_scr, sr_scr, base_pos):
    base = jnp.zeros((8, 1), jnp.int32) + base_pos
    ang_b = base.astype(F32) * inv_ref[...]
    cb = jnp.cos(ang_b)[:1]
    sb = jnp.sin(ang_b)[:1]
    cr = cr_scr[...]
    sr = sr_scr[...]
    return cb * cr - sb * sr, sb * cr + cb * sr


def _rope_head(r, cos, sin, scale, o_ref, hh):
    x1 = r[:, :HALF]
    x2 = r[:, HALF:]
    o1 = x1 * cos - x2 * sin
    o2 = x2 * cos + x1 * sin
    if scale != 1.0:
        o1 = o1 * scale
        o2 = o2 * scale
    o_ref[:, RET_DK * hh:RET_DK * hh + HALF] = o1.astype(BF16)
    o_ref[:, RET_DK * hh + HALF:RET_DK * (hh + 1)] = o2.astype(BF16)


def _rope_proj(h, w, cos, sin, scale, o_ref):
    r = _dot(h, w)
    for hh in range(RET_HEADS):
        _rope_head(r[:, RET_DK * hh:RET_DK * (hh + 1)], cos, sin, scale, o_ref, hh)


def _gate_proj(h, w, o_ref):
    rg = _dot(h, w)
    o_ref[...] = (rg / (1.0 + jnp.exp(-rg))).astype(BF16)


def _half_split(a):
    lane = lax.broadcasted_iota(jnp.int32, a.shape, 1)
    lo = lane < ATT_HEAD_DIM
    sw = pltpu.roll(a, ATT_HEAD_DIM, 1)
    zero = jnp.zeros_like(a)
    h0 = (jnp.where(lo, a, zero).astype(BF16), jnp.where(lo, zero, sw).astype(BF16))
    h1 = (jnp.where(lo, sw, zero).astype(BF16), jnp.where(lo, zero, a).astype(BF16))
    return h0, h1


SWA_GROUP = 4


def _swa_phases(sink_ref, q_ref, kvc_ref, kvp_ref, o_ref, has_prev):
    nsub = q_ref.shape[0] // WINDOW
    npair = ATT_HEADS // 2
    row = lax.broadcasted_iota(jnp.int32, (WINDOW, 2 * WINDOW), 0)
    col = lax.broadcasted_iota(jnp.int32, (WINDOW, 2 * WINDOW), 1)
    band_cur = (col >= WINDOW) & (col - WINDOW <= row)
    first_col = jnp.where(has_prev, 0, WINDOW)
    lane_lo = lax.broadcasted_iota(jnp.int32, (WINDOW, LANES), 1) < ATT_HEAD_DIM
    ctx = [{} for _ in range(nsub)]
    st = {}

    def prep(g):
        j, first = divmod(g * SWA_GROUP, npair)
        if first:
            return
        prev = kvp_ref[...] if j == 0 else kvc_ref[WINDOW * (j - 1):WINDOW * j, :]
        cur = kvc_ref[WINDOW * j:WINDOW * (j + 1), :]
        kk = jnp.concatenate([prev[:, :KV_WIDTH], cur[:, :KV_WIDTH]], axis=0)
        vv = jnp.concatenate([prev[:, KV_WIDTH:], cur[:, KV_WIDTH:]], axis=0)
        ctx[j]["k"] = _half_split(kk)
        ctx[j]["v"] = _half_split(vv)
        lo_col = first_col if j == 0 else 0
        ctx[j]["mask"] = band_cur | ((col < WINDOW) & (col >= row) & (col >= lo_col))

    def units(g):
        for u in range(g * SWA_GROUP, (g + 1) * SWA_GROUP):
            j, t = divmod(u, npair)
            yield j, t, t // (npair // ATT_KV_HEADS), slice(WINDOW * j, WINDOW * (j + 1))

    def qk(g):
        us = list(units(g))
        j, _, kh, rows = us[0]
        qs = jnp.concatenate([q_ref[rows, LANES * t:LANES * (t + 1)] for _, t, _, _ in us], axis=0)
        s = [_dot_nt(qs, ctx[j]["k"][kh][par]) for par in range(2)]
        for n, (_, t, _, _) in enumerate(us):
            st[j, t] = {"s": [sp[WINDOW * n:WINDOW * (n + 1), :] for sp in s]}

    def softmax(g):
        for j, t, kh, rows in units(g):
            p, inv = [], []
            for par in range(2):
                sink = sink_ref[2 * t + par]
                s = jnp.where(ctx[j]["mask"], st[j, t]["s"][par], -jnp.inf)
                mx = jnp.maximum(jnp.max(s, axis=-1, keepdims=True), sink)
                e = jnp.exp(s - mx)
                den = jnp.sum(e, axis=-1, keepdims=True) + jnp.exp(sink - mx)
                p.append(e.astype(BF16))
                inv.append(1.0 / den)
            st[j, t] = {"p": p, "inv": jnp.where(lane_lo, inv[0], inv[1])}

    def pv(g):
        us = list(units(g))
        j, _, kh, rows = us[0]
        got = [st.pop((j, t)) for _, t, _, _ in us]
        acc = sum(_dot(jnp.concatenate([u["p"][par] for u in got], axis=0), ctx[j]["v"][kh][par])
                  for par in range(2))
        for n, (_, t, _, _) in enumerate(us):
            o_ref[rows, LANES * t:LANES * (t + 1)] = (
                acc[WINDOW * n:WINDOW * (n + 1), :] * got[n]["inv"]).astype(BF16)

    ngroup = nsub * npair // SWA_GROUP
    return ngroup, prep, qk, softmax, pv


def _emit_pipelined(main, ngroup, prep, first, middle, last, finish=None):
    nstage = ngroup + 2
    done = 0
    prep(0)
    for k in range(nstage):
        if k < ngroup:
            first(k)
        if 0 <= k - 2 < ngroup:
            last(k - 2)
        if 0 <= k - 1 < ngroup:
            middle(k - 1)
        if k + 1 < ngroup:
            prep(k + 1)
        upto = -(-len(main) * (k + 1) // nstage)
        for piece in main[done:upto]:
            piece()
        done = upto
    if finish is not None:
        finish()


def _ret_phases(rq_ref, rk_ref, rv_ref, gt_ref, g_ref, o_ref, s_scr):
    c = RET_CHUNK
    nsub = rq_ref.shape[0] // c
    ri = lax.broadcasted_iota(jnp.int32, (c, c), 0)
    ci = lax.broadcasted_iota(jnp.int32, (c, c), 1)
    rel = (ri - ci).astype(F32)
    idx = lax.broadcasted_iota(jnp.int32, (c, 1), 0).astype(F32)
    head = []
    for hh in range(RET_HEADS):
        lg = LOG_G[hh]
        head.append(dict(
            dmask=jnp.where(rel >= 0, jnp.exp(lg * jnp.maximum(rel, 0.0)), 0.0),
            qdec=jnp.exp(lg * (idx + 1.0)), kdec=jnp.exp(lg * (c - 1.0 - idx)),
            cdec=math.exp(lg * c), cols=slice(RET_DK * hh, RET_DK * (hh + 1))))
    st = {}

    def update_state(g):
        for hh, hd in enumerate(head):
            u = st[g, hh]
            s_scr[hh] = hd["cdec"] * u.pop("s_prev") + u.pop("kv")

    def prep(g):
        if g > 0:
            update_state(g - 1)
        rows = slice(c * g, c * (g + 1))
        for hh, hd in enumerate(head):
            q = rq_ref[rows, hd["cols"]]
            k = rk_ref[rows, hd["cols"]]
            s_prev = s_scr[hh]
            st[g, hh] = dict(q=q, k=k, v=rv_ref[rows, hd["cols"]], s_prev=s_prev,
                             qd=(q.astype(F32) * hd["qdec"]).astype(BF16),
                             kd=(k.astype(F32) * hd["kdec"]).astype(BF16),
                             s_bf=s_prev.astype(BF16))

    def first(g):
        for hh in range(RET_HEADS):
            u = st[g, hh]
            u["att"] = _dot_nt(u.pop("q"), u.pop("k"))
            u["inter"] = _dot(u.pop("qd"), u.pop("s_bf"))
            u["kv"] = _dot_tn(u.pop("kd"), u["v"])

    def middle(g):
        for hh, hd in enumerate(head):
            u = st[g, hh]
            u["att"] = (u["att"] * hd["dmask"]).astype(BF16)

    def last(g):
        rows = slice(c * g, c * (g + 1))
        for hh, hd in enumerate(head):
            u = st[g, hh]
            o = _dot(u.pop("att"), u.pop("v")) + u.pop("inter")
            var = jnp.mean(o * o, axis=-1, keepdims=True)
            on = o * lax.rsqrt(var + EPS) * g_ref[:, hd["cols"]]
            o_ref[rows, hd["cols"]] = (on * gt_ref[rows, hd["cols"]].astype(F32)).astype(BF16)

    return nsub, prep, first, middle, last, functools.partial(update_state, nsub - 1)


def _proj_swa_body(sink_ref, x_ref, g_ref, w_ref, qg_ref, kg_ref, inv_ref, ones_ref,
                   wo_ref, wu_ref,
                   a_out, kv_out, rq_out, wo_bf, wu_bf,
                   q_scr, kv_scr, kvp_scr, cr_scr, sr_scr, *, nblk):
    i = pl.program_id(0)
    tm = x_ref.shape[0]

    @pl.when(i == 0)
    def _():
        _rope_init(inv_ref, cr_scr, sr_scr, 1)
        q_scr[...] = jnp.zeros_like(q_scr)
        kv_scr[...] = jnp.zeros_like(kv_scr)
        kvp_scr[...] = jnp.zeros_like(kvp_scr)

    wo_bf[...] = wo_ref[...].astype(BF16)
    wu_bf[...] = wu_ref[...].astype(BF16)

    cur = lax.rem(i, 2)
    prv = 1 - cur
    blk = jnp.minimum(i, nblk - 1)

    h = _norm_rows(x_ref, g_ref)
    ones = ones_ref[...]
    inv_hd = 1.0 / ATT_HEAD_DIM
    cos, sin = _rope_tables(inv_ref, cr_scr, sr_scr, blk * tm)


    def q_piece(n):
        y = _dot(h, w_ref[:, 512 * n:512 * (n + 1)])
        for t in range(2):
            yt = y[:, 256 * t:256 * (t + 1)]
            ssq = _dot((yt * yt).astype(BF16), ones)
            c0 = 512 * n + 256 * t
            q_scr[cur, :, c0:c0 + 256] = (
                yt * lax.rsqrt(ssq * inv_hd + EPS) * qg_ref[...]).astype(BF16)

    kv0 = ATT_WIDTH
    rq0 = ATT_WIDTH + 2 * KV_WIDTH

    def kv_rq_piece():
        y = _dot(h, w_ref[:, kv0:rq0 + RET_DK])
        k = y[:, :KV_WIDTH]
        ssq = _dot((k * k).astype(BF16), ones[:KV_WIDTH, :KV_WIDTH])
        kn = k * lax.rsqrt(ssq * inv_hd + EPS) * kg_ref[...]
        v = y[:, KV_WIDTH:2 * KV_WIDTH]
        kv_scr[cur, :, :KV_WIDTH] = kn
        kv_scr[cur, :, KV_WIDTH:] = v
        kv_out[:, :KV_WIDTH] = kn
        kv_out[:, KV_WIDTH:] = v
        _rope_head(y[:, 2 * KV_WIDTH:], cos, sin, 1.0, rq_out, 0)

    def rq_piece():
        y = _dot(h, w_ref[:, rq0 + RET_DK:])
        for hh in range(1, RET_HEADS):
            _rope_head(y[:, RET_DK * (hh - 1):RET_DK * hh], cos, sin, 1.0, rq_out, hh)

    main = [functools.partial(q_piece, 0), functools.partial(q_piece, 1), kv_rq_piece, rq_piece]
    _emit_pipelined(main, *_swa_phases(sink_ref, q_scr.at[prv], kv_scr.at[prv], kvp_scr, a_out,
                                       has_prev=i > 1))
    kvp_scr[...] = kv_scr[prv, tm - WINDOW:, :]


def _proj_swa(x, ln_g, w_in, qg, kg, inv, ones, sinks, w_out, w_up):
    m = x.shape[0]
    nblk = m // TM
    cl = lambda i: jnp.minimum(i, nblk - 1)
    row = lambda w: pl.BlockSpec((TM, w), lambda i: (cl(i), 0))
    wo_spec = pl.BlockSpec((w_out.shape[0] // nblk, w_out.shape[1]), lambda i: (cl(i), 0))
    wu_spec = pl.BlockSpec((w_up.shape[0], w_up.shape[1] // nblk), lambda i: (0, cl(i)))
    return pl.pallas_call(
        functools.partial(_proj_swa_body, nblk=nblk),
        grid=(nblk + 1,),
        in_specs=[pl.BlockSpec(memory_space=pltpu.SMEM),
                  row(D_MODEL), _const_spec((1, D_MODEL)), _w_in_spec(0),
                  _const_spec((1, 256)), _const_spec((1, KV_WIDTH)),
                  _const_spec((1, HALF)), _const_spec((256, 256)), wo_spec, wu_spec],
        out_specs=[pl.BlockSpec((TM, ATT_WIDTH), lambda i: (jnp.maximum(i - 1, 0), 0)),
                   row(2 * KV_WIDTH), row(RET_WIDTH), wo_spec, wu_spec],
        out_shape=(jax.ShapeDtypeStruct((m, ATT_WIDTH), BF16),
                   jax.ShapeDtypeStruct((m, 2 * KV_WIDTH), F32),
                   jax.ShapeDtypeStruct((m, RET_WIDTH), BF16),
                   jax.ShapeDtypeStruct(w_out.shape, BF16),
                   jax.ShapeDtypeStruct(w_up.shape, BF16)),
        scratch_shapes=[pltpu.VMEM((2, TM, ATT_WIDTH), BF16),
                        pltpu.VMEM((2, TM, 2 * KV_WIDTH), F32),
                        pltpu.VMEM((WINDOW, 2 * KV_WIDTH), F32),
                        pltpu.VMEM((TM, HALF), F32), pltpu.VMEM((TM, HALF), F32)],
        compiler_params=_params(),
        name="proj_swa",
    )(sinks, x, ln_g, w_in, qg, kg, inv, ones, w_out, w_up)


def _proj_ret_body(x_ref, g_ref, w_ref, inv_ref, rq_ref, rg_ref, wd_ref,
                   r_out, s_out, wd_bf,
                   rk_scr, rv_scr, gt_scr, s_scr, cr_scr, sr_scr, *, nblk):
    i = pl.program_id(0)
    tm = x_ref.shape[0]

    @pl.when(i == 0)
    def _():
        _rope_init(inv_ref, cr_scr, sr_scr, 1)
        rk_scr[...] = jnp.zeros_like(rk_scr)
        rv_scr[...] = jnp.zeros_like(rv_scr)
        gt_scr[...] = jnp.zeros_like(gt_scr)
        s_scr[...] = jnp.zeros_like(s_scr)

    wd_bf[...] = wd_ref[...].astype(BF16)

    cur = lax.rem(i, 2)
    prv = 1 - cur
    blk = jnp.minimum(i, nblk - 1)

    h = _norm_rows(x_ref, g_ref)
    cos, sin = _rope_tables(inv_ref, cr_scr, sr_scr, blk * tm)

    def rk_piece(n):
        y = _dot(h, w_ref[:, 512 * n:512 * (n + 1)])
        for t in range(2):
            _rope_head(y[:, RET_DK * t:RET_DK * (t + 1)], cos, sin, RET_DK ** -0.5,
                       rk_scr.at[cur], 2 * n + t)

    def rv_piece(n):
        c = slice(512 * n, 512 * (n + 1))
        rv_scr[cur, :, c] = _dot(h, w_ref[:, RET_WIDTH + 512 * n:RET_WIDTH + 512 * (n + 1)]
                                 ).astype(BF16)

    def gate_piece(n):
        c0 = 2 * RET_WIDTH + 512 * n
        rg = _dot(h, w_ref[:, c0:c0 + 512])
        gt_scr[cur, :, 512 * n:512 * (n + 1)] = (rg / (1.0 + jnp.exp(-rg))).astype(BF16)

    main = [functools.partial(f, n) for f in (rk_piece, rv_piece, gate_piece) for n in range(2)]
    _emit_pipelined(main, *_ret_phases(rq_ref, rk_scr.at[prv], rv_scr.at[prv], gt_scr.at[prv],
                                       rg_ref, r_out, s_scr))

    @pl.when(i == nblk)
    def _():
        s_out[...] = s_scr[...]


def _proj_ret(x, ln_g, w_in, inv, rq, rg_g, w_down):
    m = x.shape[0]
    nblk = m // TM
    cl = lambda i: jnp.minimum(i, nblk - 1)
    prev = lambda w: pl.BlockSpec((TM, w), lambda i: (jnp.maximum(i - 1, 0), 0))
    wd_spec = pl.BlockSpec((w_down.shape[0] // nblk, w_down.shape[1]), lambda i: (cl(i), 0))
    state = (RET_HEADS, RET_DK, RET_DV)
    slot = pltpu.VMEM((2, TM, RET_WIDTH), BF16)
    return pl.pallas_call(
        functools.partial(_proj_ret_body, nblk=nblk),
        grid=(nblk + 1,),
        in_specs=[pl.BlockSpec((TM, D_MODEL), lambda i: (cl(i), 0)),
                  _const_spec((1, D_MODEL)), _w_in_spec(1), _const_spec((1, HALF)),
                  prev(RET_WIDTH), _const_spec((1, RET_WIDTH)), wd_spec],
        out_specs=[prev(RET_WIDTH), pl.BlockSpec(state, lambda i: (0, 0, 0)), wd_spec],
        out_shape=(jax.ShapeDtypeStruct((m, RET_WIDTH), BF16),
                   jax.ShapeDtypeStruct(state, F32),
                   jax.ShapeDtypeStruct(w_down.shape, BF16)),
        scratch_shapes=[slot, slot, slot, pltpu.VMEM(state, F32),
                        pltpu.VMEM((TM, HALF), F32), pltpu.VMEM((TM, HALF), F32)],
        compiler_params=_params(),
        name="proj_ret",
    )(x, ln_g, w_in, inv, rq, rg_g, w_down)


def _in_proj_body(x_ref, g_ref, wa_ref, wb_ref, qg_ref, kg_ref, inv_ref, ones_ref,
                  q_out, kv_out, rq_out, rk_out, rv_out, gt_out, kvt_out, rqt_out, rkt_out,
                  cr_scr, sr_scr, *, pos):
    _rope_init(inv_ref, cr_scr, sr_scr, 0)
    h = _norm_rows(x_ref, g_ref)
    q_tiles, kn, v = _attn_proj(h, wa_ref, qg_ref, kg_ref, ones_ref)
    for t, qt in enumerate(q_tiles):
        q_out[:, 256 * t:256 * (t + 1)] = qt
    kv_out[:, :KV_WIDTH] = kn
    kv_out[:, KV_WIDTH:] = v
    cos, sin = _rope_tables(inv_ref, cr_scr, sr_scr, pos)
    _rope_proj(h, wa_ref[:, ATT_WIDTH + 2 * KV_WIDTH:], cos, sin, 1.0, rq_out)
    _rope_proj(h, wb_ref[:, :RET_WIDTH], cos, sin, RET_DK ** -0.5, rk_out)
    rv_out[...] = _dot(h, wb_ref[:, RET_WIDTH:2 * RET_WIDTH]).astype(BF16)
    _gate_proj(h, wb_ref[:, 2 * RET_WIDTH:], gt_out)
    kvt_out[...] = kv_out[...].T
    rqt_out[...] = rq_out[...].astype(F32).T
    rkt_out[...] = rk_out[...].astype(F32).T


def _in_proj(x, ln_g, w_in, qg, kg, inv, ones, pos):
    m = x.shape[0]
    assert m == LANES
    full = lambda w: pl.BlockSpec((m, w), lambda i: (0, 0))
    colm = lambda w: pl.BlockSpec((w, m), lambda i: (0, 0))
    ret = jax.ShapeDtypeStruct((m, RET_WIDTH), BF16)
    ret_t = jax.ShapeDtypeStruct((RET_WIDTH, m), F32)
    return pl.pallas_call(
        functools.partial(_in_proj_body, pos=pos),
        grid=(1,),
        in_specs=[full(D_MODEL), _const_spec((1, D_MODEL)), _w_in_spec(0),
                  _w_in_spec(1), _const_spec((1, 256)), _const_spec((1, KV_WIDTH)),
                  _const_spec((1, HALF)), _const_spec((256, 256))],
        out_specs=[full(ATT_WIDTH), full(2 * KV_WIDTH)] + [full(RET_WIDTH)] * 4
        + [colm(2 * KV_WIDTH), colm(RET_WIDTH), colm(RET_WIDTH)],
        out_shape=(jax.ShapeDtypeStruct((m, ATT_WIDTH), BF16),
                   jax.ShapeDtypeStruct((m, 2 * KV_WIDTH), F32), ret, ret, ret, ret,
                   jax.ShapeDtypeStruct((2 * KV_WIDTH, m), F32), ret_t, ret_t),
        scratch_shapes=[pltpu.VMEM((m, HALF), F32), pltpu.VMEM((m, HALF), F32)],
        compiler_params=_params(),
        name="in_proj",
    )(x, ln_g, w_in, w_in, qg, kg, inv, ones)


def _this_steps_columns(t_ref, rows, bb):
    shift = lax.rem(LANES - bb * pl.program_id(0), LANES)
    return pltpu.roll(t_ref[rows, :], shift, 1)


def _swa_dec_body(q_ref, kvn_ref, kvt_ref, ck_ref, cv_ref, sink_ref, o_ref, nk_ref, nv_ref):
    bb = q_ref.shape[0]
    npair = ATT_HEADS // 2
    q8 = q_ref[...].astype(F32)
    q8r = pltpu.roll(q8, ATT_HEAD_DIM, 2)
    lane = lax.broadcasted_iota(jnp.int32, q8.shape, 2)
    pair = lax.broadcasted_iota(jnp.int32, q8.shape, 1)
    lo = lane < ATT_HEAD_DIM
    kv0 = pair < npair // ATT_KV_HEADS
    own = lo == kv0
    zero = jnp.zeros_like(q8)
    qe = jnp.where(own, jnp.where(kv0, q8, q8r), zero)
    qo = jnp.where(own, jnp.where(kv0, q8r, q8), zero)
    qb = jnp.concatenate([qe, qo], axis=1)

    ck = ck_ref[...]
    cv = cv_ref[...]
    kn = kvn_ref[:, :KV_WIDTH]
    vn = kvn_ref[:, KV_WIDTH:]
    s = lax.dot_general(qb.astype(BF16), ck.astype(BF16), (((2,), (1,)), ((0,), (0,))),
                        preferred_element_type=F32)
    s_new = jnp.sum(qb * kn[:, None, :], axis=-1, keepdims=True)
    sink = sink_ref[...][None, :, :]
    mx = jnp.maximum(jnp.maximum(jnp.max(s, axis=-1, keepdims=True), s_new), sink)
    p = jnp.exp(s - mx)
    p_new = jnp.exp(s_new - mx)
    den = jnp.sum(p, axis=-1, keepdims=True) + p_new + jnp.exp(sink - mx)
    o = lax.dot_general(p.astype(BF16), cv.astype(BF16), (((2,), (2,)), ((0,), (0,))),
                        preferred_element_type=F32)
    o = (o + p_new * vn[:, None, :]) / den
    oe = o[:, :npair, :]
    oo = o[:, npair:, :]
    oer = pltpu.roll(oe, ATT_HEAD_DIM, 2)
    oor = pltpu.roll(oo, ATT_HEAD_DIM, 2)
    o_ref[...] = jnp.where(lo, jnp.where(kv0, oe, oer), jnp.where(kv0, oor, oo)).astype(BF16)

    newcol = _this_steps_columns(kvt_ref, slice(None), bb)
    last = lax.broadcasted_iota(jnp.int32, (KV_WIDTH, WINDOW), 1) == WINDOW - 1
    for jb in range(bb):
        nk_ref[jb] = jnp.where(last, newcol[:KV_WIDTH, jb:jb + 1],
                               pltpu.roll(ck[jb], WINDOW - 1, 1))
        nv_ref[jb] = jnp.where(last, newcol[KV_WIDTH:, jb:jb + 1],
                               pltpu.roll(cv[jb], WINDOW - 1, 1))


def _swa_dec(q8, kvn, kvt, ck, cv, sink_col):
    b = q8.shape[0]
    bb = BB_ATT
    cache = pl.BlockSpec((bb, KV_WIDTH, WINDOW), lambda i: (i, 0, 0))
    return pl.pallas_call(
        _swa_dec_body,
        grid=(b // bb,),
        in_specs=[pl.BlockSpec((bb, ATT_HEADS // 2, LANES), lambda i: (i, 0, 0)),
                  pl.BlockSpec((bb, 2 * KV_WIDTH), lambda i: (i, 0)),
                  _const_spec(kvt.shape), cache, cache, _const_spec((ATT_HEADS, 1))],
        out_specs=[pl.BlockSpec((bb, ATT_HEADS // 2, LANES), lambda i: (i, 0, 0)), cache, cache],
        out_shape=(jax.ShapeDtypeStruct((b, ATT_HEADS // 2, LANES), BF16),
                   jax.ShapeDtypeStruct(ck.shape, F32),
                   jax.ShapeDtypeStruct(cv.shape, F32)),
        compiler_params=_params(),
        name="swa_dec",
    )(q8, kvn, kvt, ck, cv, sink_col)


def _out_proj_body(x_ref, a_ref, r_ref, w_ref, g_ref, x1_ref, h2_ref):
    x1 = (x_ref[...] + _dot(a_ref[...], w_ref[:ATT_WIDTH, :])
          + _dot(r_ref[...], w_ref[ATT_WIDTH:, :]))
    x1_ref[...] = x1
    var = jnp.mean(x1 * x1, axis=-1, keepdims=True)
    h2_ref[...] = (x1 * lax.rsqrt(var + EPS) * g_ref[...]).astype(BF16)


def _out_proj(x, a, r, w, g, tm):
    m = x.shape[0]
    row = lambda w: pl.BlockSpec((tm, w), lambda i: (i, 0))
    return pl.pallas_call(
        _out_proj_body,
        grid=(m // tm,),
        in_specs=[row(D_MODEL), row(ATT_WIDTH), row(RET_WIDTH),
                  _const_spec(w.shape), _const_spec((1, D_MODEL))],
        out_specs=[row(D_MODEL), row(D_MODEL)],
        out_shape=(jax.ShapeDtypeStruct((m, D_MODEL), F32),
                   jax.ShapeDtypeStruct((m, D_MODEL), BF16)),
        compiler_params=_params(),
        name="out_proj",
    )(x, a, r, w, g)


def _mlp_step(x1_ref, h2_ref, wu_ref, wd_ref, o_ref):
    @pl.when(pl.program_id(1) == 0)
    def _():
        o_ref[...] = x1_ref[...]

    u = jnp.maximum(_dot(h2_ref[...], wu_ref[...]), 0.0)
    o_ref[...] += _dot((u * u).astype(BF16), wd_ref[...])


def _mlp_specs(tm, tf):
    row = pl.BlockSpec((tm, D_MODEL), lambda i, f: (i, 0))
    return [row, row, pl.BlockSpec((D_MODEL, tf), lambda i, f: (0, f)),
            pl.BlockSpec((tf, D_MODEL), lambda i, f: (f, 0))], row


def _mlp(x1, h2, wu, wd, tm):
    m = x1.shape[0]
    in_specs, out_spec = _mlp_specs(tm, TF)
    return pl.pallas_call(
        _mlp_step,
        grid=(m // tm, D_FF // TF),
        in_specs=in_specs,
        out_specs=out_spec,
        out_shape=jax.ShapeDtypeStruct((m, D_MODEL), F32),
        compiler_params=_params(2),
        name="mlp",
    )(x1, h2, wu, wd)


def _mlp_ret_body(x1_ref, h2_ref, wu_ref, wd_ref,
                  qt_ref, kt_ref, rq_ref, rk_ref, rv_ref, gt_ref, g_ref, s_ref,
                  o_ref, r_ref, ns_ref):
    _mlp_step(x1_ref, h2_ref, wu_ref, wd_ref, o_ref)

    b = pl.program_id(0) * pl.num_programs(1) + pl.program_id(1)
    row = pl.ds(b, 1)
    shift = lax.rem(LANES - b, LANES)
    for hh in range(RET_HEADS):
        g1 = math.exp(LOG_G[hh])
        cols = slice(RET_DK * hh, RET_DK * (hh + 1))
        qc = pltpu.roll(qt_ref[cols, :], shift, 1)[:, :1] * g1
        kc = pltpu.roll(kt_ref[cols, :], shift, 1)[:, :1]
        v = rv_ref[row, cols]
        s0 = s_ref[0, hh]
        ns_ref[0, hh] = g1 * s0 + kc * v
        qk = jnp.sum(rq_ref[row, cols] * rk_ref[row, cols], axis=-1, keepdims=True)
        o = jnp.sum(qc * s0, axis=0, keepdims=True) + qk * v
        var = jnp.mean(o * o, axis=-1, keepdims=True)
        r_ref[row, cols] = o * lax.rsqrt(var + EPS) * g_ref[:, cols] * gt_ref[row, cols]


def _mlp_ret(x1, h2, wu, wd, qt, kt, rq, rk, rv, gt, g, state):
    m = x1.shape[0]
    nb = state.shape[0]
    nf = D_FF // TF_RET
    assert m // TM * nf == nb and nb == LANES
    in_specs, out_spec = _mlp_specs(TM, TF_RET)
    st = pl.BlockSpec((1, RET_HEADS, RET_DK, RET_DV), lambda i, f: (i * nf + f, 0, 0, 0))
    rows = _const_spec((nb, RET_WIDTH))
    return pl.pallas_call(
        _mlp_ret_body,
        grid=(m // TM, nf),
        in_specs=in_specs + [_const_spec(qt.shape), _const_spec(kt.shape), rows, rows, rows, rows,
                             _const_spec((1, RET_WIDTH)), st],
        out_specs=[out_spec, pl.BlockSpec((nb, RET_WIDTH), lambda i, f: (0, 0)), st],
        out_shape=(jax.ShapeDtypeStruct((m, D_MODEL), F32),
                   jax.ShapeDtypeStruct((nb, RET_WIDTH), F32),
                   jax.ShapeDtypeStruct(state.shape, F32)),
        compiler_params=_params(2),
        name="mlp_ret",
    )(x1, h2, wu, wd, qt, kt, rq, rk, rv, gt, g, state)


def kernel(x_prompt, x_sample, cache_k_win, cache_v_win, state_ret, ln1_g, w_in, q_norm_g,
           k_norm_g, attn_sinks, ret_norm_g, w_out, ln2_g, w_up, w_down):
    seq = x_prompt.shape[1]
    nb = x_sample.shape[0]
    assert x_prompt.shape[0] == 1 and x_sample.shape[1] == 1 and w_in.shape[0] == 1
    assert seq % TM == 0 and w_in.shape[2] == SPLIT_A + SPLIT_B

    wi = w_in[0].astype(BF16)
    inv = (ROPE_BASE ** (-jnp.arange(HALF, dtype=F32) / HALF)).reshape(1, HALF)

    ln1 = ln1_g.reshape(1, D_MODEL)
    ln2 = ln2_g.reshape(1, D_MODEL)
    qg = jnp.tile(q_norm_g.reshape(1, ATT_HEAD_DIM) * (ATT_HEAD_DIM ** -0.5), (1, 256 // ATT_HEAD_DIM))
    kg = jnp.tile(k_norm_g.reshape(1, ATT_HEAD_DIM), (1, KV_WIDTH // ATT_HEAD_DIM))
    rg_g = ret_norm_g.reshape(1, RET_WIDTH)
    sinks = attn_sinks.reshape(ATT_HEADS)
    blk = jnp.arange(256) // ATT_HEAD_DIM
    ones = (blk[:, None] == blk[None, :]).astype(BF16)

    xs = x_sample[:, 0]
    qs, kvs, rqs, rks, rvs, rgs, kvt, rqt, rkt = _in_proj(xs, ln1, wi, qg, kg, inv, ones,
                                                          pos=PAST_LEN)
    sink_col = jnp.concatenate([sinks[0::2], sinks[1::2]]).reshape(ATT_HEADS, 1)
    to_fm = lambda c: c[0].transpose(0, 2, 3, 1).reshape(nb, KV_WIDTH, WINDOW)
    from_fm = lambda c: c.reshape(nb, ATT_KV_HEADS, ATT_HEAD_DIM, WINDOW).transpose(0, 3, 1, 2)[None]
    a8, nk, nv = _swa_dec(qs.reshape(nb, ATT_HEADS // 2, LANES), kvs, kvt,
                          to_fm(cache_k_win), to_fm(cache_v_win), sink_col)
    a_s = a8.reshape(nb, ATT_WIDTH)

    xp = x_prompt[0]
    a_out, kv, rq, wo, wu = _proj_swa(xp, ln1, wi, qg, kg, inv, ones, sinks, w_out[0], w_up[0])
    r_out, s_fin, wd = _proj_ret(xp, ln1, wi, inv, rq, rg_g, w_down[0])
    x1, h2 = _out_proj(xp, a_out, r_out, wo, ln2, tm=TM)
    f32 = lambda a: a.astype(F32)
    yp, r_s, ns = _mlp_ret(x1, h2, wu, wd, rqt, rkt, f32(rqs), f32(rks), f32(rvs), f32(rgs),
                           rg_g, state_ret[0])

    wb = min(WINDOW, seq)
    kp = kv[seq - wb:, :KV_WIDTH].reshape(1, 1, wb, ATT_KV_HEADS, ATT_HEAD_DIM)
    vp = kv[seq - wb:, KV_WIDTH:].reshape(1, 1, wb, ATT_KV_HEADS, ATT_HEAD_DIM)
    sp = s_fin.reshape(1, 1, RET_HEADS, RET_DK, RET_DV)

    x1s, h2s = _out_proj(xs, a_s, r_s.astype(BF16), wo, ln2, tm=nb)
    ys = _mlp(x1s, h2s, wu, wd, tm=nb)

    return (yp[None], ys[:, None, :], kp, vp, sp, from_fm(nk), from_fm(nv), ns[None])
```

```python
import functools
import math

import jax
import jax.numpy as jnp
from jax import lax
from jax.experimental import pallas as pl
from jax.experimental.pallas import tpu as pltpu

D_MODEL = 2048
ATT_HEADS = 16
ATT_KV_HEADS = 2
ATT_HEAD_DIM = 64
WINDOW = 128
RET_HEADS = 4
RET_DK = 256
RET_DV = 256
RET_CHUNK = 128
ROPE_BASE = 10000.0
D_FF = 4 * D_MODEL
EPS = 1e-6
PAST_LEN = 8192

ATT_WIDTH = ATT_HEADS * ATT_HEAD_DIM
KV_WIDTH = ATT_KV_HEADS * ATT_HEAD_DIM
RET_WIDTH = RET_HEADS * RET_DK
LANES = 128
HALF = RET_DK // 2
SPLIT_A = ATT_WIDTH + 2 * KV_WIDTH + RET_WIDTH
SPLIT_B = 3 * RET_WIDTH

F32 = jnp.float32
BF16 = jnp.bfloat16
VMEM_LIMIT = 60 * 1024 * 1024

TM = 512
TF = 2048
TF_RET = 1024
BB_ATT = 16

LOG_G = tuple(math.log1p(-(2.0 ** (-5.0 - h))) for h in range(RET_HEADS))


def _dot(a, b):
    return jnp.dot(a, b, preferred_element_type=F32)


def _dot_nt(a, b):
    return lax.dot_general(a, b, (((1,), (1,)), ((), ())), preferred_element_type=F32)


def _dot_tn(a, b):
    return lax.dot_general(a, b, (((0,), (0,)), ((), ())), preferred_element_type=F32)


def _const_spec(shape):
    n = len(shape)
    return pl.BlockSpec(shape, lambda *_: (0,) * n, pipeline_mode=pl.Buffered(1))


def _w_in_spec(part):
    width, start = ((SPLIT_A, 0), (SPLIT_B, SPLIT_A))[part]
    return pl.BlockSpec((pl.Element(D_MODEL), pl.Element(width)), lambda *_: (0, start),
                        pipeline_mode=pl.Buffered(1))


def _params(n_axes=1):
    return pltpu.CompilerParams(dimension_semantics=("arbitrary",) * n_axes,
                                vmem_limit_bytes=VMEM_LIMIT)


def _norm_rows(x_ref, g_ref):
    x = x_ref[...]
    var = jnp.mean(x * x, axis=-1, keepdims=True)
    return (x * lax.rsqrt(var + EPS) * g_ref[...]).astype(BF16)


def _bf16_into(o_ref, v):
    return v.astype(BF16).astype(o_ref.dtype)


def _attn_proj(h, w_ref, qg_ref, kg_ref, ones_ref):
    ones = ones_ref[...]
    inv_hd = 1.0 / ATT_HEAD_DIM
    aq = _dot(h, w_ref[:, :ATT_WIDTH])
    q_tiles = []
    for t in range(ATT_WIDTH // 256):
        blk = aq[:, 256 * t:256 * (t + 1)]
        ssq = _dot((blk * blk).astype(BF16), ones)
        q_tiles.append((blk * lax.rsqrt(ssq * inv_hd + EPS) * qg_ref[...]).astype(BF16))
    kvr = _dot(h, w_ref[:, ATT_WIDTH:ATT_WIDTH + 2 * KV_WIDTH])
    k = kvr[:, :KV_WIDTH]
    ssq = _dot((k * k).astype(BF16), ones[:KV_WIDTH, :KV_WIDTH])
    kn = k * lax.rsqrt(ssq * inv_hd + EPS) * kg_ref[...]
    return q_tiles, kn, kvr[:, KV_WIDTH:]


def _rope_init(inv_ref, cr_scr, sr_scr, pos_step):
    tm = cr_scr.shape[0]
    row = lax.broadcasted_iota(jnp.int32, (tm, 1), 0)
    ang_r = (pos_step * row).astype(F32) * inv_ref[...]
    cr_scr[...] = jnp.cos(ang_r)
    sr_scr[...] = jnp.sin(ang_r)


def _rope_tables(inv_ref, cr_scr, sr_scr, base_pos):
    base = jnp.zeros((8, 1), jnp.int32) + base_pos
    ang_b = base.astype(F32) * inv_ref[...]
    cb = jnp.cos(ang_b)[:1]
    sb = jnp.sin(ang_b)[:1]
    cr = cr_scr[...]
    sr = sr_scr[...]
    return cb * cr - sb * sr, sb * cr + cb * sr


def _rope_head(r, cos, sin, scale, o_ref, hh):
    x1 = r[:, :HALF]
    x2 = r[:, HALF:]
    o1 = x1 * cos - x2 * sin
    o2 = x2 * cos + x1 * sin
    if scale != 1.0:
        o1 = o1 * scale
        o2 = o2 * scale
    o_ref[:, RET_DK * hh:RET_DK * hh + HALF] = _bf16_into(o_ref, o1)
    o_ref[:, RET_DK * hh + HALF:RET_DK * (hh + 1)] = _bf16_into(o_ref, o2)


def _rope_proj(h, w, cos, sin, scale, o_ref):
    r = _dot(h, w)
    for hh in range(RET_HEADS):
        _rope_head(r[:, RET_DK * hh:RET_DK * (hh + 1)], cos, sin, scale, o_ref, hh)


def _gate_proj(h, w, o_ref):
    rg = _dot(h, w)
    o_ref[...] = _bf16_into(o_ref, rg / (1.0 + jnp.exp(-rg)))


def _half_split(a):
    lane = lax.broadcasted_iota(jnp.int32, a.shape, 1)
    lo = lane < ATT_HEAD_DIM
    sw = pltpu.roll(a, ATT_HEAD_DIM, 1)
    zero = jnp.zeros_like(a)
    h0 = (jnp.where(lo, a, zero).astype(BF16), jnp.where(lo, zero, sw).astype(BF16))
    h1 = (jnp.where(lo, sw, zero).astype(BF16), jnp.where(lo, zero, a).astype(BF16))
    return h0, h1


SWA_STACKS = 1


def _swa_phases(sink_ref, q_ref, kvc_ref, kvp_ref, o_ref, has_prev):
    nsub = q_ref.shape[0] // WINDOW
    npair = ATT_HEADS // 2
    row = lax.broadcasted_iota(jnp.int32, (WINDOW, 2 * WINDOW), 0)
    col = lax.broadcasted_iota(jnp.int32, (WINDOW, 2 * WINDOW), 1)
    band_cur = (col >= WINDOW) & (col - WINDOW <= row)
    first_col = jnp.where(has_prev, 0, WINDOW)
    lane_lo = lax.broadcasted_iota(jnp.int32, (WINDOW, LANES), 1) < ATT_HEAD_DIM
    ctx = [{} for _ in range(nsub)]
    st = {}

    ppk = npair // ATT_KV_HEADS

    def stacks(g):
        for m in range(g * SWA_STACKS, (g + 1) * SWA_STACKS):
            j, kh = divmod(m, ATT_KV_HEADS)
            yield j, kh, slice(WINDOW * j, WINDOW * (j + 1)), range(ppk * kh, ppk * (kh + 1))

    def prep(g):
        for j in sorted({j for j, _, _, _ in stacks(g)} - {j for j in range(nsub) if ctx[j]}):
            prev = kvp_ref[...] if j == 0 else kvc_ref[WINDOW * (j - 1):WINDOW * j, :]
            cur = kvc_ref[WINDOW * j:WINDOW * (j + 1), :]
            kk = jnp.concatenate([prev[:, :KV_WIDTH], cur[:, :KV_WIDTH]], axis=0)
            vv = jnp.concatenate([prev[:, KV_WIDTH:], cur[:, KV_WIDTH:]], axis=0)
            ctx[j]["k"] = _half_split(kk)
            ctx[j]["v"] = _half_split(vv)
            lo_col = first_col if j == 0 else 0
            ctx[j]["mask"] = band_cur | ((col < WINDOW) & (col >= row) & (col >= lo_col))

    def qk(g):
        for j, kh, rows, tiles in stacks(g):
            qs = jnp.concatenate([q_ref[rows, LANES * t:LANES * (t + 1)] for t in tiles], axis=0)
            s = [_dot_nt(qs, ctx[j]["k"][kh][par]) for par in range(2)]
            for n, t in enumerate(tiles):
                st[j, t] = {"s": [sp[WINDOW * n:WINDOW * (n + 1), :] for sp in s]}

    def softmax(g):
        for j, kh, rows, tiles in stacks(g):
            for t in tiles:
                p, inv = [], []
                for par in range(2):
                    sink = sink_ref[2 * t + par]
                    s = jnp.where(ctx[j]["mask"], st[j, t]["s"][par], -jnp.inf)
                    mx = jnp.maximum(jnp.max(s, axis=-1, keepdims=True), sink)
                    e = jnp.exp(s - mx)
                    den = jnp.sum(e, axis=-1, keepdims=True) + jnp.exp(sink - mx)
                    p.append(e.astype(BF16))
                    inv.append(1.0 / den)
                st[j, t] = {"p": p, "inv": jnp.where(lane_lo, inv[0], inv[1])}

    def pv(g):
        for j, kh, rows, tiles in stacks(g):
            got = [st.pop((j, t)) for t in tiles]
            acc = sum(_dot(jnp.concatenate([u["p"][par] for u in got], axis=0),
                           ctx[j]["v"][kh][par]) for par in range(2))
            for n, t in enumerate(tiles):
                o_ref[rows, LANES * t:LANES * (t + 1)] = (
                    acc[WINDOW * n:WINDOW * (n + 1), :] * got[n]["inv"]).astype(BF16)

    ngroup = nsub * ATT_KV_HEADS // SWA_STACKS
    return ngroup, prep, qk, softmax, pv


def _emit_pipelined(main, ngroup, prep, first, middle, last, finish=None):
    nstage = ngroup + 2
    done = 0
    prep(0)
    for k in range(nstage):
        if k < ngroup:
            first(k)
        if 0 <= k - 2 < ngroup:
            last(k - 2)
        if 0 <= k - 1 < ngroup:
            middle(k - 1)
        if k + 1 < ngroup:
            prep(k + 1)
        upto = -(-len(main) * (k + 1) // nstage)
        for piece in main[done:upto]:
            piece()
        done = upto
    if finish is not None:
        finish()


def _ret_phases(rq_ref, rk_ref, rv_ref, gt_ref, g_ref, o_ref, s_scr):
    c = RET_CHUNK
    nsub = rq_ref.shape[0] // c
    ri = lax.broadcasted_iota(jnp.int32, (c, c), 0)
    ci = lax.broadcasted_iota(jnp.int32, (c, c), 1)
    rel = (ri - ci).astype(F32)
    idx = lax.broadcasted_iota(jnp.int32, (c, 1), 0).astype(F32)
    head = []
    for hh in range(RET_HEADS):
        lg = LOG_G[hh]
        head.append(dict(
            dmask=jnp.where(rel >= 0, jnp.exp(lg * jnp.maximum(rel, 0.0)), 0.0),
            qdec=jnp.exp(lg * (idx + 1.0)), kdec=jnp.exp(lg * (c - 1.0 - idx)),
            cdec=math.exp(lg * c), cols=slice(RET_DK * hh, RET_DK * (hh + 1))))
    st = {}

    def update_state(g):
        for hh, hd in enumerate(head):
            u = st[g, hh]
            s_scr[hh] = hd["cdec"] * u.pop("s_prev") + u.pop("kv")

    def prep(g):
        if g > 0:
            update_state(g - 1)
        rows = slice(c * g, c * (g + 1))
        for hh, hd in enumerate(head):
            q = rq_ref[rows, hd["cols"]]
            k = rk_ref[rows, hd["cols"]]
            s_prev = s_scr[hh]
            st[g, hh] = dict(q=q, k=k, v=rv_ref[rows, hd["cols"]], s_prev=s_prev,
                             qd=(q.astype(F32) * hd["qdec"]).astype(BF16),
                             kd=(k.astype(F32) * hd["kdec"]).astype(BF16),
                             s_bf=s_prev.astype(BF16))

    def first(g):
        for hh in range(RET_HEADS):
            u = st[g, hh]
            u["att"] = _dot_nt(u.pop("q"), u.pop("k"))
            u["inter"] = _dot(u.pop("qd"), u.pop("s_bf"))
            u["kv"] = _dot_tn(u.pop("kd"), u["v"])

    def middle(g):
        for hh, hd in enumerate(head):
            u = st[g, hh]
            u["att"] = (u["att"] * hd["dmask"]).astype(BF16)

    def last(g):
        rows = slice(c * g, c * (g + 1))
        for hh, hd in enumerate(head):
            u = st[g, hh]
            o = _dot(u.pop("att"), u.pop("v")) + u.pop("inter")
            var = jnp.mean(o * o, axis=-1, keepdims=True)
            on = o * lax.rsqrt(var + EPS) * g_ref[:, hd["cols"]]
            o_ref[rows, hd["cols"]] = (on * gt_ref[rows, hd["cols"]].astype(F32)).astype(BF16)

    return nsub, prep, first, middle, last, functools.partial(update_state, nsub - 1)


def _proj_swa_body(sink_ref, x_ref, g_ref, w_ref, qg_ref, kg_ref, inv_ref, ones_ref,
                   wo_ref, wu_ref,
                   a_out, kv_out, rq_out, wo_bf, wu_bf,
                   q_scr, kv_scr, kvp_scr, cr_scr, sr_scr, *, nblk):
    i = pl.program_id(0)
    tm = x_ref.shape[0]

    @pl.when(i == 0)
    def _():
        _rope_init(inv_ref, cr_scr, sr_scr, 1)
        q_scr[...] = jnp.zeros_like(q_scr)
        kv_scr[...] = jnp.zeros_like(kv_scr)
        kvp_scr[...] = jnp.zeros_like(kvp_scr)

    wo_bf[...] = wo_ref[...].astype(BF16)
    wu_bf[...] = wu_ref[...].astype(BF16)

    cur = lax.rem(i, 2)
    prv = 1 - cur
    blk = jnp.minimum(i, nblk - 1)

    h = _norm_rows(x_ref, g_ref)
    ones = ones_ref[...]
    inv_hd = 1.0 / ATT_HEAD_DIM
    cos, sin = _rope_tables(inv_ref, cr_scr, sr_scr, blk * tm)


    def q_piece(n):
        y = _dot(h, w_ref[:, 512 * n:512 * (n + 1)])
        for t in range(2):
            yt = y[:, 256 * t:256 * (t + 1)]
            ssq = _dot((yt * yt).astype(BF16), ones)
            c0 = 512 * n + 256 * t
            q_scr[cur, :, c0:c0 + 256] = (
                yt * lax.rsqrt(ssq * inv_hd + EPS) * qg_ref[...]).astype(BF16)

    kv0 = ATT_WIDTH
    rq0 = ATT_WIDTH + 2 * KV_WIDTH

    def kv_rq_piece():
        y = _dot(h, w_ref[:, kv0:rq0 + RET_DK])
        k = y[:, :KV_WIDTH]
        ssq = _dot((k * k).astype(BF16), ones[:KV_WIDTH, :KV_WIDTH])
        kn = k * lax.rsqrt(ssq * inv_hd + EPS) * kg_ref[...]
        v = y[:, KV_WIDTH:2 * KV_WIDTH]
        kv_scr[cur, :, :KV_WIDTH] = kn
        kv_scr[cur, :, KV_WIDTH:] = v
        kv_out[:, :KV_WIDTH] = kn
        kv_out[:, KV_WIDTH:] = v
        _rope_head(y[:, 2 * KV_WIDTH:], cos, sin, 1.0, rq_out, 0)

    def rq_piece():
        y = _dot(h, w_ref[:, rq0 + RET_DK:])
        for hh in range(1, RET_HEADS):
            _rope_head(y[:, RET_DK * (hh - 1):RET_DK * hh], cos, sin, 1.0, rq_out, hh)

    main = [functools.partial(q_piece, 0), functools.partial(q_piece, 1), kv_rq_piece, rq_piece]
    _emit_pipelined(main, *_swa_phases(sink_ref, q_scr.at[prv], kv_scr.at[prv], kvp_scr, a_out,
                                       has_prev=i > 1))
    kvp_scr[...] = kv_scr[prv, tm - WINDOW:, :]


def _proj_swa(x, ln_g, w_in, qg, kg, inv, ones, sinks, w_out, w_up):
    m = x.shape[0]
    nblk = m // TM
    cl = lambda i: jnp.minimum(i, nblk - 1)
    row = lambda w: pl.BlockSpec((TM, w), lambda i: (cl(i), 0))
    wo_spec = pl.BlockSpec((w_out.shape[0] // nblk, w_out.shape[1]), lambda i: (cl(i), 0))
    wu_spec = pl.BlockSpec((w_up.shape[0], w_up.shape[1] // nblk), lambda i: (0, cl(i)))
    return pl.pallas_call(
        functools.partial(_proj_swa_body, nblk=nblk),
        grid=(nblk + 1,),
        in_specs=[pl.BlockSpec(memory_space=pltpu.SMEM),
                  row(D_MODEL), _const_spec((1, D_MODEL)), _w_in_spec(0),
                  _const_spec((1, 256)), _const_spec((1, KV_WIDTH)),
                  _const_spec((1, HALF)), _const_spec((256, 256)), wo_spec, wu_spec],
        out_specs=[pl.BlockSpec((TM, ATT_WIDTH), lambda i: (jnp.maximum(i - 1, 0), 0)),
                   row(2 * KV_WIDTH), row(RET_WIDTH), wo_spec, wu_spec],
        out_shape=(jax.ShapeDtypeStruct((m, ATT_WIDTH), BF16),
                   jax.ShapeDtypeStruct((m, 2 * KV_WIDTH), F32),
                   jax.ShapeDtypeStruct((m, RET_WIDTH), BF16),
                   jax.ShapeDtypeStruct(w_out.shape, BF16),
                   jax.ShapeDtypeStruct(w_up.shape, BF16)),
        scratch_shapes=[pltpu.VMEM((2, TM, ATT_WIDTH), BF16),
                        pltpu.VMEM((2, TM, 2 * KV_WIDTH), F32),
                        pltpu.VMEM((WINDOW, 2 * KV_WIDTH), F32),
                        pltpu.VMEM((TM, HALF), F32), pltpu.VMEM((TM, HALF), F32)],
        compiler_params=_params(),
        name="proj_swa",
    )(sinks, x, ln_g, w_in, qg, kg, inv, ones, w_out, w_up)


def _proj_ret_body(x_ref, g_ref, w_ref, inv_ref, rq_ref, rg_ref, wd_ref,
                   r_out, s_out, wd_bf,
                   rk_scr, rv_scr, gt_scr, s_scr, cr_scr, sr_scr, *, nblk):
    i = pl.program_id(0)
    tm = x_ref.shape[0]

    @pl.when(i == 0)
    def _():
        _rope_init(inv_ref, cr_scr, sr_scr, 1)
        rk_scr[...] = jnp.zeros_like(rk_scr)
        rv_scr[...] = jnp.zeros_like(rv_scr)
        gt_scr[...] = jnp.zeros_like(gt_scr)
        s_scr[...] = jnp.zeros_like(s_scr)

    wd_bf[...] = wd_ref[...].astype(BF16)

    cur = lax.rem(i, 2)
    prv = 1 - cur
    blk = jnp.minimum(i, nblk - 1)

    h = _norm_rows(x_ref, g_ref)
    cos, sin = _rope_tables(inv_ref, cr_scr, sr_scr, blk * tm)

    def rk_piece(n):
        y = _dot(h, w_ref[:, 512 * n:512 * (n + 1)])
        for t in range(2):
            _rope_head(y[:, RET_DK * t:RET_DK * (t + 1)], cos, sin, RET_DK ** -0.5,
                       rk_scr.at[cur], 2 * n + t)

    def rv_piece(n):
        c = slice(512 * n, 512 * (n + 1))
        rv_scr[cur, :, c] = _dot(h, w_ref[:, RET_WIDTH + 512 * n:RET_WIDTH + 512 * (n + 1)]
                                 ).astype(BF16)

    def gate_piece(n):
        c0 = 2 * RET_WIDTH + 512 * n
        rg = _dot(h, w_ref[:, c0:c0 + 512])
        gt_scr[cur, :, 512 * n:512 * (n + 1)] = (rg / (1.0 + jnp.exp(-rg))).astype(BF16)

    main = [functools.partial(f, n) for f in (rk_piece, rv_piece, gate_piece) for n in range(2)]
    _emit_pipelined(main, *_ret_phases(rq_ref, rk_scr.at[prv], rv_scr.at[prv], gt_scr.at[prv],
                                       rg_ref, r_out, s_scr))

    @pl.when(i == nblk)
    def _():
        s_out[...] = s_scr[...]


def _proj_ret(x, ln_g, w_in, inv, rq, rg_g, w_down):
    m = x.shape[0]
    nblk = m // TM
    cl = lambda i: jnp.minimum(i, nblk - 1)
    prev = lambda w: pl.BlockSpec((TM, w), lambda i: (jnp.maximum(i - 1, 0), 0))
    wd_spec = pl.BlockSpec((w_down.shape[0] // nblk, w_down.shape[1]), lambda i: (cl(i), 0))
    state = (RET_HEADS, RET_DK, RET_DV)
    slot = pltpu.VMEM((2, TM, RET_WIDTH), BF16)
    return pl.pallas_call(
        functools.partial(_proj_ret_body, nblk=nblk),
        grid=(nblk + 1,),
        in_specs=[pl.BlockSpec((TM, D_MODEL), lambda i: (cl(i), 0)),
                  _const_spec((1, D_MODEL)), _w_in_spec(1), _const_spec((1, HALF)),
                  prev(RET_WIDTH), _const_spec((1, RET_WIDTH)), wd_spec],
        out_specs=[prev(RET_WIDTH), pl.BlockSpec(state, lambda i: (0, 0, 0)), wd_spec],
        out_shape=(jax.ShapeDtypeStruct((m, RET_WIDTH), BF16),
                   jax.ShapeDtypeStruct(state, F32),
                   jax.ShapeDtypeStruct(w_down.shape, BF16)),
        scratch_shapes=[slot, slot, slot, pltpu.VMEM(state, F32),
                        pltpu.VMEM((TM, HALF), F32), pltpu.VMEM((TM, HALF), F32)],
        compiler_params=_params(),
        name="proj_ret",
    )(x, ln_g, w_in, inv, rq, rg_g, w_down)


def _in_proj_body(x_ref, g_ref, wa_ref, wb_ref, qg_ref, kg_ref, inv_ref, ones_ref,
                  q_out, kv_out, rq_out, rk_out, rv_out, gt_out, kvt_out, rqt_out, rkt_out,
                  cr_scr, sr_scr, *, pos):
    _rope_init(inv_ref, cr_scr, sr_scr, 0)
    h = _norm_rows(x_ref, g_ref)
    q_tiles, kn, v = _attn_proj(h, wa_ref, qg_ref, kg_ref, ones_ref)
    for t, qt in enumerate(q_tiles):
        q_out[:, 256 * t:256 * (t + 1)] = qt
    kv_out[:, :KV_WIDTH] = kn
    kv_out[:, KV_WIDTH:] = v
    cos, sin = _rope_tables(inv_ref, cr_scr, sr_scr, pos)
    _rope_proj(h, wa_ref[:, ATT_WIDTH + 2 * KV_WIDTH:], cos, sin, 1.0, rq_out)
    _rope_proj(h, wb_ref[:, :RET_WIDTH], cos, sin, RET_DK ** -0.5, rk_out)
    rv_out[...] = _bf16_into(rv_out, _dot(h, wb_ref[:, RET_WIDTH:2 * RET_WIDTH]))
    _gate_proj(h, wb_ref[:, 2 * RET_WIDTH:], gt_out)
    kvt_out[...] = kv_out[...].T
    rqt_out[...] = rq_out[...].T
    rkt_out[...] = rk_out[...].T


def _in_proj(x, ln_g, w_in, qg, kg, inv, ones, pos):
    m = x.shape[0]
    assert m == LANES
    full = lambda w: pl.BlockSpec((m, w), lambda i: (0, 0))
    colm = lambda w: pl.BlockSpec((w, m), lambda i: (0, 0))
    ret = jax.ShapeDtypeStruct((m, RET_WIDTH), F32)
    ret_t = jax.ShapeDtypeStruct((RET_WIDTH, m), F32)
    return pl.pallas_call(
        functools.partial(_in_proj_body, pos=pos),
        grid=(1,),
        in_specs=[full(D_MODEL), _const_spec((1, D_MODEL)), _w_in_spec(0),
                  _w_in_spec(1), _const_spec((1, 256)), _const_spec((1, KV_WIDTH)),
                  _const_spec((1, HALF)), _const_spec((256, 256))],
        out_specs=[full(ATT_WIDTH), full(2 * KV_WIDTH)] + [full(RET_WIDTH)] * 4
        + [colm(2 * KV_WIDTH), colm(RET_WIDTH), colm(RET_WIDTH)],
        out_shape=(jax.ShapeDtypeStruct((m, ATT_WIDTH), BF16),
                   jax.ShapeDtypeStruct((m, 2 * KV_WIDTH), F32), ret, ret, ret, ret,
                   jax.ShapeDtypeStruct((2 * KV_WIDTH, m), F32), ret_t, ret_t),
        scratch_shapes=[pltpu.VMEM((m, HALF), F32), pltpu.VMEM((m, HALF), F32)],
        compiler_params=_params(),
        name="in_proj",
    )(x, ln_g, w_in, w_in, qg, kg, inv, ones)


def _this_steps_columns(t_ref, rows, bb):
    shift = lax.rem(LANES - bb * pl.program_id(0), LANES)
    return pltpu.roll(t_ref[rows, :], shift, 1)


def _swa_dec_body(q_ref, kvn_ref, kvt_ref, ck_ref, cv_ref, sink_ref, o_ref, nk_ref, nv_ref):
    bb = q_ref.shape[0]
    npair = ATT_HEADS // 2
    q8 = q_ref[...].astype(F32)
    q8r = pltpu.roll(q8, ATT_HEAD_DIM, 2)
    lane = lax.broadcasted_iota(jnp.int32, q8.shape, 2)
    pair = lax.broadcasted_iota(jnp.int32, q8.shape, 1)
    lo = lane < ATT_HEAD_DIM
    kv0 = pair < npair // ATT_KV_HEADS
    own = lo == kv0
    zero = jnp.zeros_like(q8)
    qe = jnp.where(own, jnp.where(kv0, q8, q8r), zero)
    qo = jnp.where(own, jnp.where(kv0, q8r, q8), zero)
    qb = jnp.concatenate([qe, qo], axis=1)

    ck = ck_ref[...]
    cv = cv_ref[...]
    kn = kvn_ref[:, :KV_WIDTH]
    vn = kvn_ref[:, KV_WIDTH:]
    s = lax.dot_general(qb.astype(BF16), ck.astype(BF16), (((2,), (1,)), ((0,), (0,))),
                        preferred_element_type=F32)
    s_new = jnp.sum(qb * kn[:, None, :], axis=-1, keepdims=True)
    sink = sink_ref[...][None, :, :]
    mx = jnp.maximum(jnp.maximum(jnp.max(s, axis=-1, keepdims=True), s_new), sink)
    p = jnp.exp(s - mx)
    p_new = jnp.exp(s_new - mx)
    den = jnp.sum(p, axis=-1, keepdims=True) + p_new + jnp.exp(sink - mx)
    o = lax.dot_general(p.astype(BF16), cv.astype(BF16), (((2,), (2,)), ((0,), (0,))),
                        preferred_element_type=F32)
    o = (o + p_new * vn[:, None, :]) / den
    oe = o[:, :npair, :]
    oo = o[:, npair:, :]
    oer = pltpu.roll(oe, ATT_HEAD_DIM, 2)
    oor = pltpu.roll(oo, ATT_HEAD_DIM, 2)
    o_ref[...] = jnp.where(lo, jnp.where(kv0, oe, oer), jnp.where(kv0, oor, oo)).astype(BF16)

    newcol = _this_steps_columns(kvt_ref, slice(None), bb)
    last = lax.broadcasted_iota(jnp.int32, (KV_WIDTH, WINDOW), 1) == WINDOW - 1
    for jb in range(bb):
        nk_ref[jb] = jnp.where(last, newcol[:KV_WIDTH, jb:jb + 1],
                               pltpu.roll(ck[jb], WINDOW - 1, 1))
        nv_ref[jb] = jnp.where(last, newcol[KV_WIDTH:, jb:jb + 1],
                               pltpu.roll(cv[jb], WINDOW - 1, 1))


def _swa_dec(q8, kvn, kvt, ck, cv, sink_col):
    b = q8.shape[0]
    bb = BB_ATT
    cache = pl.BlockSpec((bb, KV_WIDTH, WINDOW), lambda i: (i, 0, 0))
    return pl.pallas_call(
        _swa_dec_body,
        grid=(b // bb,),
        in_specs=[pl.BlockSpec((bb, ATT_HEADS // 2, LANES), lambda i: (i, 0, 0)),
                  pl.BlockSpec((bb, 2 * KV_WIDTH), lambda i: (i, 0)),
                  _const_spec(kvt.shape), cache, cache, _const_spec((ATT_HEADS, 1))],
        out_specs=[pl.BlockSpec((bb, ATT_HEADS // 2, LANES), lambda i: (i, 0, 0)), cache, cache],
        out_shape=(jax.ShapeDtypeStruct((b, ATT_HEADS // 2, LANES), BF16),
                   jax.ShapeDtypeStruct(ck.shape, F32),
                   jax.ShapeDtypeStruct(cv.shape, F32)),
        compiler_params=_params(),
        name="swa_dec",
    )(q8, kvn, kvt, ck, cv, sink_col)


def _out_proj_body(x_ref, a_ref, r_ref, w_ref, g_ref, x1_ref, h2_ref):
    x1 = (x_ref[...] + _dot(a_ref[...].astype(BF16), w_ref[:ATT_WIDTH, :])
          + _dot(r_ref[...].astype(BF16), w_ref[ATT_WIDTH:, :]))
    x1_ref[...] = x1
    var = jnp.mean(x1 * x1, axis=-1, keepdims=True)
    h2_ref[...] = (x1 * lax.rsqrt(var + EPS) * g_ref[...]).astype(BF16)


def _out_proj(x, a, r, w, g, tm):
    m = x.shape[0]
    row = lambda w: pl.BlockSpec((tm, w), lambda i: (i, 0))
    return pl.pallas_call(
        _out_proj_body,
        grid=(m // tm,),
        in_specs=[row(D_MODEL), row(ATT_WIDTH), row(RET_WIDTH),
                  _const_spec(w.shape), _const_spec((1, D_MODEL))],
        out_specs=[row(D_MODEL), row(D_MODEL)],
        out_shape=(jax.ShapeDtypeStruct((m, D_MODEL), F32),
                   jax.ShapeDtypeStruct((m, D_MODEL), BF16)),
        compiler_params=_params(),
        name="out_proj",
    )(x, a, r, w, g)


def _mlp_step(x1_ref, h2_ref, wu_ref, wd_ref, o_ref):
    @pl.when(pl.program_id(1) == 0)
    def _():
        o_ref[...] = x1_ref[...]

    u = jnp.maximum(_dot(h2_ref[...], wu_ref[...]), 0.0)
    o_ref[...] += _dot((u * u).astype(BF16), wd_ref[...])


def _mlp_specs(tm, tf):
    row = pl.BlockSpec((tm, D_MODEL), lambda i, f: (i, 0))
    return [row, row, pl.BlockSpec((D_MODEL, tf), lambda i, f: (0, f)),
            pl.BlockSpec((tf, D_MODEL), lambda i, f: (f, 0))], row


def _mlp(x1, h2, wu, wd, tm):
    m = x1.shape[0]
    in_specs, out_spec = _mlp_specs(tm, TF)
    return pl.pallas_call(
        _mlp_step,
        grid=(m // tm, D_FF // TF),
        in_specs=in_specs,
        out_specs=out_spec,
        out_shape=jax.ShapeDtypeStruct((m, D_MODEL), F32),
        compiler_params=_params(2),
        name="mlp",
    )(x1, h2, wu, wd)


def _mlp_ret_body(x1_ref, h2_ref, wu_ref, wd_ref,
                  qt_ref, kt_ref, rq_ref, rk_ref, rv_ref, gt_ref, g_ref, s_ref,
                  o_ref, r_ref, ns_ref):
    _mlp_step(x1_ref, h2_ref, wu_ref, wd_ref, o_ref)

    b = pl.program_id(0) * pl.num_programs(1) + pl.program_id(1)
    row = pl.ds(b, 1)
    shift = lax.rem(LANES - b, LANES)
    for hh in range(RET_HEADS):
        g1 = math.exp(LOG_G[hh])
        cols = slice(RET_DK * hh, RET_DK * (hh + 1))
        qc = pltpu.roll(qt_ref[cols, :], shift, 1)[:, :1] * g1
        kc = pltpu.roll(kt_ref[cols, :], shift, 1)[:, :1]
        v = rv_ref[row, cols]
        s0 = s_ref[0, hh]
        ns_ref[0, hh] = g1 * s0 + kc * v
        qk = jnp.sum(rq_ref[row, cols] * rk_ref[row, cols], axis=-1, keepdims=True)
        o = jnp.sum(qc * s0, axis=0, keepdims=True) + qk * v
        var = jnp.mean(o * o, axis=-1, keepdims=True)
        r_ref[row, cols] = o * lax.rsqrt(var + EPS) * g_ref[:, cols] * gt_ref[row, cols]


def _mlp_ret(x1, h2, wu, wd, qt, kt, rq, rk, rv, gt, g, state):
    m = x1.shape[0]
    nb = state.shape[0]
    nf = D_FF // TF_RET
    assert m // TM * nf == nb and nb == LANES
    in_specs, out_spec = _mlp_specs(TM, TF_RET)
    st = pl.BlockSpec((1, RET_HEADS, RET_DK, RET_DV), lambda i, f: (i * nf + f, 0, 0, 0))
    rows = _const_spec((nb, RET_WIDTH))
    return pl.pallas_call(
        _mlp_ret_body,
        grid=(m // TM, nf),
        in_specs=in_specs + [_const_spec(qt.shape), _const_spec(kt.shape), rows, rows, rows, rows,
                             _const_spec((1, RET_WIDTH)), st],
        out_specs=[out_spec, pl.BlockSpec((nb, RET_WIDTH), lambda i, f: (0, 0)), st],
        out_shape=(jax.ShapeDtypeStruct((m, D_MODEL), F32),
                   jax.ShapeDtypeStruct((nb, RET_WIDTH), F32),
                   jax.ShapeDtypeStruct(state.shape, F32)),
        compiler_params=_params(2),
        name="mlp_ret",
    )(x1, h2, wu, wd, qt, kt, rq, rk, rv, gt, g, state)


def kernel(x_prompt, x_sample, cache_k_win, cache_v_win, state_ret, ln1_g, w_in, q_norm_g,
           k_norm_g, attn_sinks, ret_norm_g, w_out, ln2_g, w_up, w_down):
    seq = x_prompt.shape[1]
    nb = x_sample.shape[0]
    assert x_prompt.shape[0] == 1 and x_sample.shape[1] == 1 and w_in.shape[0] == 1
    assert seq % TM == 0 and w_in.shape[2] == SPLIT_A + SPLIT_B

    wi = w_in[0].astype(BF16)
    inv = (ROPE_BASE ** (-jnp.arange(HALF, dtype=F32) / HALF)).reshape(1, HALF)

    ln1 = ln1_g.reshape(1, D_MODEL)
    ln2 = ln2_g.reshape(1, D_MODEL)
    qg = jnp.tile(q_norm_g.reshape(1, ATT_HEAD_DIM) * (ATT_HEAD_DIM ** -0.5), (1, 256 // ATT_HEAD_DIM))
    kg = jnp.tile(k_norm_g.reshape(1, ATT_HEAD_DIM), (1, KV_WIDTH // ATT_HEAD_DIM))
    rg_g = ret_norm_g.reshape(1, RET_WIDTH)
    sinks = attn_sinks.reshape(ATT_HEADS)
    blk = jnp.arange(256) // ATT_HEAD_DIM
    ones = (blk[:, None] == blk[None, :]).astype(BF16)

    xs = x_sample[:, 0]
    qs, kvs, rqs, rks, rvs, rgs, kvt, rqt, rkt = _in_proj(xs, ln1, wi, qg, kg, inv, ones,
                                                          pos=PAST_LEN)
    sink_col = jnp.concatenate([sinks[0::2], sinks[1::2]]).reshape(ATT_HEADS, 1)
    to_fm = lambda c: c[0].transpose(0, 2, 3, 1).reshape(nb, KV_WIDTH, WINDOW)
    from_fm = lambda c: c.reshape(nb, ATT_KV_HEADS, ATT_HEAD_DIM, WINDOW).transpose(0, 3, 1, 2)[None]
    a8, nk, nv = _swa_dec(qs.reshape(nb, ATT_HEADS // 2, LANES), kvs, kvt,
                          to_fm(cache_k_win), to_fm(cache_v_win), sink_col)
    a_s = a8.reshape(nb, ATT_WIDTH)

    xp = x_prompt[0]
    a_out, kv, rq, wo, wu = _proj_swa(xp, ln1, wi, qg, kg, inv, ones, sinks, w_out[0], w_up[0])
    r_out, s_fin, wd = _proj_ret(xp, ln1, wi, inv, rq, rg_g, w_down[0])
    x1, h2 = _out_proj(xp, a_out, r_out, wo, ln2, tm=TM)
    yp, r_s, ns = _mlp_ret(x1, h2, wu, wd, rqt, rkt, rqs, rks, rvs, rgs, rg_g, state_ret[0])

    wb = min(WINDOW, seq)
    kp = kv[seq - wb:, :KV_WIDTH].reshape(1, 1, wb, ATT_KV_HEADS, ATT_HEAD_DIM)
    vp = kv[seq - wb:, KV_WIDTH:].reshape(1, 1, wb, ATT_KV_HEADS, ATT_HEAD_DIM)
    sp = s_fin.reshape(1, 1, RET_HEADS, RET_DK, RET_DV)

    x1s, h2s = _out_proj(xs, a_s, r_s, wo, ln2, tm=nb)
    ys = _mlp(x1s, h2s, wu, wd, tm=nb)

    return (yp[None], ys[:, None, :], kp, vp, sp, from_fm(nk), from_fm(nv), ns[None])
```

```python
import functools
import math

import jax
import jax.numpy as jnp
from jax import lax
from jax.experimental import pallas as pl
from jax.experimental.pallas import tpu as pltpu

D_MODEL = 2048
ATT_HEADS = 16
ATT_KV_HEADS = 2
ATT_HEAD_DIM = 64
WINDOW = 128
RET_HEADS = 4
RET_DK = 256
RET_DV = 256
RET_CHUNK = 128
ROPE_BASE = 10000.0
D_FF = 4 * D_MODEL
EPS = 1e-6
PAST_LEN = 8192

ATT_WIDTH = ATT_HEADS * ATT_HEAD_DIM
KV_WIDTH = ATT_KV_HEADS * ATT_HEAD_DIM
RET_WIDTH = RET_HEADS * RET_DK
LANES = 128
HALF = RET_DK // 2
SPLIT_A = ATT_WIDTH + 2 * KV_WIDTH + RET_WIDTH
SPLIT_B = 3 * RET_WIDTH

F32 = jnp.float32
BF16 = jnp.bfloat16
VMEM_LIMIT = 60 * 1024 * 1024

TM = 512
TF = 2048
TF_RET = 1024
BB_ATT = 16

LOG_G = tuple(math.log1p(-(2.0 ** (-5.0 - h))) for h in range(RET_HEADS))


def _dot(a, b):
    return jnp.dot(a, b, preferred_element_type=F32)


def _dot_nt(a, b):
    return lax.dot_general(a, b, (((1,), (1,)), ((), ())), preferred_element_type=F32)


def _dot_tn(a, b):
    return lax.dot_general(a, b, (((0,), (0,)), ((), ())), preferred_element_type=F32)


def _const_spec(shape):
    n = len(shape)
    return pl.BlockSpec(shape, lambda *_: (0,) * n, pipeline_mode=pl.Buffered(1))


def _w_in_spec(part):
    width, start = ((SPLIT_A, 0), (SPLIT_B, SPLIT_A))[part]
    return pl.BlockSpec((pl.Element(D_MODEL), pl.Element(width)), lambda *_: (0, start),
                        pipeline_mode=pl.Buffered(1))


def _params(n_axes=1):
    return pltpu.CompilerParams(dimension_semantics=("arbitrary",) * n_axes,
                                vmem_limit_bytes=VMEM_LIMIT)


def _norm_rows(x_ref, g_ref):
    x = x_ref[...]
    var = jnp.mean(x * x, axis=-1, keepdims=True)
    return (x * lax.rsqrt(var + EPS) * g_ref[...]).astype(BF16)


def _bf16_into(o_ref, v):
    return v.astype(BF16).astype(o_ref.dtype)


def _rope_init(inv_ref, cr_scr, sr_scr, pos_step):
    tm = cr_scr.shape[0]
    row = lax.broadcasted_iota(jnp.int32, (tm, 1), 0)
    ang_r = (pos_step * row).astype(F32) * inv_ref[...]
    cr_scr[...] = jnp.cos(ang_r)
    sr_scr[...] = jnp.sin(ang_r)


def _rope_tables(inv_ref, cr_scr, sr_scr, base_pos):
    base = jnp.zeros((8, 1), jnp.int32) + base_pos
    ang_b = base.astype(F32) * inv_ref[...]
    cb = jnp.cos(ang_b)[:1]
    sb = jnp.sin(ang_b)[:1]
    cr = cr_scr[...]
    sr = sr_scr[...]
    return cb * cr - sb * sr, sb * cr + cb * sr


def _rope_head(r, cos, sin, scale, o_ref, hh):
    x1 = r[:, :HALF]
    x2 = r[:, HALF:]
    o1 = x1 * cos - x2 * sin
    o2 = x2 * cos + x1 * sin
    if scale != 1.0:
        o1 = o1 * scale
        o2 = o2 * scale
    o_ref[:, RET_DK * hh:RET_DK * hh + HALF] = _bf16_into(o_ref, o1)
    o_ref[:, RET_DK * hh + HALF:RET_DK * (hh + 1)] = _bf16_into(o_ref, o2)


def _half_split(a):
    lane = lax.broadcasted_iota(jnp.int32, a.shape, 1)
    lo = lane < ATT_HEAD_DIM
    sw = pltpu.roll(a, ATT_HEAD_DIM, 1)
    zero = jnp.zeros_like(a)
    h0 = (jnp.where(lo, a, zero).astype(BF16), jnp.where(lo, zero, sw).astype(BF16))
    h1 = (jnp.where(lo, sw, zero).astype(BF16), jnp.where(lo, zero, a).astype(BF16))
    return h0, h1


SWA_STACKS = 1


def _swa_phases(sink_ref, q_ref, kvc_ref, kvp_ref, o_ref, has_prev):
    nsub = q_ref.shape[0] // WINDOW
    npair = ATT_HEADS // 2
    row = lax.broadcasted_iota(jnp.int32, (WINDOW, 2 * WINDOW), 0)
    col = lax.broadcasted_iota(jnp.int32, (WINDOW, 2 * WINDOW), 1)
    band_cur = (col >= WINDOW) & (col - WINDOW <= row)
    first_col = jnp.where(has_prev, 0, WINDOW)
    lane_lo = lax.broadcasted_iota(jnp.int32, (WINDOW, LANES), 1) < ATT_HEAD_DIM
    ctx = [{} for _ in range(nsub)]
    st = {}

    ppk = npair // ATT_KV_HEADS

    def stacks(g):
        for m in range(g * SWA_STACKS, (g + 1) * SWA_STACKS):
            j, kh = divmod(m, ATT_KV_HEADS)
            yield j, kh, slice(WINDOW * j, WINDOW * (j + 1)), range(ppk * kh, ppk * (kh + 1))

    def prep(g):
        for j in sorted({j for j, _, _, _ in stacks(g)} - {j for j in range(nsub) if ctx[j]}):
            prev = kvp_ref[...] if j == 0 else kvc_ref[WINDOW * (j - 1):WINDOW * j, :]
            cur = kvc_ref[WINDOW * j:WINDOW * (j + 1), :]
            kk = jnp.concatenate([prev[:, :KV_WIDTH], cur[:, :KV_WIDTH]], axis=0)
            vv = jnp.concatenate([prev[:, KV_WIDTH:], cur[:, KV_WIDTH:]], axis=0)
            ctx[j]["k"] = _half_split(kk)
            ctx[j]["v"] = _half_split(vv)
            lo_col = first_col if j == 0 else 0
            ctx[j]["mask"] = band_cur | ((col < WINDOW) & (col >= row) & (col >= lo_col))

    def qk(g):
        for j, kh, rows, tiles in stacks(g):
            qs = jnp.concatenate([q_ref[rows, LANES * t:LANES * (t + 1)] for t in tiles], axis=0)
            s = [_dot_nt(qs, ctx[j]["k"][kh][par]) for par in range(2)]
            for n, t in enumerate(tiles):
                st[j, t] = {"s": [sp[WINDOW * n:WINDOW * (n + 1), :] for sp in s]}

    def softmax(g):
        for j, kh, rows, tiles in stacks(g):
            for t in tiles:
                p, inv = [], []
                for par in range(2):
                    sink = sink_ref[2 * t + par]
                    s = jnp.where(ctx[j]["mask"], st[j, t]["s"][par], -jnp.inf)
                    mx = jnp.maximum(jnp.max(s, axis=-1, keepdims=True), sink)
                    e = jnp.exp(s - mx)
                    den = jnp.sum(e, axis=-1, keepdims=True) + jnp.exp(sink - mx)
                    p.append(e.astype(BF16))
                    inv.append(1.0 / den)
                st[j, t] = {"p": p, "inv": jnp.where(lane_lo, inv[0], inv[1])}

    def pv(g):
        for j, kh, rows, tiles in stacks(g):
            got = [st.pop((j, t)) for t in tiles]
            acc = sum(_dot(jnp.concatenate([u["p"][par] for u in got], axis=0),
                           ctx[j]["v"][kh][par]) for par in range(2))
            for n, t in enumerate(tiles):
                o_ref[rows, LANES * t:LANES * (t + 1)] = (
                    acc[WINDOW * n:WINDOW * (n + 1), :] * got[n]["inv"]).astype(BF16)

    ngroup = nsub * ATT_KV_HEADS // SWA_STACKS
    return ngroup, prep, qk, softmax, pv


def _emit_pipelined(main, ngroup, prep, first, middle, last, finish=None):
    nstage = ngroup + 2
    done = 0
    prep(0)
    for k in range(nstage):
        if k < ngroup:
            first(k)
        if 0 <= k - 2 < ngroup:
            last(k - 2)
        if 0 <= k - 1 < ngroup:
            middle(k - 1)
        if k + 1 < ngroup:
            prep(k + 1)
        upto = -(-len(main) * (k + 1) // nstage)
        for piece in main[done:upto]:
            piece()
        done = upto
    if finish is not None:
        finish()


def _ret_phases(rq_ref, rk_ref, rv_ref, gt_ref, g_ref, o_ref, s_scr):
    c = RET_CHUNK
    nsub = rq_ref.shape[0] // c
    ri = lax.broadcasted_iota(jnp.int32, (c, c), 0)
    ci = lax.broadcasted_iota(jnp.int32, (c, c), 1)
    rel = (ri - ci).astype(F32)
    idx = lax.broadcasted_iota(jnp.int32, (c, 1), 0).astype(F32)
    head = []
    for hh in range(RET_HEADS):
        lg = LOG_G[hh]
        head.append(dict(
            dmask=jnp.where(rel >= 0, jnp.exp(lg * jnp.maximum(rel, 0.0)), 0.0),
            qdec=jnp.exp(lg * (idx + 1.0)), kdec=jnp.exp(lg * (c - 1.0 - idx)),
            cdec=math.exp(lg * c), cols=slice(RET_DK * hh, RET_DK * (hh + 1))))
    st = {}

    def update_state(g):
        for hh, hd in enumerate(head):
            u = st[g, hh]
            s_scr[hh] = hd["cdec"] * u.pop("s_prev") + u.pop("kv")

    def prep(g):
        if g > 0:
            update_state(g - 1)
        rows = slice(c * g, c * (g + 1))
        for hh, hd in enumerate(head):
            q = rq_ref[rows, hd["cols"]]
            k = rk_ref[rows, hd["cols"]]
            s_prev = s_scr[hh]
            st[g, hh] = dict(q=q, k=k, v=rv_ref[rows, hd["cols"]], s_prev=s_prev,
                             qd=(q.astype(F32) * hd["qdec"]).astype(BF16),
                             kd=(k.astype(F32) * hd["kdec"]).astype(BF16),
                             s_bf=s_prev.astype(BF16))

    def first(g):
        for hh in range(RET_HEADS):
            u = st[g, hh]
            u["att"] = _dot_nt(u.pop("q"), u.pop("k"))
            u["inter"] = _dot(u.pop("qd"), u.pop("s_bf"))
            u["kv"] = _dot_tn(u.pop("kd"), u["v"])

    def middle(g):
        for hh, hd in enumerate(head):
            u = st[g, hh]
            u["att"] = (u["att"] * hd["dmask"]).astype(BF16)

    def last(g):
        rows = slice(c * g, c * (g + 1))
        for hh, hd in enumerate(head):
            u = st[g, hh]
            o = _dot(u.pop("att"), u.pop("v")) + u.pop("inter")
            var = jnp.mean(o * o, axis=-1, keepdims=True)
            on = o * lax.rsqrt(var + EPS) * g_ref[:, hd["cols"]]
            o_ref[rows, hd["cols"]] = (on * gt_ref[rows, hd["cols"]].astype(F32)).astype(BF16)

    return nsub, prep, first, middle, last, functools.partial(update_state, nsub - 1)


def _proj_swa_body(sink_ref, x_ref, g_ref, w_ref, qg_ref, kg_ref, inv_ref, ones_ref,
                   wo_ref, wu_ref,
                   a_out, kv_out, rq_out, wo_bf, wu_bf,
                   q_scr, kv_scr, kvp_scr, cr_scr, sr_scr, *, nblk):
    i = pl.program_id(0)
    tm = x_ref.shape[0]

    @pl.when(i == 0)
    def _():
        _rope_init(inv_ref, cr_scr, sr_scr, 1)
        q_scr[...] = jnp.zeros_like(q_scr)
        kv_scr[...] = jnp.zeros_like(kv_scr)
        kvp_scr[...] = jnp.zeros_like(kvp_scr)

    wo_bf[...] = wo_ref[...].astype(BF16)
    wu_bf[...] = wu_ref[...].astype(BF16)

    cur = lax.rem(i, 2)
    prv = 1 - cur
    blk = jnp.minimum(i, nblk - 1)

    h = _norm_rows(x_ref, g_ref)
    ones = ones_ref[...]
    inv_hd = 1.0 / ATT_HEAD_DIM
    cos, sin = _rope_tables(inv_ref, cr_scr, sr_scr, blk * tm)


    def q_piece(n):
        y = _dot(h, w_ref[:, 512 * n:512 * (n + 1)])
        for t in range(2):
            yt = y[:, 256 * t:256 * (t + 1)]
            ssq = _dot((yt * yt).astype(BF16), ones)
            c0 = 512 * n + 256 * t
            q_scr[cur, :, c0:c0 + 256] = (
                yt * lax.rsqrt(ssq * inv_hd + EPS) * qg_ref[...]).astype(BF16)

    kv0 = ATT_WIDTH
    rq0 = ATT_WIDTH + 2 * KV_WIDTH

    def kv_rq_piece():
        y = _dot(h, w_ref[:, kv0:rq0 + RET_DK])
        k = y[:, :KV_WIDTH]
        ssq = _dot((k * k).astype(BF16), ones[:KV_WIDTH, :KV_WIDTH])
        kn = k * lax.rsqrt(ssq * inv_hd + EPS) * kg_ref[...]
        v = y[:, KV_WIDTH:2 * KV_WIDTH]
        kv_scr[cur, :, :KV_WIDTH] = kn
        kv_scr[cur, :, KV_WIDTH:] = v
        kv_out[:, :KV_WIDTH] = kn
        kv_out[:, KV_WIDTH:] = v
        _rope_head(y[:, 2 * KV_WIDTH:], cos, sin, 1.0, rq_out, 0)

    def rq_piece():
        y = _dot(h, w_ref[:, rq0 + RET_DK:])
        for hh in range(1, RET_HEADS):
            _rope_head(y[:, RET_DK * (hh - 1):RET_DK * hh], cos, sin, 1.0, rq_out, hh)

    main = [functools.partial(q_piece, 0), functools.partial(q_piece, 1), kv_rq_piece, rq_piece]
    _emit_pipelined(main, *_swa_phases(sink_ref, q_scr.at[prv], kv_scr.at[prv], kvp_scr, a_out,
                                       has_prev=i > 1))
    kvp_scr[...] = kv_scr[prv, tm - WINDOW:, :]


def _proj_swa(x, ln_g, w_in, qg, kg, inv, ones, sinks, w_out, w_up):
    m = x.shape[0]
    nblk = m // TM
    cl = lambda i: jnp.minimum(i, nblk - 1)
    row = lambda w: pl.BlockSpec((TM, w), lambda i: (cl(i), 0))
    wo_spec = pl.BlockSpec((w_out.shape[0] // nblk, w_out.shape[1]), lambda i: (cl(i), 0))
    wu_spec = pl.BlockSpec((w_up.shape[0], w_up.shape[1] // nblk), lambda i: (0, cl(i)))
    return pl.pallas_call(
        functools.partial(_proj_swa_body, nblk=nblk),
        grid=(nblk + 1,),
        in_specs=[pl.BlockSpec(memory_space=pltpu.SMEM),
                  row(D_MODEL), _const_spec((1, D_MODEL)), _w_in_spec(0),
                  _const_spec((1, 256)), _const_spec((1, KV_WIDTH)),
                  _const_spec((1, HALF)), _const_spec((256, 256)), wo_spec, wu_spec],
        out_specs=[pl.BlockSpec((TM, ATT_WIDTH), lambda i: (jnp.maximum(i - 1, 0), 0)),
                   row(2 * KV_WIDTH), row(RET_WIDTH), wo_spec, wu_spec],
        out_shape=(jax.ShapeDtypeStruct((m, ATT_WIDTH), BF16),
                   jax.ShapeDtypeStruct((m, 2 * KV_WIDTH), F32),
                   jax.ShapeDtypeStruct((m, RET_WIDTH), BF16),
                   jax.ShapeDtypeStruct(w_out.shape, BF16),
                   jax.ShapeDtypeStruct(w_up.shape, BF16)),
        scratch_shapes=[pltpu.VMEM((2, TM, ATT_WIDTH), BF16),
                        pltpu.VMEM((2, TM, 2 * KV_WIDTH), F32),
                        pltpu.VMEM((WINDOW, 2 * KV_WIDTH), F32),
                        pltpu.VMEM((TM, HALF), F32), pltpu.VMEM((TM, HALF), F32)],
        compiler_params=_params(),
        name="proj_swa",
    )(sinks, x, ln_g, w_in, qg, kg, inv, ones, w_out, w_up)


def _proj_ret_body(x_ref, g_ref, w_ref, inv_ref, rq_ref, rg_ref, wd_ref,
                   r_out, s_out, wd_bf,
                   rk_scr, rv_scr, gt_scr, s_scr, cr_scr, sr_scr, *, nblk):
    i = pl.program_id(0)
    tm = x_ref.shape[0]

    @pl.when(i == 0)
    def _():
        _rope_init(inv_ref, cr_scr, sr_scr, 1)
        rk_scr[...] = jnp.zeros_like(rk_scr)
        rv_scr[...] = jnp.zeros_like(rv_scr)
        gt_scr[...] = jnp.zeros_like(gt_scr)
        s_scr[...] = jnp.zeros_like(s_scr)

    wd_bf[...] = wd_ref[...].astype(BF16)

    cur = lax.rem(i, 2)
    prv = 1 - cur
    blk = jnp.minimum(i, nblk - 1)

    h = _norm_rows(x_ref, g_ref)
    cos, sin = _rope_tables(inv_ref, cr_scr, sr_scr, blk * tm)

    def rk_piece(n):
        y = _dot(h, w_ref[:, 512 * n:512 * (n + 1)])
        for t in range(2):
            _rope_head(y[:, RET_DK * t:RET_DK * (t + 1)], cos, sin, RET_DK ** -0.5,
                       rk_scr.at[cur], 2 * n + t)

    def rv_piece(n):
        c = slice(512 * n, 512 * (n + 1))
        rv_scr[cur, :, c] = _dot(h, w_ref[:, RET_WIDTH + 512 * n:RET_WIDTH + 512 * (n + 1)]
                                 ).astype(BF16)

    def gate_piece(n):
        c0 = 2 * RET_WIDTH + 512 * n
        rg = _dot(h, w_ref[:, c0:c0 + 512])
        gt_scr[cur, :, 512 * n:512 * (n + 1)] = (rg / (1.0 + jnp.exp(-rg))).astype(BF16)

    main = [functools.partial(f, n) for f in (rk_piece, rv_piece, gate_piece) for n in range(2)]
    _emit_pipelined(main, *_ret_phases(rq_ref, rk_scr.at[prv], rv_scr.at[prv], gt_scr.at[prv],
                                       rg_ref, r_out, s_scr))

    @pl.when(i == nblk)
    def _():
        s_out[...] = s_scr[...]


def _proj_ret(x, ln_g, w_in, inv, rq, rg_g, w_down):
    m = x.shape[0]
    nblk = m // TM
    cl = lambda i: jnp.minimum(i, nblk - 1)
    prev = lambda w: pl.BlockSpec((TM, w), lambda i: (jnp.maximum(i - 1, 0), 0))
    wd_spec = pl.BlockSpec((w_down.shape[0] // nblk, w_down.shape[1]), lambda i: (cl(i), 0))
    state = (RET_HEADS, RET_DK, RET_DV)
    slot = pltpu.VMEM((2, TM, RET_WIDTH), BF16)
    return pl.pallas_call(
        functools.partial(_proj_ret_body, nblk=nblk),
        grid=(nblk + 1,),
        in_specs=[pl.BlockSpec((TM, D_MODEL), lambda i: (cl(i), 0)),
                  _const_spec((1, D_MODEL)), _w_in_spec(1), _const_spec((1, HALF)),
                  prev(RET_WIDTH), _const_spec((1, RET_WIDTH)), wd_spec],
        out_specs=[prev(RET_WIDTH), pl.BlockSpec(state, lambda i: (0, 0, 0)), wd_spec],
        out_shape=(jax.ShapeDtypeStruct((m, RET_WIDTH), BF16),
                   jax.ShapeDtypeStruct(state, F32),
                   jax.ShapeDtypeStruct(w_down.shape, BF16)),
        scratch_shapes=[slot, slot, slot, pltpu.VMEM(state, F32),
                        pltpu.VMEM((TM, HALF), F32), pltpu.VMEM((TM, HALF), F32)],
        compiler_params=_params(),
        name="proj_ret",
    )(x, ln_g, w_in, inv, rq, rg_g, w_down)


TN_IN = 256


def _in_proj_body(x_ref, g_ref, w_ref, qg_ref, kg_ref, inv_ref, ones_ref,
                  w_bf, q_out, kv_out, rq_out, rk_out, rv_out, gt_out, kvt_out, rqt_out, rkt_out,
                  h_scr, z_scr, *, pos):
    c = pl.program_id(0)

    @pl.when(c == 0)
    def _():
        h_scr[...] = _norm_rows(x_ref, g_ref)

    wb = w_ref[...].astype(BF16)
    w_bf[...] = wb
    z_scr[c] = _dot(h_scr[...], wb)

    @pl.when(c == pl.num_programs(0) - 1)
    def _():
        tiles = lambda col0, n: [z_scr[col0 // TN_IN + t] for t in range(n)]
        ones = ones_ref[...]
        inv_hd = 1.0 / ATT_HEAD_DIM
        for t, y in enumerate(tiles(0, ATT_WIDTH // TN_IN)):
            ssq = _dot((y * y).astype(BF16), ones)
            q_out[:, 256 * t:256 * (t + 1)] = (
                y * lax.rsqrt(ssq * inv_hd + EPS) * qg_ref[...]).astype(BF16)
        (kvr,) = tiles(ATT_WIDTH, 1)
        k = kvr[:, :KV_WIDTH]
        ssq = _dot((k * k).astype(BF16), ones[:KV_WIDTH, :KV_WIDTH])
        kv_out[:, :KV_WIDTH] = k * lax.rsqrt(ssq * inv_hd + EPS) * kg_ref[...]
        kv_out[:, KV_WIDTH:] = kvr[:, KV_WIDTH:]
        ang = jnp.full((8, 1), pos, jnp.int32).astype(F32) * inv_ref[...]
        cos = jnp.cos(ang)[:1]
        sin = jnp.sin(ang)[:1]
        rq0 = ATT_WIDTH + 2 * KV_WIDTH
        for hh, y in enumerate(tiles(rq0, RET_HEADS)):
            _rope_head(y, cos, sin, 1.0, rq_out, hh)
        for hh, y in enumerate(tiles(rq0 + RET_WIDTH, RET_HEADS)):
            _rope_head(y, cos, sin, RET_DK ** -0.5, rk_out, hh)
        for hh, y in enumerate(tiles(rq0 + 2 * RET_WIDTH, RET_HEADS)):
            rv_out[:, RET_DV * hh:RET_DV * (hh + 1)] = _bf16_into(rv_out, y)
        for hh, y in enumerate(tiles(rq0 + 3 * RET_WIDTH, RET_HEADS)):
            gt_out[:, RET_DV * hh:RET_DV * (hh + 1)] = _bf16_into(gt_out, y / (1.0 + jnp.exp(-y)))
        kvt_out[...] = kv_out[...].T
        rqt_out[...] = rq_out[...].T
        rkt_out[...] = rk_out[...].T


def _in_proj(x, ln_g, w_in, qg, kg, inv, ones, pos):
    m = x.shape[0]
    assert m == LANES
    assert RET_DK == TN_IN and w_in.shape[1] % TN_IN == 0
    nstep = w_in.shape[1] // TN_IN
    full = lambda w: pl.BlockSpec((m, w), lambda c: (0, 0))
    colm = lambda w: pl.BlockSpec((w, m), lambda c: (0, 0))
    wcol = pl.BlockSpec((D_MODEL, TN_IN), lambda c: (0, c))
    ret = jax.ShapeDtypeStruct((m, RET_WIDTH), F32)
    ret_t = jax.ShapeDtypeStruct((RET_WIDTH, m), F32)
    return pl.pallas_call(
        functools.partial(_in_proj_body, pos=pos),
        grid=(nstep,),
        in_specs=[_const_spec((m, D_MODEL)), _const_spec((1, D_MODEL)), wcol,
                  _const_spec((1, 256)), _const_spec((1, KV_WIDTH)),
                  _const_spec((1, HALF)), _const_spec((256, 256))],
        out_specs=[wcol, full(ATT_WIDTH), full(2 * KV_WIDTH)] + [full(RET_WIDTH)] * 4
        + [colm(2 * KV_WIDTH), colm(RET_WIDTH), colm(RET_WIDTH)],
        out_shape=(jax.ShapeDtypeStruct(w_in.shape, BF16),
                   jax.ShapeDtypeStruct((m, ATT_WIDTH), BF16),
                   jax.ShapeDtypeStruct((m, 2 * KV_WIDTH), F32), ret, ret, ret, ret,
                   jax.ShapeDtypeStruct((2 * KV_WIDTH, m), F32), ret_t, ret_t),
        scratch_shapes=[pltpu.VMEM((m, D_MODEL), BF16), pltpu.VMEM((nstep, m, TN_IN), F32)],
        compiler_params=_params(),
        name="in_proj",
    )(x, ln_g, w_in, qg, kg, inv, ones)


def _this_steps_columns(t_ref, rows, bb):
    shift = lax.rem(LANES - bb * pl.program_id(0), LANES)
    return pltpu.roll(t_ref[rows, :], shift, 1)


def _swa_dec_body(q_ref, kvn_ref, kvt_ref, ck_ref, cv_ref, sink_ref, o_ref, nk_ref, nv_ref):
    bb = q_ref.shape[0]
    npair = ATT_HEADS // 2
    q8 = q_ref[...].astype(F32)
    q8r = pltpu.roll(q8, ATT_HEAD_DIM, 2)
    lane = lax.broadcasted_iota(jnp.int32, q8.shape, 2)
    pair = lax.broadcasted_iota(jnp.int32, q8.shape, 1)
    lo = lane < ATT_HEAD_DIM
    kv0 = pair < npair // ATT_KV_HEADS
    own = lo == kv0
    zero = jnp.zeros_like(q8)
    qe = jnp.where(own, jnp.where(kv0, q8, q8r), zero)
    qo = jnp.where(own, jnp.where(kv0, q8r, q8), zero)
    qb = jnp.concatenate([qe, qo], axis=1)

    ck = ck_ref[...]
    cv = cv_ref[...]
    kn = kvn_ref[:, :KV_WIDTH]
    vn = kvn_ref[:, KV_WIDTH:]
    s = lax.dot_general(qb.astype(BF16), ck.astype(BF16), (((2,), (1,)), ((0,), (0,))),
                        preferred_element_type=F32)
    s_new = jnp.sum(qb * kn[:, None, :], axis=-1, keepdims=True)
    sink = sink_ref[...][None, :, :]
    mx = jnp.maximum(jnp.maximum(jnp.max(s, axis=-1, keepdims=True), s_new), sink)
    p = jnp.exp(s - mx)
    p_new = jnp.exp(s_new - mx)
    den = jnp.sum(p, axis=-1, keepdims=True) + p_new + jnp.exp(sink - mx)
    o = lax.dot_general(p.astype(BF16), cv.astype(BF16), (((2,), (2,)), ((0,), (0,))),
                        preferred_element_type=F32)
    o = (o + p_new * vn[:, None, :]) / den
    oe = o[:, :npair, :]
    oo = o[:, npair:, :]
    oer = pltpu.roll(oe, ATT_HEAD_DIM, 2)
    oor = pltpu.roll(oo, ATT_HEAD_DIM, 2)
    o_ref[...] = jnp.where(lo, jnp.where(kv0, oe, oer), jnp.where(kv0, oor, oo)).astype(BF16)

    newcol = _this_steps_columns(kvt_ref, slice(None), bb)
    last = lax.broadcasted_iota(jnp.int32, (KV_WIDTH, WINDOW), 1) == WINDOW - 1
    for jb in range(bb):
        nk_ref[jb] = jnp.where(last, newcol[:KV_WIDTH, jb:jb + 1],
                               pltpu.roll(ck[jb], WINDOW - 1, 1))
        nv_ref[jb] = jnp.where(last, newcol[KV_WIDTH:, jb:jb + 1],
                               pltpu.roll(cv[jb], WINDOW - 1, 1))


def _swa_dec(q8, kvn, kvt, ck, cv, sink_col):
    b = q8.shape[0]
    bb = BB_ATT
    cache = pl.BlockSpec((bb, KV_WIDTH, WINDOW), lambda i: (i, 0, 0))
    return pl.pallas_call(
        _swa_dec_body,
        grid=(b // bb,),
        in_specs=[pl.BlockSpec((bb, ATT_HEADS // 2, LANES), lambda i: (i, 0, 0)),
                  pl.BlockSpec((bb, 2 * KV_WIDTH), lambda i: (i, 0)),
                  _const_spec(kvt.shape), cache, cache, _const_spec((ATT_HEADS, 1))],
        out_specs=[pl.BlockSpec((bb, ATT_HEADS // 2, LANES), lambda i: (i, 0, 0)), cache, cache],
        out_shape=(jax.ShapeDtypeStruct((b, ATT_HEADS // 2, LANES), BF16),
                   jax.ShapeDtypeStruct(ck.shape, F32),
                   jax.ShapeDtypeStruct(cv.shape, F32)),
        compiler_params=_params(),
        name="swa_dec",
    )(q8, kvn, kvt, ck, cv, sink_col)


def _out_proj_body(x_ref, a_ref, r_ref, w_ref, g_ref, x1_ref, h2_ref):
    x1 = (x_ref[...] + _dot(a_ref[...].astype(BF16), w_ref[:ATT_WIDTH, :])
          + _dot(r_ref[...].astype(BF16), w_ref[ATT_WIDTH:, :]))
    x1_ref[...] = x1
    var = jnp.mean(x1 * x1, axis=-1, keepdims=True)
    h2_ref[...] = (x1 * lax.rsqrt(var + EPS) * g_ref[...]).astype(BF16)


def _out_proj(x, a, r, w, g, tm):
    m = x.shape[0]
    row = lambda w: pl.BlockSpec((tm, w), lambda i: (i, 0))
    return pl.pallas_call(
        _out_proj_body,
        grid=(m // tm,),
        in_specs=[row(D_MODEL), row(ATT_WIDTH), row(RET_WIDTH),
                  _const_spec(w.shape), _const_spec((1, D_MODEL))],
        out_specs=[row(D_MODEL), row(D_MODEL)],
        out_shape=(jax.ShapeDtypeStruct((m, D_MODEL), F32),
                   jax.ShapeDtypeStruct((m, D_MODEL), BF16)),
        compiler_params=_params(),
        name="out_proj",
    )(x, a, r, w, g)


def _mlp_step(x1_ref, h2_ref, wu_ref, wd_ref, o_ref):
    @pl.when(pl.program_id(1) == 0)
    def _():
        o_ref[...] = x1_ref[...]

    u = jnp.maximum(_dot(h2_ref[...], wu_ref[...]), 0.0)
    o_ref[...] += _dot((u * u).astype(BF16), wd_ref[...])


def _mlp_specs(tm, tf):
    row = pl.BlockSpec((tm, D_MODEL), lambda i, f: (i, 0))
    return [row, row, pl.BlockSpec((D_MODEL, tf), lambda i, f: (0, f)),
            pl.BlockSpec((tf, D_MODEL), lambda i, f: (f, 0))], row


def _mlp(x1, h2, wu, wd, tm):
    m = x1.shape[0]
    in_specs, out_spec = _mlp_specs(tm, TF)
    return pl.pallas_call(
        _mlp_step,
        grid=(m // tm, D_FF // TF),
        in_specs=in_specs,
        out_specs=out_spec,
        out_shape=jax.ShapeDtypeStruct((m, D_MODEL), F32),
        compiler_params=_params(2),
        name="mlp",
    )(x1, h2, wu, wd)


def _mlp_ret_body(x1_ref, h2_ref, wu_ref, wd_ref,
                  qt_ref, kt_ref, rq_ref, rk_ref, rv_ref, gt_ref, g_ref, s_ref,
                  o_ref, r_ref, ns_ref):
    _mlp_step(x1_ref, h2_ref, wu_ref, wd_ref, o_ref)

    b = pl.program_id(0) * pl.num_programs(1) + pl.program_id(1)
    row = pl.ds(b, 1)
    shift = lax.rem(LANES - b, LANES)
    for hh in range(RET_HEADS):
        g1 = math.exp(LOG_G[hh])
        cols = slice(RET_DK * hh, RET_DK * (hh + 1))
        qc = pltpu.roll(qt_ref[cols, :], shift, 1)[:, :1] * g1
        kc = pltpu.roll(kt_ref[cols, :], shift, 1)[:, :1]
        v = rv_ref[row, cols]
        s0 = s_ref[0, hh]
        ns_ref[0, hh] = g1 * s0 + kc * v
        qk = jnp.sum(rq_ref[row, cols] * rk_ref[row, cols], axis=-1, keepdims=True)
        o = jnp.sum(qc * s0, axis=0, keepdims=True) + qk * v
        var = jnp.mean(o * o, axis=-1, keepdims=True)
        r_ref[row, cols] = o * lax.rsqrt(var + EPS) * g_ref[:, cols] * gt_ref[row, cols]


def _mlp_ret(x1, h2, wu, wd, qt, kt, rq, rk, rv, gt, g, state):
    m = x1.shape[0]
    nb = state.shape[0]
    nf = D_FF // TF_RET
    assert m // TM * nf == nb and nb == LANES
    in_specs, out_spec = _mlp_specs(TM, TF_RET)
    st = pl.BlockSpec((1, RET_HEADS, RET_DK, RET_DV), lambda i, f: (i * nf + f, 0, 0, 0))
    rows = _const_spec((nb, RET_WIDTH))
    return pl.pallas_call(
        _mlp_ret_body,
        grid=(m // TM, nf),
        in_specs=in_specs + [_const_spec(qt.shape), _const_spec(kt.shape), rows, rows, rows, rows,
                             _const_spec((1, RET_WIDTH)), st],
        out_specs=[out_spec, pl.BlockSpec((nb, RET_WIDTH), lambda i, f: (0, 0)), st],
        out_shape=(jax.ShapeDtypeStruct((m, D_MODEL), F32),
                   jax.ShapeDtypeStruct((nb, RET_WIDTH), F32),
                   jax.ShapeDtypeStruct(state.shape, F32)),
        compiler_params=_params(2),
        name="mlp_ret",
    )(x1, h2, wu, wd, qt, kt, rq, rk, rv, gt, g, state)


def kernel(x_prompt, x_sample, cache_k_win, cache_v_win, state_ret, ln1_g, w_in, q_norm_g,
           k_norm_g, attn_sinks, ret_norm_g, w_out, ln2_g, w_up, w_down):
    seq = x_prompt.shape[1]
    nb = x_sample.shape[0]
    assert x_prompt.shape[0] == 1 and x_sample.shape[1] == 1 and w_in.shape[0] == 1
    assert seq % TM == 0 and w_in.shape[2] == SPLIT_A + SPLIT_B

    inv = (ROPE_BASE ** (-jnp.arange(HALF, dtype=F32) / HALF)).reshape(1, HALF)

    ln1 = ln1_g.reshape(1, D_MODEL)
    ln2 = ln2_g.reshape(1, D_MODEL)
    qg = jnp.tile(q_norm_g.reshape(1, ATT_HEAD_DIM) * (ATT_HEAD_DIM ** -0.5), (1, 256 // ATT_HEAD_DIM))
    kg = jnp.tile(k_norm_g.reshape(1, ATT_HEAD_DIM), (1, KV_WIDTH // ATT_HEAD_DIM))
    rg_g = ret_norm_g.reshape(1, RET_WIDTH)
    sinks = attn_sinks.reshape(ATT_HEADS)
    blk = jnp.arange(256) // ATT_HEAD_DIM
    ones = (blk[:, None] == blk[None, :]).astype(BF16)

    xs = x_sample[:, 0]
    wi, qs, kvs, rqs, rks, rvs, rgs, kvt, rqt, rkt = _in_proj(xs, ln1, w_in[0], qg, kg, inv, ones,
                                                              pos=PAST_LEN)
    sink_col = jnp.concatenate([sinks[0::2], sinks[1::2]]).reshape(ATT_HEADS, 1)
    to_fm = lambda c: c[0].transpose(0, 2, 3, 1).reshape(nb, KV_WIDTH, WINDOW)
    from_fm = lambda c: c.reshape(nb, ATT_KV_HEADS, ATT_HEAD_DIM, WINDOW).transpose(0, 3, 1, 2)[None]
    a8, nk, nv = _swa_dec(qs.reshape(nb, ATT_HEADS // 2, LANES), kvs, kvt,
                          to_fm(cache_k_win), to_fm(cache_v_win), sink_col)
    a_s = a8.reshape(nb, ATT_WIDTH)

    xp = x_prompt[0]
    a_out, kv, rq, wo, wu = _proj_swa(xp, ln1, wi, qg, kg, inv, ones, sinks, w_out[0], w_up[0])
    r_out, s_fin, wd = _proj_ret(xp, ln1, wi, inv, rq, rg_g, w_down[0])
    x1, h2 = _out_proj(xp, a_out, r_out, wo, ln2, tm=TM)
    yp, r_s, ns = _mlp_ret(x1, h2, wu, wd, rqt, rkt, rqs, rks, rvs, rgs, rg_g, state_ret[0])

    wb = min(WINDOW, seq)
    kp = kv[seq - wb:, :KV_WIDTH].reshape(1, 1, wb, ATT_KV_HEADS, ATT_HEAD_DIM)
    vp = kv[seq - wb:, KV_WIDTH:].reshape(1, 1, wb, ATT_KV_HEADS, ATT_HEAD_DIM)
    sp = s_fin.reshape(1, 1, RET_HEADS, RET_DK, RET_DV)

    x1s, h2s = _out_proj(xs, a_s, r_s, wo, ln2, tm=nb)
    ys = _mlp(x1s, h2s, wu, wd, tm=nb)

    return (yp[None], ys[:, None, :], kp, vp, sp, from_fm(nk), from_fm(nv), ns[None])
```

```python
import functools
import math

import jax
import jax.numpy as jnp
from jax import lax
from jax.experimental import pallas as pl
from jax.experimental.pallas import tpu as pltpu

D_MODEL = 2048
ATT_HEADS = 16
ATT_KV_HEADS = 2
ATT_HEAD_DIM = 64
WINDOW = 128
RET_HEADS = 4
RET_DK = 256
RET_DV = 256
RET_CHUNK = 128
ROPE_BASE = 10000.0
D_FF = 4 * D_MODEL
EPS = 1e-6
PAST_LEN = 8192

ATT_WIDTH = ATT_HEADS * ATT_HEAD_DIM
KV_WIDTH = ATT_KV_HEADS * ATT_HEAD_DIM
RET_WIDTH = RET_HEADS * RET_DK
LANES = 128
HALF = RET_DK // 2
SPLIT_A = ATT_WIDTH + 2 * KV_WIDTH + RET_WIDTH
SPLIT_B = 3 * RET_WIDTH

F32 = jnp.float32
BF16 = jnp.bfloat16
VMEM_LIMIT = 60 * 1024 * 1024

TM = 512
TF = 512
TF_RET = 1024
BB_ATT = 16

LOG_G = tuple(math.log1p(-(2.0 ** (-5.0 - h))) for h in range(RET_HEADS))


def _dot(a, b):
    return jnp.dot(a, b, preferred_element_type=F32)


def _dot_nt(a, b):
    return lax.dot_general(a, b, (((1,), (1,)), ((), ())), preferred_element_type=F32)


def _dot_tn(a, b):
    return lax.dot_general(a, b, (((0,), (0,)), ((), ())), preferred_element_type=F32)


def _const_spec(shape):
    n = len(shape)
    return pl.BlockSpec(shape, lambda *_: (0,) * n, pipeline_mode=pl.Buffered(1))


def _w_in_spec(part):
    width, start = ((SPLIT_A, 0), (SPLIT_B, SPLIT_A))[part]
    return pl.BlockSpec((pl.Element(D_MODEL), pl.Element(width)), lambda *_: (0, start),
                        pipeline_mode=pl.Buffered(1))


def _params(n_axes=1):
    return pltpu.CompilerParams(dimension_semantics=("arbitrary",) * n_axes,
                                vmem_limit_bytes=VMEM_LIMIT)


def _norm_rows(x_ref, g_ref):
    x = x_ref[...]
    var = jnp.mean(x * x, axis=-1, keepdims=True)
    return (x * lax.rsqrt(var + EPS) * g_ref[...]).astype(BF16)


def _bf16_into(o_ref, v):
    return v.astype(BF16).astype(o_ref.dtype)


def _rope_init(inv_ref, cr_scr, sr_scr, pos_step):
    tm = cr_scr.shape[0]
    row = lax.broadcasted_iota(jnp.int32, (tm, 1), 0)
    ang_r = (pos_step * row).astype(F32) * inv_ref[...]
    cr_scr[...] = jnp.cos(ang_r)
    sr_scr[...] = jnp.sin(ang_r)


def _rope_tables(inv_ref, cr_scr, sr_scr, base_pos):
    base = jnp.zeros((8, 1), jnp.int32) + base_pos
    ang_b = base.astype(F32) * inv_ref[...]
    cb = jnp.cos(ang_b)[:1]
    sb = jnp.sin(ang_b)[:1]
    cr = cr_scr[...]
    sr = sr_scr[...]
    return cb * cr - sb * sr, sb * cr + cb * sr


def _rope_head(r, cos, sin, scale, o_ref, hh):
    x1 = r[:, :HALF]
    x2 = r[:, HALF:]
    o1 = x1 * cos - x2 * sin
    o2 = x2 * cos + x1 * sin
    if scale != 1.0:
        o1 = o1 * scale
        o2 = o2 * scale
    o_ref[:, RET_DK * hh:RET_DK * hh + HALF] = _bf16_into(o_ref, o1)
    o_ref[:, RET_DK * hh + HALF:RET_DK * (hh + 1)] = _bf16_into(o_ref, o2)


def _half_split(a):
    lane = lax.broadcasted_iota(jnp.int32, a.shape, 1)
    lo = lane < ATT_HEAD_DIM
    sw = pltpu.roll(a, ATT_HEAD_DIM, 1)
    zero = jnp.zeros_like(a)
    h0 = (jnp.where(lo, a, zero).astype(BF16), jnp.where(lo, zero, sw).astype(BF16))
    h1 = (jnp.where(lo, sw, zero).astype(BF16), jnp.where(lo, zero, a).astype(BF16))
    return h0, h1


SWA_STACKS = 1


def _swa_phases(sink_ref, q_ref, kvc_ref, kvp_ref, o_ref, has_prev):
    nsub = q_ref.shape[0] // WINDOW
    npair = ATT_HEADS // 2
    row = lax.broadcasted_iota(jnp.int32, (WINDOW, 2 * WINDOW), 0)
    col = lax.broadcasted_iota(jnp.int32, (WINDOW, 2 * WINDOW), 1)
    band_cur = (col >= WINDOW) & (col - WINDOW <= row)
    first_col = jnp.where(has_prev, 0, WINDOW)
    lane_lo = lax.broadcasted_iota(jnp.int32, (WINDOW, LANES), 1) < ATT_HEAD_DIM
    ctx = [{} for _ in range(nsub)]
    st = {}

    ppk = npair // ATT_KV_HEADS

    def stacks(g):
        for m in range(g * SWA_STACKS, (g + 1) * SWA_STACKS):
            j, kh = divmod(m, ATT_KV_HEADS)
            yield j, kh, slice(WINDOW * j, WINDOW * (j + 1)), range(ppk * kh, ppk * (kh + 1))

    def prep(g):
        for j in sorted({j for j, _, _, _ in stacks(g)} - {j for j in range(nsub) if ctx[j]}):
            prev = kvp_ref[...] if j == 0 else kvc_ref[WINDOW * (j - 1):WINDOW * j, :]
            cur = kvc_ref[WINDOW * j:WINDOW * (j + 1), :]
            kk = jnp.concatenate([prev[:, :KV_WIDTH], cur[:, :KV_WIDTH]], axis=0)
            vv = jnp.concatenate([prev[:, KV_WIDTH:], cur[:, KV_WIDTH:]], axis=0)
            ctx[j]["k"] = _half_split(kk)
            ctx[j]["v"] = _half_split(vv)
            lo_col = first_col if j == 0 else 0
            ctx[j]["mask"] = band_cur | ((col < WINDOW) & (col >= row) & (col >= lo_col))

    def qk(g):
        for j, kh, rows, tiles in stacks(g):
            qs = jnp.concatenate([q_ref[rows, LANES * t:LANES * (t + 1)] for t in tiles], axis=0)
            s = [_dot_nt(qs, ctx[j]["k"][kh][par]) for par in range(2)]
            for n, t in enumerate(tiles):
                st[j, t] = {"s": [sp[WINDOW * n:WINDOW * (n + 1), :] for sp in s]}

    def softmax(g):
        for j, kh, rows, tiles in stacks(g):
            for t in tiles:
                p, inv = [], []
                for par in range(2):
                    sink = sink_ref[2 * t + par]
                    s = jnp.where(ctx[j]["mask"], st[j, t]["s"][par], -jnp.inf)
                    mx = jnp.maximum(jnp.max(s, axis=-1, keepdims=True), sink)
                    e = jnp.exp(s - mx)
                    den = jnp.sum(e, axis=-1, keepdims=True) + jnp.exp(sink - mx)
                    p.append(e.astype(BF16))
                    inv.append(1.0 / den)
                st[j, t] = {"p": p, "inv": jnp.where(lane_lo, inv[0], inv[1])}

    def pv(g):
        for j, kh, rows, tiles in stacks(g):
            got = [st.pop((j, t)) for t in tiles]
            acc = sum(_dot(jnp.concatenate([u["p"][par] for u in got], axis=0),
                           ctx[j]["v"][kh][par]) for par in range(2))
            for n, t in enumerate(tiles):
                o_ref[rows, LANES * t:LANES * (t + 1)] = (
                    acc[WINDOW * n:WINDOW * (n + 1), :] * got[n]["inv"]).astype(BF16)

    ngroup = nsub * ATT_KV_HEADS // SWA_STACKS
    return ngroup, prep, qk, softmax, pv


def _emit_pipelined(main, ngroup, prep, first, middle, last, finish=None):
    nstage = ngroup + 2
    done = 0
    prep(0)
    for k in range(nstage):
        if k < ngroup:
            first(k)
        if 0 <= k - 2 < ngroup:
            last(k - 2)
        if 0 <= k - 1 < ngroup:
            middle(k - 1)
        if k + 1 < ngroup:
            prep(k + 1)
        upto = -(-len(main) * (k + 1) // nstage)
        for piece in main[done:upto]:
            piece()
        done = upto
    if finish is not None:
        finish()


def _ret_phases(rq_ref, rk_ref, rv_ref, gt_ref, g_ref, o_ref, s_scr):
    c = RET_CHUNK
    nsub = rq_ref.shape[0] // c
    ri = lax.broadcasted_iota(jnp.int32, (c, c), 0)
    ci = lax.broadcasted_iota(jnp.int32, (c, c), 1)
    rel = (ri - ci).astype(F32)
    idx = lax.broadcasted_iota(jnp.int32, (c, 1), 0).astype(F32)
    head = []
    for hh in range(RET_HEADS):
        lg = LOG_G[hh]
        head.append(dict(
            dmask=jnp.where(rel >= 0, jnp.exp(lg * jnp.maximum(rel, 0.0)), 0.0),
            qdec=jnp.exp(lg * (idx + 1.0)), kdec=jnp.exp(lg * (c - 1.0 - idx)),
            cdec=math.exp(lg * c), cols=slice(RET_DK * hh, RET_DK * (hh + 1))))
    st = {}

    def update_state(g):
        for hh, hd in enumerate(head):
            u = st[g, hh]
            s_scr[hh] = hd["cdec"] * u.pop("s_prev") + u.pop("kv")

    def prep(g):
        if g > 0:
            update_state(g - 1)
        rows = slice(c * g, c * (g + 1))
        for hh, hd in enumerate(head):
            q = rq_ref[rows, hd["cols"]]
            k = rk_ref[rows, hd["cols"]]
            s_prev = s_scr[hh]
            st[g, hh] = dict(q=q, k=k, v=rv_ref[rows, hd["cols"]], s_prev=s_prev,
                             qd=(q.astype(F32) * hd["qdec"]).astype(BF16),
                             kd=(k.astype(F32) * hd["kdec"]).astype(BF16),
                             s_bf=s_prev.astype(BF16))

    def first(g):
        for hh in range(RET_HEADS):
            u = st[g, hh]
            u["att"] = _dot_nt(u.pop("q"), u.pop("k"))
            u["inter"] = _dot(u.pop("qd"), u.pop("s_bf"))
            u["kv"] = _dot_tn(u.pop("kd"), u["v"])

    def middle(g):
        for hh, hd in enumerate(head):
            u = st[g, hh]
            u["att"] = (u["att"] * hd["dmask"]).astype(BF16)

    def last(g):
        rows = slice(c * g, c * (g + 1))
        for hh, hd in enumerate(head):
            u = st[g, hh]
            o = _dot(u.pop("att"), u.pop("v")) + u.pop("inter")
            var = jnp.mean(o * o, axis=-1, keepdims=True)
            on = o * lax.rsqrt(var + EPS) * g_ref[:, hd["cols"]]
            o_ref[rows, hd["cols"]] = (on * gt_ref[rows, hd["cols"]].astype(F32)).astype(BF16)

    return nsub, prep, first, middle, last, functools.partial(update_state, nsub - 1)


def _proj_swa_body(sink_ref, x_ref, g_ref, w_ref, qg_ref, kg_ref, inv_ref, ones_ref,
                   wo_ref, wu_ref,
                   a_out, kv_out, rq_out, wo_bf, wu_bf,
                   q_scr, kv_scr, kvp_scr, cr_scr, sr_scr, *, nblk):
    i = pl.program_id(0)
    tm = x_ref.shape[0]

    @pl.when(i == 0)
    def _():
        _rope_init(inv_ref, cr_scr, sr_scr, 1)
        q_scr[...] = jnp.zeros_like(q_scr)
        kv_scr[...] = jnp.zeros_like(kv_scr)
        kvp_scr[...] = jnp.zeros_like(kvp_scr)

    wo_bf[...] = wo_ref[...].astype(BF16)
    wu_bf[...] = wu_ref[...].astype(BF16)

    cur = lax.rem(i, 2)
    prv = 1 - cur
    blk = jnp.minimum(i, nblk - 1)

    h = _norm_rows(x_ref, g_ref)
    ones = ones_ref[...]
    inv_hd = 1.0 / ATT_HEAD_DIM
    cos, sin = _rope_tables(inv_ref, cr_scr, sr_scr, blk * tm)


    def q_piece(n):
        y = _dot(h, w_ref[:, 512 * n:512 * (n + 1)])
        for t in range(2):
            yt = y[:, 256 * t:256 * (t + 1)]
            ssq = _dot((yt * yt).astype(BF16), ones)
            c0 = 512 * n + 256 * t
            q_scr[cur, :, c0:c0 + 256] = (
                yt * lax.rsqrt(ssq * inv_hd + EPS) * qg_ref[...]).astype(BF16)

    kv0 = ATT_WIDTH
    rq0 = ATT_WIDTH + 2 * KV_WIDTH

    def kv_rq_piece():
        y = _dot(h, w_ref[:, kv0:rq0 + RET_DK])
        k = y[:, :KV_WIDTH]
        ssq = _dot((k * k).astype(BF16), ones[:KV_WIDTH, :KV_WIDTH])
        kn = k * lax.rsqrt(ssq * inv_hd + EPS) * kg_ref[...]
        v = y[:, KV_WIDTH:2 * KV_WIDTH]
        kv_scr[cur, :, :KV_WIDTH] = kn
        kv_scr[cur, :, KV_WIDTH:] = v
        kv_out[:, :KV_WIDTH] = kn
        kv_out[:, KV_WIDTH:] = v
        _rope_head(y[:, 2 * KV_WIDTH:], cos, sin, 1.0, rq_out, 0)

    def rq_piece():
        y = _dot(h, w_ref[:, rq0 + RET_DK:])
        for hh in range(1, RET_HEADS):
            _rope_head(y[:, RET_DK * (hh - 1):RET_DK * hh], cos, sin, 1.0, rq_out, hh)

    main = [functools.partial(q_piece, 0), functools.partial(q_piece, 1), kv_rq_piece, rq_piece]
    _emit_pipelined(main, *_swa_phases(sink_ref, q_scr.at[prv], kv_scr.at[prv], kvp_scr, a_out,
                                       has_prev=i > 1))
    kvp_scr[...] = kv_scr[prv, tm - WINDOW:, :]


def _proj_swa(x, ln_g, w_in, qg, kg, inv, ones, sinks, w_out, w_up):
    m = x.shape[0]
    nblk = m // TM
    cl = lambda i: jnp.minimum(i, nblk - 1)
    row = lambda w: pl.BlockSpec((TM, w), lambda i: (cl(i), 0))
    wo_spec = pl.BlockSpec((w_out.shape[0] // nblk, w_out.shape[1]), lambda i: (cl(i), 0))
    wu_spec = pl.BlockSpec((w_up.shape[0], w_up.shape[1] // nblk), lambda i: (0, cl(i)))
    return pl.pallas_call(
        functools.partial(_proj_swa_body, nblk=nblk),
        grid=(nblk + 1,),
        in_specs=[pl.BlockSpec(memory_space=pltpu.SMEM),
                  row(D_MODEL), _const_spec((1, D_MODEL)), _w_in_spec(0),
                  _const_spec((1, 256)), _const_spec((1, KV_WIDTH)),
                  _const_spec((1, HALF)), _const_spec((256, 256)), wo_spec, wu_spec],
        out_specs=[pl.BlockSpec((TM, ATT_WIDTH), lambda i: (jnp.maximum(i - 1, 0), 0)),
                   row(2 * KV_WIDTH), row(RET_WIDTH), wo_spec, wu_spec],
        out_shape=(jax.ShapeDtypeStruct((m, ATT_WIDTH), BF16),
                   jax.ShapeDtypeStruct((m, 2 * KV_WIDTH), F32),
                   jax.ShapeDtypeStruct((m, RET_WIDTH), BF16),
                   jax.ShapeDtypeStruct(w_out.shape, BF16),
                   jax.ShapeDtypeStruct(w_up.shape, BF16)),
        scratch_shapes=[pltpu.VMEM((2, TM, ATT_WIDTH), BF16),
                        pltpu.VMEM((2, TM, 2 * KV_WIDTH), F32),
                        pltpu.VMEM((WINDOW, 2 * KV_WIDTH), F32),
                        pltpu.VMEM((TM, HALF), F32), pltpu.VMEM((TM, HALF), F32)],
        compiler_params=_params(),
        name="proj_swa",
    )(sinks, x, ln_g, w_in, qg, kg, inv, ones, w_out, w_up)


def _proj_ret_body(x_ref, g_ref, w_ref, inv_ref, rq_ref, rg_ref, wd_ref,
                   r_out, s_out, wd_bf,
                   rk_scr, rv_scr, gt_scr, s_scr, cr_scr, sr_scr, *, nblk):
    i = pl.program_id(0)
    tm = x_ref.shape[0]

    @pl.when(i == 0)
    def _():
        _rope_init(inv_ref, cr_scr, sr_scr, 1)
        rk_scr[...] = jnp.zeros_like(rk_scr)
        rv_scr[...] = jnp.zeros_like(rv_scr)
        gt_scr[...] = jnp.zeros_like(gt_scr)
        s_scr[...] = jnp.zeros_like(s_scr)

    wd_bf[...] = wd_ref[...].astype(BF16)

    cur = lax.rem(i, 2)
    prv = 1 - cur
    blk = jnp.minimum(i, nblk - 1)

    h = _norm_rows(x_ref, g_ref)
    cos, sin = _rope_tables(inv_ref, cr_scr, sr_scr, blk * tm)

    def rk_piece(n):
        y = _dot(h, w_ref[:, 512 * n:512 * (n + 1)])
        for t in range(2):
            _rope_head(y[:, RET_DK * t:RET_DK * (t + 1)], cos, sin, RET_DK ** -0.5,
                       rk_scr.at[cur], 2 * n + t)

    def rv_piece(n):
        c = slice(512 * n, 512 * (n + 1))
        rv_scr[cur, :, c] = _dot(h, w_ref[:, RET_WIDTH + 512 * n:RET_WIDTH + 512 * (n + 1)]
                                 ).astype(BF16)

    def gate_piece(n):
        c0 = 2 * RET_WIDTH + 512 * n
        rg = _dot(h, w_ref[:, c0:c0 + 512])
        gt_scr[cur, :, 512 * n:512 * (n + 1)] = (rg / (1.0 + jnp.exp(-rg))).astype(BF16)

    main = [functools.partial(f, n) for f in (rk_piece, rv_piece, gate_piece) for n in range(2)]
    _emit_pipelined(main, *_ret_phases(rq_ref, rk_scr.at[prv], rv_scr.at[prv], gt_scr.at[prv],
                                       rg_ref, r_out, s_scr))

    @pl.when(i == nblk)
    def _():
        s_out[...] = s_scr[...]


def _proj_ret(x, ln_g, w_in, inv, rq, rg_g, w_down):
    m = x.shape[0]
    nblk = m // TM
    cl = lambda i: jnp.minimum(i, nblk - 1)
    prev = lambda w: pl.BlockSpec((TM, w), lambda i: (jnp.maximum(i - 1, 0), 0))
    wd_spec = pl.BlockSpec((w_down.shape[0] // nblk, w_down.shape[1]), lambda i: (cl(i), 0))
    state = (RET_HEADS, RET_DK, RET_DV)
    slot = pltpu.VMEM((2, TM, RET_WIDTH), BF16)
    return pl.pallas_call(
        functools.partial(_proj_ret_body, nblk=nblk),
        grid=(nblk + 1,),
        in_specs=[pl.BlockSpec((TM, D_MODEL), lambda i: (cl(i), 0)),
                  _const_spec((1, D_MODEL)), _w_in_spec(1), _const_spec((1, HALF)),
                  prev(RET_WIDTH), _const_spec((1, RET_WIDTH)), wd_spec],
        out_specs=[prev(RET_WIDTH), pl.BlockSpec(state, lambda i: (0, 0, 0)), wd_spec],
        out_shape=(jax.ShapeDtypeStruct((m, RET_WIDTH), BF16),
                   jax.ShapeDtypeStruct(state, F32),
                   jax.ShapeDtypeStruct(w_down.shape, BF16)),
        scratch_shapes=[slot, slot, slot, pltpu.VMEM(state, F32),
                        pltpu.VMEM((TM, HALF), F32), pltpu.VMEM((TM, HALF), F32)],
        compiler_params=_params(),
        name="proj_ret",
    )(x, ln_g, w_in, inv, rq, rg_g, w_down)


TN_IN = 768
Z_TILE = 256


def _in_proj_body(x_ref, g_ref, w_ref, qg_ref, kg_ref, inv_ref, ones_ref,
                  w_bf, q_out, kv_out, rq_out, rk_out, rv_out, gt_out, kvt_out, rqt_out, rkt_out,
                  h_scr, z_scr, *, pos):
    c = pl.program_id(0)

    @pl.when(c == 0)
    def _():
        h_scr[...] = _norm_rows(x_ref, g_ref)

    wb = w_ref[...].astype(BF16)
    w_bf[...] = wb
    z = _dot(h_scr[...], wb)
    per_step = TN_IN // Z_TILE
    for t in range(per_step):
        z_scr[c * per_step + t] = z[:, Z_TILE * t:Z_TILE * (t + 1)]

    @pl.when(c == pl.num_programs(0) - 1)
    def _():
        tiles = lambda col0, n: [z_scr[col0 // Z_TILE + t] for t in range(n)]
        ones = ones_ref[...]
        inv_hd = 1.0 / ATT_HEAD_DIM
        for t, y in enumerate(tiles(0, ATT_WIDTH // Z_TILE)):
            ssq = _dot((y * y).astype(BF16), ones)
            q_out[:, 256 * t:256 * (t + 1)] = (
                y * lax.rsqrt(ssq * inv_hd + EPS) * qg_ref[...]).astype(BF16)
        (kvr,) = tiles(ATT_WIDTH, 1)
        k = kvr[:, :KV_WIDTH]
        ssq = _dot((k * k).astype(BF16), ones[:KV_WIDTH, :KV_WIDTH])
        kv_out[:, :KV_WIDTH] = k * lax.rsqrt(ssq * inv_hd + EPS) * kg_ref[...]
        kv_out[:, KV_WIDTH:] = kvr[:, KV_WIDTH:]
        ang = jnp.full((8, 1), pos, jnp.int32).astype(F32) * inv_ref[...]
        cos = jnp.cos(ang)[:1]
        sin = jnp.sin(ang)[:1]
        rq0 = ATT_WIDTH + 2 * KV_WIDTH
        for hh, y in enumerate(tiles(rq0, RET_HEADS)):
            _rope_head(y, cos, sin, 1.0, rq_out, hh)
        for hh, y in enumerate(tiles(rq0 + RET_WIDTH, RET_HEADS)):
            _rope_head(y, cos, sin, RET_DK ** -0.5, rk_out, hh)
        for hh, y in enumerate(tiles(rq0 + 2 * RET_WIDTH, RET_HEADS)):
            rv_out[:, RET_DV * hh:RET_DV * (hh + 1)] = _bf16_into(rv_out, y)
        for hh, y in enumerate(tiles(rq0 + 3 * RET_WIDTH, RET_HEADS)):
            gt_out[:, RET_DV * hh:RET_DV * (hh + 1)] = _bf16_into(gt_out, y / (1.0 + jnp.exp(-y)))
        kvt_out[...] = kv_out[...].T
        rqt_out[...] = rq_out[...].T
        rkt_out[...] = rk_out[...].T


def _in_proj(x, ln_g, w_in, qg, kg, inv, ones, pos):
    m = x.shape[0]
    assert m == LANES
    assert RET_DK == Z_TILE and TN_IN % Z_TILE == 0 and w_in.shape[1] % TN_IN == 0
    nstep = w_in.shape[1] // TN_IN
    full = lambda w: pl.BlockSpec((m, w), lambda c: (0, 0))
    colm = lambda w: pl.BlockSpec((w, m), lambda c: (0, 0))
    wcol = pl.BlockSpec((D_MODEL, TN_IN), lambda c: (0, c))
    ret = jax.ShapeDtypeStruct((m, RET_WIDTH), F32)
    ret_t = jax.ShapeDtypeStruct((RET_WIDTH, m), F32)
    return pl.pallas_call(
        functools.partial(_in_proj_body, pos=pos),
        grid=(nstep,),
        in_specs=[_const_spec((m, D_MODEL)), _const_spec((1, D_MODEL)), wcol,
                  _const_spec((1, 256)), _const_spec((1, KV_WIDTH)),
                  _const_spec((1, HALF)), _const_spec((256, 256))],
        out_specs=[wcol, full(ATT_WIDTH), full(2 * KV_WIDTH)] + [full(RET_WIDTH)] * 4
        + [colm(2 * KV_WIDTH), colm(RET_WIDTH), colm(RET_WIDTH)],
        out_shape=(jax.ShapeDtypeStruct(w_in.shape, BF16),
                   jax.ShapeDtypeStruct((m, ATT_WIDTH), BF16),
                   jax.ShapeDtypeStruct((m, 2 * KV_WIDTH), F32), ret, ret, ret, ret,
                   jax.ShapeDtypeStruct((2 * KV_WIDTH, m), F32), ret_t, ret_t),
        scratch_shapes=[pltpu.VMEM((m, D_MODEL), BF16),
                        pltpu.VMEM((w_in.shape[1] // Z_TILE, m, Z_TILE), F32)],
        compiler_params=_params(),
        name="in_proj",
    )(x, ln_g, w_in, qg, kg, inv, ones)


def _this_steps_columns(t_ref, rows, bb):
    shift = lax.rem(LANES - bb * pl.program_id(0), LANES)
    return pltpu.roll(t_ref[rows, :], shift, 1)


def _swa_dec_body(q_ref, kvn_ref, kvt_ref, ck_ref, cv_ref, sink_ref, o_ref, nk_ref, nv_ref):
    bb = q_ref.shape[0]
    npair = ATT_HEADS // 2
    q8 = q_ref[...].astype(F32)
    q8r = pltpu.roll(q8, ATT_HEAD_DIM, 2)
    lane = lax.broadcasted_iota(jnp.int32, q8.shape, 2)
    pair = lax.broadcasted_iota(jnp.int32, q8.shape, 1)
    lo = lane < ATT_HEAD_DIM
    kv0 = pair < npair // ATT_KV_HEADS
    own = lo == kv0
    zero = jnp.zeros_like(q8)
    qe = jnp.where(own, jnp.where(kv0, q8, q8r), zero)
    qo = jnp.where(own, jnp.where(kv0, q8r, q8), zero)
    qb = jnp.concatenate([qe, qo], axis=1)

    ck = ck_ref[...]
    cv = cv_ref[...]
    kn = kvn_ref[:, :KV_WIDTH]
    vn = kvn_ref[:, KV_WIDTH:]
    s = lax.dot_general(qb.astype(BF16), ck.astype(BF16), (((2,), (1,)), ((0,), (0,))),
                        preferred_element_type=F32)
    s_new = jnp.sum(qb * kn[:, None, :], axis=-1, keepdims=True)
    sink = sink_ref[...][None, :, :]
    mx = jnp.maximum(jnp.maximum(jnp.max(s, axis=-1, keepdims=True), s_new), sink)
    p = jnp.exp(s - mx)
    p_new = jnp.exp(s_new - mx)
    den = jnp.sum(p, axis=-1, keepdims=True) + p_new + jnp.exp(sink - mx)
    o = lax.dot_general(p.astype(BF16), cv.astype(BF16), (((2,), (2,)), ((0,), (0,))),
                        preferred_element_type=F32)
    o = (o + p_new * vn[:, None, :]) / den
    oe = o[:, :npair, :]
    oo = o[:, npair:, :]
    oer = pltpu.roll(oe, ATT_HEAD_DIM, 2)
    oor = pltpu.roll(oo, ATT_HEAD_DIM, 2)
    o_ref[...] = jnp.where(lo, jnp.where(kv0, oe, oer), jnp.where(kv0, oor, oo)).astype(BF16)

    newcol = _this_steps_columns(kvt_ref, slice(None), bb)
    last = lax.broadcasted_iota(jnp.int32, (KV_WIDTH, WINDOW), 1) == WINDOW - 1
    for jb in range(bb):
        nk_ref[jb] = jnp.where(last, newcol[:KV_WIDTH, jb:jb + 1],
                               pltpu.roll(ck[jb], WINDOW - 1, 1))
        nv_ref[jb] = jnp.where(last, newcol[KV_WIDTH:, jb:jb + 1],
                               pltpu.roll(cv[jb], WINDOW - 1, 1))


def _swa_dec(q8, kvn, kvt, ck, cv, sink_col):
    b = q8.shape[0]
    bb = BB_ATT
    cache = pl.BlockSpec((bb, KV_WIDTH, WINDOW), lambda i: (i, 0, 0))
    return pl.pallas_call(
        _swa_dec_body,
        grid=(b // bb,),
        in_specs=[pl.BlockSpec((bb, ATT_HEADS // 2, LANES), lambda i: (i, 0, 0)),
                  pl.BlockSpec((bb, 2 * KV_WIDTH), lambda i: (i, 0)),
                  _const_spec(kvt.shape), cache, cache, _const_spec((ATT_HEADS, 1))],
        out_specs=[pl.BlockSpec((bb, ATT_HEADS // 2, LANES), lambda i: (i, 0, 0)), cache, cache],
        out_shape=(jax.ShapeDtypeStruct((b, ATT_HEADS // 2, LANES), BF16),
                   jax.ShapeDtypeStruct(ck.shape, F32),
                   jax.ShapeDtypeStruct(cv.shape, F32)),
        compiler_params=_params(),
        name="swa_dec",
    )(q8, kvn, kvt, ck, cv, sink_col)


def _out_proj_body(x_ref, a_ref, r_ref, w_ref, g_ref, x1_ref, h2_ref):
    x1 = (x_ref[...] + _dot(a_ref[...].astype(BF16), w_ref[:ATT_WIDTH, :])
          + _dot(r_ref[...].astype(BF16), w_ref[ATT_WIDTH:, :]))
    x1_ref[...] = x1
    var = jnp.mean(x1 * x1, axis=-1, keepdims=True)
    h2_ref[...] = (x1 * lax.rsqrt(var + EPS) * g_ref[...]).astype(BF16)


def _out_proj(x, a, r, w, g, tm):
    m = x.shape[0]
    row = lambda w: pl.BlockSpec((tm, w), lambda i: (i, 0))
    return pl.pallas_call(
        _out_proj_body,
        grid=(m // tm,),
        in_specs=[row(D_MODEL), row(ATT_WIDTH), row(RET_WIDTH),
                  _const_spec(w.shape), _const_spec((1, D_MODEL))],
        out_specs=[row(D_MODEL), row(D_MODEL)],
        out_shape=(jax.ShapeDtypeStruct((m, D_MODEL), F32),
                   jax.ShapeDtypeStruct((m, D_MODEL), BF16)),
        compiler_params=_params(),
        name="out_proj",
    )(x, a, r, w, g)


def _mlp_step(x1_ref, h2_ref, wu_ref, wd_ref, o_ref):
    @pl.when(pl.program_id(1) == 0)
    def _():
        o_ref[...] = x1_ref[...]

    u = jnp.maximum(_dot(h2_ref[...], wu_ref[...]), 0.0)
    o_ref[...] += _dot((u * u).astype(BF16), wd_ref[...])


def _mlp_specs(tm, tf):
    row = pl.BlockSpec((tm, D_MODEL), lambda i, f: (i, 0))
    return [row, row, pl.BlockSpec((D_MODEL, tf), lambda i, f: (0, f)),
            pl.BlockSpec((tf, D_MODEL), lambda i, f: (f, 0))], row


def _mlp(x1, h2, wu, wd, tm):
    m = x1.shape[0]
    in_specs, out_spec = _mlp_specs(tm, TF)
    return pl.pallas_call(
        _mlp_step,
        grid=(m // tm, D_FF // TF),
        in_specs=in_specs,
        out_specs=out_spec,
        out_shape=jax.ShapeDtypeStruct((m, D_MODEL), F32),
        compiler_params=_params(2),
        name="mlp",
    )(x1, h2, wu, wd)


def _mlp_ret_body(x1_ref, h2_ref, wu_ref, wd_ref,
                  qt_ref, kt_ref, rq_ref, rk_ref, rv_ref, gt_ref, g_ref, s_ref,
                  o_ref, r_ref, ns_ref):
    _mlp_step(x1_ref, h2_ref, wu_ref, wd_ref, o_ref)

    b = pl.program_id(0) * pl.num_programs(1) + pl.program_id(1)
    row = pl.ds(b, 1)
    shift = lax.rem(LANES - b, LANES)
    for hh in range(RET_HEADS):
        g1 = math.exp(LOG_G[hh])
        cols = slice(RET_DK * hh, RET_DK * (hh + 1))
        qc = pltpu.roll(qt_ref[cols, :], shift, 1)[:, :1] * g1
        kc = pltpu.roll(kt_ref[cols, :], shift, 1)[:, :1]
        v = rv_ref[row, cols]
        s0 = s_ref[0, hh]
        ns_ref[0, hh] = g1 * s0 + kc * v
        qk = jnp.sum(rq_ref[row, cols] * rk_ref[row, cols], axis=-1, keepdims=True)
        o = jnp.sum(qc * s0, axis=0, keepdims=True) + qk * v
        var = jnp.mean(o * o, axis=-1, keepdims=True)
        r_ref[row, cols] = o * lax.rsqrt(var + EPS) * g_ref[:, cols] * gt_ref[row, cols]


def _mlp_ret(x1, h2, wu, wd, qt, kt, rq, rk, rv, gt, g, state):
    m = x1.shape[0]
    nb = state.shape[0]
    nf = D_FF // TF_RET
    assert m // TM * nf == nb and nb == LANES
    in_specs, out_spec = _mlp_specs(TM, TF_RET)
    st = pl.BlockSpec((1, RET_HEADS, RET_DK, RET_DV), lambda i, f: (i * nf + f, 0, 0, 0))
    rows = _const_spec((nb, RET_WIDTH))
    return pl.pallas_call(
        _mlp_ret_body,
        grid=(m // TM, nf),
        in_specs=in_specs + [_const_spec(qt.shape), _const_spec(kt.shape), rows, rows, rows, rows,
                             _const_spec((1, RET_WIDTH)), st],
        out_specs=[out_spec, pl.BlockSpec((nb, RET_WIDTH), lambda i, f: (0, 0)), st],
        out_shape=(jax.ShapeDtypeStruct((m, D_MODEL), F32),
                   jax.ShapeDtypeStruct((nb, RET_WIDTH), F32),
                   jax.ShapeDtypeStruct(state.shape, F32)),
        compiler_params=_params(2),
        name="mlp_ret",
    )(x1, h2, wu, wd, qt, kt, rq, rk, rv, gt, g, state)


def kernel(x_prompt, x_sample, cache_k_win, cache_v_win, state_ret, ln1_g, w_in, q_norm_g,
           k_norm_g, attn_sinks, ret_norm_g, w_out, ln2_g, w_up, w_down):
    seq = x_prompt.shape[1]
    nb = x_sample.shape[0]
    assert x_prompt.shape[0] == 1 and x_sample.shape[1] == 1 and w_in.shape[0] == 1
    assert seq % TM == 0 and w_in.shape[2] == SPLIT_A + SPLIT_B

    inv = (ROPE_BASE ** (-jnp.arange(HALF, dtype=F32) / HALF)).reshape(1, HALF)

    ln1 = ln1_g.reshape(1, D_MODEL)
    ln2 = ln2_g.reshape(1, D_MODEL)
    qg = jnp.tile(q_norm_g.reshape(1, ATT_HEAD_DIM) * (ATT_HEAD_DIM ** -0.5), (1, 256 // ATT_HEAD_DIM))
    kg = jnp.tile(k_norm_g.reshape(1, ATT_HEAD_DIM), (1, KV_WIDTH // ATT_HEAD_DIM))
    rg_g = ret_norm_g.reshape(1, RET_WIDTH)
    sinks = attn_sinks.reshape(ATT_HEADS)
    blk = jnp.arange(256) // ATT_HEAD_DIM
    ones = (blk[:, None] == blk[None, :]).astype(BF16)

    xs = x_sample[:, 0]
    wi, qs, kvs, rqs, rks, rvs, rgs, kvt, rqt, rkt = _in_proj(xs, ln1, w_in[0], qg, kg, inv, ones,
                                                              pos=PAST_LEN)
    sink_col = jnp.concatenate([sinks[0::2], sinks[1::2]]).reshape(ATT_HEADS, 1)
    to_fm = lambda c: c[0].transpose(0, 2, 3, 1).reshape(nb, KV_WIDTH, WINDOW)
    from_fm = lambda c: c.reshape(nb, ATT_KV_HEADS, ATT_HEAD_DIM, WINDOW).transpose(0, 3, 1, 2)[None]
    a8, nk, nv = _swa_dec(qs.reshape(nb, ATT_HEADS // 2, LANES), kvs, kvt,
                          to_fm(cache_k_win), to_fm(cache_v_win), sink_col)
    a_s = a8.reshape(nb, ATT_WIDTH)

    xp = x_prompt[0]
    a_out, kv, rq, wo, wu = _proj_swa(xp, ln1, wi, qg, kg, inv, ones, sinks, w_out[0], w_up[0])
    r_out, s_fin, wd = _proj_ret(xp, ln1, wi, inv, rq, rg_g, w_down[0])
    x1, h2 = _out_proj(xp, a_out, r_out, wo, ln2, tm=TM)
    yp, r_s, ns = _mlp_ret(x1, h2, wu, wd, rqt, rkt, rqs, rks, rvs, rgs, rg_g, state_ret[0])

    wb = min(WINDOW, seq)
    kp = kv[seq - wb:, :KV_WIDTH].reshape(1, 1, wb, ATT_KV_HEADS, ATT_HEAD_DIM)
    vp = kv[seq - wb:, KV_WIDTH:].reshape(1, 1, wb, ATT_KV_HEADS, ATT_HEAD_DIM)
    sp = s_fin.reshape(1, 1, RET_HEADS, RET_DK, RET_DV)

    x1s, h2s = _out_proj(xs, a_s, r_s, wo, ln2, tm=nb)
    ys = _mlp(x1s, h2s, wu, wd, tm=nb)

    return (yp[None], ys[:, None, :], kp, vp, sp, from_fm(nk), from_fm(nv), ns[None])
```

```python
import functools
import math

import jax
import jax.numpy as jnp
from jax import lax
from jax.experimental import pallas as pl
from jax.experimental.pallas import tpu as pltpu

D_MODEL = 2048
ATT_HEADS = 16
ATT_KV_HEADS = 2
ATT_HEAD_DIM = 64
WINDOW = 128
RET_HEADS = 4
RET_DK = 256
RET_DV = 256
RET_CHUNK = 128
ROPE_BASE = 10000.0
D_FF = 4 * D_MODEL
EPS = 1e-6
PAST_LEN = 8192

ATT_WIDTH = ATT_HEADS * ATT_HEAD_DIM
KV_WIDTH = ATT_KV_HEADS * ATT_HEAD_DIM
RET_WIDTH = RET_HEADS * RET_DK
LANES = 128
HALF = RET_DK // 2
SPLIT_A = ATT_WIDTH + 2 * KV_WIDTH + RET_WIDTH
SPLIT_B = 3 * RET_WIDTH

F32 = jnp.float32
BF16 = jnp.bfloat16
VMEM_LIMIT = 60 * 1024 * 1024

TM = 512
TF = 1024
TF_RET = 1024
BB_ATT = 16

LOG_G = tuple(math.log1p(-(2.0 ** (-5.0 - h))) for h in range(RET_HEADS))


def _dot(a, b):
    return jnp.dot(a, b, preferred_element_type=F32)


def _dot_nt(a, b):
    return lax.dot_general(a, b, (((1,), (1,)), ((), ())), preferred_element_type=F32)


def _dot_tn(a, b):
    return lax.dot_general(a, b, (((0,), (0,)), ((), ())), preferred_element_type=F32)


def _const_spec(shape):
    n = len(shape)
    return pl.BlockSpec(shape, lambda *_: (0,) * n, pipeline_mode=pl.Buffered(1))


def _w_in_spec(part):
    width, start = ((SPLIT_A, 0), (SPLIT_B, SPLIT_A))[part]
    return pl.BlockSpec((pl.Element(D_MODEL), pl.Element(width)), lambda *_: (0, start),
                        pipeline_mode=pl.Buffered(1))


def _params(n_axes=1):
    return pltpu.CompilerParams(dimension_semantics=("arbitrary",) * n_axes,
                                vmem_limit_bytes=VMEM_LIMIT)


def _norm_rows(x_ref, g_ref):
    x = x_ref[...]
    var = jnp.mean(x * x, axis=-1, keepdims=True)
    return (x * lax.rsqrt(var + EPS) * g_ref[...]).astype(BF16)


def _bf16_into(o_ref, v):
    return v.astype(BF16).astype(o_ref.dtype)


def _rope_init(inv_ref, cr_scr, sr_scr, pos_step):
    tm = cr_scr.shape[0]
    row = lax.broadcasted_iota(jnp.int32, (tm, 1), 0)
    ang_r = (pos_step * row).astype(F32) * inv_ref[...]
    cr_scr[...] = jnp.cos(ang_r)
    sr_scr[...] = jnp.sin(ang_r)


def _rope_tables(inv_ref, cr_scr, sr_scr, base_pos):
    base = jnp.zeros((8, 1), jnp.int32) + base_pos
    ang_b = base.astype(F32) * inv_ref[...]
    cb = jnp.cos(ang_b)[:1]
    sb = jnp.sin(ang_b)[:1]
    cr = cr_scr[...]
    sr = sr_scr[...]
    return cb * cr - sb * sr, sb * cr + cb * sr


def _rope_head(r, cos, sin, scale, o_ref, hh):
    x1 = r[:, :HALF]
    x2 = r[:, HALF:]
    o1 = x1 * cos - x2 * sin
    o2 = x2 * cos + x1 * sin
    if scale != 1.0:
        o1 = o1 * scale
        o2 = o2 * scale
    o_ref[:, RET_DK * hh:RET_DK * hh + HALF] = _bf16_into(o_ref, o1)
    o_ref[:, RET_DK * hh + HALF:RET_DK * (hh + 1)] = _bf16_into(o_ref, o2)


def _half_split(a):
    lane = lax.broadcasted_iota(jnp.int32, a.shape, 1)
    lo = lane < ATT_HEAD_DIM
    sw = pltpu.roll(a, ATT_HEAD_DIM, 1)
    zero = jnp.zeros_like(a)
    h0 = (jnp.where(lo, a, zero).astype(BF16), jnp.where(lo, zero, sw).astype(BF16))
    h1 = (jnp.where(lo, sw, zero).astype(BF16), jnp.where(lo, zero, a).astype(BF16))
    return h0, h1


SWA_STACKS = 1


def _swa_phases(sink_ref, q_ref, kvc_ref, kvp_ref, o_ref, has_prev):
    nsub = q_ref.shape[0] // WINDOW
    npair = ATT_HEADS // 2
    row = lax.broadcasted_iota(jnp.int32, (WINDOW, 2 * WINDOW), 0)
    col = lax.broadcasted_iota(jnp.int32, (WINDOW, 2 * WINDOW), 1)
    band_cur = (col >= WINDOW) & (col - WINDOW <= row)
    first_col = jnp.where(has_prev, 0, WINDOW)
    lane_lo = lax.broadcasted_iota(jnp.int32, (WINDOW, LANES), 1) < ATT_HEAD_DIM
    ctx = [{} for _ in range(nsub)]
    st = {}

    ppk = npair // ATT_KV_HEADS

    def stacks(g):
        for m in range(g * SWA_STACKS, (g + 1) * SWA_STACKS):
            j, kh = divmod(m, ATT_KV_HEADS)
            yield j, kh, slice(WINDOW * j, WINDOW * (j + 1)), range(ppk * kh, ppk * (kh + 1))

    def prep(g):
        for j in sorted({j for j, _, _, _ in stacks(g)} - {j for j in range(nsub) if ctx[j]}):
            prev = kvp_ref[...] if j == 0 else kvc_ref[WINDOW * (j - 1):WINDOW * j, :]
            cur = kvc_ref[WINDOW * j:WINDOW * (j + 1), :]
            kk = jnp.concatenate([prev[:, :KV_WIDTH], cur[:, :KV_WIDTH]], axis=0)
            vv = jnp.concatenate([prev[:, KV_WIDTH:], cur[:, KV_WIDTH:]], axis=0)
            ctx[j]["k"] = _half_split(kk)
            ctx[j]["v"] = _half_split(vv)
            lo_col = first_col if j == 0 else 0
            ctx[j]["mask"] = band_cur | ((col < WINDOW) & (col >= row) & (col >= lo_col))

    def qk(g):
        for j, kh, rows, tiles in stacks(g):
            qs = jnp.concatenate([q_ref[rows, LANES * t:LANES * (t + 1)] for t in tiles], axis=0)
            s = [_dot_nt(qs, ctx[j]["k"][kh][par]) for par in range(2)]
            for n, t in enumerate(tiles):
                st[j, t] = {"s": [sp[WINDOW * n:WINDOW * (n + 1), :] for sp in s]}

    def softmax(g):
        for j, kh, rows, tiles in stacks(g):
            for t in tiles:
                p, inv = [], []
                for par in range(2):
                    sink = sink_ref[2 * t + par]
                    s = jnp.where(ctx[j]["mask"], st[j, t]["s"][par], -jnp.inf)
                    mx = jnp.maximum(jnp.max(s, axis=-1, keepdims=True), sink)
                    e = jnp.exp(s - mx)
                    den = jnp.sum(e, axis=-1, keepdims=True) + jnp.exp(sink - mx)
                    p.append(e.astype(BF16))
                    inv.append(1.0 / den)
                st[j, t] = {"p": p, "inv": jnp.where(lane_lo, inv[0], inv[1])}

    def pv(g):
        for j, kh, rows, tiles in stacks(g):
            got = [st.pop((j, t)) for t in tiles]
            acc = sum(_dot(jnp.concatenate([u["p"][par] for u in got], axis=0),
                           ctx[j]["v"][kh][par]) for par in range(2))
            for n, t in enumerate(tiles):
                o_ref[rows, LANES * t:LANES * (t + 1)] = (
                    acc[WINDOW * n:WINDOW * (n + 1), :] * got[n]["inv"]).astype(BF16)

    ngroup = nsub * ATT_KV_HEADS // SWA_STACKS
    return ngroup, prep, qk, softmax, pv


def _emit_pipelined(main, ngroup, prep, first, middle, last, finish=None):
    nstage = ngroup + 2
    done = 0
    prep(0)
    for k in range(nstage):
        if k < ngroup:
            first(k)
        if 0 <= k - 2 < ngroup:
            last(k - 2)
        if 0 <= k - 1 < ngroup:
            middle(k - 1)
        if k + 1 < ngroup:
            prep(k + 1)
        upto = -(-len(main) * (k + 1) // nstage)
        for piece in main[done:upto]:
            piece()
        done = upto
    if finish is not None:
        finish()


def _ret_phases(rq_ref, rk_ref, rv_ref, gt_ref, g_ref, o_ref, s_scr):
    c = RET_CHUNK
    nsub = rq_ref.shape[0] // c
    ri = lax.broadcasted_iota(jnp.int32, (c, c), 0)
    ci = lax.broadcasted_iota(jnp.int32, (c, c), 1)
    rel = (ri - ci).astype(F32)
    idx = lax.broadcasted_iota(jnp.int32, (c, 1), 0).astype(F32)
    head = []
    for hh in range(RET_HEADS):
        lg = LOG_G[hh]
        head.append(dict(
            dmask=jnp.where(rel >= 0, jnp.exp(lg * jnp.maximum(rel, 0.0)), 0.0),
            qdec=jnp.exp(lg * (idx + 1.0)), kdec=jnp.exp(lg * (c - 1.0 - idx)),
            cdec=math.exp(lg * c), cols=slice(RET_DK * hh, RET_DK * (hh + 1))))
    st = {}

    def update_state(g):
        for hh, hd in enumerate(head):
            u = st[g, hh]
            s_scr[hh] = hd["cdec"] * u.pop("s_prev") + u.pop("kv")

    def prep(g):
        if g > 0:
            update_state(g - 1)
        rows = slice(c * g, c * (g + 1))
        for hh, hd in enumerate(head):
            q = rq_ref[rows, hd["cols"]]
            k = rk_ref[rows, hd["cols"]]
            s_prev = s_scr[hh]
            st[g, hh] = dict(q=q, k=k, v=rv_ref[rows, hd["cols"]], s_prev=s_prev,
                             qd=(q.astype(F32) * hd["qdec"]).astype(BF16),
                             kd=(k.astype(F32) * hd["kdec"]).astype(BF16),
                             s_bf=s_prev.astype(BF16))

    def first(g):
        for hh in range(RET_HEADS):
            u = st[g, hh]
            u["att"] = _dot_nt(u.pop("q"), u.pop("k"))
            u["inter"] = _dot(u.pop("qd"), u.pop("s_bf"))
            u["kv"] = _dot_tn(u.pop("kd"), u["v"])

    def middle(g):
        for hh, hd in enumerate(head):
            u = st[g, hh]
            u["att"] = (u["att"] * hd["dmask"]).astype(BF16)

    def last(g):
        rows = slice(c * g, c * (g + 1))
        for hh, hd in enumerate(head):
            u = st[g, hh]
            o = _dot(u.pop("att"), u.pop("v")) + u.pop("inter")
            var = jnp.mean(o * o, axis=-1, keepdims=True)
            on = o * lax.rsqrt(var + EPS) * g_ref[:, hd["cols"]]
            o_ref[rows, hd["cols"]] = (on * gt_ref[rows, hd["cols"]].astype(F32)).astype(BF16)

    return nsub, prep, first, middle, last, functools.partial(update_state, nsub - 1)


def _proj_swa_body(sink_ref, x_ref, g_ref, w_ref, qg_ref, kg_ref, inv_ref, ones_ref,
                   wo_ref, wu_ref,
                   a_out, kv_out, rq_out, wo_bf, wu_bf,
                   q_scr, kv_scr, kvp_scr, cr_scr, sr_scr, *, nblk):
    i = pl.program_id(0)
    tm = x_ref.shape[0]

    @pl.when(i == 0)
    def _():
        _rope_init(inv_ref, cr_scr, sr_scr, 1)
        q_scr[...] = jnp.zeros_like(q_scr)
        kv_scr[...] = jnp.zeros_like(kv_scr)
        kvp_scr[...] = jnp.zeros_like(kvp_scr)

    wo_bf[...] = wo_ref[...].astype(BF16)
    wu_bf[...] = wu_ref[...].astype(BF16)

    cur = lax.rem(i, 2)
    prv = 1 - cur
    blk = jnp.minimum(i, nblk - 1)

    h = _norm_rows(x_ref, g_ref)
    ones = ones_ref[...]
    inv_hd = 1.0 / ATT_HEAD_DIM
    cos, sin = _rope_tables(inv_ref, cr_scr, sr_scr, blk * tm)


    def q_piece(n):
        y = _dot(h, w_ref[:, 512 * n:512 * (n + 1)])
        for t in range(2):
            yt = y[:, 256 * t:256 * (t + 1)]
            ssq = _dot((yt * yt).astype(BF16), ones)
            c0 = 512 * n + 256 * t
            q_scr[cur, :, c0:c0 + 256] = (
                yt * lax.rsqrt(ssq * inv_hd + EPS) * qg_ref[...]).astype(BF16)

    kv0 = ATT_WIDTH
    rq0 = ATT_WIDTH + 2 * KV_WIDTH

    def kv_rq_piece():
        y = _dot(h, w_ref[:, kv0:rq0 + RET_DK])
        k = y[:, :KV_WIDTH]
        ssq = _dot((k * k).astype(BF16), ones[:KV_WIDTH, :KV_WIDTH])
        kn = k * lax.rsqrt(ssq * inv_hd + EPS) * kg_ref[...]
        v = y[:, KV_WIDTH:2 * KV_WIDTH]
        kv_scr[cur, :, :KV_WIDTH] = kn
        kv_scr[cur, :, KV_WIDTH:] = v
        kv_out[:, :KV_WIDTH] = kn
        kv_out[:, KV_WIDTH:] = v
        _rope_head(y[:, 2 * KV_WIDTH:], cos, sin, 1.0, rq_out, 0)

    def rq_piece():
        y = _dot(h, w_ref[:, rq0 + RET_DK:])
        for hh in range(1, RET_HEADS):
            _rope_head(y[:, RET_DK * (hh - 1):RET_DK * hh], cos, sin, 1.0, rq_out, hh)

    main = [functools.partial(q_piece, 0), functools.partial(q_piece, 1), kv_rq_piece, rq_piece]
    _emit_pipelined(main, *_swa_phases(sink_ref, q_scr.at[prv], kv_scr.at[prv], kvp_scr, a_out,
                                       has_prev=i > 1))
    kvp_scr[...] = kv_scr[prv, tm - WINDOW:, :]


def _proj_swa(x, ln_g, w_in, qg, kg, inv, ones, sinks, w_out, w_up):
    m = x.shape[0]
    nblk = m // TM
    cl = lambda i: jnp.minimum(i, nblk - 1)
    row = lambda w: pl.BlockSpec((TM, w), lambda i: (cl(i), 0))
    wo_spec = pl.BlockSpec((w_out.shape[0] // nblk, w_out.shape[1]), lambda i: (cl(i), 0))
    wu_spec = pl.BlockSpec((w_up.shape[0], w_up.shape[1] // nblk), lambda i: (0, cl(i)))
    return pl.pallas_call(
        functools.partial(_proj_swa_body, nblk=nblk),
        grid=(nblk + 1,),
        in_specs=[pl.BlockSpec(memory_space=pltpu.SMEM),
                  row(D_MODEL), _const_spec((1, D_MODEL)), _w_in_spec(0),
                  _const_spec((1, 256)), _const_spec((1, KV_WIDTH)),
                  _const_spec((1, HALF)), _const_spec((256, 256)), wo_spec, wu_spec],
        out_specs=[pl.BlockSpec((TM, ATT_WIDTH), lambda i: (jnp.maximum(i - 1, 0), 0)),
                   row(2 * KV_WIDTH), row(RET_WIDTH), wo_spec, wu_spec],
        out_shape=(jax.ShapeDtypeStruct((m, ATT_WIDTH), BF16),
                   jax.ShapeDtypeStruct((m, 2 * KV_WIDTH), F32),
                   jax.ShapeDtypeStruct((m, RET_WIDTH), BF16),
                   jax.ShapeDtypeStruct(w_out.shape, BF16),
                   jax.ShapeDtypeStruct(w_up.shape, BF16)),
        scratch_shapes=[pltpu.VMEM((2, TM, ATT_WIDTH), BF16),
                        pltpu.VMEM((2, TM, 2 * KV_WIDTH), F32),
                        pltpu.VMEM((WINDOW, 2 * KV_WIDTH), F32),
                        pltpu.VMEM((TM, HALF), F32), pltpu.VMEM((TM, HALF), F32)],
        compiler_params=_params(),
        name="proj_swa",
    )(sinks, x, ln_g, w_in, qg, kg, inv, ones, w_out, w_up)


def _proj_ret_body(x_ref, g_ref, w_ref, inv_ref, rq_ref, rg_ref, wd_ref,
                   r_out, s_out, wd_bf,
                   rk_scr, rv_scr, gt_scr, s_scr, cr_scr, sr_scr, *, nblk):
    i = pl.program_id(0)
    tm = x_ref.shape[0]

    @pl.when(i == 0)
    def _():
        _rope_init(inv_ref, cr_scr, sr_scr, 1)
        rk_scr[...] = jnp.zeros_like(rk_scr)
        rv_scr[...] = jnp.zeros_like(rv_scr)
        gt_scr[...] = jnp.zeros_like(gt_scr)
        s_scr[...] = jnp.zeros_like(s_scr)

    wd_bf[...] = wd_ref[...].astype(BF16)

    cur = lax.rem(i, 2)
    prv = 1 - cur
    blk = jnp.minimum(i, nblk - 1)

    h = _norm_rows(x_ref, g_ref)
    cos, sin = _rope_tables(inv_ref, cr_scr, sr_scr, blk * tm)

    def rk_piece(n):
        y = _dot(h, w_ref[:, 512 * n:512 * (n + 1)])
        for t in range(2):
            _rope_head(y[:, RET_DK * t:RET_DK * (t + 1)], cos, sin, RET_DK ** -0.5,
                       rk_scr.at[cur], 2 * n + t)

    def rv_piece(n):
        c = slice(512 * n, 512 * (n + 1))
        rv_scr[cur, :, c] = _dot(h, w_ref[:, RET_WIDTH + 512 * n:RET_WIDTH + 512 * (n + 1)]
                                 ).astype(BF16)

    def gate_piece(n):
        c0 = 2 * RET_WIDTH + 512 * n
        rg = _dot(h, w_ref[:, c0:c0 + 512])
        gt_scr[cur, :, 512 * n:512 * (n + 1)] = (rg / (1.0 + jnp.exp(-rg))).astype(BF16)

    main = [functools.partial(f, n) for f in (rk_piece, rv_piece, gate_piece) for n in range(2)]
    _emit_pipelined(main, *_ret_phases(rq_ref, rk_scr.at[prv], rv_scr.at[prv], gt_scr.at[prv],
                                       rg_ref, r_out, s_scr))

    @pl.when(i == nblk)
    def _():
        s_out[...] = s_scr[...]


def _proj_ret(x, ln_g, w_in, inv, rq, rg_g, w_down):
    m = x.shape[0]
    nblk = m // TM
    cl = lambda i: jnp.minimum(i, nblk - 1)
    prev = lambda w: pl.BlockSpec((TM, w), lambda i: (jnp.maximum(i - 1, 0), 0))
    wd_spec = pl.BlockSpec((w_down.shape[0] // nblk, w_down.shape[1]), lambda i: (cl(i), 0))
    state = (RET_HEADS, RET_DK, RET_DV)
    slot = pltpu.VMEM((2, TM, RET_WIDTH), BF16)
    return pl.pallas_call(
        functools.partial(_proj_ret_body, nblk=nblk),
        grid=(nblk + 1,),
        in_specs=[pl.BlockSpec((TM, D_MODEL), lambda i: (cl(i), 0)),
                  _const_spec((1, D_MODEL)), _w_in_spec(1), _const_spec((1, HALF)),
                  prev(RET_WIDTH), _const_spec((1, RET_WIDTH)), wd_spec],
        out_specs=[prev(RET_WIDTH), pl.BlockSpec(state, lambda i: (0, 0, 0)), wd_spec],
        out_shape=(jax.ShapeDtypeStruct((m, RET_WIDTH), BF16),
                   jax.ShapeDtypeStruct(state, F32),
                   jax.ShapeDtypeStruct(w_down.shape, BF16)),
        scratch_shapes=[slot, slot, slot, pltpu.VMEM(state, F32),
                        pltpu.VMEM((TM, HALF), F32), pltpu.VMEM((TM, HALF), F32)],
        compiler_params=_params(),
        name="proj_ret",
    )(x, ln_g, w_in, inv, rq, rg_g, w_down)


TN_IN = 1792
Z_TILE = 256


def _in_proj_body(x_ref, g_ref, w_ref, qg_ref, kg_ref, inv_ref, ones_ref,
                  w_bf, q_out, kv_out, rq_out, rk_out, rv_out, gt_out, kvt_out, rqt_out, rkt_out,
                  h_scr, z_scr, *, pos):
    c = pl.program_id(0)

    @pl.when(c == 0)
    def _():
        h_scr[...] = _norm_rows(x_ref, g_ref)

    wb = w_ref[...].astype(BF16)
    w_bf[...] = wb
    z = _dot(h_scr[...], wb)
    per_step = TN_IN // Z_TILE
    for t in range(per_step):
        z_scr[c * per_step + t] = z[:, Z_TILE * t:Z_TILE * (t + 1)]

    @pl.when(c == pl.num_programs(0) - 1)
    def _():
        tiles = lambda col0, n: [z_scr[col0 // Z_TILE + t] for t in range(n)]
        ones = ones_ref[...]
        inv_hd = 1.0 / ATT_HEAD_DIM
        for t, y in enumerate(tiles(0, ATT_WIDTH // Z_TILE)):
            ssq = _dot((y * y).astype(BF16), ones)
            q_out[:, 256 * t:256 * (t + 1)] = (
                y * lax.rsqrt(ssq * inv_hd + EPS) * qg_ref[...]).astype(BF16)
        (kvr,) = tiles(ATT_WIDTH, 1)
        k = kvr[:, :KV_WIDTH]
        ssq = _dot((k * k).astype(BF16), ones[:KV_WIDTH, :KV_WIDTH])
        kv_out[:, :KV_WIDTH] = k * lax.rsqrt(ssq * inv_hd + EPS) * kg_ref[...]
        kv_out[:, KV_WIDTH:] = kvr[:, KV_WIDTH:]
        ang = jnp.full((8, 1), pos, jnp.int32).astype(F32) * inv_ref[...]
        cos = jnp.cos(ang)[:1]
        sin = jnp.sin(ang)[:1]
        rq0 = ATT_WIDTH + 2 * KV_WIDTH
        for hh, y in enumerate(tiles(rq0, RET_HEADS)):
            _rope_head(y, cos, sin, 1.0, rq_out, hh)
        for hh, y in enumerate(tiles(rq0 + RET_WIDTH, RET_HEADS)):
            _rope_head(y, cos, sin, RET_DK ** -0.5, rk_out, hh)
        for hh, y in enumerate(tiles(rq0 + 2 * RET_WIDTH, RET_HEADS)):
            rv_out[:, RET_DV * hh:RET_DV * (hh + 1)] = _bf16_into(rv_out, y)
        for hh, y in enumerate(tiles(rq0 + 3 * RET_WIDTH, RET_HEADS)):
            gt_out[:, RET_DV * hh:RET_DV * (hh + 1)] = _bf16_into(gt_out, y / (1.0 + jnp.exp(-y)))
        kvt_out[...] = kv_out[...].T
        rqt_out[...] = rq_out[...].T
        rkt_out[...] = rk_out[...].T


def _in_proj(x, ln_g, w_in, qg, kg, inv, ones, pos):
    m = x.shape[0]
    assert m == LANES
    assert RET_DK == Z_TILE and TN_IN % Z_TILE == 0 and w_in.shape[1] % TN_IN == 0
    nstep = w_in.shape[1] // TN_IN
    full = lambda w: pl.BlockSpec((m, w), lambda c: (0, 0))
    colm = lambda w: pl.BlockSpec((w, m), lambda c: (0, 0))
    wcol = pl.BlockSpec((D_MODEL, TN_IN), lambda c: (0, c))
    ret = jax.ShapeDtypeStruct((m, RET_WIDTH), F32)
    ret_t = jax.ShapeDtypeStruct((RET_WIDTH, m), F32)
    return pl.pallas_call(
        functools.partial(_in_proj_body, pos=pos),
        grid=(nstep,),
        in_specs=[_const_spec((m, D_MODEL)), _const_spec((1, D_MODEL)), wcol,
                  _const_spec((1, 256)), _const_spec((1, KV_WIDTH)),
                  _const_spec((1, HALF)), _const_spec((256, 256))],
        out_specs=[wcol, full(ATT_WIDTH), full(2 * KV_WIDTH)] + [full(RET_WIDTH)] * 4
        + [colm(2 * KV_WIDTH), colm(RET_WIDTH), colm(RET_WIDTH)],
        out_shape=(jax.ShapeDtypeStruct(w_in.shape, BF16),
                   jax.ShapeDtypeStruct((m, ATT_WIDTH), BF16),
                   jax.ShapeDtypeStruct((m, 2 * KV_WIDTH), F32), ret, ret, ret, ret,
                   jax.ShapeDtypeStruct((2 * KV_WIDTH, m), F32), ret_t, ret_t),
        scratch_shapes=[pltpu.VMEM((m, D_MODEL), BF16),
                        pltpu.VMEM((w_in.shape[1] // Z_TILE, m, Z_TILE), F32)],
        compiler_params=_params(),
        name="in_proj",
    )(x, ln_g, w_in, qg, kg, inv, ones)


def _this_steps_columns(t_ref, rows, bb):
    shift = lax.rem(LANES - bb * pl.program_id(0), LANES)
    return pltpu.roll(t_ref[rows, :], shift, 1)


def _swa_dec_body(q_ref, kvn_ref, kvt_ref, ck_ref, cv_ref, sink_ref, o_ref, nk_ref, nv_ref):
    bb = q_ref.shape[0]
    npair = ATT_HEADS // 2
    q8 = q_ref[...].astype(F32)
    q8r = pltpu.roll(q8, ATT_HEAD_DIM, 2)
    lane = lax.broadcasted_iota(jnp.int32, q8.shape, 2)
    pair = lax.broadcasted_iota(jnp.int32, q8.shape, 1)
    lo = lane < ATT_HEAD_DIM
    kv0 = pair < npair // ATT_KV_HEADS
    own = lo == kv0
    zero = jnp.zeros_like(q8)
    qe = jnp.where(own, jnp.where(kv0, q8, q8r), zero)
    qo = jnp.where(own, jnp.where(kv0, q8r, q8), zero)
    qb = jnp.concatenate([qe, qo], axis=1)

    ck = ck_ref[...]
    cv = cv_ref[...]
    kn = kvn_ref[:, :KV_WIDTH]
    vn = kvn_ref[:, KV_WIDTH:]
    s = lax.dot_general(qb.astype(BF16), ck.astype(BF16), (((2,), (1,)), ((0,), (0,))),
                        preferred_element_type=F32)
    s_new = jnp.sum(qb * kn[:, None, :], axis=-1, keepdims=True)
    sink = sink_ref[...][None, :, :]
    mx = jnp.maximum(jnp.maximum(jnp.max(s, axis=-1, keepdims=True), s_new), sink)
    p = jnp.exp(s - mx)
    p_new = jnp.exp(s_new - mx)
    den = jnp.sum(p, axis=-1, keepdims=True) + p_new + jnp.exp(sink - mx)
    o = lax.dot_general(p.astype(BF16), cv.astype(BF16), (((2,), (2,)), ((0,), (0,))),
                        preferred_element_type=F32)
    o = (o + p_new * vn[:, None, :]) / den
    oe = o[:, :npair, :]
    oo = o[:, npair:, :]
    oer = pltpu.roll(oe, ATT_HEAD_DIM, 2)
    oor = pltpu.roll(oo, ATT_HEAD_DIM, 2)
    o_ref[...] = jnp.where(lo, jnp.where(kv0, oe, oer), jnp.where(kv0, oor, oo)).astype(BF16)

    newcol = _this_steps_columns(kvt_ref, slice(None), bb)
    last = lax.broadcasted_iota(jnp.int32, (KV_WIDTH, WINDOW), 1) == WINDOW - 1
    for jb in range(bb):
        nk_ref[jb] = jnp.where(last, newcol[:KV_WIDTH, jb:jb + 1],
                               pltpu.roll(ck[jb], WINDOW - 1, 1))
        nv_ref[jb] = jnp.where(last, newcol[KV_WIDTH:, jb:jb + 1],
                               pltpu.roll(cv[jb], WINDOW - 1, 1))


def _swa_dec(q8, kvn, kvt, ck, cv, sink_col):
    b = q8.shape[0]
    bb = BB_ATT
    cache = pl.BlockSpec((bb, KV_WIDTH, WINDOW), lambda i: (i, 0, 0))
    return pl.pallas_call(
        _swa_dec_body,
        grid=(b // bb,),
        in_specs=[pl.BlockSpec((bb, ATT_HEADS // 2, LANES), lambda i: (i, 0, 0)),
                  pl.BlockSpec((bb, 2 * KV_WIDTH), lambda i: (i, 0)),
                  _const_spec(kvt.shape), cache, cache, _const_spec((ATT_HEADS, 1))],
        out_specs=[pl.BlockSpec((bb, ATT_HEADS // 2, LANES), lambda i: (i, 0, 0)), cache, cache],
        out_shape=(jax.ShapeDtypeStruct((b, ATT_HEADS // 2, LANES), BF16),
                   jax.ShapeDtypeStruct(ck.shape, F32),
                   jax.ShapeDtypeStruct(cv.shape, F32)),
        compiler_params=_params(),
        name="swa_dec",
    )(q8, kvn, kvt, ck, cv, sink_col)


def _out_proj_body(x_ref, a_ref, r_ref, w_ref, g_ref, x1_ref, h2_ref):
    x1 = (x_ref[...] + _dot(a_ref[...].astype(BF16), w_ref[:ATT_WIDTH, :])
          + _dot(r_ref[...].astype(BF16), w_ref[ATT_WIDTH:, :]))
    x1_ref[...] = x1
    var = jnp.mean(x1 * x1, axis=-1, keepdims=True)
    h2_ref[...] = (x1 * lax.rsqrt(var + EPS) * g_ref[...]).astype(BF16)


def _out_proj(x, a, r, w, g, tm):
    m = x.shape[0]
    row = lambda w: pl.BlockSpec((tm, w), lambda i: (i, 0))
    return pl.pallas_call(
        _out_proj_body,
        grid=(m // tm,),
        in_specs=[row(D_MODEL), row(ATT_WIDTH), row(RET_WIDTH),
                  _const_spec(w.shape), _const_spec((1, D_MODEL))],
        out_specs=[row(D_MODEL), row(D_MODEL)],
        out_shape=(jax.ShapeDtypeStruct((m, D_MODEL), F32),
                   jax.ShapeDtypeStruct((m, D_MODEL), BF16)),
        compiler_params=_params(),
        name="out_proj",
    )(x, a, r, w, g)


def _mlp_step(x1_ref, h2_ref, wu_ref, wd_ref, o_ref):
    @pl.when(pl.program_id(1) == 0)
    def _():
        o_ref[...] = x1_ref[...]

    u = jnp.maximum(_dot(h2_ref[...], wu_ref[...]), 0.0)
    o_ref[...] += _dot((u * u).astype(BF16), wd_ref[...])


def _mlp_specs(tm, tf):
    row = pl.BlockSpec((tm, D_MODEL), lambda i, f: (i, 0))
    return [row, row, pl.BlockSpec((D_MODEL, tf), lambda i, f: (0, f)),
            pl.BlockSpec((tf, D_MODEL), lambda i, f: (f, 0))], row


def _mlp(x1, h2, wu, wd, tm):
    m = x1.shape[0]
    in_specs, out_spec = _mlp_specs(tm, TF)
    return pl.pallas_call(
        _mlp_step,
        grid=(m // tm, D_FF // TF),
        in_specs=in_specs,
        out_specs=out_spec,
        out_shape=jax.ShapeDtypeStruct((m, D_MODEL), F32),
        compiler_params=_params(2),
        name="mlp",
    )(x1, h2, wu, wd)


def _mlp_ret_body(x_ref, a_ref, ro_ref, wo_ref, g2_ref, wu_ref, wd_ref,
                  qt_ref, kt_ref, rq_ref, rk_ref, rv_ref, gt_ref, g_ref, s_ref,
                  o_ref, r_ref, ns_ref, h2_scr):
    @pl.when(pl.program_id(1) == 0)
    def _():
        _out_proj_body(x_ref, a_ref, ro_ref, wo_ref, g2_ref, o_ref, h2_scr)

    u = jnp.maximum(_dot(h2_scr[...], wu_ref[...]), 0.0)
    o_ref[...] += _dot((u * u).astype(BF16), wd_ref[...])

    b = pl.program_id(0) * pl.num_programs(1) + pl.program_id(1)
    row = pl.ds(b, 1)
    shift = lax.rem(LANES - b, LANES)
    for hh in range(RET_HEADS):
        g1 = math.exp(LOG_G[hh])
        cols = slice(RET_DK * hh, RET_DK * (hh + 1))
        qc = pltpu.roll(qt_ref[cols, :], shift, 1)[:, :1] * g1
        kc = pltpu.roll(kt_ref[cols, :], shift, 1)[:, :1]
        v = rv_ref[row, cols]
        s0 = s_ref[0, hh]
        ns_ref[0, hh] = g1 * s0 + kc * v
        qk = jnp.sum(rq_ref[row, cols] * rk_ref[row, cols], axis=-1, keepdims=True)
        o = jnp.sum(qc * s0, axis=0, keepdims=True) + qk * v
        var = jnp.mean(o * o, axis=-1, keepdims=True)
        r_ref[row, cols] = o * lax.rsqrt(var + EPS) * g_ref[:, cols] * gt_ref[row, cols]


def _mlp_ret(x, a, r, wo, g2, wu, wd, qt, kt, rq, rk, rv, gt, g, state):
    m = x.shape[0]
    nb = state.shape[0]
    nf = D_FF // TF_RET
    assert m // TM * nf == nb and nb == LANES
    row = lambda w: pl.BlockSpec((TM, w), lambda i, f: (i, 0))
    st = pl.BlockSpec((1, RET_HEADS, RET_DK, RET_DV), lambda i, f: (i * nf + f, 0, 0, 0))
    rows = _const_spec((nb, RET_WIDTH))
    return pl.pallas_call(
        _mlp_ret_body,
        grid=(m // TM, nf),
        in_specs=[row(D_MODEL), row(ATT_WIDTH), row(RET_WIDTH), _const_spec(wo.shape),
                  _const_spec((1, D_MODEL)),
                  pl.BlockSpec((D_MODEL, TF_RET), lambda i, f: (0, f)),
                  pl.BlockSpec((TF_RET, D_MODEL), lambda i, f: (f, 0)),
                  _const_spec(qt.shape), _const_spec(kt.shape), rows, rows, rows, rows,
                  _const_spec((1, RET_WIDTH)), st],
        out_specs=[row(D_MODEL), pl.BlockSpec((nb, RET_WIDTH), lambda i, f: (0, 0)), st],
        out_shape=(jax.ShapeDtypeStruct((m, D_MODEL), F32),
                   jax.ShapeDtypeStruct((nb, RET_WIDTH), F32),
                   jax.ShapeDtypeStruct(state.shape, F32)),
        scratch_shapes=[pltpu.VMEM((TM, D_MODEL), BF16)],
        compiler_params=_params(2),
        name="mlp_ret",
    )(x, a, r, wo, g2, wu, wd, qt, kt, rq, rk, rv, gt, g, state)


def kernel(x_prompt, x_sample, cache_k_win, cache_v_win, state_ret, ln1_g, w_in, q_norm_g,
           k_norm_g, attn_sinks, ret_norm_g, w_out, ln2_g, w_up, w_down):
    seq = x_prompt.shape[1]
    nb = x_sample.shape[0]
    assert x_prompt.shape[0] == 1 and x_sample.shape[1] == 1 and w_in.shape[0] == 1
    assert seq % TM == 0 and w_in.shape[2] == SPLIT_A + SPLIT_B

    inv = (ROPE_BASE ** (-jnp.arange(HALF, dtype=F32) / HALF)).reshape(1, HALF)

    ln1 = ln1_g.reshape(1, D_MODEL)
    ln2 = ln2_g.reshape(1, D_MODEL)
    qg = jnp.tile(q_norm_g.reshape(1, ATT_HEAD_DIM) * (ATT_HEAD_DIM ** -0.5), (1, 256 // ATT_HEAD_DIM))
    kg = jnp.tile(k_norm_g.reshape(1, ATT_HEAD_DIM), (1, KV_WIDTH // ATT_HEAD_DIM))
    rg_g = ret_norm_g.reshape(1, RET_WIDTH)
    sinks = attn_sinks.reshape(ATT_HEADS)
    blk = jnp.arange(256) // ATT_HEAD_DIM
    ones = (blk[:, None] == blk[None, :]).astype(BF16)

    xs = x_sample[:, 0]
    wi, qs, kvs, rqs, rks, rvs, rgs, kvt, rqt, rkt = _in_proj(xs, ln1, w_in[0], qg, kg, inv, ones,
                                                              pos=PAST_LEN)
    sink_col = jnp.concatenate([sinks[0::2], sinks[1::2]]).reshape(ATT_HEADS, 1)
    to_fm = lambda c: c[0].transpose(0, 2, 3, 1).reshape(nb, KV_WIDTH, WINDOW)
    from_fm = lambda c: c.reshape(nb, ATT_KV_HEADS, ATT_HEAD_DIM, WINDOW).transpose(0, 3, 1, 2)[None]
    a8, nk, nv = _swa_dec(qs.reshape(nb, ATT_HEADS // 2, LANES), kvs, kvt,
                          to_fm(cache_k_win), to_fm(cache_v_win), sink_col)
    a_s = a8.reshape(nb, ATT_WIDTH)

    xp = x_prompt[0]
    a_out, kv, rq, wo, wu = _proj_swa(xp, ln1, wi, qg, kg, inv, ones, sinks, w_out[0], w_up[0])
    r_out, s_fin, wd = _proj_ret(xp, ln1, wi, inv, rq, rg_g, w_down[0])
    yp, r_s, ns = _mlp_ret(xp, a_out, r_out, wo, ln2, wu, wd,
                           rqt, rkt, rqs, rks, rvs, rgs, rg_g, state_ret[0])

    wb = min(WINDOW, seq)
    kp = kv[seq - wb:, :KV_WIDTH].reshape(1, 1, wb, ATT_KV_HEADS, ATT_HEAD_DIM)
    vp = kv[seq - wb:, KV_WIDTH:].reshape(1, 1, wb, ATT_KV_HEADS, ATT_HEAD_DIM)
    sp = s_fin.reshape(1, 1, RET_HEADS, RET_DK, RET_DV)

    x1s, h2s = _out_proj(xs, a_s, r_s, wo, ln2, tm=nb)
    ys = _mlp(x1s, h2s, wu, wd, tm=nb)

    return (yp[None], ys[:, None, :], kp, vp, sp, from_fm(nk), from_fm(nv), ns[None])
```

```python
import functools
import math

import jax
import jax.numpy as jnp
from jax import lax
from jax.experimental import pallas as pl
from jax.experimental.pallas import tpu as pltpu

D_MODEL = 2048
ATT_HEADS = 16
ATT_KV_HEADS = 2
ATT_HEAD_DIM = 64
WINDOW = 128
RET_HEADS = 4
RET_DK = 256
RET_DV = 256
RET_CHUNK = 128
ROPE_BASE = 10000.0
D_FF = 4 * D_MODEL
EPS = 1e-6
PAST_LEN = 8192

ATT_WIDTH = ATT_HEADS * ATT_HEAD_DIM
KV_WIDTH = ATT_KV_HEADS * ATT_HEAD_DIM
RET_WIDTH = RET_HEADS * RET_DK
LANES = 128
HALF = RET_DK // 2
SPLIT_A = ATT_WIDTH + 2 * KV_WIDTH + RET_WIDTH
SPLIT_B = 3 * RET_WIDTH

F32 = jnp.float32
BF16 = jnp.bfloat16
VMEM_LIMIT = 60 * 1024 * 1024

TM = 512
TF_RET = 1024
BB_ATT = 32

LOG_G = tuple(math.log1p(-(2.0 ** (-5.0 - h))) for h in range(RET_HEADS))


def _dot(a, b):
    return jnp.dot(a, b, preferred_element_type=F32)


def _dot_nt(a, b):
    return lax.dot_general(a, b, (((1,), (1,)), ((), ())), preferred_element_type=F32)


def _dot_tn(a, b):
    return lax.dot_general(a, b, (((0,), (0,)), ((), ())), preferred_element_type=F32)


def _const_spec(shape):
    n = len(shape)
    return pl.BlockSpec(shape, lambda *_: (0,) * n, pipeline_mode=pl.Buffered(1))


def _w_in_spec(part):
    width, start = ((SPLIT_A, 0), (SPLIT_B, SPLIT_A))[part]
    return pl.BlockSpec((pl.Element(D_MODEL), pl.Element(width)), lambda *_: (0, start),
                        pipeline_mode=pl.Buffered(1))


def _params(n_axes=1):
    return pltpu.CompilerParams(dimension_semantics=("arbitrary",) * n_axes,
                                vmem_limit_bytes=VMEM_LIMIT)


def _norm_rows(x_ref, g_ref):
    x = x_ref[...]
    var = jnp.mean(x * x, axis=-1, keepdims=True)
    return (x * lax.rsqrt(var + EPS) * g_ref[...]).astype(BF16)


def _bf16_into(o_ref, v):
    return v.astype(BF16).astype(o_ref.dtype)


def _rope_init(inv_ref, cr_scr, sr_scr, pos_step):
    tm = cr_scr.shape[0]
    row = lax.broadcasted_iota(jnp.int32, (tm, 1), 0)
    ang_r = (pos_step * row).astype(F32) * inv_ref[...]
    cr_scr[...] = jnp.cos(ang_r)
    sr_scr[...] = jnp.sin(ang_r)


def _rope_tables(inv_ref, cr_scr, sr_scr, base_pos):
    base = jnp.zeros((8, 1), jnp.int32) + base_pos
    ang_b = base.astype(F32) * inv_ref[...]
    cb = jnp.cos(ang_b)[:1]
    sb = jnp.sin(ang_b)[:1]
    cr = cr_scr[...]
    sr = sr_scr[...]
    return cb * cr - sb * sr, sb * cr + cb * sr


def _rope_head(r, cos, sin, scale, o_ref, hh):
    x1 = r[:, :HALF]
    x2 = r[:, HALF:]
    o1 = x1 * cos - x2 * sin
    o2 = x2 * cos + x1 * sin
    if scale != 1.0:
        o1 = o1 * scale
        o2 = o2 * scale
    o_ref[:, RET_DK * hh:RET_DK * hh + HALF] = _bf16_into(o_ref, o1)
    o_ref[:, RET_DK * hh + HALF:RET_DK * (hh + 1)] = _bf16_into(o_ref, o2)


def _half_split(a):
    lane = lax.broadcasted_iota(jnp.int32, a.shape, 1)
    lo = lane < ATT_HEAD_DIM
    sw = pltpu.roll(a, ATT_HEAD_DIM, 1)
    zero = jnp.zeros_like(a)
    h0 = (jnp.where(lo, a, zero).astype(BF16), jnp.where(lo, zero, sw).astype(BF16))
    h1 = (jnp.where(lo, sw, zero).astype(BF16), jnp.where(lo, zero, a).astype(BF16))
    return h0, h1


SWA_STACKS = 1


def _swa_phases(sink_ref, q_ref, kvc_ref, kvp_ref, o_ref, has_prev):
    nsub = q_ref.shape[0] // WINDOW
    npair = ATT_HEADS // 2
    row = lax.broadcasted_iota(jnp.int32, (WINDOW, 2 * WINDOW), 0)
    col = lax.broadcasted_iota(jnp.int32, (WINDOW, 2 * WINDOW), 1)
    band_cur = (col >= WINDOW) & (col - WINDOW <= row)
    first_col = jnp.where(has_prev, 0, WINDOW)
    lane_lo = lax.broadcasted_iota(jnp.int32, (WINDOW, LANES), 1) < ATT_HEAD_DIM
    ctx = [{} for _ in range(nsub)]
    st = {}

    ppk = npair // ATT_KV_HEADS

    def stacks(g):
        for m in range(g * SWA_STACKS, (g + 1) * SWA_STACKS):
            j, kh = divmod(m, ATT_KV_HEADS)
            yield j, kh, slice(WINDOW * j, WINDOW * (j + 1)), range(ppk * kh, ppk * (kh + 1))

    def prep(g):
        for j in sorted({j for j, _, _, _ in stacks(g)} - {j for j in range(nsub) if ctx[j]}):
            prev = kvp_ref[...] if j == 0 else kvc_ref[WINDOW * (j - 1):WINDOW * j, :]
            cur = kvc_ref[WINDOW * j:WINDOW * (j + 1), :]
            kk = jnp.concatenate([prev[:, :KV_WIDTH], cur[:, :KV_WIDTH]], axis=0)
            vv = jnp.concatenate([prev[:, KV_WIDTH:], cur[:, KV_WIDTH:]], axis=0)
            ctx[j]["k"] = _half_split(kk)
            ctx[j]["v"] = _half_split(vv)
            lo_col = first_col if j == 0 else 0
            ctx[j]["mask"] = band_cur | ((col < WINDOW) & (col >= row) & (col >= lo_col))

    def qk(g):
        for j, kh, rows, tiles in stacks(g):
            qs = jnp.concatenate([q_ref[rows, LANES * t:LANES * (t + 1)] for t in tiles], axis=0)
            s = [_dot_nt(qs, ctx[j]["k"][kh][par]) for par in range(2)]
            for n, t in enumerate(tiles):
                st[j, t] = {"s": [sp[WINDOW * n:WINDOW * (n + 1), :] for sp in s]}

    def softmax(g):
        for j, kh, rows, tiles in stacks(g):
            for t in tiles:
                p, inv = [], []
                for par in range(2):
                    sink = sink_ref[2 * t + par]
                    s = jnp.where(ctx[j]["mask"], st[j, t]["s"][par], -jnp.inf)
                    mx = jnp.maximum(jnp.max(s, axis=-1, keepdims=True), sink)
                    e = jnp.exp(s - mx)
                    den = jnp.sum(e, axis=-1, keepdims=True) + jnp.exp(sink - mx)
                    p.append(e.astype(BF16))
                    inv.append(1.0 / den)
                st[j, t] = {"p": p, "inv": jnp.where(lane_lo, inv[0], inv[1])}

    def pv(g):
        for j, kh, rows, tiles in stacks(g):
            got = [st.pop((j, t)) for t in tiles]
            acc = sum(_dot(jnp.concatenate([u["p"][par] for u in got], axis=0),
                           ctx[j]["v"][kh][par]) for par in range(2))
            for n, t in enumerate(tiles):
                o_ref[rows, LANES * t:LANES * (t + 1)] = (
                    acc[WINDOW * n:WINDOW * (n + 1), :] * got[n]["inv"]).astype(BF16)

    ngroup = nsub * ATT_KV_HEADS // SWA_STACKS
    return ngroup, prep, qk, softmax, pv


def _emit_pipelined(main, ngroup, prep, first, middle, last, finish=None):
    nstage = ngroup + 2
    done = 0
    prep(0)
    for k in range(nstage):
        if k < ngroup:
            first(k)
        if 0 <= k - 2 < ngroup:
            last(k - 2)
        if 0 <= k - 1 < ngroup:
            middle(k - 1)
        if k + 1 < ngroup:
            prep(k + 1)
        upto = -(-len(main) * (k + 1) // nstage)
        for piece in main[done:upto]:
            piece()
        done = upto
    if finish is not None:
        finish()


def _ret_phases(rq_ref, rk_ref, rv_ref, gt_ref, g_ref, o_ref, s_scr):
    c = RET_CHUNK
    nsub = rq_ref.shape[0] // c
    ri = lax.broadcasted_iota(jnp.int32, (c, c), 0)
    ci = lax.broadcasted_iota(jnp.int32, (c, c), 1)
    rel = (ri - ci).astype(F32)
    idx = lax.broadcasted_iota(jnp.int32, (c, 1), 0).astype(F32)
    head = []
    for hh in range(RET_HEADS):
        lg = LOG_G[hh]
        head.append(dict(
            dmask=jnp.where(rel >= 0, jnp.exp(lg * jnp.maximum(rel, 0.0)), 0.0),
            qdec=jnp.exp(lg * (idx + 1.0)), kdec=jnp.exp(lg * (c - 1.0 - idx)),
            cdec=math.exp(lg * c), cols=slice(RET_DK * hh, RET_DK * (hh + 1))))
    st = {}

    def update_state(g):
        for hh, hd in enumerate(head):
            u = st[g, hh]
            s_scr[hh] = hd["cdec"] * u.pop("s_prev") + u.pop("kv")

    def prep(g):
        if g > 0:
            update_state(g - 1)
        rows = slice(c * g, c * (g + 1))
        for hh, hd in enumerate(head):
            q = rq_ref[rows, hd["cols"]]
            k = rk_ref[rows, hd["cols"]]
            s_prev = s_scr[hh]
            st[g, hh] = dict(q=q, k=k, v=rv_ref[rows, hd["cols"]], s_prev=s_prev,
                             qd=(q.astype(F32) * hd["qdec"]).astype(BF16),
                             kd=(k.astype(F32) * hd["kdec"]).astype(BF16),
                             s_bf=s_prev.astype(BF16))

    def first(g):
        for hh in range(RET_HEADS):
            u = st[g, hh]
            u["att"] = _dot_nt(u.pop("q"), u.pop("k"))
            u["inter"] = _dot(u.pop("qd"), u.pop("s_bf"))
            u["kv"] = _dot_tn(u.pop("kd"), u["v"])

    def middle(g):
        for hh, hd in enumerate(head):
            u = st[g, hh]
            u["att"] = (u["att"] * hd["dmask"]).astype(BF16)

    def last(g):
        rows = slice(c * g, c * (g + 1))
        for hh, hd in enumerate(head):
            u = st[g, hh]
            o = _dot(u.pop("att"), u.pop("v")) + u.pop("inter")
            var = jnp.mean(o * o, axis=-1, keepdims=True)
            on = o * lax.rsqrt(var + EPS) * g_ref[:, hd["cols"]]
            o_ref[rows, hd["cols"]] = (on * gt_ref[rows, hd["cols"]].astype(F32)).astype(BF16)

    return nsub, prep, first, middle, last, functools.partial(update_state, nsub - 1)


def _proj_swa_body(sink_ref, x_ref, g_ref, w_ref, qg_ref, kg_ref, inv_ref, ones_ref,
                   wo_ref, wu_ref,
                   a_out, kv_out, rq_out, wo_bf, wu_bf,
                   q_scr, kv_scr, kvp_scr, cr_scr, sr_scr, *, nblk):
    i = pl.program_id(0)
    tm = x_ref.shape[0]

    @pl.when(i == 0)
    def _():
        _rope_init(inv_ref, cr_scr, sr_scr, 1)
        q_scr[...] = jnp.zeros_like(q_scr)
        kv_scr[...] = jnp.zeros_like(kv_scr)
        kvp_scr[...] = jnp.zeros_like(kvp_scr)

    wo_bf[...] = wo_ref[...].astype(BF16)
    wu_bf[0] = wu_ref[...].astype(BF16)

    cur = lax.rem(i, 2)
    prv = 1 - cur
    blk = jnp.minimum(i, nblk - 1)

    h = _norm_rows(x_ref, g_ref)
    ones = ones_ref[...]
    inv_hd = 1.0 / ATT_HEAD_DIM
    cos, sin = _rope_tables(inv_ref, cr_scr, sr_scr, blk * tm)


    def q_piece(n):
        y = _dot(h, w_ref[:, 512 * n:512 * (n + 1)])
        for t in range(2):
            yt = y[:, 256 * t:256 * (t + 1)]
            ssq = _dot((yt * yt).astype(BF16), ones)
            c0 = 512 * n + 256 * t
            q_scr[cur, :, c0:c0 + 256] = (
                yt * lax.rsqrt(ssq * inv_hd + EPS) * qg_ref[...]).astype(BF16)

    kv0 = ATT_WIDTH
    rq0 = ATT_WIDTH + 2 * KV_WIDTH

    def kv_rq_piece():
        y = _dot(h, w_ref[:, kv0:rq0 + RET_DK])
        k = y[:, :KV_WIDTH]
        ssq = _dot((k * k).astype(BF16), ones[:KV_WIDTH, :KV_WIDTH])
        kn = k * lax.rsqrt(ssq * inv_hd + EPS) * kg_ref[...]
        v = y[:, KV_WIDTH:2 * KV_WIDTH]
        kv_scr[cur, :, :KV_WIDTH] = kn
        kv_scr[cur, :, KV_WIDTH:] = v
        kv_out[:, :KV_WIDTH] = kn
        kv_out[:, KV_WIDTH:] = v
        _rope_head(y[:, 2 * KV_WIDTH:], cos, sin, 1.0, rq_out, 0)

    def rq_piece():
        y = _dot(h, w_ref[:, rq0 + RET_DK:])
        for hh in range(1, RET_HEADS):
            _rope_head(y[:, RET_DK * (hh - 1):RET_DK * hh], cos, sin, 1.0, rq_out, hh)

    main = [functools.partial(q_piece, 0), functools.partial(q_piece, 1), kv_rq_piece, rq_piece]
    _emit_pipelined(main, *_swa_phases(sink_ref, q_scr.at[prv], kv_scr.at[prv], kvp_scr, a_out,
                                       has_prev=i > 1))
    kvp_scr[...] = kv_scr[prv, tm - WINDOW:, :]


def _proj_swa(x, ln_g, w_in, qg, kg, inv, ones, sinks, w_out, w_up):
    m = x.shape[0]
    nblk = m // TM
    cl = lambda i: jnp.minimum(i, nblk - 1)
    row = lambda w: pl.BlockSpec((TM, w), lambda i: (cl(i), 0))
    wo_spec = pl.BlockSpec((w_out.shape[0] // nblk, w_out.shape[1]), lambda i: (cl(i), 0))
    wcast = w_up.shape[1] // nblk
    per_slab = TF_RET // wcast
    assert w_up.shape[1] % nblk == 0 and TF_RET % wcast == 0
    wu_spec = pl.BlockSpec((w_up.shape[0], wcast), lambda i: (0, cl(i)))
    wu_out_spec = pl.BlockSpec((1, w_up.shape[0], wcast),
                               lambda i: (cl(i) // per_slab, 0, cl(i) % per_slab))
    return pl.pallas_call(
        functools.partial(_proj_swa_body, nblk=nblk),
        grid=(nblk + 1,),
        in_specs=[pl.BlockSpec(memory_space=pltpu.SMEM),
                  row(D_MODEL), _const_spec((1, D_MODEL)), _w_in_spec(0),
                  _const_spec((1, 256)), _const_spec((1, KV_WIDTH)),
                  _const_spec((1, HALF)), _const_spec((256, 256)), wo_spec, wu_spec],
        out_specs=[pl.BlockSpec((TM, ATT_WIDTH), lambda i: (jnp.maximum(i - 1, 0), 0)),
                   row(2 * KV_WIDTH), row(RET_WIDTH), wo_spec, wu_out_spec],
        out_shape=(jax.ShapeDtypeStruct((m, ATT_WIDTH), BF16),
                   jax.ShapeDtypeStruct((m, 2 * KV_WIDTH), F32),
                   jax.ShapeDtypeStruct((m, RET_WIDTH), BF16),
                   jax.ShapeDtypeStruct(w_out.shape, BF16),
                   jax.ShapeDtypeStruct((w_up.shape[1] // TF_RET, w_up.shape[0], TF_RET), BF16)),
        scratch_shapes=[pltpu.VMEM((2, TM, ATT_WIDTH), BF16),
                        pltpu.VMEM((2, TM, 2 * KV_WIDTH), F32),
                        pltpu.VMEM((WINDOW, 2 * KV_WIDTH), F32),
                        pltpu.VMEM((TM, HALF), F32), pltpu.VMEM((TM, HALF), F32)],
        compiler_params=_params(),
        name="proj_swa",
    )(sinks, x, ln_g, w_in, qg, kg, inv, ones, w_out, w_up)


def _proj_ret_body(x_ref, g_ref, w_ref, inv_ref, rq_ref, rg_ref, wd_ref,
                   r_out, s_out, wd_bf,
                   rk_scr, rv_scr, gt_scr, s_scr, cr_scr, sr_scr, *, nblk):
    i = pl.program_id(0)
    tm = x_ref.shape[0]

    @pl.when(i == 0)
    def _():
        _rope_init(inv_ref, cr_scr, sr_scr, 1)
        rk_scr[...] = jnp.zeros_like(rk_scr)
        rv_scr[...] = jnp.zeros_like(rv_scr)
        gt_scr[...] = jnp.zeros_like(gt_scr)
        s_scr[...] = jnp.zeros_like(s_scr)

    wd_bf[...] = wd_ref[...].astype(BF16)

    cur = lax.rem(i, 2)
    prv = 1 - cur
    blk = jnp.minimum(i, nblk - 1)

    h = _norm_rows(x_ref, g_ref)
    cos, sin = _rope_tables(inv_ref, cr_scr, sr_scr, blk * tm)

    def rk_piece(n):
        y = _dot(h, w_ref[:, 512 * n:512 * (n + 1)])
        for t in range(2):
            _rope_head(y[:, RET_DK * t:RET_DK * (t + 1)], cos, sin, RET_DK ** -0.5,
                       rk_scr.at[cur], 2 * n + t)

    def rv_piece(n):
        c = slice(512 * n, 512 * (n + 1))
        rv_scr[cur, :, c] = _dot(h, w_ref[:, RET_WIDTH + 512 * n:RET_WIDTH + 512 * (n + 1)]
                                 ).astype(BF16)

    def gate_piece(n):
        c0 = 2 * RET_WIDTH + 512 * n
        rg = _dot(h, w_ref[:, c0:c0 + 512])
        gt_scr[cur, :, 512 * n:512 * (n + 1)] = (rg / (1.0 + jnp.exp(-rg))).astype(BF16)

    main = [functools.partial(f, n) for f in (rk_piece, rv_piece, gate_piece) for n in range(2)]
    _emit_pipelined(main, *_ret_phases(rq_ref, rk_scr.at[prv], rv_scr.at[prv], gt_scr.at[prv],
                                       rg_ref, r_out, s_scr))

    @pl.when(i == nblk)
    def _():
        s_out[...] = s_scr[...]


def _proj_ret(x, ln_g, w_in, inv, rq, rg_g, w_down):
    m = x.shape[0]
    nblk = m // TM
    cl = lambda i: jnp.minimum(i, nblk - 1)
    prev = lambda w: pl.BlockSpec((TM, w), lambda i: (jnp.maximum(i - 1, 0), 0))
    wd_spec = pl.BlockSpec((w_down.shape[0] // nblk, w_down.shape[1]), lambda i: (cl(i), 0))
    state = (RET_HEADS, RET_DK, RET_DV)
    slot = pltpu.VMEM((2, TM, RET_WIDTH), BF16)
    return pl.pallas_call(
        functools.partial(_proj_ret_body, nblk=nblk),
        grid=(nblk + 1,),
        in_specs=[pl.BlockSpec((TM, D_MODEL), lambda i: (cl(i), 0)),
                  _const_spec((1, D_MODEL)), _w_in_spec(1), _const_spec((1, HALF)),
                  prev(RET_WIDTH), _const_spec((1, RET_WIDTH)), wd_spec],
        out_specs=[prev(RET_WIDTH), pl.BlockSpec(state, lambda i: (0, 0, 0)), wd_spec],
        out_shape=(jax.ShapeDtypeStruct((m, RET_WIDTH), BF16),
                   jax.ShapeDtypeStruct(state, F32),
                   jax.ShapeDtypeStruct(w_down.shape, BF16)),
        scratch_shapes=[slot, slot, slot, pltpu.VMEM(state, F32),
                        pltpu.VMEM((TM, HALF), F32), pltpu.VMEM((TM, HALF), F32)],
        compiler_params=_params(),
        name="proj_ret",
    )(x, ln_g, w_in, inv, rq, rg_g, w_down)


TN_IN = 1792
Z_TILE = 256


def _in_proj_body(x_ref, g_ref, w_ref, qg_ref, kg_ref, inv_ref, ones_ref,
                  w_bf, q_out, kv_out, rq_out, rk_out, rv_out, gt_out, kvt_out, rqt_out, rkt_out,
                  h_scr, z_scr, *, pos):
    c = pl.program_id(0)

    @pl.when(c == 0)
    def _():
        h_scr[...] = _norm_rows(x_ref, g_ref)

    wb = w_ref[...].astype(BF16)
    w_bf[...] = wb
    z = _dot(h_scr[...], wb)
    per_step = TN_IN // Z_TILE
    for t in range(per_step):
        z_scr[c * per_step + t] = z[:, Z_TILE * t:Z_TILE * (t + 1)]

    @pl.when(c == pl.num_programs(0) - 1)
    def _():
        tiles = lambda col0, n: [z_scr[col0 // Z_TILE + t] for t in range(n)]
        ones = ones_ref[...]
        inv_hd = 1.0 / ATT_HEAD_DIM
        for t, y in enumerate(tiles(0, ATT_WIDTH // Z_TILE)):
            ssq = _dot((y * y).astype(BF16), ones)
            q_out[:, 256 * t:256 * (t + 1)] = (
                y * lax.rsqrt(ssq * inv_hd + EPS) * qg_ref[...]).astype(BF16)
        (kvr,) = tiles(ATT_WIDTH, 1)
        k = kvr[:, :KV_WIDTH]
        ssq = _dot((k * k).astype(BF16), ones[:KV_WIDTH, :KV_WIDTH])
        kv_out[:, :KV_WIDTH] = k * lax.rsqrt(ssq * inv_hd + EPS) * kg_ref[...]
        kv_out[:, KV_WIDTH:] = kvr[:, KV_WIDTH:]
        ang = jnp.full((8, 1), pos, jnp.int32).astype(F32) * inv_ref[...]
        cos = jnp.cos(ang)[:1]
        sin = jnp.sin(ang)[:1]
        rq0 = ATT_WIDTH + 2 * KV_WIDTH
        for hh, y in enumerate(tiles(rq0, RET_HEADS)):
            _rope_head(y, cos, sin, 1.0, rq_out, hh)
        for hh, y in enumerate(tiles(rq0 + RET_WIDTH, RET_HEADS)):
            _rope_head(y, cos, sin, RET_DK ** -0.5, rk_out, hh)
        for hh, y in enumerate(tiles(rq0 + 2 * RET_WIDTH, RET_HEADS)):
            rv_out[:, RET_DV * hh:RET_DV * (hh + 1)] = _bf16_into(rv_out, y)
        for hh, y in enumerate(tiles(rq0 + 3 * RET_WIDTH, RET_HEADS)):
            gt_out[:, RET_DV * hh:RET_DV * (hh + 1)] = _bf16_into(gt_out, y / (1.0 + jnp.exp(-y)))
        kvt_out[...] = kv_out[...].T
        rqt_out[...] = rq_out[...].T
        rkt_out[...] = rk_out[...].T


def _in_proj(x, ln_g, w_in, qg, kg, inv, ones, pos):
    m = x.shape[0]
    assert m == LANES
    assert RET_DK == Z_TILE and TN_IN % Z_TILE == 0 and w_in.shape[1] % TN_IN == 0
    nstep = w_in.shape[1] // TN_IN
    full = lambda w: pl.BlockSpec((m, w), lambda c: (0, 0))
    colm = lambda w: pl.BlockSpec((w, m), lambda c: (0, 0))
    wcol = pl.BlockSpec((D_MODEL, TN_IN), lambda c: (0, c))
    ret = jax.ShapeDtypeStruct((m, RET_WIDTH), F32)
    ret_t = jax.ShapeDtypeStruct((RET_WIDTH, m), F32)
    return pl.pallas_call(
        functools.partial(_in_proj_body, pos=pos),
        grid=(nstep,),
        in_specs=[_const_spec((m, D_MODEL)), _const_spec((1, D_MODEL)), wcol,
                  _const_spec((1, 256)), _const_spec((1, KV_WIDTH)),
                  _const_spec((1, HALF)), _const_spec((256, 256))],
        out_specs=[wcol, full(ATT_WIDTH), full(2 * KV_WIDTH)] + [full(RET_WIDTH)] * 4
        + [colm(2 * KV_WIDTH), colm(RET_WIDTH), colm(RET_WIDTH)],
        out_shape=(jax.ShapeDtypeStruct(w_in.shape, BF16),
                   jax.ShapeDtypeStruct((m, ATT_WIDTH), BF16),
                   jax.ShapeDtypeStruct((m, 2 * KV_WIDTH), F32), ret, ret, ret, ret,
                   jax.ShapeDtypeStruct((2 * KV_WIDTH, m), F32), ret_t, ret_t),
        scratch_shapes=[pltpu.VMEM((m, D_MODEL), BF16),
                        pltpu.VMEM((w_in.shape[1] // Z_TILE, m, Z_TILE), F32)],
        compiler_params=_params(),
        name="in_proj",
    )(x, ln_g, w_in, qg, kg, inv, ones)


def _this_steps_columns(t_ref, rows, bb):
    shift = lax.rem(LANES - bb * pl.program_id(0), LANES)
    return pltpu.roll(t_ref[rows, :], shift, 1)


def _swa_dec_body(q_ref, kvn_ref, kvt_ref, ck_ref, cv_ref, sink_ref, o_ref, nk_ref, nv_ref):
    bb = q_ref.shape[0]
    npair = ATT_HEADS // 2
    q8 = q_ref[...].astype(F32)
    q8r = pltpu.roll(q8, ATT_HEAD_DIM, 2)
    lane = lax.broadcasted_iota(jnp.int32, q8.shape, 2)
    pair = lax.broadcasted_iota(jnp.int32, q8.shape, 1)
    lo = lane < ATT_HEAD_DIM
    kv0 = pair < npair // ATT_KV_HEADS
    own = lo == kv0
    zero = jnp.zeros_like(q8)
    qe = jnp.where(own, jnp.where(kv0, q8, q8r), zero)
    qo = jnp.where(own, jnp.where(kv0, q8r, q8), zero)
    qb = jnp.concatenate([qe, qo], axis=1)

    ck = ck_ref[...]
    cv = cv_ref[...]
    kn = kvn_ref[:, :KV_WIDTH]
    vn = kvn_ref[:, KV_WIDTH:]
    s = lax.dot_general(qb.astype(BF16), ck.astype(BF16), (((2,), (1,)), ((0,), (0,))),
                        preferred_element_type=F32)
    s_new = jnp.sum(qb * kn[:, None, :], axis=-1, keepdims=True)
    sink = sink_ref[...][None, :, :]
    mx = jnp.maximum(jnp.maximum(jnp.max(s, axis=-1, keepdims=True), s_new), sink)
    p = jnp.exp(s - mx)
    p_new = jnp.exp(s_new - mx)
    den = jnp.sum(p, axis=-1, keepdims=True) + p_new + jnp.exp(sink - mx)
    o = lax.dot_general(p.astype(BF16), cv.astype(BF16), (((2,), (2,)), ((0,), (0,))),
                        preferred_element_type=F32)
    o = (o + p_new * vn[:, None, :]) / den
    oe = o[:, :npair, :]
    oo = o[:, npair:, :]
    oer = pltpu.roll(oe, ATT_HEAD_DIM, 2)
    oor = pltpu.roll(oo, ATT_HEAD_DIM, 2)
    o_ref[...] = jnp.where(lo, jnp.where(kv0, oe, oer), jnp.where(kv0, oor, oo)).astype(BF16)

    newcol = _this_steps_columns(kvt_ref, slice(None), bb)
    last = lax.broadcasted_iota(jnp.int32, (KV_WIDTH, WINDOW), 1) == WINDOW - 1
    for jb in range(bb):
        nk_ref[jb] = jnp.where(last, newcol[:KV_WIDTH, jb:jb + 1],
                               pltpu.roll(ck[jb], WINDOW - 1, 1))
        nv_ref[jb] = jnp.where(last, newcol[KV_WIDTH:, jb:jb + 1],
                               pltpu.roll(cv[jb], WINDOW - 1, 1))


def _swa_dec(q8, kvn, kvt, ck, cv, sink_col):
    b = q8.shape[0]
    bb = BB_ATT
    cache = pl.BlockSpec((bb, KV_WIDTH, WINDOW), lambda i: (i, 0, 0))
    return pl.pallas_call(
        _swa_dec_body,
        grid=(b // bb,),
        in_specs=[pl.BlockSpec((bb, ATT_HEADS // 2, LANES), lambda i: (i, 0, 0)),
                  pl.BlockSpec((bb, 2 * KV_WIDTH), lambda i: (i, 0)),
                  _const_spec(kvt.shape), cache, cache, _const_spec((ATT_HEADS, 1))],
        out_specs=[pl.BlockSpec((bb, ATT_HEADS // 2, LANES), lambda i: (i, 0, 0)), cache, cache],
        out_shape=(jax.ShapeDtypeStruct((b, ATT_HEADS // 2, LANES), BF16),
                   jax.ShapeDtypeStruct(ck.shape, F32),
                   jax.ShapeDtypeStruct(cv.shape, F32)),
        compiler_params=_params(),
        name="swa_dec",
    )(q8, kvn, kvt, ck, cv, sink_col)


def _out_proj_body(x_ref, a_ref, r_ref, w_ref, g_ref, x1_ref, h2_ref):
    x1 = (x_ref[...] + _dot(a_ref[...].astype(BF16), w_ref[:ATT_WIDTH, :])
          + _dot(r_ref[...].astype(BF16), w_ref[ATT_WIDTH:, :]))
    x1_ref[...] = x1
    var = jnp.mean(x1 * x1, axis=-1, keepdims=True)
    h2_ref[...] = (x1 * lax.rsqrt(var + EPS) * g_ref[...]).astype(BF16)


def _out_proj(x, a, r, w, g, tm):
    m = x.shape[0]
    row = lambda w: pl.BlockSpec((tm, w), lambda i: (i, 0))
    return pl.pallas_call(
        _out_proj_body,
        grid=(m // tm,),
        in_specs=[row(D_MODEL), row(ATT_WIDTH), row(RET_WIDTH),
                  _const_spec(w.shape), _const_spec((1, D_MODEL))],
        out_specs=[row(D_MODEL), row(D_MODEL)],
        out_shape=(jax.ShapeDtypeStruct((m, D_MODEL), F32),
                   jax.ShapeDtypeStruct((m, D_MODEL), BF16)),
        compiler_params=_params(),
        name="out_proj",
    )(x, a, r, w, g)


def _mlp_acc(h2_ref, wu_ref, wd_ref, o_ref):
    u = jnp.maximum(_dot(h2_ref[...], wu_ref[0]), 0.0)
    o_ref[...] += _dot((u * u).astype(BF16), wd_ref[...])


def _mlp_body(x1_ref, h2_ref, wu_ref, wd_ref, o_ref):
    @pl.when(pl.program_id(1) == 0)
    def _():
        o_ref[...] = x1_ref[...]

    _mlp_acc(h2_ref, wu_ref, wd_ref, o_ref)


def _mlp_weight_specs():
    return [pl.BlockSpec((1, D_MODEL, TF_RET), lambda i, f: (f, 0, 0)),
            pl.BlockSpec((TF_RET, D_MODEL), lambda i, f: (f, 0))]


def _mlp(x1, h2, wu, wd, tm):
    m = x1.shape[0]
    row = pl.BlockSpec((tm, D_MODEL), lambda i, f: (i, 0))
    return pl.pallas_call(
        _mlp_body,
        grid=(m // tm, D_FF // TF_RET),
        in_specs=[row, row] + _mlp_weight_specs(),
        out_specs=row,
        out_shape=jax.ShapeDtypeStruct((m, D_MODEL), F32),
        compiler_params=_params(2),
        name="mlp",
    )(x1, h2, wu, wd)


def _mlp_ret_body(x_ref, a_ref, ro_ref, wo_ref, g2_ref, wu_ref, wd_ref,
                  qt_ref, kt_ref, rq_ref, rk_ref, rv_ref, gt_ref, g_ref, s_ref,
                  o_ref, r_ref, ns_ref, h2_scr):
    @pl.when(pl.program_id(1) == 0)
    def _():
        _out_proj_body(x_ref, a_ref, ro_ref, wo_ref, g2_ref, o_ref, h2_scr)

    _mlp_acc(h2_scr, wu_ref, wd_ref, o_ref)

    b = pl.program_id(0) * pl.num_programs(1) + pl.program_id(1)
    row = pl.ds(b, 1)
    shift = lax.rem(LANES - b, LANES)
    for hh in range(RET_HEADS):
        g1 = math.exp(LOG_G[hh])
        cols = slice(RET_DK * hh, RET_DK * (hh + 1))
        qc = pltpu.roll(qt_ref[cols, :], shift, 1)[:, :1] * g1
        kc = pltpu.roll(kt_ref[cols, :], shift, 1)[:, :1]
        v = rv_ref[row, cols]
        s0 = s_ref[0, hh]
        ns_ref[0, hh] = g1 * s0 + kc * v
        qk = jnp.sum(rq_ref[row, cols] * rk_ref[row, cols], axis=-1, keepdims=True)
        o = jnp.sum(qc * s0, axis=0, keepdims=True) + qk * v
        var = jnp.mean(o * o, axis=-1, keepdims=True)
        r_ref[row, cols] = o * lax.rsqrt(var + EPS) * g_ref[:, cols] * gt_ref[row, cols]


def _mlp_ret(x, a, r, wo, g2, wu, wd, qt, kt, rq, rk, rv, gt, g, state):
    m = x.shape[0]
    nb = state.shape[0]
    nf = D_FF // TF_RET
    assert m // TM * nf == nb and nb == LANES
    row = lambda w: pl.BlockSpec((TM, w), lambda i, f: (i, 0))
    st = pl.BlockSpec((1, RET_HEADS, RET_DK, RET_DV), lambda i, f: (i * nf + f, 0, 0, 0))
    rows = _const_spec((nb, RET_WIDTH))
    return pl.pallas_call(
        _mlp_ret_body,
        grid=(m // TM, nf),
        in_specs=[row(D_MODEL), row(ATT_WIDTH), row(RET_WIDTH), _const_spec(wo.shape),
                  _const_spec((1, D_MODEL))] + _mlp_weight_specs() + [
                  _const_spec(qt.shape), _const_spec(kt.shape), rows, rows, rows, rows,
                  _const_spec((1, RET_WIDTH)), st],
        out_specs=[row(D_MODEL), pl.BlockSpec((nb, RET_WIDTH), lambda i, f: (0, 0)), st],
        out_shape=(jax.ShapeDtypeStruct((m, D_MODEL), F32),
                   jax.ShapeDtypeStruct((nb, RET_WIDTH), F32),
                   jax.ShapeDtypeStruct(state.shape, F32)),
        scratch_shapes=[pltpu.VMEM((TM, D_MODEL), BF16)],
        compiler_params=_params(2),
        name="mlp_ret",
    )(x, a, r, wo, g2, wu, wd, qt, kt, rq, rk, rv, gt, g, state)


def kernel(x_prompt, x_sample, cache_k_win, cache_v_win, state_ret, ln1_g, w_in, q_norm_g,
           k_norm_g, attn_sinks, ret_norm_g, w_out, ln2_g, w_up, w_down):
    seq = x_prompt.shape[1]
    nb = x_sample.shape[0]
    assert x_prompt.shape[0] == 1 and x_sample.shape[1] == 1 and w_in.shape[0] == 1
    assert seq % TM == 0 and w_in.shape[2] == SPLIT_A + SPLIT_B

    inv = (ROPE_BASE ** (-jnp.arange(HALF, dtype=F32) / HALF)).reshape(1, HALF)

    ln1 = ln1_g.reshape(1, D_MODEL)
    ln2 = ln2_g.reshape(1, D_MODEL)
    qg = jnp.tile(q_norm_g.reshape(1, ATT_HEAD_DIM) * (ATT_HEAD_DIM ** -0.5), (1, 256 // ATT_HEAD_DIM))
    kg = jnp.tile(k_norm_g.reshape(1, ATT_HEAD_DIM), (1, KV_WIDTH // ATT_HEAD_DIM))
    rg_g = ret_norm_g.reshape(1, RET_WIDTH)
    sinks = attn_sinks.reshape(ATT_HEADS)
    blk = jnp.arange(256) // ATT_HEAD_DIM
    ones = (blk[:, None] == blk[None, :]).astype(BF16)

    xs = x_sample[:, 0]
    wi, qs, kvs, rqs, rks, rvs, rgs, kvt, rqt, rkt = _in_proj(xs, ln1, w_in[0], qg, kg, inv, ones,
                                                              pos=PAST_LEN)
    sink_col = jnp.concatenate([sinks[0::2], sinks[1::2]]).reshape(ATT_HEADS, 1)
    to_fm = lambda c: c[0].transpose(0, 2, 3, 1).reshape(nb, KV_WIDTH, WINDOW)
    from_fm = lambda c: c.reshape(nb, ATT_KV_HEADS, ATT_HEAD_DIM, WINDOW).transpose(0, 3, 1, 2)[None]
    a8, nk, nv = _swa_dec(qs.reshape(nb, ATT_HEADS // 2, LANES), kvs, kvt,
                          to_fm(cache_k_win), to_fm(cache_v_win), sink_col)
    a_s = a8.reshape(nb, ATT_WIDTH)

    xp = x_prompt[0]
    a_out, kv, rq, wo, wu = _proj_swa(xp, ln1, wi, qg, kg, inv, ones, sinks, w_out[0], w_up[0])
    r_out, s_fin, wd = _proj_ret(xp, ln1, wi, inv, rq, rg_g, w_down[0])
    yp, r_s, ns = _mlp_ret(xp, a_out, r_out, wo, ln2, wu, wd,
                           rqt, rkt, rqs, rks, rvs, rgs, rg_g, state_ret[0])

    wb = min(WINDOW, seq)
    kp = kv[seq - wb:, :KV_WIDTH].reshape(1, 1, wb, ATT_KV_HEADS, ATT_HEAD_DIM)
    vp = kv[seq - wb:, KV_WIDTH:].reshape(1, 1, wb, ATT_KV_HEADS, ATT_HEAD_DIM)
    sp = s_fin.reshape(1, 1, RET_HEADS, RET_DK, RET_DV)

    x1s, h2s = _out_proj(xs, a_s, r_s, wo, ln2, tm=nb)
    ys = _mlp(x1s, h2s, wu, wd, tm=nb)

    return (yp[None], ys[:, None, :], kp, vp, sp, from_fm(nk), from_fm(nv), ns[None])
```

```python
import functools
import math

import jax
import jax.numpy as jnp
from jax import lax
from jax.experimental import pallas as pl
from jax.experimental.pallas import tpu as pltpu

D_MODEL = 2048
ATT_HEADS = 16
ATT_KV_HEADS = 2
ATT_HEAD_DIM = 64
WINDOW = 128
RET_HEADS = 4
RET_DK = 256
RET_DV = 256
RET_CHUNK = 128
ROPE_BASE = 10000.0
D_FF = 4 * D_MODEL
EPS = 1e-6
PAST_LEN = 8192

ATT_WIDTH = ATT_HEADS * ATT_HEAD_DIM
KV_WIDTH = ATT_KV_HEADS * ATT_HEAD_DIM
RET_WIDTH = RET_HEADS * RET_DK
LANES = 128
HALF = RET_DK // 2
SPLIT_A = ATT_WIDTH + 2 * KV_WIDTH + RET_WIDTH
SPLIT_B = 3 * RET_WIDTH

F32 = jnp.float32
BF16 = jnp.bfloat16
VMEM_LIMIT = 60 * 1024 * 1024

TM = 512
TF_RET = 1024
BB_ATT = 32

LOG_G = tuple(math.log1p(-(2.0 ** (-5.0 - h))) for h in range(RET_HEADS))


def _dot(a, b):
    return jnp.dot(a, b, preferred_element_type=F32)


def _dot_nt(a, b):
    return lax.dot_general(a, b, (((1,), (1,)), ((), ())), preferred_element_type=F32)


def _dot_tn(a, b):
    return lax.dot_general(a, b, (((0,), (0,)), ((), ())), preferred_element_type=F32)


def _const_spec(shape):
    n = len(shape)
    return pl.BlockSpec(shape, lambda *_: (0,) * n, pipeline_mode=pl.Buffered(1))


def _w_in_spec(part):
    width, start = ((SPLIT_A, 0), (SPLIT_B, SPLIT_A))[part]
    return pl.BlockSpec((pl.Element(D_MODEL), pl.Element(width)), lambda *_: (0, start),
                        pipeline_mode=pl.Buffered(1))


def _params(n_axes=1):
    return pltpu.CompilerParams(dimension_semantics=("arbitrary",) * n_axes,
                                vmem_limit_bytes=VMEM_LIMIT)


def _norm_rows(x_ref, g_ref):
    x = x_ref[...]
    var = jnp.mean(x * x, axis=-1, keepdims=True)
    return (x * lax.rsqrt(var + EPS) * g_ref[...]).astype(BF16)


def _bf16_into(o_ref, v):
    return v.astype(BF16).astype(o_ref.dtype)


def _rope_init(inv_ref, cr_scr, sr_scr, pos_step):
    tm = cr_scr.shape[0]
    row = lax.broadcasted_iota(jnp.int32, (tm, 1), 0)
    ang_r = (pos_step * row).astype(F32) * inv_ref[...]
    cr_scr[...] = jnp.cos(ang_r)
    sr_scr[...] = jnp.sin(ang_r)


def _rope_tables(inv_ref, cr_scr, sr_scr, base_pos):
    base = jnp.zeros((8, 1), jnp.int32) + base_pos
    ang_b = base.astype(F32) * inv_ref[...]
    cb = jnp.cos(ang_b)[:1]
    sb = jnp.sin(ang_b)[:1]
    cr = cr_scr[...]
    sr = sr_scr[...]
    return cb * cr - sb * sr, sb * cr + cb * sr


def _rope_head(r, cos, sin, scale, o_ref, hh):
    x1 = r[:, :HALF]
    x2 = r[:, HALF:]
    o1 = x1 * cos - x2 * sin
    o2 = x2 * cos + x1 * sin
    if scale != 1.0:
        o1 = o1 * scale
        o2 = o2 * scale
    o_ref[:, RET_DK * hh:RET_DK * hh + HALF] = _bf16_into(o_ref, o1)
    o_ref[:, RET_DK * hh + HALF:RET_DK * (hh + 1)] = _bf16_into(o_ref, o2)


def _half_split(a):
    lane = lax.broadcasted_iota(jnp.int32, a.shape, 1)
    lo = lane < ATT_HEAD_DIM
    sw = pltpu.roll(a, ATT_HEAD_DIM, 1)
    zero = jnp.zeros_like(a)
    h0 = (jnp.where(lo, a, zero).astype(BF16), jnp.where(lo, zero, sw).astype(BF16))
    h1 = (jnp.where(lo, sw, zero).astype(BF16), jnp.where(lo, zero, a).astype(BF16))
    return h0, h1


SWA_STACKS = 1


def _swa_phases(sink_ref, q_ref, kvc_ref, kvp_ref, o_ref, has_prev):
    nsub = q_ref.shape[0] // WINDOW
    npair = ATT_HEADS // 2
    row = lax.broadcasted_iota(jnp.int32, (WINDOW, 2 * WINDOW), 0)
    col = lax.broadcasted_iota(jnp.int32, (WINDOW, 2 * WINDOW), 1)
    band_cur = (col >= WINDOW) & (col - WINDOW <= row)
    first_col = jnp.where(has_prev, 0, WINDOW)
    lane_lo = lax.broadcasted_iota(jnp.int32, (WINDOW, LANES), 1) < ATT_HEAD_DIM
    ctx = [{} for _ in range(nsub)]
    st = {}

    ppk = npair // ATT_KV_HEADS

    def stacks(g):
        for m in range(g * SWA_STACKS, (g + 1) * SWA_STACKS):
            j, kh = divmod(m, ATT_KV_HEADS)
            yield j, kh, slice(WINDOW * j, WINDOW * (j + 1)), range(ppk * kh, ppk * (kh + 1))

    def prep(g):
        for j in sorted({j for j, _, _, _ in stacks(g)} - {j for j in range(nsub) if ctx[j]}):
            prev = kvp_ref[...] if j == 0 else kvc_ref[WINDOW * (j - 1):WINDOW * j, :]
            cur = kvc_ref[WINDOW * j:WINDOW * (j + 1), :]
            kk = jnp.concatenate([prev[:, :KV_WIDTH], cur[:, :KV_WIDTH]], axis=0)
            vv = jnp.concatenate([prev[:, KV_WIDTH:], cur[:, KV_WIDTH:]], axis=0)
            ctx[j]["k"] = _half_split(kk)
            ctx[j]["v"] = _half_split(vv)
            lo_col = first_col if j == 0 else 0
            ctx[j]["mask"] = band_cur | ((col < WINDOW) & (col >= row) & (col >= lo_col))

    def qk(g):
        for j, kh, rows, tiles in stacks(g):
            qs = jnp.concatenate([q_ref[rows, LANES * t:LANES * (t + 1)] for t in tiles], axis=0)
            s = [_dot_nt(qs, ctx[j]["k"][kh][par]) for par in range(2)]
            for n, t in enumerate(tiles):
                st[j, t] = {"s": [sp[WINDOW * n:WINDOW * (n + 1), :] for sp in s]}

    def softmax(g):
        for j, kh, rows, tiles in stacks(g):
            for t in tiles:
                p, inv = [], []
                for par in range(2):
                    sink = sink_ref[2 * t + par]
                    s = jnp.where(ctx[j]["mask"], st[j, t]["s"][par], -jnp.inf)
                    mx = jnp.maximum(jnp.max(s, axis=-1, keepdims=True), sink)
                    e = jnp.exp(s - mx)
                    den = jnp.sum(e, axis=-1, keepdims=True) + jnp.exp(sink - mx)
                    p.append(e.astype(BF16))
                    inv.append(1.0 / den)
                st[j, t] = {"p": p, "inv": jnp.where(lane_lo, inv[0], inv[1])}

    def pv(g):
        for j, kh, rows, tiles in stacks(g):
            got = [st.pop((j, t)) for t in tiles]
            acc = sum(_dot(jnp.concatenate([u["p"][par] for u in got], axis=0),
                           ctx[j]["v"][kh][par]) for par in range(2))
            for n, t in enumerate(tiles):
                o_ref[rows, LANES * t:LANES * (t + 1)] = (
                    acc[WINDOW * n:WINDOW * (n + 1), :] * got[n]["inv"]).astype(BF16)

    ngroup = nsub * ATT_KV_HEADS // SWA_STACKS
    return ngroup, prep, qk, softmax, pv


def _emit_pipelined(main, ngroup, prep, first, middle, last, finish=None):
    nstage = ngroup + 2
    done = 0
    prep(0)
    for k in range(nstage):
        if k < ngroup:
            first(k)
        if 0 <= k - 2 < ngroup:
            last(k - 2)
        if 0 <= k - 1 < ngroup:
            middle(k - 1)
        if k + 1 < ngroup:
            prep(k + 1)
        upto = -(-len(main) * (k + 1) // nstage)
        for piece in main[done:upto]:
            piece()
        done = upto
    if finish is not None:
        finish()


def _ret_phases(rq_ref, rk_ref, rv_ref, gt_ref, g_ref, o_ref, s_scr):
    c = RET_CHUNK
    nsub = rq_ref.shape[0] // c
    ri = lax.broadcasted_iota(jnp.int32, (c, c), 0)
    ci = lax.broadcasted_iota(jnp.int32, (c, c), 1)
    rel = (ri - ci).astype(F32)
    idx = lax.broadcasted_iota(jnp.int32, (c, 1), 0).astype(F32)
    head = []
    for hh in range(RET_HEADS):
        lg = LOG_G[hh]
        head.append(dict(
            dmask=jnp.where(rel >= 0, jnp.exp(lg * jnp.maximum(rel, 0.0)), 0.0),
            qdec=jnp.exp(lg * (idx + 1.0)), kdec=jnp.exp(lg * (c - 1.0 - idx)),
            cdec=math.exp(lg * c), cols=slice(RET_DK * hh, RET_DK * (hh + 1))))
    st = {}

    def update_state(g):
        for hh, hd in enumerate(head):
            u = st[g, hh]
            s_scr[hh] = hd["cdec"] * u.pop("s_prev") + u.pop("kv")

    def prep(g):
        if g > 0:
            update_state(g - 1)
        rows = slice(c * g, c * (g + 1))
        for hh, hd in enumerate(head):
            q = rq_ref[rows, hd["cols"]]
            k = rk_ref[rows, hd["cols"]]
            s_prev = s_scr[hh]
            st[g, hh] = dict(q=q, k=k, v=rv_ref[rows, hd["cols"]], s_prev=s_prev,
                             qd=(q.astype(F32) * hd["qdec"]).astype(BF16),
                             kd=(k.astype(F32) * hd["kdec"]).astype(BF16),
                             s_bf=s_prev.astype(BF16))

    def first(g):
        for hh in range(RET_HEADS):
            u = st[g, hh]
            u["att"] = _dot_nt(u.pop("q"), u.pop("k"))
            u["inter"] = _dot(u.pop("qd"), u.pop("s_bf"))
            u["kv"] = _dot_tn(u.pop("kd"), u["v"])

    def middle(g):
        for hh, hd in enumerate(head):
            u = st[g, hh]
            u["att"] = (u["att"] * hd["dmask"]).astype(BF16)

    def last(g):
        rows = slice(c * g, c * (g + 1))
        for hh, hd in enumerate(head):
            u = st[g, hh]
            o = _dot(u.pop("att"), u.pop("v")) + u.pop("inter")
            var = jnp.mean(o * o, axis=-1, keepdims=True)
            on = o * lax.rsqrt(var + EPS) * g_ref[:, hd["cols"]]
            o_ref[rows, hd["cols"]] = (on * gt_ref[rows, hd["cols"]].astype(F32)).astype(BF16)

    return nsub, prep, first, middle, last, functools.partial(update_state, nsub - 1)


def _proj_swa_body(sink_ref, x_ref, g_ref, w_ref, qg_ref, kg_ref, inv_ref, ones_ref,
                   wo_ref, wu_ref,
                   a_out, kv_out, rq_out, wo_bf, wu_bf,
                   q_scr, kv_scr, kvp_scr, cr_scr, sr_scr, *, nblk):
    i = pl.program_id(0)
    tm = x_ref.shape[0]

    @pl.when(i == 0)
    def _():
        _rope_init(inv_ref, cr_scr, sr_scr, 1)
        q_scr[...] = jnp.zeros_like(q_scr)
        kv_scr[...] = jnp.zeros_like(kv_scr)
        kvp_scr[...] = jnp.zeros_like(kvp_scr)

    wo_bf[...] = wo_ref[...].astype(BF16)
    wu_bf[0] = wu_ref[...].astype(BF16)

    cur = lax.rem(i, 2)
    prv = 1 - cur
    blk = jnp.minimum(i, nblk - 1)

    h = _norm_rows(x_ref, g_ref)
    ones = ones_ref[...]
    inv_hd = 1.0 / ATT_HEAD_DIM
    cos, sin = _rope_tables(inv_ref, cr_scr, sr_scr, blk * tm)


    def q_piece(n):
        y = _dot(h, w_ref[:, 512 * n:512 * (n + 1)])
        for t in range(2):
            yt = y[:, 256 * t:256 * (t + 1)]
            ssq = _dot((yt * yt).astype(BF16), ones)
            c0 = 512 * n + 256 * t
            q_scr[cur, :, c0:c0 + 256] = (
                yt * lax.rsqrt(ssq * inv_hd + EPS) * qg_ref[...]).astype(BF16)

    kv0 = ATT_WIDTH
    rq0 = ATT_WIDTH + 2 * KV_WIDTH

    def kv_rq_piece():
        y = _dot(h, w_ref[:, kv0:rq0 + RET_DK])
        k = y[:, :KV_WIDTH]
        ssq = _dot((k * k).astype(BF16), ones[:KV_WIDTH, :KV_WIDTH])
        kn = k * lax.rsqrt(ssq * inv_hd + EPS) * kg_ref[...]
        v = y[:, KV_WIDTH:2 * KV_WIDTH]
        kv_scr[cur, :, :KV_WIDTH] = kn
        kv_scr[cur, :, KV_WIDTH:] = v
        kv_out[:, :KV_WIDTH] = kn
        kv_out[:, KV_WIDTH:] = v
        _rope_head(y[:, 2 * KV_WIDTH:], cos, sin, 1.0, rq_out, 0)

    def rq_piece():
        y = _dot(h, w_ref[:, rq0 + RET_DK:])
        for hh in range(1, RET_HEADS):
            _rope_head(y[:, RET_DK * (hh - 1):RET_DK * hh], cos, sin, 1.0, rq_out, hh)

    main = [functools.partial(q_piece, 0), functools.partial(q_piece, 1), kv_rq_piece, rq_piece]
    _emit_pipelined(main, *_swa_phases(sink_ref, q_scr.at[prv], kv_scr.at[prv], kvp_scr, a_out,
                                       has_prev=i > 1))
    kvp_scr[...] = kv_scr[prv, tm - WINDOW:, :]


def _proj_swa(x, ln_g, w_in, qg, kg, inv, ones, sinks, w_out, w_up):
    m = x.shape[0]
    nblk = m // TM
    cl = lambda i: jnp.minimum(i, nblk - 1)
    row = lambda w: pl.BlockSpec((TM, w), lambda i: (cl(i), 0))
    wo_spec = pl.BlockSpec((w_out.shape[0] // nblk, w_out.shape[1]), lambda i: (cl(i), 0))
    wcast = w_up.shape[1] // nblk
    per_slab = TF_RET // wcast
    assert w_up.shape[1] % nblk == 0 and TF_RET % wcast == 0
    wu_spec = pl.BlockSpec((w_up.shape[0], wcast), lambda i: (0, cl(i)))
    wu_out_spec = pl.BlockSpec((1, w_up.shape[0], wcast),
                               lambda i: (cl(i) // per_slab, 0, cl(i) % per_slab))
    return pl.pallas_call(
        functools.partial(_proj_swa_body, nblk=nblk),
        grid=(nblk + 1,),
        in_specs=[pl.BlockSpec(memory_space=pltpu.SMEM),
                  row(D_MODEL), _const_spec((1, D_MODEL)), _w_in_spec(0),
                  _const_spec((1, 256)), _const_spec((1, KV_WIDTH)),
                  _const_spec((1, HALF)), _const_spec((256, 256)), wo_spec, wu_spec],
        out_specs=[pl.BlockSpec((TM, ATT_WIDTH), lambda i: (jnp.maximum(i - 1, 0), 0)),
                   row(2 * KV_WIDTH), row(RET_WIDTH), wo_spec, wu_out_spec],
        out_shape=(jax.ShapeDtypeStruct((m, ATT_WIDTH), BF16),
                   jax.ShapeDtypeStruct((m, 2 * KV_WIDTH), F32),
                   jax.ShapeDtypeStruct((m, RET_WIDTH), BF16),
                   jax.ShapeDtypeStruct(w_out.shape, BF16),
                   jax.ShapeDtypeStruct((w_up.shape[1] // TF_RET, w_up.shape[0], TF_RET), BF16)),
        scratch_shapes=[pltpu.VMEM((2, TM, ATT_WIDTH), BF16),
                        pltpu.VMEM((2, TM, 2 * KV_WIDTH), F32),
                        pltpu.VMEM((WINDOW, 2 * KV_WIDTH), F32),
                        pltpu.VMEM((TM, HALF), F32), pltpu.VMEM((TM, HALF), F32)],
        compiler_params=_params(),
        name="proj_swa",
    )(sinks, x, ln_g, w_in, qg, kg, inv, ones, w_out, w_up)


def _proj_ret_body(x_ref, g_ref, w_ref, inv_ref, rq_ref, rg_ref, wd_ref,
                   r_out, s_out, wd_bf,
                   rk_scr, rv_scr, gt_scr, s_scr, cr_scr, sr_scr, *, nblk):
    i = pl.program_id(0)
    tm = x_ref.shape[0]

    @pl.when(i == 0)
    def _():
        _rope_init(inv_ref, cr_scr, sr_scr, 1)
        rk_scr[...] = jnp.zeros_like(rk_scr)
        rv_scr[...] = jnp.zeros_like(rv_scr)
        gt_scr[...] = jnp.zeros_like(gt_scr)
        s_scr[...] = jnp.zeros_like(s_scr)

    wd_bf[...] = wd_ref[...].astype(BF16)

    cur = lax.rem(i, 2)
    prv = 1 - cur
    blk = jnp.minimum(i, nblk - 1)

    h = _norm_rows(x_ref, g_ref)
    cos, sin = _rope_tables(inv_ref, cr_scr, sr_scr, blk * tm)

    def rk_piece(n):
        y = _dot(h, w_ref[:, 512 * n:512 * (n + 1)])
        for t in range(2):
            _rope_head(y[:, RET_DK * t:RET_DK * (t + 1)], cos, sin, RET_DK ** -0.5,
                       rk_scr.at[cur], 2 * n + t)

    def rv_piece(n):
        c = slice(512 * n, 512 * (n + 1))
        rv_scr[cur, :, c] = _dot(h, w_ref[:, RET_WIDTH + 512 * n:RET_WIDTH + 512 * (n + 1)]
                                 ).astype(BF16)

    def gate_piece(n):
        c0 = 2 * RET_WIDTH + 512 * n
        rg = _dot(h, w_ref[:, c0:c0 + 512])
        gt_scr[cur, :, 512 * n:512 * (n + 1)] = (rg / (1.0 + jnp.exp(-rg))).astype(BF16)

    main = [functools.partial(f, n) for f in (rk_piece, rv_piece, gate_piece) for n in range(2)]
    _emit_pipelined(main, *_ret_phases(rq_ref, rk_scr.at[prv], rv_scr.at[prv], gt_scr.at[prv],
                                       rg_ref, r_out, s_scr))

    @pl.when(i == nblk)
    def _():
        s_out[...] = s_scr[...]


def _proj_ret(x, ln_g, w_in, inv, rq, rg_g, w_down):
    m = x.shape[0]
    nblk = m // TM
    cl = lambda i: jnp.minimum(i, nblk - 1)
    prev = lambda w: pl.BlockSpec((TM, w), lambda i: (jnp.maximum(i - 1, 0), 0))
    wd_spec = pl.BlockSpec((w_down.shape[0] // nblk, w_down.shape[1]), lambda i: (cl(i), 0))
    state = (RET_HEADS, RET_DK, RET_DV)
    slot = pltpu.VMEM((2, TM, RET_WIDTH), BF16)
    return pl.pallas_call(
        functools.partial(_proj_ret_body, nblk=nblk),
        grid=(nblk + 1,),
        in_specs=[pl.BlockSpec((TM, D_MODEL), lambda i: (cl(i), 0)),
                  _const_spec((1, D_MODEL)), _w_in_spec(1), _const_spec((1, HALF)),
                  prev(RET_WIDTH), _const_spec((1, RET_WIDTH)), wd_spec],
        out_specs=[prev(RET_WIDTH), pl.BlockSpec(state, lambda i: (0, 0, 0)), wd_spec],
        out_shape=(jax.ShapeDtypeStruct((m, RET_WIDTH), BF16),
                   jax.ShapeDtypeStruct(state, F32),
                   jax.ShapeDtypeStruct(w_down.shape, BF16)),
        scratch_shapes=[slot, slot, slot, pltpu.VMEM(state, F32),
                        pltpu.VMEM((TM, HALF), F32), pltpu.VMEM((TM, HALF), F32)],
        compiler_params=_params(),
        name="proj_ret",
    )(x, ln_g, w_in, inv, rq, rg_g, w_down)


TN_IN = 1792
Z_TILE = 256


def _in_proj_body(x_ref, g_ref, w_ref, qg_ref, kg_ref, inv_ref, ones_ref,
                  w_bf, q_out, kv_out, rq_out, rk_out, rv_out, gt_out, kvt_out, rqt_out, rkt_out,
                  h_scr, z_scr, *, pos):
    c = pl.program_id(0)

    @pl.when(c == 0)
    def _():
        h_scr[...] = _norm_rows(x_ref, g_ref)

    wb = w_ref[...].astype(BF16)
    w_bf[...] = wb
    z = _dot(h_scr[...], wb)
    per_step = TN_IN // Z_TILE
    for t in range(per_step):
        z_scr[c * per_step + t] = z[:, Z_TILE * t:Z_TILE * (t + 1)]

    @pl.when(c == pl.num_programs(0) - 1)
    def _():
        tiles = lambda col0, n: [z_scr[col0 // Z_TILE + t] for t in range(n)]
        ones = ones_ref[...]
        inv_hd = 1.0 / ATT_HEAD_DIM
        for t, y in enumerate(tiles(0, ATT_WIDTH // Z_TILE)):
            ssq = _dot((y * y).astype(BF16), ones)
            q_out[:, 256 * t:256 * (t + 1)] = (
                y * lax.rsqrt(ssq * inv_hd + EPS) * qg_ref[...]).astype(BF16)
        (kvr,) = tiles(ATT_WIDTH, 1)
        k = kvr[:, :KV_WIDTH]
        ssq = _dot((k * k).astype(BF16), ones[:KV_WIDTH, :KV_WIDTH])
        kv_out[:, :KV_WIDTH] = k * lax.rsqrt(ssq * inv_hd + EPS) * kg_ref[...]
        kv_out[:, KV_WIDTH:] = kvr[:, KV_WIDTH:]
        ang = jnp.full((8, 1), pos, jnp.int32).astype(F32) * inv_ref[...]
        cos = jnp.cos(ang)[:1]
        sin = jnp.sin(ang)[:1]
        rq0 = ATT_WIDTH + 2 * KV_WIDTH
        for hh, y in enumerate(tiles(rq0, RET_HEADS)):
            _rope_head(y, cos, sin, 1.0, rq_out, hh)
        for hh, y in enumerate(tiles(rq0 + RET_WIDTH, RET_HEADS)):
            _rope_head(y, cos, sin, RET_DK ** -0.5, rk_out, hh)
        for hh, y in enumerate(tiles(rq0 + 2 * RET_WIDTH, RET_HEADS)):
            rv_out[:, RET_DV * hh:RET_DV * (hh + 1)] = _bf16_into(rv_out, y)
        for hh, y in enumerate(tiles(rq0 + 3 * RET_WIDTH, RET_HEADS)):
            gt_out[:, RET_DV * hh:RET_DV * (hh + 1)] = _bf16_into(gt_out, y / (1.0 + jnp.exp(-y)))
        kvt_out[...] = kv_out[...].T
        rqt_out[...] = rq_out[...].T
        rkt_out[...] = rk_out[...].T


def _in_proj(x, ln_g, w_in, qg, kg, inv, ones, pos):
    m = x.shape[0]
    assert m == LANES
    assert RET_DK == Z_TILE and TN_IN % Z_TILE == 0 and w_in.shape[1] % TN_IN == 0
    nstep = w_in.shape[1] // TN_IN
    full = lambda w: pl.BlockSpec((m, w), lambda c: (0, 0))
    colm = lambda w: pl.BlockSpec((w, m), lambda c: (0, 0))
    wcol = pl.BlockSpec((D_MODEL, TN_IN), lambda c: (0, c))
    ret = jax.ShapeDtypeStruct((m, RET_WIDTH), F32)
    ret_t = jax.ShapeDtypeStruct((RET_WIDTH, m), F32)
    return pl.pallas_call(
        functools.partial(_in_proj_body, pos=pos),
        grid=(nstep,),
        in_specs=[_const_spec((m, D_MODEL)), _const_spec((1, D_MODEL)), wcol,
                  _const_spec((1, 256)), _const_spec((1, KV_WIDTH)),
                  _const_spec((1, HALF)), _const_spec((256, 256))],
        out_specs=[wcol, full(ATT_WIDTH), full(2 * KV_WIDTH)] + [full(RET_WIDTH)] * 4
        + [colm(2 * KV_WIDTH), colm(RET_WIDTH), colm(RET_WIDTH)],
        out_shape=(jax.ShapeDtypeStruct(w_in.shape, BF16),
                   jax.ShapeDtypeStruct((m, ATT_WIDTH), BF16),
                   jax.ShapeDtypeStruct((m, 2 * KV_WIDTH), F32), ret, ret, ret, ret,
                   jax.ShapeDtypeStruct((2 * KV_WIDTH, m), F32), ret_t, ret_t),
        scratch_shapes=[pltpu.VMEM((m, D_MODEL), BF16),
                        pltpu.VMEM((w_in.shape[1] // Z_TILE, m, Z_TILE), F32)],
        compiler_params=_params(),
        name="in_proj",
    )(x, ln_g, w_in, qg, kg, inv, ones)


def _this_steps_columns(t_ref, rows, bb):
    shift = lax.rem(LANES - bb * pl.program_id(0), LANES)
    return pltpu.roll(t_ref[rows, :], shift, 1)


def _swa_dec_body(q_ref, kvn_ref, kvt_ref, ck_ref, cv_ref, sink_ref, o_ref, nk_ref, nv_ref):
    bb = q_ref.shape[0]
    npair = ATT_HEADS // 2
    q8 = q_ref[...].astype(F32)
    q8r = pltpu.roll(q8, ATT_HEAD_DIM, 2)
    lane = lax.broadcasted_iota(jnp.int32, q8.shape, 2)
    pair = lax.broadcasted_iota(jnp.int32, q8.shape, 1)
    lo = lane < ATT_HEAD_DIM
    kv0 = pair < npair // ATT_KV_HEADS
    own = lo == kv0
    zero = jnp.zeros_like(q8)
    qe = jnp.where(own, jnp.where(kv0, q8, q8r), zero)
    qo = jnp.where(own, jnp.where(kv0, q8r, q8), zero)
    qb = jnp.concatenate([qe, qo], axis=1)

    ck = ck_ref[...]
    cv = cv_ref[...]
    kn = kvn_ref[:, :KV_WIDTH]
    vn = kvn_ref[:, KV_WIDTH:]
    s = lax.dot_general(qb.astype(BF16), ck.astype(BF16), (((2,), (1,)), ((0,), (0,))),
                        preferred_element_type=F32)
    s_new = jnp.sum(qb * kn[:, None, :], axis=-1, keepdims=True)
    sink = sink_ref[...][None, :, :]
    mx = jnp.maximum(jnp.maximum(jnp.max(s, axis=-1, keepdims=True), s_new), sink)
    p = jnp.exp(s - mx)
    p_new = jnp.exp(s_new - mx)
    den = jnp.sum(p, axis=-1, keepdims=True) + p_new + jnp.exp(sink - mx)
    o = lax.dot_general(p.astype(BF16), cv.astype(BF16), (((2,), (2,)), ((0,), (0,))),
                        preferred_element_type=F32)
    o = (o + p_new * vn[:, None, :]) / den
    oe = o[:, :npair, :]
    oo = o[:, npair:, :]
    oer = pltpu.roll(oe, ATT_HEAD_DIM, 2)
    oor = pltpu.roll(oo, ATT_HEAD_DIM, 2)
    o_ref[...] = jnp.where(lo, jnp.where(kv0, oe, oer), jnp.where(kv0, oor, oo)).astype(BF16)

    newcol = _this_steps_columns(kvt_ref, slice(None), bb)
    last = lax.broadcasted_iota(jnp.int32, (KV_WIDTH, WINDOW), 1) == WINDOW - 1
    for jb in range(bb):
        nk_ref[jb] = jnp.where(last, newcol[:KV_WIDTH, jb:jb + 1],
                               pltpu.roll(ck[jb], WINDOW - 1, 1))
        nv_ref[jb] = jnp.where(last, newcol[KV_WIDTH:, jb:jb + 1],
                               pltpu.roll(cv[jb], WINDOW - 1, 1))


def _swa_dec(q8, kvn, kvt, ck, cv, sink_col):
    b = q8.shape[0]
    bb = BB_ATT
    cache = pl.BlockSpec((bb, KV_WIDTH, WINDOW), lambda i: (i, 0, 0))
    return pl.pallas_call(
        _swa_dec_body,
        grid=(b // bb,),
        in_specs=[pl.BlockSpec((bb, ATT_HEADS // 2, LANES), lambda i: (i, 0, 0)),
                  pl.BlockSpec((bb, 2 * KV_WIDTH), lambda i: (i, 0)),
                  _const_spec(kvt.shape), cache, cache, _const_spec((ATT_HEADS, 1))],
        out_specs=[pl.BlockSpec((bb, ATT_HEADS // 2, LANES), lambda i: (i, 0, 0)), cache, cache],
        out_shape=(jax.ShapeDtypeStruct((b, ATT_HEADS // 2, LANES), BF16),
                   jax.ShapeDtypeStruct(ck.shape, F32),
                   jax.ShapeDtypeStruct(cv.shape, F32)),
        compiler_params=_params(),
        name="swa_dec",
    )(q8, kvn, kvt, ck, cv, sink_col)


def _out_proj_body(x_ref, a_ref, r_ref, w_ref, g_ref, x1_ref, h2_ref):
    x1 = (x_ref[...] + _dot(a_ref[...].astype(BF16), w_ref[:ATT_WIDTH, :])
          + _dot(r_ref[...].astype(BF16), w_ref[ATT_WIDTH:, :]))
    x1_ref[...] = x1
    var = jnp.mean(x1 * x1, axis=-1, keepdims=True)
    h2_ref[...] = (x1 * lax.rsqrt(var + EPS) * g_ref[...]).astype(BF16)


def _out_proj(x, a, r, w, g, tm):
    m = x.shape[0]
    row = lambda w: pl.BlockSpec((tm, w), lambda i: (i, 0))
    return pl.pallas_call(
        _out_proj_body,
        grid=(m // tm,),
        in_specs=[row(D_MODEL), row(ATT_WIDTH), row(RET_WIDTH),
                  _const_spec(w.shape), _const_spec((1, D_MODEL))],
        out_specs=[row(D_MODEL), row(D_MODEL)],
        out_shape=(jax.ShapeDtypeStruct((m, D_MODEL), F32),
                   jax.ShapeDtypeStruct((m, D_MODEL), BF16)),
        compiler_params=_params(),
        name="out_proj",
    )(x, a, r, w, g)


def _mlp_acc(h2_ref, wu_ref, wd_ref, o_ref):
    u = jnp.maximum(_dot(h2_ref[...], wu_ref[0]), 0.0)
    o_ref[...] += _dot((u * u).astype(BF16), wd_ref[...])


def _mlp_body(x1_ref, h2_ref, wu_ref, wd_ref, o_ref):
    @pl.when(pl.program_id(1) == 0)
    def _():
        o_ref[...] = x1_ref[...]

    _mlp_acc(h2_ref, wu_ref, wd_ref, o_ref)


def _mlp_weight_specs():
    return [pl.BlockSpec((1, D_MODEL, TF_RET), lambda i, f: (f, 0, 0)),
            pl.BlockSpec((TF_RET, D_MODEL), lambda i, f: (f, 0))]


def _mlp(x1, h2, wu, wd, tm):
    m = x1.shape[0]
    row = pl.BlockSpec((tm, D_MODEL), lambda i, f: (i, 0))
    return pl.pallas_call(
        _mlp_body,
        grid=(m // tm, D_FF // TF_RET),
        in_specs=[row, row] + _mlp_weight_specs(),
        out_specs=row,
        out_shape=jax.ShapeDtypeStruct((m, D_MODEL), F32),
        compiler_params=_params(2),
        name="mlp",
    )(x1, h2, wu, wd)


RING = 3


def _mlp_ret_body(x1_ref, h2_ref, wu_hbm, wd_hbm,
                  qt_ref, kt_ref, rq_ref, rk_ref, rv_ref, gt_ref, g_ref, s_ref,
                  o_ref, r_ref, ns_ref, wu_buf, wd_buf, sem):
    nf = pl.num_programs(1)
    b = pl.program_id(0) * nf + pl.program_id(1)
    nstep = pl.num_programs(0) * nf

    def slab_copies(step):
        slab = lax.rem(step, nf)
        slot = lax.rem(step, RING)
        rows = pl.ds(pl.multiple_of(slab * TF_RET, TF_RET), TF_RET)
        return (pltpu.make_async_copy(wu_hbm.at[slab], wu_buf.at[slot], sem.at[0, slot]),
                pltpu.make_async_copy(wd_hbm.at[rows, :], wd_buf.at[slot], sem.at[1, slot]))

    @pl.when(b == 0)
    def _():
        for step in range(RING - 1):
            for copy in slab_copies(step):
                copy.start()

    @pl.when(b + RING - 1 < nstep)
    def _():
        for copy in slab_copies(b + RING - 1):
            copy.start()

    for copy in slab_copies(b):
        copy.wait()
    slot = lax.rem(b, RING)

    @pl.when(pl.program_id(1) == 0)
    def _():
        o_ref[...] = x1_ref[...]

    u = jnp.maximum(_dot(h2_ref[...], wu_buf[slot]), 0.0)
    o_ref[...] += _dot((u * u).astype(BF16), wd_buf[slot])

    row = pl.ds(b, 1)
    shift = lax.rem(LANES - b, LANES)
    for hh in range(RET_HEADS):
        g1 = math.exp(LOG_G[hh])
        cols = slice(RET_DK * hh, RET_DK * (hh + 1))
        qc = pltpu.roll(qt_ref[cols, :], shift, 1)[:, :1] * g1
        kc = pltpu.roll(kt_ref[cols, :], shift, 1)[:, :1]
        v = rv_ref[row, cols]
        s0 = s_ref[0, hh]
        ns_ref[0, hh] = g1 * s0 + kc * v
        qk = jnp.sum(rq_ref[row, cols] * rk_ref[row, cols], axis=-1, keepdims=True)
        o = jnp.sum(qc * s0, axis=0, keepdims=True) + qk * v
        var = jnp.mean(o * o, axis=-1, keepdims=True)
        r_ref[row, cols] = o * lax.rsqrt(var + EPS) * g_ref[:, cols] * gt_ref[row, cols]


def _mlp_ret(x1, h2, wu, wd, qt, kt, rq, rk, rv, gt, g, state):
    m = x1.shape[0]
    nb = state.shape[0]
    nf = D_FF // TF_RET
    assert m // TM * nf == nb and nb == LANES
    assert nb >= RING
    row = pl.BlockSpec((TM, D_MODEL), lambda i, f: (i, 0))
    hbm = pl.BlockSpec(memory_space=pl.ANY)
    st = pl.BlockSpec((1, RET_HEADS, RET_DK, RET_DV), lambda i, f: (i * nf + f, 0, 0, 0))
    rows = _const_spec((nb, RET_WIDTH))
    return pl.pallas_call(
        _mlp_ret_body,
        grid=(m // TM, nf),
        in_specs=[row, row, hbm, hbm,
                  _const_spec(qt.shape), _const_spec(kt.shape), rows, rows, rows, rows,
                  _const_spec((1, RET_WIDTH)), st],
        out_specs=[row, pl.BlockSpec((nb, RET_WIDTH), lambda i, f: (0, 0)), st],
        out_shape=(jax.ShapeDtypeStruct((m, D_MODEL), F32),
                   jax.ShapeDtypeStruct((nb, RET_WIDTH), F32),
                   jax.ShapeDtypeStruct(state.shape, F32)),
        scratch_shapes=[pltpu.VMEM((RING, D_MODEL, TF_RET), BF16),
                        pltpu.VMEM((RING, TF_RET, D_MODEL), BF16),
                        pltpu.SemaphoreType.DMA((2, RING))],
        compiler_params=_params(2),
        name="mlp_ret",
    )(x1, h2, wu, wd, qt, kt, rq, rk, rv, gt, g, state)


def kernel(x_prompt, x_sample, cache_k_win, cache_v_win, state_ret, ln1_g, w_in, q_norm_g,
           k_norm_g, attn_sinks, ret_norm_g, w_out, ln2_g, w_up, w_down):
    seq = x_prompt.shape[1]
    nb = x_sample.shape[0]
    assert x_prompt.shape[0] == 1 and x_sample.shape[1] == 1 and w_in.shape[0] == 1
    assert seq % TM == 0 and w_in.shape[2] == SPLIT_A + SPLIT_B

    inv = (ROPE_BASE ** (-jnp.arange(HALF, dtype=F32) / HALF)).reshape(1, HALF)

    ln1 = ln1_g.reshape(1, D_MODEL)
    ln2 = ln2_g.reshape(1, D_MODEL)
    qg = jnp.tile(q_norm_g.reshape(1, ATT_HEAD_DIM) * (ATT_HEAD_DIM ** -0.5), (1, 256 // ATT_HEAD_DIM))
    kg = jnp.tile(k_norm_g.reshape(1, ATT_HEAD_DIM), (1, KV_WIDTH // ATT_HEAD_DIM))
    rg_g = ret_norm_g.reshape(1, RET_WIDTH)
    sinks = attn_sinks.reshape(ATT_HEADS)
    blk = jnp.arange(256) // ATT_HEAD_DIM
    ones = (blk[:, None] == blk[None, :]).astype(BF16)

    xs = x_sample[:, 0]
    wi, qs, kvs, rqs, rks, rvs, rgs, kvt, rqt, rkt = _in_proj(xs, ln1, w_in[0], qg, kg, inv, ones,
                                                              pos=PAST_LEN)
    sink_col = jnp.concatenate([sinks[0::2], sinks[1::2]]).reshape(ATT_HEADS, 1)
    to_fm = lambda c: c[0].transpose(0, 2, 3, 1).reshape(nb, KV_WIDTH, WINDOW)
    from_fm = lambda c: c.reshape(nb, ATT_KV_HEADS, ATT_HEAD_DIM, WINDOW).transpose(0, 3, 1, 2)[None]
    a8, nk, nv = _swa_dec(qs.reshape(nb, ATT_HEADS // 2, LANES), kvs, kvt,
                          to_fm(cache_k_win), to_fm(cache_v_win), sink_col)
    a_s = a8.reshape(nb, ATT_WIDTH)

    xp = x_prompt[0]
    a_out, kv, rq, wo, wu = _proj_swa(xp, ln1, wi, qg, kg, inv, ones, sinks, w_out[0], w_up[0])
    r_out, s_fin, wd = _proj_ret(xp, ln1, wi, inv, rq, rg_g, w_down[0])
    x1, h2 = _out_proj(xp, a_out, r_out, wo, ln2, tm=TM)
    yp, r_s, ns = _mlp_ret(x1, h2, wu, wd, rqt, rkt, rqs, rks, rvs, rgs, rg_g, state_ret[0])

    wb = min(WINDOW, seq)
    kp = kv[seq - wb:, :KV_WIDTH].reshape(1, 1, wb, ATT_KV_HEADS, ATT_HEAD_DIM)
    vp = kv[seq - wb:, KV_WIDTH:].reshape(1, 1, wb, ATT_KV_HEADS, ATT_HEAD_DIM)
    sp = s_fin.reshape(1, 1, RET_HEADS, RET_DK, RET_DV)

    x1s, h2s = _out_proj(xs, a_s, r_s, wo, ln2, tm=nb)
    ys = _mlp(x1s, h2s, wu, wd, tm=nb)

    return (yp[None], ys[:, None, :], kp, vp, sp, from_fm(nk), from_fm(nv), ns[None])
```

```python
import functools
import math

import jax
import jax.numpy as jnp
from jax import lax
from jax.experimental import pallas as pl
from jax.experimental.pallas import tpu as pltpu

D_MODEL = 2048
ATT_HEADS = 16
ATT_KV_HEADS = 2
ATT_HEAD_DIM = 64
WINDOW = 128
RET_HEADS = 4
RET_DK = 256
RET_DV = 256
RET_CHUNK = 128
ROPE_BASE = 10000.0
D_FF = 4 * D_MODEL
EPS = 1e-6
PAST_LEN = 8192

ATT_WIDTH = ATT_HEADS * ATT_HEAD_DIM
KV_WIDTH = ATT_KV_HEADS * ATT_HEAD_DIM
RET_WIDTH = RET_HEADS * RET_DK
LANES = 128
HALF = RET_DK // 2
SPLIT_A = ATT_WIDTH + 2 * KV_WIDTH + RET_WIDTH
SPLIT_B = 3 * RET_WIDTH

F32 = jnp.float32
BF16 = jnp.bfloat16
VMEM_LIMIT = 60 * 1024 * 1024

TM = 512
TF_RET = 1024
BB_ATT = 32

LOG_G = tuple(math.log1p(-(2.0 ** (-5.0 - h))) for h in range(RET_HEADS))


def _dot(a, b):
    return jnp.dot(a, b, preferred_element_type=F32)


def _dot_nt(a, b):
    return lax.dot_general(a, b, (((1,), (1,)), ((), ())), preferred_element_type=F32)


def _dot_tn(a, b):
    return lax.dot_general(a, b, (((0,), (0,)), ((), ())), preferred_element_type=F32)


def _const_spec(shape):
    n = len(shape)
    return pl.BlockSpec(shape, lambda *_: (0,) * n, pipeline_mode=pl.Buffered(1))


def _w_in_spec(part):
    width, start = ((SPLIT_A, 0), (SPLIT_B, SPLIT_A))[part]
    return pl.BlockSpec((pl.Element(D_MODEL), pl.Element(width)), lambda *_: (0, start),
                        pipeline_mode=pl.Buffered(1))


def _params(n_axes=1):
    return pltpu.CompilerParams(dimension_semantics=("arbitrary",) * n_axes,
                                vmem_limit_bytes=VMEM_LIMIT)


def _norm_rows(x_ref, g_ref):
    x = x_ref[...]
    var = jnp.mean(x * x, axis=-1, keepdims=True)
    return (x * lax.rsqrt(var + EPS) * g_ref[...]).astype(BF16)


def _bf16_into(o_ref, v):
    return v.astype(BF16).astype(o_ref.dtype)


def _rope_init(inv_ref, cr_scr, sr_scr, pos_step):
    tm = cr_scr.shape[0]
    row = lax.broadcasted_iota(jnp.int32, (tm, 1), 0)
    ang_r = (pos_step * row).astype(F32) * inv_ref[...]
    cr_scr[...] = jnp.cos(ang_r)
    sr_scr[...] = jnp.sin(ang_r)


def _rope_tables(inv_ref, cr_scr, sr_scr, base_pos):
    base = jnp.zeros((8, 1), jnp.int32) + base_pos
    ang_b = base.astype(F32) * inv_ref[...]
    cb = jnp.cos(ang_b)[:1]
    sb = jnp.sin(ang_b)[:1]
    cr = cr_scr[...]
    sr = sr_scr[...]
    return cb * cr - sb * sr, sb * cr + cb * sr


def _rope_head(r, cos, sin, scale, o_ref, hh):
    x1 = r[:, :HALF]
    x2 = r[:, HALF:]
    o1 = x1 * cos - x2 * sin
    o2 = x2 * cos + x1 * sin
    if scale != 1.0:
        o1 = o1 * scale
        o2 = o2 * scale
    o_ref[:, RET_DK * hh:RET_DK * hh + HALF] = _bf16_into(o_ref, o1)
    o_ref[:, RET_DK * hh + HALF:RET_DK * (hh + 1)] = _bf16_into(o_ref, o2)


def _half_split(a):
    lane = lax.broadcasted_iota(jnp.int32, a.shape, 1)
    lo = lane < ATT_HEAD_DIM
    sw = pltpu.roll(a, ATT_HEAD_DIM, 1)
    zero = jnp.zeros_like(a)
    h0 = (jnp.where(lo, a, zero).astype(BF16), jnp.where(lo, zero, sw).astype(BF16))
    h1 = (jnp.where(lo, sw, zero).astype(BF16), jnp.where(lo, zero, a).astype(BF16))
    return h0, h1


SWA_STACKS = 1


def _swa_phases(sink_ref, q_ref, kvc_ref, kvp_ref, o_ref, has_prev):
    nsub = q_ref.shape[0] // WINDOW
    npair = ATT_HEADS // 2
    row = lax.broadcasted_iota(jnp.int32, (WINDOW, 2 * WINDOW), 0)
    col = lax.broadcasted_iota(jnp.int32, (WINDOW, 2 * WINDOW), 1)
    band_cur = (col >= WINDOW) & (col - WINDOW <= row)
    first_col = jnp.where(has_prev, 0, WINDOW)
    lane_lo = lax.broadcasted_iota(jnp.int32, (WINDOW, LANES), 1) < ATT_HEAD_DIM
    ctx = [{} for _ in range(nsub)]
    st = {}

    ppk = npair // ATT_KV_HEADS

    def stacks(g):
        for m in range(g * SWA_STACKS, (g + 1) * SWA_STACKS):
            j, kh = divmod(m, ATT_KV_HEADS)
            yield j, kh, slice(WINDOW * j, WINDOW * (j + 1)), range(ppk * kh, ppk * (kh + 1))

    def prep(g):
        for j in sorted({j for j, _, _, _ in stacks(g)} - {j for j in range(nsub) if ctx[j]}):
            prev = kvp_ref[...] if j == 0 else kvc_ref[WINDOW * (j - 1):WINDOW * j, :]
            cur = kvc_ref[WINDOW * j:WINDOW * (j + 1), :]
            kk = jnp.concatenate([prev[:, :KV_WIDTH], cur[:, :KV_WIDTH]], axis=0)
            vv = jnp.concatenate([prev[:, KV_WIDTH:], cur[:, KV_WIDTH:]], axis=0)
            ctx[j]["k"] = _half_split(kk)
            ctx[j]["v"] = _half_split(vv)
            lo_col = first_col if j == 0 else 0
            ctx[j]["mask"] = band_cur | ((col < WINDOW) & (col >= row) & (col >= lo_col))

    def qk(g):
        for j, kh, rows, tiles in stacks(g):
            qs = jnp.concatenate([q_ref[rows, LANES * t:LANES * (t + 1)] for t in tiles], axis=0)
            s = [_dot_nt(qs, ctx[j]["k"][kh][par]) for par in range(2)]
            for n, t in enumerate(tiles):
                st[j, t] = {"s": [sp[WINDOW * n:WINDOW * (n + 1), :] for sp in s]}

    def softmax(g):
        for j, kh, rows, tiles in stacks(g):
            for t in tiles:
                p, inv = [], []
                for par in range(2):
                    sink = sink_ref[2 * t + par]
                    s = jnp.where(ctx[j]["mask"], st[j, t]["s"][par], -jnp.inf)
                    mx = jnp.maximum(jnp.max(s, axis=-1, keepdims=True), sink)
                    e = jnp.exp(s - mx)
                    den = jnp.sum(e, axis=-1, keepdims=True) + jnp.exp(sink - mx)
                    p.append(e.astype(BF16))
                    inv.append(1.0 / den)
                st[j, t] = {"p": p, "inv": jnp.where(lane_lo, inv[0], inv[1])}

    def pv(g):
        for j, kh, rows, tiles in stacks(g):
            got = [st.pop((j, t)) for t in tiles]
            acc = sum(_dot(jnp.concatenate([u["p"][par] for u in got], axis=0),
                           ctx[j]["v"][kh][par]) for par in range(2))
            for n, t in enumerate(tiles):
                o_ref[rows, LANES * t:LANES * (t + 1)] = (
                    acc[WINDOW * n:WINDOW * (n + 1), :] * got[n]["inv"]).astype(BF16)

    ngroup = nsub * ATT_KV_HEADS // SWA_STACKS
    return ngroup, prep, qk, softmax, pv


def _emit_pipelined(main, ngroup, prep, first, middle, last, finish=None):
    nstage = ngroup + 2
    done = 0
    prep(0)
    for k in range(nstage):
        if k < ngroup:
            first(k)
        if 0 <= k - 2 < ngroup:
            last(k - 2)
        if 0 <= k - 1 < ngroup:
            middle(k - 1)
        if k + 1 < ngroup:
            prep(k + 1)
        upto = -(-len(main) * (k + 1) // nstage)
        for piece in main[done:upto]:
            piece()
        done = upto
    if finish is not None:
        finish()


def _ret_phases(rq_ref, rk_ref, rv_ref, gt_ref, g_ref, o_ref, s_scr):
    c = RET_CHUNK
    nsub = rq_ref.shape[0] // c
    ri = lax.broadcasted_iota(jnp.int32, (c, c), 0)
    ci = lax.broadcasted_iota(jnp.int32, (c, c), 1)
    rel = (ri - ci).astype(F32)
    idx = lax.broadcasted_iota(jnp.int32, (c, 1), 0).astype(F32)
    head = []
    for hh in range(RET_HEADS):
        lg = LOG_G[hh]
        head.append(dict(
            dmask=jnp.where(rel >= 0, jnp.exp(lg * jnp.maximum(rel, 0.0)), 0.0),
            qdec=jnp.exp(lg * (idx + 1.0)), kdec=jnp.exp(lg * (c - 1.0 - idx)),
            cdec=math.exp(lg * c), cols=slice(RET_DK * hh, RET_DK * (hh + 1))))
    st = {}

    def update_state(g):
        for hh, hd in enumerate(head):
            u = st[g, hh]
            s_scr[hh] = hd["cdec"] * u.pop("s_prev") + u.pop("kv")

    def prep(g):
        if g > 0:
            update_state(g - 1)
        rows = slice(c * g, c * (g + 1))
        for hh, hd in enumerate(head):
            q = rq_ref[rows, hd["cols"]]
            k = rk_ref[rows, hd["cols"]]
            s_prev = s_scr[hh]
            st[g, hh] = dict(q=q, k=k, v=rv_ref[rows, hd["cols"]], s_prev=s_prev,
                             qd=(q.astype(F32) * hd["qdec"]).astype(BF16),
                             kd=(k.astype(F32) * hd["kdec"]).astype(BF16),
                             s_bf=s_prev.astype(BF16))

    def first(g):
        for hh in range(RET_HEADS):
            u = st[g, hh]
            u["att"] = _dot_nt(u.pop("q"), u.pop("k"))
            u["inter"] = _dot(u.pop("qd"), u.pop("s_bf"))
            u["kv"] = _dot_tn(u.pop("kd"), u["v"])

    def middle(g):
        for hh, hd in enumerate(head):
            u = st[g, hh]
            u["att"] = (u["att"] * hd["dmask"]).astype(BF16)

    def last(g):
        rows = slice(c * g, c * (g + 1))
        for hh, hd in enumerate(head):
            u = st[g, hh]
            o = _dot(u.pop("att"), u.pop("v")) + u.pop("inter")
            var = jnp.mean(o * o, axis=-1, keepdims=True)
            on = o * lax.rsqrt(var + EPS) * g_ref[:, hd["cols"]]
            o_ref[rows, hd["cols"]] = (on * gt_ref[rows, hd["cols"]].astype(F32)).astype(BF16)

    return nsub, prep, first, middle, last, functools.partial(update_state, nsub - 1)


def _proj_swa_body(sink_ref, x_ref, g_ref, w_ref, qg_ref, kg_ref, inv_ref, ones_ref,
                   wo_ref, wu_ref,
                   a_out, kv_out, rq_out, wo_bf, wu_bf,
                   q_scr, kv_scr, kvp_scr, cr_scr, sr_scr, *, nblk):
    i = pl.program_id(0)
    tm = x_ref.shape[0]

    @pl.when(i == 0)
    def _():
        _rope_init(inv_ref, cr_scr, sr_scr, 1)
        q_scr[...] = jnp.zeros_like(q_scr)
        kv_scr[...] = jnp.zeros_like(kv_scr)
        kvp_scr[...] = jnp.zeros_like(kvp_scr)

    wo_bf[...] = wo_ref[...].astype(BF16)
    wu_bf[0] = wu_ref[...].astype(BF16)

    cur = lax.rem(i, 2)
    prv = 1 - cur
    blk = jnp.minimum(i, nblk - 1)

    h = _norm_rows(x_ref, g_ref)
    ones = ones_ref[...]
    inv_hd = 1.0 / ATT_HEAD_DIM
    cos, sin = _rope_tables(inv_ref, cr_scr, sr_scr, blk * tm)


    def q_piece(n):
        y = _dot(h, w_ref[:, 512 * n:512 * (n + 1)])
        for t in range(2):
            yt = y[:, 256 * t:256 * (t + 1)]
            ssq = _dot((yt * yt).astype(BF16), ones)
            c0 = 512 * n + 256 * t
            q_scr[cur, :, c0:c0 + 256] = (
                yt * lax.rsqrt(ssq * inv_hd + EPS) * qg_ref[...]).astype(BF16)

    kv0 = ATT_WIDTH
    rq0 = ATT_WIDTH + 2 * KV_WIDTH

    def kv_rq_piece():
        y = _dot(h, w_ref[:, kv0:rq0 + RET_DK])
        k = y[:, :KV_WIDTH]
        ssq = _dot((k * k).astype(BF16), ones[:KV_WIDTH, :KV_WIDTH])
        kn = k * lax.rsqrt(ssq * inv_hd + EPS) * kg_ref[...]
        v = y[:, KV_WIDTH:2 * KV_WIDTH]
        kv_scr[cur, :, :KV_WIDTH] = kn
        kv_scr[cur, :, KV_WIDTH:] = v
        kv_out[:, :KV_WIDTH] = kn
        kv_out[:, KV_WIDTH:] = v
        _rope_head(y[:, 2 * KV_WIDTH:], cos, sin, 1.0, rq_out, 0)

    def rq_piece():
        y = _dot(h, w_ref[:, rq0 + RET_DK:])
        for hh in range(1, RET_HEADS):
            _rope_head(y[:, RET_DK * (hh - 1):RET_DK * hh], cos, sin, 1.0, rq_out, hh)

    main = [functools.partial(q_piece, 0), functools.partial(q_piece, 1), kv_rq_piece, rq_piece]
    _emit_pipelined(main, *_swa_phases(sink_ref, q_scr.at[prv], kv_scr.at[prv], kvp_scr, a_out,
                                       has_prev=i > 1))
    kvp_scr[...] = kv_scr[prv, tm - WINDOW:, :]


def _proj_swa(x, ln_g, w_in, qg, kg, inv, ones, sinks, w_out, w_up):
    m = x.shape[0]
    nblk = m // TM
    cl = lambda i: jnp.minimum(i, nblk - 1)
    row = lambda w: pl.BlockSpec((TM, w), lambda i: (cl(i), 0))
    wo_spec = pl.BlockSpec((w_out.shape[0] // nblk, w_out.shape[1]), lambda i: (cl(i), 0))
    wcast = w_up.shape[1] // nblk
    per_slab = TF_RET // wcast
    assert w_up.shape[1] % nblk == 0 and TF_RET % wcast == 0
    wu_spec = pl.BlockSpec((w_up.shape[0], wcast), lambda i: (0, cl(i)))
    wu_out_spec = pl.BlockSpec((1, w_up.shape[0], wcast),
                               lambda i: (cl(i) // per_slab, 0, cl(i) % per_slab))
    return pl.pallas_call(
        functools.partial(_proj_swa_body, nblk=nblk),
        grid=(nblk + 1,),
        in_specs=[pl.BlockSpec(memory_space=pltpu.SMEM),
                  row(D_MODEL), _const_spec((1, D_MODEL)), _w_in_spec(0),
                  _const_spec((1, 256)), _const_spec((1, KV_WIDTH)),
                  _const_spec((1, HALF)), _const_spec((256, 256)), wo_spec, wu_spec],
        out_specs=[pl.BlockSpec((TM, ATT_WIDTH), lambda i: (jnp.maximum(i - 1, 0), 0)),
                   row(2 * KV_WIDTH), row(RET_WIDTH), wo_spec, wu_out_spec],
        out_shape=(jax.ShapeDtypeStruct((m, ATT_WIDTH), BF16),
                   jax.ShapeDtypeStruct((m, 2 * KV_WIDTH), F32),
                   jax.ShapeDtypeStruct((m, RET_WIDTH), BF16),
                   jax.ShapeDtypeStruct(w_out.shape, BF16),
                   jax.ShapeDtypeStruct((w_up.shape[1] // TF_RET, w_up.shape[0], TF_RET), BF16)),
        scratch_shapes=[pltpu.VMEM((2, TM, ATT_WIDTH), BF16),
                        pltpu.VMEM((2, TM, 2 * KV_WIDTH), F32),
                        pltpu.VMEM((WINDOW, 2 * KV_WIDTH), F32),
                        pltpu.VMEM((TM, HALF), F32), pltpu.VMEM((TM, HALF), F32)],
        compiler_params=_params(),
        name="proj_swa",
    )(sinks, x, ln_g, w_in, qg, kg, inv, ones, w_out, w_up)


def _proj_ret_body(x_ref, g_ref, w_ref, inv_ref, rq_ref, rg_ref, wd_ref,
                   r_out, s_out, wd_bf,
                   rk_scr, rv_scr, gt_scr, s_scr, cr_scr, sr_scr, *, nblk):
    i = pl.program_id(0)
    tm = x_ref.shape[0]

    @pl.when(i == 0)
    def _():
        _rope_init(inv_ref, cr_scr, sr_scr, 1)
        rk_scr[...] = jnp.zeros_like(rk_scr)
        rv_scr[...] = jnp.zeros_like(rv_scr)
        gt_scr[...] = jnp.zeros_like(gt_scr)
        s_scr[...] = jnp.zeros_like(s_scr)

    wd_bf[...] = wd_ref[...].astype(BF16)

    cur = lax.rem(i, 2)
    prv = 1 - cur
    blk = jnp.minimum(i, nblk - 1)

    h = _norm_rows(x_ref, g_ref)
    cos, sin = _rope_tables(inv_ref, cr_scr, sr_scr, blk * tm)

    def rk_piece(n):
        y = _dot(h, w_ref[:, 512 * n:512 * (n + 1)])
        for t in range(2):
            _rope_head(y[:, RET_DK * t:RET_DK * (t + 1)], cos, sin, RET_DK ** -0.5,
                       rk_scr.at[cur], 2 * n + t)

    def rv_piece(n):
        c = slice(512 * n, 512 * (n + 1))
        rv_scr[cur, :, c] = _dot(h, w_ref[:, RET_WIDTH + 512 * n:RET_WIDTH + 512 * (n + 1)]
                                 ).astype(BF16)

    def gate_piece(n):
        c0 = 2 * RET_WIDTH + 512 * n
        rg = _dot(h, w_ref[:, c0:c0 + 512])
        gt_scr[cur, :, 512 * n:512 * (n + 1)] = (rg / (1.0 + jnp.exp(-rg))).astype(BF16)

    main = [functools.partial(f, n) for f in (rk_piece, rv_piece, gate_piece) for n in range(2)]
    _emit_pipelined(main, *_ret_phases(rq_ref, rk_scr.at[prv], rv_scr.at[prv], gt_scr.at[prv],
                                       rg_ref, r_out, s_scr))

    @pl.when(i == nblk)
    def _():
        s_out[...] = s_scr[...]


def _proj_ret(x, ln_g, w_in, inv, rq, rg_g, w_down):
    m = x.shape[0]
    nblk = m // TM
    cl = lambda i: jnp.minimum(i, nblk - 1)
    prev = lambda w: pl.BlockSpec((TM, w), lambda i: (jnp.maximum(i - 1, 0), 0))
    wd_spec = pl.BlockSpec((w_down.shape[0] // nblk, w_down.shape[1]), lambda i: (cl(i), 0))
    state = (RET_HEADS, RET_DK, RET_DV)
    slot = pltpu.VMEM((2, TM, RET_WIDTH), BF16)
    return pl.pallas_call(
        functools.partial(_proj_ret_body, nblk=nblk),
        grid=(nblk + 1,),
        in_specs=[pl.BlockSpec((TM, D_MODEL), lambda i: (cl(i), 0)),
                  _const_spec((1, D_MODEL)), _w_in_spec(1), _const_spec((1, HALF)),
                  prev(RET_WIDTH), _const_spec((1, RET_WIDTH)), wd_spec],
        out_specs=[prev(RET_WIDTH), pl.BlockSpec(state, lambda i: (0, 0, 0)), wd_spec],
        out_shape=(jax.ShapeDtypeStruct((m, RET_WIDTH), BF16),
                   jax.ShapeDtypeStruct(state, F32),
                   jax.ShapeDtypeStruct(w_down.shape, BF16)),
        scratch_shapes=[slot, slot, slot, pltpu.VMEM(state, F32),
                        pltpu.VMEM((TM, HALF), F32), pltpu.VMEM((TM, HALF), F32)],
        compiler_params=_params(),
        name="proj_ret",
    )(x, ln_g, w_in, inv, rq, rg_g, w_down)


TN_IN = 1792
Z_TILE = 256


def _in_proj_body(x_ref, g_ref, w_ref, qg_ref, kg_ref, inv_ref, ones_ref,
                  w_bf, q_out, kv_out, rq_out, rk_out, rv_out, gt_out, kvt_out, rqt_out, rkt_out,
                  h_scr, z_scr, *, pos):
    c = pl.program_id(0)

    @pl.when(c == 0)
    def _():
        h_scr[...] = _norm_rows(x_ref, g_ref)

    wb = w_ref[...].astype(BF16)
    w_bf[...] = wb
    z = _dot(h_scr[...], wb)
    per_step = TN_IN // Z_TILE
    for t in range(per_step):
        z_scr[c * per_step + t] = z[:, Z_TILE * t:Z_TILE * (t + 1)]

    @pl.when(c == pl.num_programs(0) - 1)
    def _():
        tiles = lambda col0, n: [z_scr[col0 // Z_TILE + t] for t in range(n)]
        ones = ones_ref[...]
        inv_hd = 1.0 / ATT_HEAD_DIM
        for t, y in enumerate(tiles(0, ATT_WIDTH // Z_TILE)):
            ssq = _dot((y * y).astype(BF16), ones)
            q_out[:, 256 * t:256 * (t + 1)] = (
                y * lax.rsqrt(ssq * inv_hd + EPS) * qg_ref[...]).astype(BF16)
        (kvr,) = tiles(ATT_WIDTH, 1)
        k = kvr[:, :KV_WIDTH]
        ssq = _dot((k * k).astype(BF16), ones[:KV_WIDTH, :KV_WIDTH])
        kv_out[:, :KV_WIDTH] = k * lax.rsqrt(ssq * inv_hd + EPS) * kg_ref[...]
        kv_out[:, KV_WIDTH:] = kvr[:, KV_WIDTH:]
        ang = jnp.full((8, 1), pos, jnp.int32).astype(F32) * inv_ref[...]
        cos = jnp.cos(ang)[:1]
        sin = jnp.sin(ang)[:1]
        rq0 = ATT_WIDTH + 2 * KV_WIDTH
        for hh, y in enumerate(tiles(rq0, RET_HEADS)):
            _rope_head(y, cos, sin, 1.0, rq_out, hh)
        for hh, y in enumerate(tiles(rq0 + RET_WIDTH, RET_HEADS)):
            _rope_head(y, cos, sin, RET_DK ** -0.5, rk_out, hh)
        for hh, y in enumerate(tiles(rq0 + 2 * RET_WIDTH, RET_HEADS)):
            rv_out[:, RET_DV * hh:RET_DV * (hh + 1)] = _bf16_into(rv_out, y)
        for hh, y in enumerate(tiles(rq0 + 3 * RET_WIDTH, RET_HEADS)):
            gt_out[:, RET_DV * hh:RET_DV * (hh + 1)] = _bf16_into(gt_out, y / (1.0 + jnp.exp(-y)))
        kvt_out[...] = kv_out[...].T
        rqt_out[...] = rq_out[...].T
        rkt_out[...] = rk_out[...].T


def _in_proj(x, ln_g, w_in, qg, kg, inv, ones, pos):
    m = x.shape[0]
    assert m == LANES
    assert RET_DK == Z_TILE and TN_IN % Z_TILE == 0 and w_in.shape[1] % TN_IN == 0
    nstep = w_in.shape[1] // TN_IN
    full = lambda w: pl.BlockSpec((m, w), lambda c: (0, 0))
    colm = lambda w: pl.BlockSpec((w, m), lambda c: (0, 0))
    wcol = pl.BlockSpec((D_MODEL, TN_IN), lambda c: (0, c))
    ret = jax.ShapeDtypeStruct((m, RET_WIDTH), F32)
    ret_t = jax.ShapeDtypeStruct((RET_WIDTH, m), F32)
    return pl.pallas_call(
        functools.partial(_in_proj_body, pos=pos),
        grid=(nstep,),
        in_specs=[_const_spec((m, D_MODEL)), _const_spec((1, D_MODEL)), wcol,
                  _const_spec((1, 256)), _const_spec((1, KV_WIDTH)),
                  _const_spec((1, HALF)), _const_spec((256, 256))],
        out_specs=[wcol, full(ATT_WIDTH), full(2 * KV_WIDTH)] + [full(RET_WIDTH)] * 4
        + [colm(2 * KV_WIDTH), colm(RET_WIDTH), colm(RET_WIDTH)],
        out_shape=(jax.ShapeDtypeStruct(w_in.shape, BF16),
                   jax.ShapeDtypeStruct((m, ATT_WIDTH), BF16),
                   jax.ShapeDtypeStruct((m, 2 * KV_WIDTH), F32), ret, ret, ret, ret,
                   jax.ShapeDtypeStruct((2 * KV_WIDTH, m), F32), ret_t, ret_t),
        scratch_shapes=[pltpu.VMEM((m, D_MODEL), BF16),
                        pltpu.VMEM((w_in.shape[1] // Z_TILE, m, Z_TILE), F32)],
        compiler_params=_params(),
        name="in_proj",
    )(x, ln_g, w_in, qg, kg, inv, ones)


def _this_steps_columns(t_ref, rows, bb):
    shift = lax.rem(LANES - bb * pl.program_id(0), LANES)
    return pltpu.roll(t_ref[rows, :], shift, 1)


def _swa_dec_body(q_ref, kvn_ref, kvt_ref, ck_ref, cv_ref, sink_ref, o_ref, nk_ref, nv_ref):
    bb = q_ref.shape[0]
    npair = ATT_HEADS // 2
    q8 = q_ref[...].astype(F32)
    q8r = pltpu.roll(q8, ATT_HEAD_DIM, 2)
    lane = lax.broadcasted_iota(jnp.int32, q8.shape, 2)
    pair = lax.broadcasted_iota(jnp.int32, q8.shape, 1)
    lo = lane < ATT_HEAD_DIM
    kv0 = pair < npair // ATT_KV_HEADS
    own = lo == kv0
    zero = jnp.zeros_like(q8)
    qe = jnp.where(own, jnp.where(kv0, q8, q8r), zero)
    qo = jnp.where(own, jnp.where(kv0, q8r, q8), zero)
    qb = jnp.concatenate([qe, qo], axis=1)

    ck = ck_ref[...]
    cv = cv_ref[...]
    kn = kvn_ref[:, :KV_WIDTH]
    vn = kvn_ref[:, KV_WIDTH:]
    s = lax.dot_general(qb.astype(BF16), ck.astype(BF16), (((2,), (1,)), ((0,), (0,))),
                        preferred_element_type=F32)
    s_new = jnp.sum(qb * kn[:, None, :], axis=-1, keepdims=True)
    sink = sink_ref[...][None, :, :]
    mx = jnp.maximum(jnp.maximum(jnp.max(s, axis=-1, keepdims=True), s_new), sink)
    p = jnp.exp(s - mx)
    p_new = jnp.exp(s_new - mx)
    den = jnp.sum(p, axis=-1, keepdims=True) + p_new + jnp.exp(sink - mx)
    o = lax.dot_general(p.astype(BF16), cv.astype(BF16), (((2,), (2,)), ((0,), (0,))),
                        preferred_element_type=F32)
    o = (o + p_new * vn[:, None, :]) / den
    oe = o[:, :npair, :]
    oo = o[:, npair:, :]
    oer = pltpu.roll(oe, ATT_HEAD_DIM, 2)
    oor = pltpu.roll(oo, ATT_HEAD_DIM, 2)
    o_ref[...] = jnp.where(lo, jnp.where(kv0, oe, oer), jnp.where(kv0, oor, oo)).astype(BF16)

    newcol = _this_steps_columns(kvt_ref, slice(None), bb)
    last = lax.broadcasted_iota(jnp.int32, (KV_WIDTH, WINDOW), 1) == WINDOW - 1
    for jb in range(bb):
        nk_ref[jb] = jnp.where(last, newcol[:KV_WIDTH, jb:jb + 1],
                               pltpu.roll(ck[jb], WINDOW - 1, 1))
        nv_ref[jb] = jnp.where(last, newcol[KV_WIDTH:, jb:jb + 1],
                               pltpu.roll(cv[jb], WINDOW - 1, 1))


def _swa_dec(q8, kvn, kvt, ck, cv, sink_col):
    b = q8.shape[0]
    bb = BB_ATT
    cache = pl.BlockSpec((bb, KV_WIDTH, WINDOW), lambda i: (i, 0, 0))
    return pl.pallas_call(
        _swa_dec_body,
        grid=(b // bb,),
        in_specs=[pl.BlockSpec((bb, ATT_HEADS // 2, LANES), lambda i: (i, 0, 0)),
                  pl.BlockSpec((bb, 2 * KV_WIDTH), lambda i: (i, 0)),
                  _const_spec(kvt.shape), cache, cache, _const_spec((ATT_HEADS, 1))],
        out_specs=[pl.BlockSpec((bb, ATT_HEADS // 2, LANES), lambda i: (i, 0, 0)), cache, cache],
        out_shape=(jax.ShapeDtypeStruct((b, ATT_HEADS // 2, LANES), BF16),
                   jax.ShapeDtypeStruct(ck.shape, F32),
                   jax.ShapeDtypeStruct(cv.shape, F32)),
        compiler_params=_params(),
        name="swa_dec",
    )(q8, kvn, kvt, ck, cv, sink_col)


def _out_proj_body(x_ref, a_ref, r_ref, w_ref, g_ref, x1_ref, h2_ref, *ring):
    if ring:
        x_buf, sem = ring
        tm = x_buf.shape[1]

        def x_copy(s):
            rows = pl.ds(pl.multiple_of(s * tm, tm), tm)
            slot = lax.rem(s, RING)
            return (pltpu.make_async_copy(x_ref.at[rows, :], x_buf.at[slot], sem.at[slot]),)

        x = x_buf[_ring_slot(pl.program_id(0), pl.num_programs(0), x_copy)]
    else:
        x = x_ref[...]
    x1 = (x + _dot(a_ref[...].astype(BF16), w_ref[:ATT_WIDTH, :])
          + _dot(r_ref[...].astype(BF16), w_ref[ATT_WIDTH:, :]))
    x1_ref[...] = x1
    var = jnp.mean(x1 * x1, axis=-1, keepdims=True)
    h2_ref[...] = (x1 * lax.rsqrt(var + EPS) * g_ref[...]).astype(BF16)


def _out_proj(x, a, r, w, g, tm):
    m = x.shape[0]
    ring = m // tm >= RING
    row = lambda w: pl.BlockSpec((tm, w), lambda i: (i, 0))
    return pl.pallas_call(
        _out_proj_body,
        grid=(m // tm,),
        in_specs=[_HBM if ring else row(D_MODEL), row(ATT_WIDTH), row(RET_WIDTH),
                  _const_spec(w.shape), _const_spec((1, D_MODEL))],
        out_specs=[row(D_MODEL), row(D_MODEL)],
        out_shape=(jax.ShapeDtypeStruct((m, D_MODEL), F32),
                   jax.ShapeDtypeStruct((m, D_MODEL), BF16)),
        scratch_shapes=[pltpu.VMEM((RING, tm, D_MODEL), F32),
                        pltpu.SemaphoreType.DMA((RING,))] if ring else [],
        compiler_params=_params(),
        name="out_proj",
    )(x, a, r, w, g)


RING = 3


def _ring_slot(step, nstep, copies_for):
    @pl.when(step == 0)
    def _():
        for s in range(RING - 1):
            for copy in copies_for(s):
                copy.start()

    @pl.when(step + RING - 1 < nstep)
    def _():
        for copy in copies_for(step + RING - 1):
            copy.start()

    for copy in copies_for(step):
        copy.wait()
    return lax.rem(step, RING)


def _mlp_step(x1_ref, h2_ref, wu_hbm, wd_hbm, o_ref, wu_buf, wd_buf, sem):
    nf = pl.num_programs(1)
    step = pl.program_id(0) * nf + pl.program_id(1)

    def slab_copies(s):
        slab = lax.rem(s, nf)
        slot = lax.rem(s, RING)
        rows = pl.ds(pl.multiple_of(slab * TF_RET, TF_RET), TF_RET)
        return (pltpu.make_async_copy(wu_hbm.at[slab], wu_buf.at[slot], sem.at[0, slot]),
                pltpu.make_async_copy(wd_hbm.at[rows, :], wd_buf.at[slot], sem.at[1, slot]))

    slot = _ring_slot(step, pl.num_programs(0) * nf, slab_copies)

    @pl.when(pl.program_id(1) == 0)
    def _():
        o_ref[...] = x1_ref[...]

    u = jnp.maximum(_dot(h2_ref[...], wu_buf[slot]), 0.0)
    o_ref[...] += _dot((u * u).astype(BF16), wd_buf[slot])
    return step


_HBM = pl.BlockSpec(memory_space=pl.ANY)


def _mlp_ring_scratch():
    return [pltpu.VMEM((RING, D_MODEL, TF_RET), BF16), pltpu.VMEM((RING, TF_RET, D_MODEL), BF16),
            pltpu.SemaphoreType.DMA((2, RING))]


def _mlp_body(*refs):
    _mlp_step(*refs)


def _mlp(x1, h2, wu, wd, tm):
    m = x1.shape[0]
    assert m // tm * (D_FF // TF_RET) >= RING
    row = pl.BlockSpec((tm, D_MODEL), lambda i, f: (i, 0))
    return pl.pallas_call(
        _mlp_body,
        grid=(m // tm, D_FF // TF_RET),
        in_specs=[row, row, _HBM, _HBM],
        out_specs=row,
        out_shape=jax.ShapeDtypeStruct((m, D_MODEL), F32),
        scratch_shapes=_mlp_ring_scratch(),
        compiler_params=_params(2),
        name="mlp",
    )(x1, h2, wu, wd)


def _mlp_ret_body(x1_ref, h2_ref, wu_hbm, wd_hbm,
                  qt_ref, kt_ref, rq_ref, rk_ref, rv_ref, gt_ref, g_ref, s_ref,
                  o_ref, r_ref, ns_ref, wu_buf, wd_buf, sem):
    b = _mlp_step(x1_ref, h2_ref, wu_hbm, wd_hbm, o_ref, wu_buf, wd_buf, sem)

    row = pl.ds(b, 1)
    shift = lax.rem(LANES - b, LANES)
    for hh in range(RET_HEADS):
        g1 = math.exp(LOG_G[hh])
        cols = slice(RET_DK * hh, RET_DK * (hh + 1))
        qc = pltpu.roll(qt_ref[cols, :], shift, 1)[:, :1] * g1
        kc = pltpu.roll(kt_ref[cols, :], shift, 1)[:, :1]
        v = rv_ref[row, cols]
        s0 = s_ref[0, hh]
        ns_ref[0, hh] = g1 * s0 + kc * v
        qk = jnp.sum(rq_ref[row, cols] * rk_ref[row, cols], axis=-1, keepdims=True)
        o = jnp.sum(qc * s0, axis=0, keepdims=True) + qk * v
        var = jnp.mean(o * o, axis=-1, keepdims=True)
        r_ref[row, cols] = o * lax.rsqrt(var + EPS) * g_ref[:, cols] * gt_ref[row, cols]


def _mlp_ret(x1, h2, wu, wd, qt, kt, rq, rk, rv, gt, g, state):
    m = x1.shape[0]
    nb = state.shape[0]
    nf = D_FF // TF_RET
    assert m // TM * nf == nb and nb == LANES
    assert nb >= RING
    row = pl.BlockSpec((TM, D_MODEL), lambda i, f: (i, 0))
    st = pl.BlockSpec((1, RET_HEADS, RET_DK, RET_DV), lambda i, f: (i * nf + f, 0, 0, 0))
    rows = _const_spec((nb, RET_WIDTH))
    return pl.pallas_call(
        _mlp_ret_body,
        grid=(m // TM, nf),
        in_specs=[row, row, _HBM, _HBM,
                  _const_spec(qt.shape), _const_spec(kt.shape), rows, rows, rows, rows,
                  _const_spec((1, RET_WIDTH)), st],
        out_specs=[row, pl.BlockSpec((nb, RET_WIDTH), lambda i, f: (0, 0)), st],
        out_shape=(jax.ShapeDtypeStruct((m, D_MODEL), F32),
                   jax.ShapeDtypeStruct((nb, RET_WIDTH), F32),
                   jax.ShapeDtypeStruct(state.shape, F32)),
        scratch_shapes=_mlp_ring_scratch(),
        compiler_params=_params(2),
        name="mlp_ret",
    )(x1, h2, wu, wd, qt, kt, rq, rk, rv, gt, g, state)


def kernel(x_prompt, x_sample, cache_k_win, cache_v_win, state_ret, ln1_g, w_in, q_norm_g,
           k_norm_g, attn_sinks, ret_norm_g, w_out, ln2_g, w_up, w_down):
    seq = x_prompt.shape[1]
    nb = x_sample.shape[0]
    assert x_prompt.shape[0] == 1 and x_sample.shape[1] == 1 and w_in.shape[0] == 1
    assert seq % TM == 0 and w_in.shape[2] == SPLIT_A + SPLIT_B

    inv = (ROPE_BASE ** (-jnp.arange(HALF, dtype=F32) / HALF)).reshape(1, HALF)

    ln1 = ln1_g.reshape(1, D_MODEL)
    ln2 = ln2_g.reshape(1, D_MODEL)
    qg = jnp.tile(q_norm_g.reshape(1, ATT_HEAD_DIM) * (ATT_HEAD_DIM ** -0.5), (1, 256 // ATT_HEAD_DIM))
    kg = jnp.tile(k_norm_g.reshape(1, ATT_HEAD_DIM), (1, KV_WIDTH // ATT_HEAD_DIM))
    rg_g = ret_norm_g.reshape(1, RET_WIDTH)
    sinks = attn_sinks.reshape(ATT_HEADS)
    blk = jnp.arange(256) // ATT_HEAD_DIM
    ones = (blk[:, None] == blk[None, :]).astype(BF16)

    xs = x_sample[:, 0]
    wi, qs, kvs, rqs, rks, rvs, rgs, kvt, rqt, rkt = _in_proj(xs, ln1, w_in[0], qg, kg, inv, ones,
                                                              pos=PAST_LEN)
    sink_col = jnp.concatenate([sinks[0::2], sinks[1::2]]).reshape(ATT_HEADS, 1)
    to_fm = lambda c: c[0].transpose(0, 2, 3, 1).reshape(nb, KV_WIDTH, WINDOW)
    from_fm = lambda c: c.reshape(nb, ATT_KV_HEADS, ATT_HEAD_DIM, WINDOW).transpose(0, 3, 1, 2)[None]
    a8, nk, nv = _swa_dec(qs.reshape(nb, ATT_HEADS // 2, LANES), kvs, kvt,
                          to_fm(cache_k_win), to_fm(cache_v_win), sink_col)
    a_s = a8.reshape(nb, ATT_WIDTH)

    xp = x_prompt[0]
    a_out, kv, rq, wo, wu = _proj_swa(xp, ln1, wi, qg, kg, inv, ones, sinks, w_out[0], w_up[0])
    r_out, s_fin, wd = _proj_ret(xp, ln1, wi, inv, rq, rg_g, w_down[0])
    x1, h2 = _out_proj(xp, a_out, r_out, wo, ln2, tm=TM)
    yp, r_s, ns = _mlp_ret(x1, h2, wu, wd, rqt, rkt, rqs, rks, rvs, rgs, rg_g, state_ret[0])

    wb = min(WINDOW, seq)
    kp = kv[seq - wb:, :KV_WIDTH].reshape(1, 1, wb, ATT_KV_HEADS, ATT_HEAD_DIM)
    vp = kv[seq - wb:, KV_WIDTH:].reshape(1, 1, wb, ATT_KV_HEADS, ATT_HEAD_DIM)
    sp = s_fin.reshape(1, 1, RET_HEADS, RET_DK, RET_DV)

    x1s, h2s = _out_proj(xs, a_s, r_s, wo, ln2, tm=nb)
    ys = _mlp(x1s, h2s, wu, wd, tm=nb)

    return (yp[None], ys[:, None, :], kp, vp, sp, from_fm(nk), from_fm(nv), ns[None])
```

```python
import functools
import math

import jax
import jax.numpy as jnp
from jax import lax
from jax.experimental import pallas as pl
from jax.experimental.pallas import tpu as pltpu

D_MODEL = 2048
ATT_HEADS = 16
ATT_KV_HEADS = 2
ATT_HEAD_DIM = 64
WINDOW = 128
RET_HEADS = 4
RET_DK = 256
RET_DV = 256
RET_CHUNK = 128
ROPE_BASE = 10000.0
D_FF = 4 * D_MODEL
EPS = 1e-6
PAST_LEN = 8192

ATT_WIDTH = ATT_HEADS * ATT_HEAD_DIM
KV_WIDTH = ATT_KV_HEADS * ATT_HEAD_DIM
RET_WIDTH = RET_HEADS * RET_DK
LANES = 128
HALF = RET_DK // 2
SPLIT_A = ATT_WIDTH + 2 * KV_WIDTH + RET_WIDTH
SPLIT_B = 3 * RET_WIDTH

F32 = jnp.float32
BF16 = jnp.bfloat16
VMEM_LIMIT = 60 * 1024 * 1024

TM = 512
TF_RET = 1024
BB_ATT = 32

LOG_G = tuple(math.log1p(-(2.0 ** (-5.0 - h))) for h in range(RET_HEADS))


def _dot(a, b):
    return jnp.dot(a, b, preferred_element_type=F32)


def _dot_nt(a, b):
    return lax.dot_general(a, b, (((1,), (1,)), ((), ())), preferred_element_type=F32)


def _dot_tn(a, b):
    return lax.dot_general(a, b, (((0,), (0,)), ((), ())), preferred_element_type=F32)


def _const_spec(shape):
    n = len(shape)
    return pl.BlockSpec(shape, lambda *_: (0,) * n, pipeline_mode=pl.Buffered(1))


def _w_in_spec(part):
    width, start = ((SPLIT_A, 0), (SPLIT_B, SPLIT_A))[part]
    return pl.BlockSpec((pl.Element(D_MODEL), pl.Element(width)), lambda *_: (0, start),
                        pipeline_mode=pl.Buffered(1))


def _params(n_axes=1):
    return pltpu.CompilerParams(dimension_semantics=("arbitrary",) * n_axes,
                                vmem_limit_bytes=VMEM_LIMIT)


def _norm_rows(x_ref, g_ref):
    x = x_ref[...]
    var = jnp.mean(x * x, axis=-1, keepdims=True)
    return (x * lax.rsqrt(var + EPS) * g_ref[...]).astype(BF16)


def _bf16_into(o_ref, v):
    return v.astype(BF16).astype(o_ref.dtype)


def _rope_init(inv_ref, cr_scr, sr_scr, pos_step):
    tm = cr_scr.shape[0]
    row = lax.broadcasted_iota(jnp.int32, (tm, 1), 0)
    ang_r = (pos_step * row).astype(F32) * inv_ref[...]
    cr_scr[...] = jnp.cos(ang_r)
    sr_scr[...] = jnp.sin(ang_r)


def _rope_tables(inv_ref, cr_scr, sr_scr, base_pos):
    base = jnp.zeros((8, 1), jnp.int32) + base_pos
    ang_b = base.astype(F32) * inv_ref[...]
    cb = jnp.cos(ang_b)[:1]
    sb = jnp.sin(ang_b)[:1]
    cr = cr_scr[...]
    sr = sr_scr[...]
    return cb * cr - sb * sr, sb * cr + cb * sr


def _rope_head(r, cos, sin, scale, o_ref, hh):
    x1 = r[:, :HALF]
    x2 = r[:, HALF:]
    o1 = x1 * cos - x2 * sin
    o2 = x2 * cos + x1 * sin
    if scale != 1.0:
        o1 = o1 * scale
        o2 = o2 * scale
    o_ref[:, RET_DK * hh:RET_DK * hh + HALF] = _bf16_into(o_ref, o1)
    o_ref[:, RET_DK * hh + HALF:RET_DK * (hh + 1)] = _bf16_into(o_ref, o2)


def _half_split(a):
    lane = lax.broadcasted_iota(jnp.int32, a.shape, 1)
    lo = lane < ATT_HEAD_DIM
    sw = pltpu.roll(a, ATT_HEAD_DIM, 1)
    zero = jnp.zeros_like(a)
    h0 = (jnp.where(lo, a, zero).astype(BF16), jnp.where(lo, zero, sw).astype(BF16))
    h1 = (jnp.where(lo, sw, zero).astype(BF16), jnp.where(lo, zero, a).astype(BF16))
    return h0, h1


SWA_STACKS = 1


def _swa_phases(sink_ref, q_ref, kvc_ref, kvp_ref, o_ref, has_prev):
    nsub = q_ref.shape[0] // WINDOW
    npair = ATT_HEADS // 2
    row = lax.broadcasted_iota(jnp.int32, (WINDOW, 2 * WINDOW), 0)
    col = lax.broadcasted_iota(jnp.int32, (WINDOW, 2 * WINDOW), 1)
    band_cur = (col >= WINDOW) & (col - WINDOW <= row)
    first_col = jnp.where(has_prev, 0, WINDOW)
    lane_lo = lax.broadcasted_iota(jnp.int32, (WINDOW, LANES), 1) < ATT_HEAD_DIM
    ctx = [{} for _ in range(nsub)]
    st = {}

    ppk = npair // ATT_KV_HEADS

    def stacks(g):
        for m in range(g * SWA_STACKS, (g + 1) * SWA_STACKS):
            j, kh = divmod(m, ATT_KV_HEADS)
            yield j, kh, slice(WINDOW * j, WINDOW * (j + 1)), range(ppk * kh, ppk * (kh + 1))

    def prep(g):
        for j in sorted({j for j, _, _, _ in stacks(g)} - {j for j in range(nsub) if ctx[j]}):
            prev = kvp_ref[...] if j == 0 else kvc_ref[WINDOW * (j - 1):WINDOW * j, :]
            cur = kvc_ref[WINDOW * j:WINDOW * (j + 1), :]
            kk = jnp.concatenate([prev[:, :KV_WIDTH], cur[:, :KV_WIDTH]], axis=0)
            vv = jnp.concatenate([prev[:, KV_WIDTH:], cur[:, KV_WIDTH:]], axis=0)
            ctx[j]["k"] = _half_split(kk)
            ctx[j]["v"] = _half_split(vv)
            lo_col = first_col if j == 0 else 0
            ctx[j]["mask"] = band_cur | ((col < WINDOW) & (col >= row) & (col >= lo_col))

    def qk(g):
        for j, kh, rows, tiles in stacks(g):
            qs = jnp.concatenate([q_ref[rows, LANES * t:LANES * (t + 1)] for t in tiles], axis=0)
            s = [_dot_nt(qs, ctx[j]["k"][kh][par]) for par in range(2)]
            for n, t in enumerate(tiles):
                st[j, t] = {"s": [sp[WINDOW * n:WINDOW * (n + 1), :] for sp in s]}

    def softmax(g):
        for j, kh, rows, tiles in stacks(g):
            for t in tiles:
                p, inv = [], []
                for par in range(2):
                    sink = sink_ref[2 * t + par]
                    s = jnp.where(ctx[j]["mask"], st[j, t]["s"][par], -jnp.inf)
                    mx = jnp.maximum(jnp.max(s, axis=-1, keepdims=True), sink)
                    e = jnp.exp(s - mx)
                    den = jnp.sum(e, axis=-1, keepdims=True) + jnp.exp(sink - mx)
                    p.append(e.astype(BF16))
                    inv.append(1.0 / den)
                st[j, t] = {"p": p, "inv": jnp.where(lane_lo, inv[0], inv[1])}

    def pv(g):
        for j, kh, rows, tiles in stacks(g):
            got = [st.pop((j, t)) for t in tiles]
            acc = sum(_dot(jnp.concatenate([u["p"][par] for u in got], axis=0),
                           ctx[j]["v"][kh][par]) for par in range(2))
            for n, t in enumerate(tiles):
                o_ref[rows, LANES * t:LANES * (t + 1)] = (
                    acc[WINDOW * n:WINDOW * (n + 1), :] * got[n]["inv"]).astype(BF16)

    ngroup = nsub * ATT_KV_HEADS // SWA_STACKS
    return ngroup, prep, qk, softmax, pv


def _emit_pipelined(main, ngroup, prep, first, middle, last, finish=None):
    nstage = ngroup + 2
    done = 0
    prep(0)
    for k in range(nstage):
        if k < ngroup:
            first(k)
        if 0 <= k - 2 < ngroup:
            last(k - 2)
        if 0 <= k - 1 < ngroup:
            middle(k - 1)
        if k + 1 < ngroup:
            prep(k + 1)
        upto = -(-len(main) * (k + 1) // nstage)
        for piece in main[done:upto]:
            piece()
        done = upto
    if finish is not None:
        finish()


def _ret_phases(rq_ref, rk_ref, rv_ref, gt_ref, g_ref, o_ref, s_scr):
    c = RET_CHUNK
    nsub = rq_ref.shape[0] // c
    ri = lax.broadcasted_iota(jnp.int32, (c, c), 0)
    ci = lax.broadcasted_iota(jnp.int32, (c, c), 1)
    rel = (ri - ci).astype(F32)
    idx = lax.broadcasted_iota(jnp.int32, (c, 1), 0).astype(F32)
    head = []
    for hh in range(RET_HEADS):
        lg = LOG_G[hh]
        head.append(dict(
            dmask=jnp.where(rel >= 0, jnp.exp(lg * jnp.maximum(rel, 0.0)), 0.0),
            qdec=jnp.exp(lg * (idx + 1.0)), kdec=jnp.exp(lg * (c - 1.0 - idx)),
            cdec=math.exp(lg * c), cols=slice(RET_DK * hh, RET_DK * (hh + 1))))
    st = {}

    def update_state(g):
        for hh, hd in enumerate(head):
            u = st[g, hh]
            s_scr[hh] = hd["cdec"] * u.pop("s_prev") + u.pop("kv")

    def prep(g):
        if g > 0:
            update_state(g - 1)
        rows = slice(c * g, c * (g + 1))
        for hh, hd in enumerate(head):
            q = rq_ref[rows, hd["cols"]]
            k = rk_ref[rows, hd["cols"]]
            s_prev = s_scr[hh]
            st[g, hh] = dict(q=q, k=k, v=rv_ref[rows, hd["cols"]], s_prev=s_prev,
                             qd=(q.astype(F32) * hd["qdec"]).astype(BF16),
                             kd=(k.astype(F32) * hd["kdec"]).astype(BF16),
                             s_bf=s_prev.astype(BF16))

    def first(g):
        for hh in range(RET_HEADS):
            u = st[g, hh]
            u["att"] = _dot_nt(u.pop("q"), u.pop("k"))
            u["inter"] = _dot(u.pop("qd"), u.pop("s_bf"))
            u["kv"] = _dot_tn(u.pop("kd"), u["v"])

    def middle(g):
        for hh, hd in enumerate(head):
            u = st[g, hh]
            u["att"] = (u["att"] * hd["dmask"]).astype(BF16)

    def last(g):
        rows = slice(c * g, c * (g + 1))
        for hh, hd in enumerate(head):
            u = st[g, hh]
            o = _dot(u.pop("att"), u.pop("v")) + u.pop("inter")
            var = jnp.mean(o * o, axis=-1, keepdims=True)
            on = o * lax.rsqrt(var + EPS) * g_ref[:, hd["cols"]]
            o_ref[rows, hd["cols"]] = (on * gt_ref[rows, hd["cols"]].astype(F32)).astype(BF16)

    return nsub, prep, first, middle, last, functools.partial(update_state, nsub - 1)


def _proj_swa_body(sink_ref, x_ref, g_ref, w_ref, qg_ref, kg_ref, inv_ref, ones_ref,
                   wo_ref, wu_ref,
                   a_out, kv_out, rq_out, wo_bf, wu_bf,
                   q_scr, kv_scr, kvp_scr, cr_scr, sr_scr, *, nblk):
    i = pl.program_id(0)
    tm = x_ref.shape[0]

    @pl.when(i == 0)
    def _():
        _rope_init(inv_ref, cr_scr, sr_scr, 1)
        q_scr[...] = jnp.zeros_like(q_scr)
        kv_scr[...] = jnp.zeros_like(kv_scr)
        kvp_scr[...] = jnp.zeros_like(kvp_scr)

    wo_bf[...] = wo_ref[...].astype(BF16)
    wu_bf[0] = wu_ref[...].astype(BF16)

    cur = lax.rem(i, 2)
    prv = 1 - cur
    blk = jnp.minimum(i, nblk - 1)

    h = _norm_rows(x_ref, g_ref)
    ones = ones_ref[...]
    inv_hd = 1.0 / ATT_HEAD_DIM
    cos, sin = _rope_tables(inv_ref, cr_scr, sr_scr, blk * tm)


    def q_piece(n):
        y = _dot(h, w_ref[:, 512 * n:512 * (n + 1)])
        for t in range(2):
            yt = y[:, 256 * t:256 * (t + 1)]
            ssq = _dot((yt * yt).astype(BF16), ones)
            c0 = 512 * n + 256 * t
            q_scr[cur, :, c0:c0 + 256] = (
                yt * lax.rsqrt(ssq * inv_hd + EPS) * qg_ref[...]).astype(BF16)

    kv0 = ATT_WIDTH
    rq0 = ATT_WIDTH + 2 * KV_WIDTH

    def kv_rq_piece():
        y = _dot(h, w_ref[:, kv0:rq0 + RET_DK])
        k = y[:, :KV_WIDTH]
        ssq = _dot((k * k).astype(BF16), ones[:KV_WIDTH, :KV_WIDTH])
        kn = k * lax.rsqrt(ssq * inv_hd + EPS) * kg_ref[...]
        v = y[:, KV_WIDTH:2 * KV_WIDTH]
        kv_scr[cur, :, :KV_WIDTH] = kn
        kv_scr[cur, :, KV_WIDTH:] = v
        kv_out[:, :KV_WIDTH] = kn
        kv_out[:, KV_WIDTH:] = v
        _rope_head(y[:, 2 * KV_WIDTH:], cos, sin, 1.0, rq_out, 0)

    def rq_piece():
        y = _dot(h, w_ref[:, rq0 + RET_DK:])
        for hh in range(1, RET_HEADS):
            _rope_head(y[:, RET_DK * (hh - 1):RET_DK * hh], cos, sin, 1.0, rq_out, hh)

    main = [functools.partial(q_piece, 0), functools.partial(q_piece, 1), kv_rq_piece, rq_piece]
    _emit_pipelined(main, *_swa_phases(sink_ref, q_scr.at[prv], kv_scr.at[prv], kvp_scr, a_out,
                                       has_prev=i > 1))
    kvp_scr[...] = kv_scr[prv, tm - WINDOW:, :]


def _proj_swa(x, ln_g, w_in, qg, kg, inv, ones, sinks, w_out, w_up):
    m = x.shape[0]
    nblk = m // TM
    cl = lambda i: jnp.minimum(i, nblk - 1)
    row = lambda w: pl.BlockSpec((TM, w), lambda i: (cl(i), 0))
    wo_spec = pl.BlockSpec((w_out.shape[0] // nblk, w_out.shape[1]), lambda i: (cl(i), 0))
    wcast = w_up.shape[1] // nblk
    per_slab = TF_RET // wcast
    assert w_up.shape[1] % nblk == 0 and TF_RET % wcast == 0
    wu_spec = pl.BlockSpec((w_up.shape[0], wcast), lambda i: (0, cl(i)))
    wu_out_spec = pl.BlockSpec((1, w_up.shape[0], wcast),
                               lambda i: (cl(i) // per_slab, 0, cl(i) % per_slab))
    return pl.pallas_call(
        functools.partial(_proj_swa_body, nblk=nblk),
        grid=(nblk + 1,),
        in_specs=[pl.BlockSpec(memory_space=pltpu.SMEM),
                  row(D_MODEL), _const_spec((1, D_MODEL)), _w_in_spec(0),
                  _const_spec((1, 256)), _const_spec((1, KV_WIDTH)),
                  _const_spec((1, HALF)), _const_spec((256, 256)), wo_spec, wu_spec],
        out_specs=[pl.BlockSpec((TM, ATT_WIDTH), lambda i: (jnp.maximum(i - 1, 0), 0)),
                   row(2 * KV_WIDTH), row(RET_WIDTH), wo_spec, wu_out_spec],
        out_shape=(jax.ShapeDtypeStruct((m, ATT_WIDTH), BF16),
                   jax.ShapeDtypeStruct((m, 2 * KV_WIDTH), F32),
                   jax.ShapeDtypeStruct((m, RET_WIDTH), BF16),
                   jax.ShapeDtypeStruct(w_out.shape, BF16),
                   jax.ShapeDtypeStruct((w_up.shape[1] // TF_RET, w_up.shape[0], TF_RET), BF16)),
        scratch_shapes=[pltpu.VMEM((2, TM, ATT_WIDTH), BF16),
                        pltpu.VMEM((2, TM, 2 * KV_WIDTH), F32),
                        pltpu.VMEM((WINDOW, 2 * KV_WIDTH), F32),
                        pltpu.VMEM((TM, HALF), F32), pltpu.VMEM((TM, HALF), F32)],
        compiler_params=_params(),
        name="proj_swa",
    )(sinks, x, ln_g, w_in, qg, kg, inv, ones, w_out, w_up)


def _proj_ret_body(x_ref, g_ref, w_ref, inv_ref, rq_ref, rg_ref, wd_ref,
                   r_out, s_out, wd_bf,
                   rk_scr, rv_scr, gt_scr, s_scr, cr_scr, sr_scr, *, nblk):
    i = pl.program_id(0)
    tm = x_ref.shape[0]

    @pl.when(i == 0)
    def _():
        _rope_init(inv_ref, cr_scr, sr_scr, 1)
        rk_scr[...] = jnp.zeros_like(rk_scr)
        rv_scr[...] = jnp.zeros_like(rv_scr)
        gt_scr[...] = jnp.zeros_like(gt_scr)
        s_scr[...] = jnp.zeros_like(s_scr)

    wd_bf[...] = wd_ref[...].astype(BF16)

    cur = lax.rem(i, 2)
    prv = 1 - cur
    blk = jnp.minimum(i, nblk - 1)

    h = _norm_rows(x_ref, g_ref)
    cos, sin = _rope_tables(inv_ref, cr_scr, sr_scr, blk * tm)

    def rk_piece(n):
        y = _dot(h, w_ref[:, 512 * n:512 * (n + 1)])
        for t in range(2):
            _rope_head(y[:, RET_DK * t:RET_DK * (t + 1)], cos, sin, RET_DK ** -0.5,
                       rk_scr.at[cur], 2 * n + t)

    def rv_piece(n):
        c = slice(512 * n, 512 * (n + 1))
        rv_scr[cur, :, c] = _dot(h, w_ref[:, RET_WIDTH + 512 * n:RET_WIDTH + 512 * (n + 1)]
                                 ).astype(BF16)

    def gate_piece(n):
        c0 = 2 * RET_WIDTH + 512 * n
        rg = _dot(h, w_ref[:, c0:c0 + 512])
        gt_scr[cur, :, 512 * n:512 * (n + 1)] = (rg / (1.0 + jnp.exp(-rg))).astype(BF16)

    main = [functools.partial(f, n) for f in (rk_piece, rv_piece, gate_piece) for n in range(2)]
    _emit_pipelined(main, *_ret_phases(rq_ref, rk_scr.at[prv], rv_scr.at[prv], gt_scr.at[prv],
                                       rg_ref, r_out, s_scr))

    @pl.when(i == nblk)
    def _():
        s_out[...] = s_scr[...]


def _proj_ret(x, ln_g, w_in, inv, rq, rg_g, w_down):
    m = x.shape[0]
    nblk = m // TM
    cl = lambda i: jnp.minimum(i, nblk - 1)
    prev = lambda w: pl.BlockSpec((TM, w), lambda i: (jnp.maximum(i - 1, 0), 0))
    wd_spec = pl.BlockSpec((w_down.shape[0] // nblk, w_down.shape[1]), lambda i: (cl(i), 0))
    state = (RET_HEADS, RET_DK, RET_DV)
    slot = pltpu.VMEM((2, TM, RET_WIDTH), BF16)
    return pl.pallas_call(
        functools.partial(_proj_ret_body, nblk=nblk),
        grid=(nblk + 1,),
        in_specs=[pl.BlockSpec((TM, D_MODEL), lambda i: (cl(i), 0)),
                  _const_spec((1, D_MODEL)), _w_in_spec(1), _const_spec((1, HALF)),
                  prev(RET_WIDTH), _const_spec((1, RET_WIDTH)), wd_spec],
        out_specs=[prev(RET_WIDTH), pl.BlockSpec(state, lambda i: (0, 0, 0)), wd_spec],
        out_shape=(jax.ShapeDtypeStruct((m, RET_WIDTH), BF16),
                   jax.ShapeDtypeStruct(state, F32),
                   jax.ShapeDtypeStruct(w_down.shape, BF16)),
        scratch_shapes=[slot, slot, slot, pltpu.VMEM(state, F32),
                        pltpu.VMEM((TM, HALF), F32), pltpu.VMEM((TM, HALF), F32)],
        compiler_params=_params(),
        name="proj_ret",
    )(x, ln_g, w_in, inv, rq, rg_g, w_down)


TN_IN = 1792
Z_TILE = 256


def _in_proj_body(x_ref, g_ref, w_ref, qg_ref, kg_ref, inv_ref, ones_ref,
                  w_bf, q_out, kv_out, rq_out, rk_out, rv_out, gt_out, kvt_out, rqt_out, rkt_out,
                  h_scr, z_scr, *, pos):
    c = pl.program_id(0)

    @pl.when(c == 0)
    def _():
        h_scr[...] = _norm_rows(x_ref.at[:, 0], g_ref)

    wb = w_ref[...].astype(BF16)
    w_bf[...] = wb
    z = _dot(h_scr[...], wb)
    per_step = TN_IN // Z_TILE
    for t in range(per_step):
        z_scr[c * per_step + t] = z[:, Z_TILE * t:Z_TILE * (t + 1)]

    @pl.when(c == pl.num_programs(0) - 1)
    def _():
        tiles = lambda col0, n: [z_scr[col0 // Z_TILE + t] for t in range(n)]
        ones = ones_ref[...]
        inv_hd = 1.0 / ATT_HEAD_DIM
        for t, y in enumerate(tiles(0, ATT_WIDTH // Z_TILE)):
            ssq = _dot((y * y).astype(BF16), ones)
            qn = _bf16_into(q_out, y * lax.rsqrt(ssq * inv_hd + EPS) * qg_ref[...])
            for half in range(2):
                q_out[:, 2 * t + half, :] = qn[:, LANES * half:LANES * (half + 1)]
        (kvr,) = tiles(ATT_WIDTH, 1)
        k = kvr[:, :KV_WIDTH]
        ssq = _dot((k * k).astype(BF16), ones[:KV_WIDTH, :KV_WIDTH])
        kv_out[:, :KV_WIDTH] = k * lax.rsqrt(ssq * inv_hd + EPS) * kg_ref[...]
        kv_out[:, KV_WIDTH:] = kvr[:, KV_WIDTH:]
        ang = jnp.full((8, 1), pos, jnp.int32).astype(F32) * inv_ref[...]
        cos = jnp.cos(ang)[:1]
        sin = jnp.sin(ang)[:1]
        rq0 = ATT_WIDTH + 2 * KV_WIDTH
        for hh, y in enumerate(tiles(rq0, RET_HEADS)):
            _rope_head(y, cos, sin, 1.0, rq_out, hh)
        for hh, y in enumerate(tiles(rq0 + RET_WIDTH, RET_HEADS)):
            _rope_head(y, cos, sin, RET_DK ** -0.5, rk_out, hh)
        for hh, y in enumerate(tiles(rq0 + 2 * RET_WIDTH, RET_HEADS)):
            rv_out[:, RET_DV * hh:RET_DV * (hh + 1)] = _bf16_into(rv_out, y)
        for hh, y in enumerate(tiles(rq0 + 3 * RET_WIDTH, RET_HEADS)):
            gt_out[:, RET_DV * hh:RET_DV * (hh + 1)] = _bf16_into(gt_out, y / (1.0 + jnp.exp(-y)))
        kvt_out[...] = kv_out[...].T
        rqt_out[...] = rq_out[...].T
        rkt_out[...] = rk_out[...].T


def _in_proj(x, ln_g, w_in, qg, kg, inv, ones, pos):
    m = x.shape[0]
    assert m == LANES
    assert RET_DK == Z_TILE and TN_IN % Z_TILE == 0 and w_in.shape[1] % TN_IN == 0
    nstep = w_in.shape[1] // TN_IN
    full = lambda w: pl.BlockSpec((m, w), lambda c: (0, 0))
    colm = lambda w: pl.BlockSpec((w, m), lambda c: (0, 0))
    wcol = pl.BlockSpec((D_MODEL, TN_IN), lambda c: (0, c))
    ret = jax.ShapeDtypeStruct((m, RET_WIDTH), F32)
    ret_t = jax.ShapeDtypeStruct((RET_WIDTH, m), F32)
    return pl.pallas_call(
        functools.partial(_in_proj_body, pos=pos),
        grid=(nstep,),
        in_specs=[_const_spec((m, 1, D_MODEL)), _const_spec((1, D_MODEL)), wcol,
                  _const_spec((1, 256)), _const_spec((1, KV_WIDTH)),
                  _const_spec((1, HALF)), _const_spec((256, 256))],
        out_specs=[wcol, pl.BlockSpec((m, ATT_HEADS // 2, LANES), lambda c: (0, 0, 0)),
                   full(2 * KV_WIDTH)] + [full(RET_WIDTH)] * 4
        + [colm(2 * KV_WIDTH), colm(RET_WIDTH), colm(RET_WIDTH)],
        out_shape=(jax.ShapeDtypeStruct(w_in.shape, BF16),
                   jax.ShapeDtypeStruct((m, ATT_HEADS // 2, LANES), F32),
                   jax.ShapeDtypeStruct((m, 2 * KV_WIDTH), F32), ret, ret, ret, ret,
                   jax.ShapeDtypeStruct((2 * KV_WIDTH, m), F32), ret_t, ret_t),
        scratch_shapes=[pltpu.VMEM((m, D_MODEL), BF16),
                        pltpu.VMEM((w_in.shape[1] // Z_TILE, m, Z_TILE), F32)],
        compiler_params=_params(),
        name="in_proj",
    )(x, ln_g, w_in, qg, kg, inv, ones)


def _this_steps_columns(t_ref, rows, bb):
    shift = lax.rem(LANES - bb * pl.program_id(0), LANES)
    return pltpu.roll(t_ref[rows, :], shift, 1)


def _swa_dec_body(q_ref, kvn_ref, kvt_ref, ck_ref, cv_ref, sink_ref, o_ref, nk_ref, nv_ref):
    bb = q_ref.shape[0]
    npair = ATT_HEADS // 2
    q8 = q_ref[...]
    q8r = pltpu.roll(q8, ATT_HEAD_DIM, 2)
    lane = lax.broadcasted_iota(jnp.int32, q8.shape, 2)
    pair = lax.broadcasted_iota(jnp.int32, q8.shape, 1)
    lo = lane < ATT_HEAD_DIM
    kv0 = pair < npair // ATT_KV_HEADS
    own = lo == kv0
    zero = jnp.zeros_like(q8)
    qe = jnp.where(own, jnp.where(kv0, q8, q8r), zero)
    qo = jnp.where(own, jnp.where(kv0, q8r, q8), zero)
    qb = jnp.concatenate([qe, qo], axis=1)

    ck = ck_ref[...]
    cv = cv_ref[...]
    kn = kvn_ref[:, :KV_WIDTH]
    vn = kvn_ref[:, KV_WIDTH:]
    s = lax.dot_general(qb.astype(BF16), ck.astype(BF16), (((2,), (1,)), ((0,), (0,))),
                        preferred_element_type=F32)
    s_new = jnp.sum(qb * kn[:, None, :], axis=-1, keepdims=True)
    sink = sink_ref[...][None, :, :]
    mx = jnp.maximum(jnp.maximum(jnp.max(s, axis=-1, keepdims=True), s_new), sink)
    p = jnp.exp(s - mx)
    p_new = jnp.exp(s_new - mx)
    den = jnp.sum(p, axis=-1, keepdims=True) + p_new + jnp.exp(sink - mx)
    o = lax.dot_general(p.astype(BF16), cv.astype(BF16), (((2,), (2,)), ((0,), (0,))),
                        preferred_element_type=F32)
    o = (o + p_new * vn[:, None, :]) / den
    oe = o[:, :npair, :]
    oo = o[:, npair:, :]
    oer = pltpu.roll(oe, ATT_HEAD_DIM, 2)
    oor = pltpu.roll(oo, ATT_HEAD_DIM, 2)
    o_ref[...] = jnp.where(lo, jnp.where(kv0, oe, oer), jnp.where(kv0, oor, oo)).astype(BF16)

    newcol = _this_steps_columns(kvt_ref, slice(None), bb)
    last = lax.broadcasted_iota(jnp.int32, (KV_WIDTH, WINDOW), 1) == WINDOW - 1
    for jb in range(bb):
        nk_ref[jb] = jnp.where(last, newcol[:KV_WIDTH, jb:jb + 1],
                               pltpu.roll(ck[jb], WINDOW - 1, 1))
        nv_ref[jb] = jnp.where(last, newcol[KV_WIDTH:, jb:jb + 1],
                               pltpu.roll(cv[jb], WINDOW - 1, 1))


def _swa_dec(q8, kvn, kvt, ck, cv, sink_col):
    b = q8.shape[0]
    bb = BB_ATT
    cache = pl.BlockSpec((bb, KV_WIDTH, WINDOW), lambda i: (i, 0, 0))
    return pl.pallas_call(
        _swa_dec_body,
        grid=(b // bb,),
        in_specs=[pl.BlockSpec((bb, ATT_HEADS // 2, LANES), lambda i: (i, 0, 0)),
                  pl.BlockSpec((bb, 2 * KV_WIDTH), lambda i: (i, 0)),
                  _const_spec(kvt.shape), cache, cache, _const_spec((ATT_HEADS, 1))],
        out_specs=[pl.BlockSpec((bb, ATT_HEADS // 2, LANES), lambda i: (i, 0, 0)), cache, cache],
        out_shape=(jax.ShapeDtypeStruct((b, ATT_HEADS // 2, LANES), BF16),
                   jax.ShapeDtypeStruct(ck.shape, F32),
                   jax.ShapeDtypeStruct(cv.shape, F32)),
        compiler_params=_params(),
        name="swa_dec",
    )(q8, kvn, kvt, ck, cv, sink_col)


def _out_proj_body(x_ref, a_ref, r_ref, w_ref, g_ref, x1_ref, h2_ref):
    if len(x_ref.shape) == 3:
        x_ref = x_ref.at[:, 0]
    x1 = (x_ref[...] + _dot(a_ref[...].astype(BF16), w_ref[:ATT_WIDTH, :])
          + _dot(r_ref[...].astype(BF16), w_ref[ATT_WIDTH:, :]))
    x1_ref[...] = x1
    var = jnp.mean(x1 * x1, axis=-1, keepdims=True)
    h2_ref[...] = (x1 * lax.rsqrt(var + EPS) * g_ref[...]).astype(BF16)


def _out_proj(x, a, r, w, g, tm):
    m = x.shape[0]
    row = lambda w: pl.BlockSpec((tm, w), lambda i: (i, 0))
    x_spec = row(D_MODEL) if x.ndim == 2 else pl.BlockSpec((tm, 1, D_MODEL), lambda i: (i, 0, 0))
    return pl.pallas_call(
        _out_proj_body,
        grid=(m // tm,),
        in_specs=[x_spec, row(ATT_WIDTH), row(RET_WIDTH),
                  _const_spec(w.shape), _const_spec((1, D_MODEL))],
        out_specs=[row(D_MODEL), row(D_MODEL)],
        out_shape=(jax.ShapeDtypeStruct((m, D_MODEL), F32),
                   jax.ShapeDtypeStruct((m, D_MODEL), BF16)),
        compiler_params=_params(),
        name="out_proj",
    )(x, a, r, w, g)


def _mlp_acc(h2_ref, wu_ref, wd_ref, o_ref):
    u = jnp.maximum(_dot(h2_ref[...], wu_ref[0]), 0.0)
    o_ref[...] += _dot((u * u).astype(BF16), wd_ref[...])


def _mlp_body(x1_ref, h2_ref, wu_ref, wd_ref, o_ref, acc):
    @pl.when(pl.program_id(1) == 0)
    def _():
        acc[...] = x1_ref[...]

    _mlp_acc(h2_ref, wu_ref, wd_ref, acc)

    @pl.when(pl.program_id(1) == pl.num_programs(1) - 1)
    def _():
        o_ref[:, 0, :] = acc[...]


def _mlp_weight_specs():
    return [pl.BlockSpec((1, D_MODEL, TF_RET), lambda i, f: (f, 0, 0)),
            pl.BlockSpec((TF_RET, D_MODEL), lambda i, f: (f, 0))]


def _mlp(x1, h2, wu, wd, tm):
    m = x1.shape[0]
    row = pl.BlockSpec((tm, D_MODEL), lambda i, f: (i, 0))
    return pl.pallas_call(
        _mlp_body,
        grid=(m // tm, D_FF // TF_RET),
        in_specs=[row, row] + _mlp_weight_specs(),
        out_specs=pl.BlockSpec((tm, 1, D_MODEL), lambda i, f: (i, 0, 0)),
        out_shape=jax.ShapeDtypeStruct((m, 1, D_MODEL), F32),
        scratch_shapes=[pltpu.VMEM((tm, D_MODEL), F32)],
        compiler_params=_params(2),
        name="mlp",
    )(x1, h2, wu, wd)


RING = 3


def _mlp_ret_body(x1_ref, h2_ref, wu_hbm, wd_hbm,
                  qt_ref, kt_ref, rq_ref, rk_ref, rv_ref, gt_ref, g_ref, s_ref,
                  o_ref, r_ref, ns_ref, wu_buf, wd_buf, sem):
    nf = pl.num_programs(1)
    b = pl.program_id(0) * nf + pl.program_id(1)
    nstep = pl.num_programs(0) * nf

    def slab_copies(step):
        slab = lax.rem(step, nf)
        slot = lax.rem(step, RING)
        rows = pl.ds(pl.multiple_of(slab * TF_RET, TF_RET), TF_RET)
        return (pltpu.make_async_copy(wu_hbm.at[slab], wu_buf.at[slot], sem.at[0, slot]),
                pltpu.make_async_copy(wd_hbm.at[rows, :], wd_buf.at[slot], sem.at[1, slot]))

    @pl.when(b == 0)
    def _():
        for step in range(RING - 1):
            for copy in slab_copies(step):
                copy.start()

    @pl.when(b + RING - 1 < nstep)
    def _():
        for copy in slab_copies(b + RING - 1):
            copy.start()

    for copy in slab_copies(b):
        copy.wait()
    slot = lax.rem(b, RING)

    @pl.when(pl.program_id(1) == 0)
    def _():
        o_ref[...] = x1_ref[...]

    u = jnp.maximum(_dot(h2_ref[...], wu_buf[slot]), 0.0)
    o_ref[...] += _dot((u * u).astype(BF16), wd_buf[slot])

    row = pl.ds(b, 1)
    shift = lax.rem(LANES - b, LANES)
    for hh in range(RET_HEADS):
        g1 = math.exp(LOG_G[hh])
        cols = slice(RET_DK * hh, RET_DK * (hh + 1))
        qc = pltpu.roll(qt_ref[cols, :], shift, 1)[:, :1] * g1
        kc = pltpu.roll(kt_ref[cols, :], shift, 1)[:, :1]
        v = rv_ref[row, cols]
        s0 = s_ref[0, hh]
        ns_ref[0, hh] = g1 * s0 + kc * v
        qk = jnp.sum(rq_ref[row, cols] * rk_ref[row, cols], axis=-1, keepdims=True)
        o = jnp.sum(qc * s0, axis=0, keepdims=True) + qk * v
        var = jnp.mean(o * o, axis=-1, keepdims=True)
        r_ref[row, cols] = o * lax.rsqrt(var + EPS) * g_ref[:, cols] * gt_ref[row, cols]


def _mlp_ret(x1, h2, wu, wd, qt, kt, rq, rk, rv, gt, g, state):
    m = x1.shape[0]
    nb = state.shape[0]
    nf = D_FF // TF_RET
    assert m // TM * nf == nb and nb == LANES
    assert nb >= RING
    row = pl.BlockSpec((TM, D_MODEL), lambda i, f: (i, 0))
    hbm = pl.BlockSpec(memory_space=pl.ANY)
    st = pl.BlockSpec((1, RET_HEADS, RET_DK, RET_DV), lambda i, f: (i * nf + f, 0, 0, 0))
    rows = _const_spec((nb, RET_WIDTH))
    return pl.pallas_call(
        _mlp_ret_body,
        grid=(m // TM, nf),
        in_specs=[row, row, hbm, hbm,
                  _const_spec(qt.shape), _const_spec(kt.shape), rows, rows, rows, rows,
                  _const_spec((1, RET_WIDTH)), st],
        out_specs=[row, pl.BlockSpec((nb, RET_WIDTH), lambda i, f: (0, 0)), st],
        out_shape=(jax.ShapeDtypeStruct((m, D_MODEL), F32),
                   jax.ShapeDtypeStruct((nb, RET_WIDTH), F32),
                   jax.ShapeDtypeStruct(state.shape, F32)),
        scratch_shapes=[pltpu.VMEM((RING, D_MODEL, TF_RET), BF16),
                        pltpu.VMEM((RING, TF_RET, D_MODEL), BF16),
                        pltpu.SemaphoreType.DMA((2, RING))],
        compiler_params=_params(2),
        name="mlp_ret",
    )(x1, h2, wu, wd, qt, kt, rq, rk, rv, gt, g, state)


def kernel(x_prompt, x_sample, cache_k_win, cache_v_win, state_ret, ln1_g, w_in, q_norm_g,
           k_norm_g, attn_sinks, ret_norm_g, w_out, ln2_g, w_up, w_down):
    seq = x_prompt.shape[1]
    nb = x_sample.shape[0]
    assert x_prompt.shape[0] == 1 and x_sample.shape[1] == 1 and w_in.shape[0] == 1
    assert seq % TM == 0 and w_in.shape[2] == SPLIT_A + SPLIT_B

    inv = (ROPE_BASE ** (-jnp.arange(HALF, dtype=F32) / HALF)).reshape(1, HALF)

    ln1 = ln1_g.reshape(1, D_MODEL)
    ln2 = ln2_g.reshape(1, D_MODEL)
    qg = jnp.tile(q_norm_g.reshape(1, ATT_HEAD_DIM) * (ATT_HEAD_DIM ** -0.5), (1, 256 // ATT_HEAD_DIM))
    kg = jnp.tile(k_norm_g.reshape(1, ATT_HEAD_DIM), (1, KV_WIDTH // ATT_HEAD_DIM))
    rg_g = ret_norm_g.reshape(1, RET_WIDTH)
    sinks = attn_sinks.reshape(ATT_HEADS)
    blk = jnp.arange(256) // ATT_HEAD_DIM
    ones = (blk[:, None] == blk[None, :]).astype(BF16)

    wi, qs, kvs, rqs, rks, rvs, rgs, kvt, rqt, rkt = _in_proj(x_sample, ln1, w_in[0], qg, kg, inv,
                                                              ones, pos=PAST_LEN)
    sink_col = jnp.concatenate([sinks[0::2], sinks[1::2]]).reshape(ATT_HEADS, 1)
    to_fm = lambda c: c[0].transpose(0, 2, 3, 1).reshape(nb, KV_WIDTH, WINDOW)
    from_fm = lambda c: c.reshape(nb, ATT_KV_HEADS, ATT_HEAD_DIM, WINDOW).transpose(0, 3, 1, 2)[None]
    a8, nk, nv = _swa_dec(qs, kvs, kvt,
                          to_fm(cache_k_win), to_fm(cache_v_win), sink_col)
    a_s = a8.reshape(nb, ATT_WIDTH)

    xp = x_prompt[0]
    a_out, kv, rq, wo, wu = _proj_swa(xp, ln1, wi, qg, kg, inv, ones, sinks, w_out[0], w_up[0])
    r_out, s_fin, wd = _proj_ret(xp, ln1, wi, inv, rq, rg_g, w_down[0])
    x1, h2 = _out_proj(xp, a_out, r_out, wo, ln2, tm=TM)
    yp, r_s, ns = _mlp_ret(x1, h2, wu, wd, rqt, rkt, rqs, rks, rvs, rgs, rg_g, state_ret[0])

    wb = min(WINDOW, seq)
    kp = kv[seq - wb:, :KV_WIDTH].reshape(1, 1, wb, ATT_KV_HEADS, ATT_HEAD_DIM)
    vp = kv[seq - wb:, KV_WIDTH:].reshape(1, 1, wb, ATT_KV_HEADS, ATT_HEAD_DIM)
    sp = s_fin.reshape(1, 1, RET_HEADS, RET_DK, RET_DV)

    x1s, h2s = _out_proj(x_sample, a_s, r_s, wo, ln2, tm=nb)
    ys = _mlp(x1s, h2s, wu, wd, tm=nb)

    return (yp[None], ys, kp, vp, sp, from_fm(nk), from_fm(nv), ns[None])
```

```python
import functools
import math

import jax
import jax.numpy as jnp
from jax import lax
from jax.experimental import pallas as pl
from jax.experimental.pallas import tpu as pltpu

D_MODEL = 2048
ATT_HEADS = 16
ATT_KV_HEADS = 2
ATT_HEAD_DIM = 64
WINDOW = 128
RET_HEADS = 4
RET_DK = 256
RET_DV = 256
RET_CHUNK = 128
ROPE_BASE = 10000.0
D_FF = 4 * D_MODEL
EPS = 1e-6
PAST_LEN = 8192

ATT_WIDTH = ATT_HEADS * ATT_HEAD_DIM
KV_WIDTH = ATT_KV_HEADS * ATT_HEAD_DIM
RET_WIDTH = RET_HEADS * RET_DK
LANES = 128
HALF = RET_DK // 2
SPLIT_A = ATT_WIDTH + 2 * KV_WIDTH + RET_WIDTH
SPLIT_B = 3 * RET_WIDTH

F32 = jnp.float32
BF16 = jnp.bfloat16
VMEM_LIMIT = 60 * 1024 * 1024

TM = 512
TF_RET = 1024
BB_ATT = 32

LOG_G = tuple(math.log1p(-(2.0 ** (-5.0 - h))) for h in range(RET_HEADS))


def _dot(a, b):
    return jnp.dot(a, b, preferred_element_type=F32)


def _dot_nt(a, b):
    return lax.dot_general(a, b, (((1,), (1,)), ((), ())), preferred_element_type=F32)


def _dot_tn(a, b):
    return lax.dot_general(a, b, (((0,), (0,)), ((), ())), preferred_element_type=F32)


def _const_spec(shape):
    n = len(shape)
    return pl.BlockSpec(shape, lambda *_: (0,) * n, pipeline_mode=pl.Buffered(1))


def _w_in_spec(part):
    width, start = ((SPLIT_A, 0), (SPLIT_B, SPLIT_A))[part]
    return pl.BlockSpec((pl.Element(D_MODEL), pl.Element(width)), lambda *_: (0, start),
                        pipeline_mode=pl.Buffered(1))


def _params(n_axes=1):
    return pltpu.CompilerParams(dimension_semantics=("arbitrary",) * n_axes,
                                vmem_limit_bytes=VMEM_LIMIT)


def _norm_rows(x_ref, g_ref):
    x = x_ref[...]
    var = jnp.mean(x * x, axis=-1, keepdims=True)
    return (x * lax.rsqrt(var + EPS) * g_ref[...]).astype(BF16)


def _bf16_into(o_ref, v):
    return v.astype(BF16).astype(o_ref.dtype)


def _rope_init(inv_ref, cr_scr, sr_scr, pos_step):
    tm = cr_scr.shape[0]
    row = lax.broadcasted_iota(jnp.int32, (tm, 1), 0)
    ang_r = (pos_step * row).astype(F32) * inv_ref[...]
    cr_scr[...] = jnp.cos(ang_r)
    sr_scr[...] = jnp.sin(ang_r)


def _rope_tables(inv_ref, cr_scr, sr_scr, base_pos):
    base = jnp.zeros((8, 1), jnp.int32) + base_pos
    ang_b = base.astype(F32) * inv_ref[...]
    cb = jnp.cos(ang_b)[:1]
    sb = jnp.sin(ang_b)[:1]
    cr = cr_scr[...]
    sr = sr_scr[...]
    return cb * cr - sb * sr, sb * cr + cb * sr


def _rope_head(r, cos, sin, scale, o_ref, hh):
    x1 = r[:, :HALF]
    x2 = r[:, HALF:]
    o1 = x1 * cos - x2 * sin
    o2 = x2 * cos + x1 * sin
    if scale != 1.0:
        o1 = o1 * scale
        o2 = o2 * scale
    o_ref[:, RET_DK * hh:RET_DK * hh + HALF] = _bf16_into(o_ref, o1)
    o_ref[:, RET_DK * hh + HALF:RET_DK * (hh + 1)] = _bf16_into(o_ref, o2)


def _half_split(a):
    lane = lax.broadcasted_iota(jnp.int32, a.shape, 1)
    lo = lane < ATT_HEAD_DIM
    sw = pltpu.roll(a, ATT_HEAD_DIM, 1)
    zero = jnp.zeros_like(a)
    h0 = (jnp.where(lo, a, zero).astype(BF16), jnp.where(lo, zero, sw).astype(BF16))
    h1 = (jnp.where(lo, sw, zero).astype(BF16), jnp.where(lo, zero, a).astype(BF16))
    return h0, h1


SWA_STACKS = 1


def _swa_phases(sink_ref, q_ref, kvc_ref, kvp_ref, o_ref, has_prev):
    nsub = q_ref.shape[0] // WINDOW
    npair = ATT_HEADS // 2
    row = lax.broadcasted_iota(jnp.int32, (WINDOW, 2 * WINDOW), 0)
    col = lax.broadcasted_iota(jnp.int32, (WINDOW, 2 * WINDOW), 1)
    band_cur = (col >= WINDOW) & (col - WINDOW <= row)
    first_col = jnp.where(has_prev, 0, WINDOW)
    lane_lo = lax.broadcasted_iota(jnp.int32, (WINDOW, LANES), 1) < ATT_HEAD_DIM
    ctx = [{} for _ in range(nsub)]
    st = {}

    ppk = npair // ATT_KV_HEADS

    def stacks(g):
        for m in range(g * SWA_STACKS, (g + 1) * SWA_STACKS):
            j, kh = divmod(m, ATT_KV_HEADS)
            yield j, kh, slice(WINDOW * j, WINDOW * (j + 1)), range(ppk * kh, ppk * (kh + 1))

    def prep(g):
        for j in sorted({j for j, _, _, _ in stacks(g)} - {j for j in range(nsub) if ctx[j]}):
            prev = kvp_ref[...] if j == 0 else kvc_ref[WINDOW * (j - 1):WINDOW * j, :]
            cur = kvc_ref[WINDOW * j:WINDOW * (j + 1), :]
            kk = jnp.concatenate([prev[:, :KV_WIDTH], cur[:, :KV_WIDTH]], axis=0)
            vv = jnp.concatenate([prev[:, KV_WIDTH:], cur[:, KV_WIDTH:]], axis=0)
            ctx[j]["k"] = _half_split(kk)
            ctx[j]["v"] = _half_split(vv)
            lo_col = first_col if j == 0 else 0
            ctx[j]["mask"] = band_cur | ((col < WINDOW) & (col >= row) & (col >= lo_col))

    def qk(g):
        for j, kh, rows, tiles in stacks(g):
            qs = jnp.concatenate([q_ref[rows, LANES * t:LANES * (t + 1)] for t in tiles], axis=0)
            s = [_dot_nt(qs, ctx[j]["k"][kh][par]) for par in range(2)]
            for n, t in enumerate(tiles):
                st[j, t] = {"s": [sp[WINDOW * n:WINDOW * (n + 1), :] for sp in s]}

    def softmax(g):
        for j, kh, rows, tiles in stacks(g):
            for t in tiles:
                p, inv = [], []
                for par in range(2):
                    sink = sink_ref[2 * t + par]
                    s = jnp.where(ctx[j]["mask"], st[j, t]["s"][par], -jnp.inf)
                    mx = jnp.maximum(jnp.max(s, axis=-1, keepdims=True), sink)
                    e = jnp.exp(s - mx)
                    den = jnp.sum(e, axis=-1, keepdims=True) + jnp.exp(sink - mx)
                    p.append(e.astype(BF16))
                    inv.append(1.0 / den)
                st[j, t] = {"p": p, "inv": jnp.where(lane_lo, inv[0], inv[1])}

    def pv(g):
        for j, kh, rows, tiles in stacks(g):
            got = [st.pop((j, t)) for t in tiles]
            acc = sum(_dot(jnp.concatenate([u["p"][par] for u in got], axis=0),
                           ctx[j]["v"][kh][par]) for par in range(2))
            for n, t in enumerate(tiles):
                o_ref[rows, LANES * t:LANES * (t + 1)] = (
                    acc[WINDOW * n:WINDOW * (n + 1), :] * got[n]["inv"]).astype(BF16)

    ngroup = nsub * ATT_KV_HEADS // SWA_STACKS
    return ngroup, prep, qk, softmax, pv


def _emit_pipelined(main, ngroup, prep, first, middle, last, finish=None):
    nstage = ngroup + 2
    done = 0
    prep(0)
    for k in range(nstage):
        if k < ngroup:
            first(k)
        if 0 <= k - 2 < ngroup:
            last(k - 2)
        if 0 <= k - 1 < ngroup:
            middle(k - 1)
        if k + 1 < ngroup:
            prep(k + 1)
        upto = min(len(main), k + 1) if k + 1 < nstage else len(main)
        for piece in main[done:upto]:
            piece()
        done = upto
    if finish is not None:
        finish()


def _ret_phases(rq_ref, rk_ref, rv_ref, gt_ref, g_ref, o_ref, s_scr):
    c = RET_CHUNK
    nsub = rq_ref.shape[0] // c
    ri = lax.broadcasted_iota(jnp.int32, (c, c), 0)
    ci = lax.broadcasted_iota(jnp.int32, (c, c), 1)
    rel = (ri - ci).astype(F32)
    idx = lax.broadcasted_iota(jnp.int32, (c, 1), 0).astype(F32)
    head = []
    for hh in range(RET_HEADS):
        lg = LOG_G[hh]
        head.append(dict(
            dmask=jnp.where(rel >= 0, jnp.exp(lg * jnp.maximum(rel, 0.0)), 0.0),
            qdec=jnp.exp(lg * (idx + 1.0)), kdec=jnp.exp(lg * (c - 1.0 - idx)),
            cdec=math.exp(lg * c), cols=slice(RET_DK * hh, RET_DK * (hh + 1))))
    st = {}

    def update_state(g):
        for hh, hd in enumerate(head):
            u = st[g, hh]
            s_scr[hh] = hd["cdec"] * u.pop("s_prev") + u.pop("kv")

    def prep(g):
        if g > 0:
            update_state(g - 1)
        rows = slice(c * g, c * (g + 1))
        for hh, hd in enumerate(head):
            q = rq_ref[rows, hd["cols"]]
            k = rk_ref[rows, hd["cols"]]
            s_prev = s_scr[hh]
            st[g, hh] = dict(q=q, k=k, v=rv_ref[rows, hd["cols"]], s_prev=s_prev,
                             qd=(q.astype(F32) * hd["qdec"]).astype(BF16),
                             kd=(k.astype(F32) * hd["kdec"]).astype(BF16),
                             s_bf=s_prev.astype(BF16))

    def first(g):
        for hh in range(RET_HEADS):
            u = st[g, hh]
            u["att"] = _dot_nt(u.pop("q"), u.pop("k"))
            u["inter"] = _dot(u.pop("qd"), u.pop("s_bf"))
            u["kv"] = _dot_tn(u.pop("kd"), u["v"])

    def middle(g):
        for hh, hd in enumerate(head):
            u = st[g, hh]
            u["att"] = (u["att"] * hd["dmask"]).astype(BF16)

    def last(g):
        rows = slice(c * g, c * (g + 1))
        for hh, hd in enumerate(head):
            u = st[g, hh]
            o = _dot(u.pop("att"), u.pop("v")) + u.pop("inter")
            var = jnp.mean(o * o, axis=-1, keepdims=True)
            on = o * lax.rsqrt(var + EPS) * g_ref[:, hd["cols"]]
            o_ref[rows, hd["cols"]] = (on * gt_ref[rows, hd["cols"]].astype(F32)).astype(BF16)

    return nsub, prep, first, middle, last, functools.partial(update_state, nsub - 1)


def _proj_swa_body(sink_ref, x_ref, g_ref, w_ref, qg_ref, kg_ref, inv_ref, ones_ref,
                   wo_ref, wu_ref,
                   a_out, kv_out, rq_out, wo_bf, wu_bf,
                   q_scr, kv_scr, kvp_scr, cr_scr, sr_scr, *, nblk):
    i = pl.program_id(0)
    tm = x_ref.shape[0]

    @pl.when(i == 0)
    def _():
        _rope_init(inv_ref, cr_scr, sr_scr, 1)
        q_scr[...] = jnp.zeros_like(q_scr)
        kv_scr[...] = jnp.zeros_like(kv_scr)
        kvp_scr[...] = jnp.zeros_like(kvp_scr)

    wo_bf[...] = wo_ref[...].astype(BF16)
    wu_bf[0] = wu_ref[...].astype(BF16)

    cur = lax.rem(i, 2)
    prv = 1 - cur
    blk = jnp.minimum(i, nblk - 1)

    h = _norm_rows(x_ref, g_ref)
    ones = ones_ref[...]
    inv_hd = 1.0 / ATT_HEAD_DIM
    cos, sin = _rope_tables(inv_ref, cr_scr, sr_scr, blk * tm)


    def q_piece(n):
        y = _dot(h, w_ref[:, 512 * n:512 * (n + 1)])
        for t in range(2):
            yt = y[:, 256 * t:256 * (t + 1)]
            ssq = _dot((yt * yt).astype(BF16), ones)
            c0 = 512 * n + 256 * t
            q_scr[cur, :, c0:c0 + 256] = (
                yt * lax.rsqrt(ssq * inv_hd + EPS) * qg_ref[...]).astype(BF16)

    kv0 = ATT_WIDTH
    rq0 = ATT_WIDTH + 2 * KV_WIDTH

    def kv_rq_piece():
        y = _dot(h, w_ref[:, kv0:rq0 + RET_DK])
        k = y[:, :KV_WIDTH]
        ssq = _dot((k * k).astype(BF16), ones[:KV_WIDTH, :KV_WIDTH])
        kn = k * lax.rsqrt(ssq * inv_hd + EPS) * kg_ref[...]
        v = y[:, KV_WIDTH:2 * KV_WIDTH]
        kv_scr[cur, :, :KV_WIDTH] = kn
        kv_scr[cur, :, KV_WIDTH:] = v
        kv_out[:, :KV_WIDTH] = kn
        kv_out[:, KV_WIDTH:] = v
        _rope_head(y[:, 2 * KV_WIDTH:], cos, sin, 1.0, rq_out, 0)

    def rq_piece():
        y = _dot(h, w_ref[:, rq0 + RET_DK:])
        for hh in range(1, RET_HEADS):
            _rope_head(y[:, RET_DK * (hh - 1):RET_DK * hh], cos, sin, 1.0, rq_out, hh)

    main = [functools.partial(q_piece, 0), functools.partial(q_piece, 1), kv_rq_piece, rq_piece]
    _emit_pipelined(main, *_swa_phases(sink_ref, q_scr.at[prv], kv_scr.at[prv], kvp_scr, a_out,
                                       has_prev=i > 1))
    kvp_scr[...] = kv_scr[prv, tm - WINDOW:, :]


def _proj_swa(x, ln_g, w_in, qg, kg, inv, ones, sinks, w_out, w_up):
    m = x.shape[0]
    nblk = m // TM
    cl = lambda i: jnp.minimum(i, nblk - 1)
    row = lambda w: pl.BlockSpec((TM, w), lambda i: (cl(i), 0))
    wo_spec = pl.BlockSpec((w_out.shape[0] // nblk, w_out.shape[1]), lambda i: (cl(i), 0))
    wcast = w_up.shape[1] // nblk
    per_slab = TF_RET // wcast
    assert w_up.shape[1] % nblk == 0 and TF_RET % wcast == 0
    wu_spec = pl.BlockSpec((w_up.shape[0], wcast), lambda i: (0, cl(i)))
    wu_out_spec = pl.BlockSpec((1, w_up.shape[0], wcast),
                               lambda i: (cl(i) // per_slab, 0, cl(i) % per_slab))
    return pl.pallas_call(
        functools.partial(_proj_swa_body, nblk=nblk),
        grid=(nblk + 1,),
        in_specs=[pl.BlockSpec(memory_space=pltpu.SMEM),
                  row(D_MODEL), _const_spec((1, D_MODEL)), _w_in_spec(0),
                  _const_spec((1, 256)), _const_spec((1, KV_WIDTH)),
                  _const_spec((1, HALF)), _const_spec((256, 256)), wo_spec, wu_spec],
        out_specs=[pl.BlockSpec((TM, ATT_WIDTH), lambda i: (jnp.maximum(i - 1, 0), 0)),
                   row(2 * KV_WIDTH), row(RET_WIDTH), wo_spec, wu_out_spec],
        out_shape=(jax.ShapeDtypeStruct((m, ATT_WIDTH), BF16),
                   jax.ShapeDtypeStruct((m, 2 * KV_WIDTH), F32),
                   jax.ShapeDtypeStruct((m, RET_WIDTH), BF16),
                   jax.ShapeDtypeStruct(w_out.shape, BF16),
                   jax.ShapeDtypeStruct((w_up.shape[1] // TF_RET, w_up.shape[0], TF_RET), BF16)),
        scratch_shapes=[pltpu.VMEM((2, TM, ATT_WIDTH), BF16),
                        pltpu.VMEM((2, TM, 2 * KV_WIDTH), F32),
                        pltpu.VMEM((WINDOW, 2 * KV_WIDTH), F32),
                        pltpu.VMEM((TM, HALF), F32), pltpu.VMEM((TM, HALF), F32)],
        compiler_params=_params(),
        name="proj_swa",
    )(sinks, x, ln_g, w_in, qg, kg, inv, ones, w_out, w_up)


def _proj_ret_body(x_ref, g_ref, w_ref, inv_ref, rq_ref, rg_ref, wd_ref,
                   r_out, s_out, wd_bf,
                   rk_scr, rv_scr, gt_scr, s_scr, cr_scr, sr_scr, *, nblk):
    i = pl.program_id(0)
    tm = x_ref.shape[0]

    @pl.when(i == 0)
    def _():
        _rope_init(inv_ref, cr_scr, sr_scr, 1)
        rk_scr[...] = jnp.zeros_like(rk_scr)
        rv_scr[...] = jnp.zeros_like(rv_scr)
        gt_scr[...] = jnp.zeros_like(gt_scr)
        s_scr[...] = jnp.zeros_like(s_scr)

    wd_bf[...] = wd_ref[...].astype(BF16)

    cur = lax.rem(i, 2)
    prv = 1 - cur
    blk = jnp.minimum(i, nblk - 1)

    h = _norm_rows(x_ref, g_ref)
    cos, sin = _rope_tables(inv_ref, cr_scr, sr_scr, blk * tm)

    def rk_piece(n):
        y = _dot(h, w_ref[:, 512 * n:512 * (n + 1)])
        for t in range(2):
            _rope_head(y[:, RET_DK * t:RET_DK * (t + 1)], cos, sin, RET_DK ** -0.5,
                       rk_scr.at[cur], 2 * n + t)

    def rv_piece(n):
        c = slice(512 * n, 512 * (n + 1))
        rv_scr[cur, :, c] = _dot(h, w_ref[:, RET_WIDTH + 512 * n:RET_WIDTH + 512 * (n + 1)]
                                 ).astype(BF16)

    def gate_piece(n):
        c0 = 2 * RET_WIDTH + 512 * n
        rg = _dot(h, w_ref[:, c0:c0 + 512])
        gt_scr[cur, :, 512 * n:512 * (n + 1)] = (rg / (1.0 + jnp.exp(-rg))).astype(BF16)

    main = [lambda f=f: (f(0), f(1)) for f in (rk_piece, rv_piece, gate_piece)]
    _emit_pipelined(main, *_ret_phases(rq_ref, rk_scr.at[prv], rv_scr.at[prv], gt_scr.at[prv],
                                       rg_ref, r_out, s_scr))

    @pl.when(i == nblk)
    def _():
        s_out[...] = s_scr[...]


def _proj_ret(x, ln_g, w_in, inv, rq, rg_g, w_down):
    m = x.shape[0]
    nblk = m // TM
    cl = lambda i: jnp.minimum(i, nblk - 1)
    prev = lambda w: pl.BlockSpec((TM, w), lambda i: (jnp.maximum(i - 1, 0), 0))
    wd_spec = pl.BlockSpec((w_down.shape[0] // nblk, w_down.shape[1]), lambda i: (cl(i), 0))
    state = (RET_HEADS, RET_DK, RET_DV)
    slot = pltpu.VMEM((2, TM, RET_WIDTH), BF16)
    return pl.pallas_call(
        functools.partial(_proj_ret_body, nblk=nblk),
        grid=(nblk + 1,),
        in_specs=[pl.BlockSpec((TM, D_MODEL), lambda i: (cl(i), 0)),
                  _const_spec((1, D_MODEL)), _w_in_spec(1), _const_spec((1, HALF)),
                  prev(RET_WIDTH), _const_spec((1, RET_WIDTH)), wd_spec],
        out_specs=[prev(RET_WIDTH), pl.BlockSpec(state, lambda i: (0, 0, 0)), wd_spec],
        out_shape=(jax.ShapeDtypeStruct((m, RET_WIDTH), BF16),
                   jax.ShapeDtypeStruct(state, F32),
                   jax.ShapeDtypeStruct(w_down.shape, BF16)),
        scratch_shapes=[slot, slot, slot, pltpu.VMEM(state, F32),
                        pltpu.VMEM((TM, HALF), F32), pltpu.VMEM((TM, HALF), F32)],
        compiler_params=_params(),
        name="proj_ret",
    )(x, ln_g, w_in, inv, rq, rg_g, w_down)


TN_IN = 1792
Z_TILE = 256


def _in_proj_body(x_ref, g_ref, w_ref, qg_ref, kg_ref, inv_ref, ones_ref,
                  w_bf, q_out, kv_out, rq_out, rk_out, rv_out, gt_out, kvt_out, rqt_out, rkt_out,
                  h_scr, z_scr, *, pos):
    c = pl.program_id(0)

    @pl.when(c == 0)
    def _():
        h_scr[...] = _norm_rows(x_ref.at[:, 0], g_ref)

    wb = w_ref[...].astype(BF16)
    w_bf[...] = wb
    z = _dot(h_scr[...], wb)
    per_step = TN_IN // Z_TILE
    for t in range(per_step):
        z_scr[c * per_step + t] = z[:, Z_TILE * t:Z_TILE * (t + 1)]

    @pl.when(c == pl.num_programs(0) - 1)
    def _():
        tiles = lambda col0, n: [z_scr[col0 // Z_TILE + t] for t in range(n)]
        ones = ones_ref[...]
        inv_hd = 1.0 / ATT_HEAD_DIM
        for t, y in enumerate(tiles(0, ATT_WIDTH // Z_TILE)):
            ssq = _dot((y * y).astype(BF16), ones)
            qn = _bf16_into(q_out, y * lax.rsqrt(ssq * inv_hd + EPS) * qg_ref[...])
            for half in range(2):
                q_out[:, 2 * t + half, :] = qn[:, LANES * half:LANES * (half + 1)]
        (kvr,) = tiles(ATT_WIDTH, 1)
        k = kvr[:, :KV_WIDTH]
        ssq = _dot((k * k).astype(BF16), ones[:KV_WIDTH, :KV_WIDTH])
        kv_out[:, :KV_WIDTH] = k * lax.rsqrt(ssq * inv_hd + EPS) * kg_ref[...]
        kv_out[:, KV_WIDTH:] = kvr[:, KV_WIDTH:]
        ang = jnp.full((8, 1), pos, jnp.int32).astype(F32) * inv_ref[...]
        cos = jnp.cos(ang)[:1]
        sin = jnp.sin(ang)[:1]
        rq0 = ATT_WIDTH + 2 * KV_WIDTH
        for hh, y in enumerate(tiles(rq0, RET_HEADS)):
            _rope_head(y, cos, sin, 1.0, rq_out, hh)
        for hh, y in enumerate(tiles(rq0 + RET_WIDTH, RET_HEADS)):
            _rope_head(y, cos, sin, RET_DK ** -0.5, rk_out, hh)
        for hh, y in enumerate(tiles(rq0 + 2 * RET_WIDTH, RET_HEADS)):
            rv_out[:, RET_DV * hh:RET_DV * (hh + 1)] = _bf16_into(rv_out, y)
        for hh, y in enumerate(tiles(rq0 + 3 * RET_WIDTH, RET_HEADS)):
            gt_out[:, RET_DV * hh:RET_DV * (hh + 1)] = _bf16_into(gt_out, y / (1.0 + jnp.exp(-y)))
        kvt_out[...] = kv_out[...].T
        rqt_out[...] = rq_out[...].T
        rkt_out[...] = rk_out[...].T


def _in_proj(x, ln_g, w_in, qg, kg, inv, ones, pos):
    m = x.shape[0]
    assert m == LANES
    assert RET_DK == Z_TILE and TN_IN % Z_TILE == 0 and w_in.shape[1] % TN_IN == 0
    nstep = w_in.shape[1] // TN_IN
    full = lambda w: pl.BlockSpec((m, w), lambda c: (0, 0))
    colm = lambda w: pl.BlockSpec((w, m), lambda c: (0, 0))
    wcol = pl.BlockSpec((D_MODEL, TN_IN), lambda c: (0, c))
    ret = jax.ShapeDtypeStruct((m, RET_WIDTH), F32)
    ret_t = jax.ShapeDtypeStruct((RET_WIDTH, m), F32)
    return pl.pallas_call(
        functools.partial(_in_proj_body, pos=pos),
        grid=(nstep,),
        in_specs=[_const_spec((m, 1, D_MODEL)), _const_spec((1, D_MODEL)), wcol,
                  _const_spec((1, 256)), _const_spec((1, KV_WIDTH)),
                  _const_spec((1, HALF)), _const_spec((256, 256))],
        out_specs=[wcol, pl.BlockSpec((m, ATT_HEADS // 2, LANES), lambda c: (0, 0, 0)),
                   full(2 * KV_WIDTH)] + [full(RET_WIDTH)] * 4
        + [colm(2 * KV_WIDTH), colm(RET_WIDTH), colm(RET_WIDTH)],
        out_shape=(jax.ShapeDtypeStruct(w_in.shape, BF16),
                   jax.ShapeDtypeStruct((m, ATT_HEADS // 2, LANES), F32),
                   jax.ShapeDtypeStruct((m, 2 * KV_WIDTH), F32), ret, ret, ret, ret,
                   jax.ShapeDtypeStruct((2 * KV_WIDTH, m), F32), ret_t, ret_t),
        scratch_shapes=[pltpu.VMEM((m, D_MODEL), BF16),
                        pltpu.VMEM((w_in.shape[1] // Z_TILE, m, Z_TILE), F32)],
        compiler_params=_params(),
        name="in_proj",
    )(x, ln_g, w_in, qg, kg, inv, ones)


def _this_steps_columns(t_ref, rows, bb):
    shift = lax.rem(LANES - bb * pl.program_id(0), LANES)
    return pltpu.roll(t_ref[rows, :], shift, 1)


def _swa_dec_body(q_ref, kvn_ref, kvt_ref, ck_ref, cv_ref, sink_ref, o_ref, nk_ref, nv_ref):
    bb = q_ref.shape[0]
    npair = ATT_HEADS // 2
    q8 = q_ref[...]
    q8r = pltpu.roll(q8, ATT_HEAD_DIM, 2)
    lane = lax.broadcasted_iota(jnp.int32, q8.shape, 2)
    pair = lax.broadcasted_iota(jnp.int32, q8.shape, 1)
    lo = lane < ATT_HEAD_DIM
    kv0 = pair < npair // ATT_KV_HEADS
    own = lo == kv0
    zero = jnp.zeros_like(q8)
    qe = jnp.where(own, jnp.where(kv0, q8, q8r), zero)
    qo = jnp.where(own, jnp.where(kv0, q8r, q8), zero)
    qb = jnp.concatenate([qe, qo], axis=1)

    ck = ck_ref[...]
    cv = cv_ref[...]
    kn = kvn_ref[:, :KV_WIDTH]
    vn = kvn_ref[:, KV_WIDTH:]
    s = lax.dot_general(qb.astype(BF16), ck.astype(BF16), (((2,), (1,)), ((0,), (0,))),
                        preferred_element_type=F32)
    s_new = jnp.sum(qb * kn[:, None, :], axis=-1, keepdims=True)
    sink = sink_ref[...][None, :, :]
    mx = jnp.maximum(jnp.maximum(jnp.max(s, axis=-1, keepdims=True), s_new), sink)
    p = jnp.exp(s - mx)
    p_new = jnp.exp(s_new - mx)
    den = jnp.sum(p, axis=-1, keepdims=True) + p_new + jnp.exp(sink - mx)
    o = lax.dot_general(p.astype(BF16), cv.astype(BF16), (((2,), (2,)), ((0,), (0,))),
                        preferred_element_type=F32)
    o = (o + p_new * vn[:, None, :]) / den
    oe = o[:, :npair, :]
    oo = o[:, npair:, :]
    oer = pltpu.roll(oe, ATT_HEAD_DIM, 2)
    oor = pltpu.roll(oo, ATT_HEAD_DIM, 2)
    o_ref[...] = jnp.where(lo, jnp.where(kv0, oe, oer), jnp.where(kv0, oor, oo)).astype(BF16)

    newcol = _this_steps_columns(kvt_ref, slice(None), bb)
    last = lax.broadcasted_iota(jnp.int32, (KV_WIDTH, WINDOW), 1) == WINDOW - 1
    for jb in range(bb):
        nk_ref[jb] = jnp.where(last, newcol[:KV_WIDTH, jb:jb + 1],
                               pltpu.roll(ck[jb], WINDOW - 1, 1))
        nv_ref[jb] = jnp.where(last, newcol[KV_WIDTH:, jb:jb + 1],
                               pltpu.roll(cv[jb], WINDOW - 1, 1))


def _swa_dec(q8, kvn, kvt, ck, cv, sink_col):
    b = q8.shape[0]
    bb = BB_ATT
    cache = pl.BlockSpec((bb, KV_WIDTH, WINDOW), lambda i: (i, 0, 0))
    return pl.pallas_call(
        _swa_dec_body,
        grid=(b // bb,),
        in_specs=[pl.BlockSpec((bb, ATT_HEADS // 2, LANES), lambda i: (i, 0, 0)),
                  pl.BlockSpec((bb, 2 * KV_WIDTH), lambda i: (i, 0)),
                  _const_spec(kvt.shape), cache, cache, _const_spec((ATT_HEADS, 1))],
        out_specs=[pl.BlockSpec((bb, ATT_HEADS // 2, LANES), lambda i: (i, 0, 0)), cache, cache],
        out_shape=(jax.ShapeDtypeStruct((b, ATT_HEADS // 2, LANES), BF16),
                   jax.ShapeDtypeStruct(ck.shape, F32),
                   jax.ShapeDtypeStruct(cv.shape, F32)),
        compiler_params=_params(),
        name="swa_dec",
    )(q8, kvn, kvt, ck, cv, sink_col)


def _out_proj_body(x_ref, a_ref, r_ref, w_ref, g_ref, x1_ref, h2_ref):
    if len(x_ref.shape) == 3:
        x_ref = x_ref.at[:, 0]
    x1 = (x_ref[...] + _dot(a_ref[...].astype(BF16), w_ref[:ATT_WIDTH, :])
          + _dot(r_ref[...].astype(BF16), w_ref[ATT_WIDTH:, :]))
    x1_ref[...] = x1
    var = jnp.mean(x1 * x1, axis=-1, keepdims=True)
    h2_ref[...] = (x1 * lax.rsqrt(var + EPS) * g_ref[...]).astype(BF16)


def _out_proj(x, a, r, w, g, tm):
    m = x.shape[0]
    row = lambda w: pl.BlockSpec((tm, w), lambda i: (i, 0))
    x_spec = row(D_MODEL) if x.ndim == 2 else pl.BlockSpec((tm, 1, D_MODEL), lambda i: (i, 0, 0))
    return pl.pallas_call(
        _out_proj_body,
        grid=(m // tm,),
        in_specs=[x_spec, row(ATT_WIDTH), row(RET_WIDTH),
                  _const_spec(w.shape), _const_spec((1, D_MODEL))],
        out_specs=[row(D_MODEL), row(D_MODEL)],
        out_shape=(jax.ShapeDtypeStruct((m, D_MODEL), F32),
                   jax.ShapeDtypeStruct((m, D_MODEL), BF16)),
        compiler_params=_params(),
        name="out_proj",
    )(x, a, r, w, g)


def _mlp_acc(h2_ref, wu_ref, wd_ref, o_ref):
    u = jnp.maximum(_dot(h2_ref[...], wu_ref[0]), 0.0)
    o_ref[...] += _dot((u * u).astype(BF16), wd_ref[...])


def _mlp_body(x1_ref, h2_ref, wu_ref, wd_ref, o_ref, acc):
    @pl.when(pl.program_id(1) == 0)
    def _():
        acc[...] = x1_ref[...]

    _mlp_acc(h2_ref, wu_ref, wd_ref, acc)

    @pl.when(pl.program_id(1) == pl.num_programs(1) - 1)
    def _():
        o_ref[:, 0, :] = acc[...]


def _mlp_weight_specs():
    return [pl.BlockSpec((1, D_MODEL, TF_RET), lambda i, f: (f, 0, 0)),
            pl.BlockSpec((TF_RET, D_MODEL), lambda i, f: (f, 0))]


def _mlp(x1, h2, wu, wd, tm):
    m = x1.shape[0]
    row = pl.BlockSpec((tm, D_MODEL), lambda i, f: (i, 0))
    return pl.pallas_call(
        _mlp_body,
        grid=(m // tm, D_FF // TF_RET),
        in_specs=[row, row] + _mlp_weight_specs(),
        out_specs=pl.BlockSpec((tm, 1, D_MODEL), lambda i, f: (i, 0, 0)),
        out_shape=jax.ShapeDtypeStruct((m, 1, D_MODEL), F32),
        scratch_shapes=[pltpu.VMEM((tm, D_MODEL), F32)],
        compiler_params=_params(2),
        name="mlp",
    )(x1, h2, wu, wd)


RING = 3


def _mlp_ret_body(x1_ref, h2_ref, wu_hbm, wd_hbm,
                  qt_ref, kt_ref, rq_ref, rk_ref, rv_ref, gt_ref, g_ref, s_ref,
                  o_ref, r_ref, ns_ref, wu_buf, wd_buf, sem):
    nf = pl.num_programs(1)
    b = pl.program_id(0) * nf + pl.program_id(1)
    nstep = pl.num_programs(0) * nf

    def slab_copies(step):
        slab = lax.rem(step, nf)
        slot = lax.rem(step, RING)
        rows = pl.ds(pl.multiple_of(slab * TF_RET, TF_RET), TF_RET)
        return (pltpu.make_async_copy(wu_hbm.at[slab], wu_buf.at[slot], sem.at[0, slot]),
                pltpu.make_async_copy(wd_hbm.at[rows, :], wd_buf.at[slot], sem.at[1, slot]))

    @pl.when(b == 0)
    def _():
        for step in range(RING - 1):
            for copy in slab_copies(step):
                copy.start()

    @pl.when(b + RING - 1 < nstep)
    def _():
        for copy in slab_copies(b + RING - 1):
            copy.start()

    for copy in slab_copies(b):
        copy.wait()
    slot = lax.rem(b, RING)

    @pl.when(pl.program_id(1) == 0)
    def _():
        o_ref[...] = x1_ref[...]

    u = jnp.maximum(_dot(h2_ref[...], wu_buf[slot]), 0.0)
    o_ref[...] += _dot((u * u).astype(BF16), wd_buf[slot])

    row = pl.ds(b, 1)
    shift = lax.rem(LANES - b, LANES)
    for hh in range(RET_HEADS):
        g1 = math.exp(LOG_G[hh])
        cols = slice(RET_DK * hh, RET_DK * (hh + 1))
        qc = pltpu.roll(qt_ref[cols, :], shift, 1)[:, :1] * g1
        kc = pltpu.roll(kt_ref[cols, :], shift, 1)[:, :1]
        v = rv_ref[row, cols]
        s0 = s_ref[0, hh]
        ns_ref[0, hh] = g1 * s0 + kc * v
        qk = jnp.sum(rq_ref[row, cols] * rk_ref[row, cols], axis=-1, keepdims=True)
        o = jnp.sum(qc * s0, axis=0, keepdims=True) + qk * v
        var = jnp.mean(o * o, axis=-1, keepdims=True)
        r_ref[row, cols] = o * lax.rsqrt(var + EPS) * g_ref[:, cols] * gt_ref[row, cols]


def _mlp_ret(x1, h2, wu, wd, qt, kt, rq, rk, rv, gt, g, state):
    m = x1.shape[0]
    nb = state.shape[0]
    nf = D_FF // TF_RET
    assert m // TM * nf == nb and nb == LANES
    assert nb >= RING
    row = pl.BlockSpec((TM, D_MODEL), lambda i, f: (i, 0))
    hbm = pl.BlockSpec(memory_space=pl.ANY)
    st = pl.BlockSpec((1, RET_HEADS, RET_DK, RET_DV), lambda i, f: (i * nf + f, 0, 0, 0))
    rows = _const_spec((nb, RET_WIDTH))
    return pl.pallas_call(
        _mlp_ret_body,
        grid=(m // TM, nf),
        in_specs=[row, row, hbm, hbm,
                  _const_spec(qt.shape), _const_spec(kt.shape), rows, rows, rows, rows,
                  _const_spec((1, RET_WIDTH)), st],
        out_specs=[row, pl.BlockSpec((nb, RET_WIDTH), lambda i, f: (0, 0)), st],
        out_shape=(jax.ShapeDtypeStruct((m, D_MODEL), F32),
                   jax.ShapeDtypeStruct((nb, RET_WIDTH), F32),
                   jax.ShapeDtypeStruct(state.shape, F32)),
        scratch_shapes=[pltpu.VMEM((RING, D_MODEL, TF_RET), BF16),
                        pltpu.VMEM((RING, TF_RET, D_MODEL), BF16),
                        pltpu.SemaphoreType.DMA((2, RING))],
        compiler_params=_params(2),
        name="mlp_ret",
    )(x1, h2, wu, wd, qt, kt, rq, rk, rv, gt, g, state)


def kernel(x_prompt, x_sample, cache_k_win, cache_v_win, state_ret, ln1_g, w_in, q_norm_g,
           k_norm_g, attn_sinks, ret_norm_g, w_out, ln2_g, w_up, w_down):
    seq = x_prompt.shape[1]
    nb = x_sample.shape[0]
    assert x_prompt.shape[0] == 1 and x_sample.shape[1] == 1 and w_in.shape[0] == 1
    assert seq % TM == 0 and w_in.shape[2] == SPLIT_A + SPLIT_B

    inv = (ROPE_BASE ** (-jnp.arange(HALF, dtype=F32) / HALF)).reshape(1, HALF)

    ln1 = ln1_g.reshape(1, D_MODEL)
    ln2 = ln2_g.reshape(1, D_MODEL)
    qg = jnp.tile(q_norm_g.reshape(1, ATT_HEAD_DIM) * (ATT_HEAD_DIM ** -0.5), (1, 256 // ATT_HEAD_DIM))
    kg = jnp.tile(k_norm_g.reshape(1, ATT_HEAD_DIM), (1, KV_WIDTH // ATT_HEAD_DIM))
    rg_g = ret_norm_g.reshape(1, RET_WIDTH)
    sinks = attn_sinks.reshape(ATT_HEADS)
    blk = jnp.arange(256) // ATT_HEAD_DIM
    ones = (blk[:, None] == blk[None, :]).astype(BF16)

    wi, qs, kvs, rqs, rks, rvs, rgs, kvt, rqt, rkt = _in_proj(x_sample, ln1, w_in[0], qg, kg, inv,
                                                              ones, pos=PAST_LEN)
    sink_col = jnp.concatenate([sinks[0::2], sinks[1::2]]).reshape(ATT_HEADS, 1)
    to_fm = lambda c: c[0].transpose(0, 2, 3, 1).reshape(nb, KV_WIDTH, WINDOW)
    from_fm = lambda c: c.reshape(nb, ATT_KV_HEADS, ATT_HEAD_DIM, WINDOW).transpose(0, 3, 1, 2)[None]
    a8, nk, nv = _swa_dec(qs, kvs, kvt,
                          to_fm(cache_k_win), to_fm(cache_v_win), sink_col)
    a_s = a8.reshape(nb, ATT_WIDTH)

    xp = x_prompt[0]
    a_out, kv, rq, wo, wu = _proj_swa(xp, ln1, wi, qg, kg, inv, ones, sinks, w_out[0], w_up[0])
    r_out, s_fin, wd = _proj_ret(xp, ln1, wi, inv, rq, rg_g, w_down[0])
    x1, h2 = _out_proj(xp, a_out, r_out, wo, ln2, tm=TM)
    yp, r_s, ns = _mlp_ret(x1, h2, wu, wd, rqt, rkt, rqs, rks, rvs, rgs, rg_g, state_ret[0])

    wb = min(WINDOW, seq)
    kp = kv[seq - wb:, :KV_WIDTH].reshape(1, 1, wb, ATT_KV_HEADS, ATT_HEAD_DIM)
    vp = kv[seq - wb:, KV_WIDTH:].reshape(1, 1, wb, ATT_KV_HEADS, ATT_HEAD_DIM)
    sp = s_fin.reshape(1, 1, RET_HEADS, RET_DK, RET_DV)

    x1s, h2s = _out_proj(x_sample, a_s, r_s, wo, ln2, tm=nb)
    ys = _mlp(x1s, h2s, wu, wd, tm=nb)

    return (yp[None], ys, kp, vp, sp, from_fm(nk), from_fm(nv), ns[None])
```

```python
import functools
import math

import jax
import jax.numpy as jnp
from jax import lax
from jax.experimental import pallas as pl
from jax.experimental.pallas import tpu as pltpu

D_MODEL = 2048
ATT_HEADS = 16
ATT_KV_HEADS = 2
ATT_HEAD_DIM = 64
WINDOW = 128
RET_HEADS = 4
RET_DK = 256
RET_DV = 256
RET_CHUNK = 128
ROPE_BASE = 10000.0
D_FF = 4 * D_MODEL
EPS = 1e-6
PAST_LEN = 8192

ATT_WIDTH = ATT_HEADS * ATT_HEAD_DIM
KV_WIDTH = ATT_KV_HEADS * ATT_HEAD_DIM
RET_WIDTH = RET_HEADS * RET_DK
LANES = 128
HALF = RET_DK // 2
SPLIT_A = ATT_WIDTH + 2 * KV_WIDTH + RET_WIDTH
SPLIT_B = 3 * RET_WIDTH

F32 = jnp.float32
BF16 = jnp.bfloat16
VMEM_LIMIT = 60 * 1024 * 1024

TM = 512
TF_RET = 1024
BB_ATT = 32

LOG_G = tuple(math.log1p(-(2.0 ** (-5.0 - h))) for h in range(RET_HEADS))


def _dot(a, b):
    return jnp.dot(a, b, preferred_element_type=F32)


def _dot_nt(a, b):
    return lax.dot_general(a, b, (((1,), (1,)), ((), ())), preferred_element_type=F32)


def _dot_tn(a, b):
    return lax.dot_general(a, b, (((0,), (0,)), ((), ())), preferred_element_type=F32)


def _const_spec(shape):
    n = len(shape)
    return pl.BlockSpec(shape, lambda *_: (0,) * n, pipeline_mode=pl.Buffered(1))


def _w_in_spec(part):
    width, start = ((SPLIT_A, 0), (SPLIT_B, SPLIT_A))[part]
    return pl.BlockSpec((pl.Element(D_MODEL), pl.Element(width)), lambda *_: (0, start),
                        pipeline_mode=pl.Buffered(1))


def _params(n_axes=1):
    return pltpu.CompilerParams(dimension_semantics=("arbitrary",) * n_axes,
                                vmem_limit_bytes=VMEM_LIMIT)


def _norm_rows(x_ref, g_ref):
    x = x_ref[...]
    var = jnp.mean(x * x, axis=-1, keepdims=True)
    return (x * lax.rsqrt(var + EPS) * g_ref[...]).astype(BF16)


def _bf16_into(o_ref, v):
    return v.astype(BF16).astype(o_ref.dtype)


def _rope_init(inv_ref, cr_scr, sr_scr, pos_step):
    tm = cr_scr.shape[0]
    row = lax.broadcasted_iota(jnp.int32, (tm, 1), 0)
    ang_r = (pos_step * row).astype(F32) * inv_ref[...]
    cr_scr[...] = jnp.cos(ang_r)
    sr_scr[...] = jnp.sin(ang_r)


def _rope_tables(inv_ref, cr_scr, sr_scr, base_pos):
    base = jnp.zeros((8, 1), jnp.int32) + base_pos
    ang_b = base.astype(F32) * inv_ref[...]
    cb = jnp.cos(ang_b)[:1]
    sb = jnp.sin(ang_b)[:1]
    cr = cr_scr[...]
    sr = sr_scr[...]
    return cb * cr - sb * sr, sb * cr + cb * sr


def _rope_head(r, cos, sin, scale, o_ref, hh):
    x1 = r[:, :HALF]
    x2 = r[:, HALF:]
    o1 = x1 * cos - x2 * sin
    o2 = x2 * cos + x1 * sin
    if scale != 1.0:
        o1 = o1 * scale
        o2 = o2 * scale
    o_ref[:, RET_DK * hh:RET_DK * hh + HALF] = _bf16_into(o_ref, o1)
    o_ref[:, RET_DK * hh + HALF:RET_DK * (hh + 1)] = _bf16_into(o_ref, o2)


def _half_split(a):
    lane = lax.broadcasted_iota(jnp.int32, a.shape, 1)
    lo = lane < ATT_HEAD_DIM
    sw = pltpu.roll(a, ATT_HEAD_DIM, 1)
    zero = jnp.zeros_like(a)
    h0 = (jnp.where(lo, a, zero).astype(BF16), jnp.where(lo, zero, sw).astype(BF16))
    h1 = (jnp.where(lo, sw, zero).astype(BF16), jnp.where(lo, zero, a).astype(BF16))
    return h0, h1


SWA_STACKS = 1


def _swa_phases(sink_ref, q_ref, kvc_ref, kvp_ref, o_ref, has_prev):
    nsub = q_ref.shape[0] // WINDOW
    npair = ATT_HEADS // 2
    row = lax.broadcasted_iota(jnp.int32, (WINDOW, 2 * WINDOW), 0)
    col = lax.broadcasted_iota(jnp.int32, (WINDOW, 2 * WINDOW), 1)
    band_cur = (col >= WINDOW) & (col - WINDOW <= row)
    first_col = jnp.where(has_prev, 0, WINDOW)
    lane_lo = lax.broadcasted_iota(jnp.int32, (WINDOW, LANES), 1) < ATT_HEAD_DIM
    ctx = [{} for _ in range(nsub)]
    st = {}

    ppk = npair // ATT_KV_HEADS

    def stacks(g):
        for m in range(g * SWA_STACKS, (g + 1) * SWA_STACKS):
            j, kh = divmod(m, ATT_KV_HEADS)
            yield j, kh, slice(WINDOW * j, WINDOW * (j + 1)), range(ppk * kh, ppk * (kh + 1))

    def prep(g):
        for j in sorted({j for j, _, _, _ in stacks(g)} - {j for j in range(nsub) if ctx[j]}):
            prev = kvp_ref[...] if j == 0 else kvc_ref[WINDOW * (j - 1):WINDOW * j, :]
            cur = kvc_ref[WINDOW * j:WINDOW * (j + 1), :]
            kk = jnp.concatenate([prev[:, :KV_WIDTH], cur[:, :KV_WIDTH]], axis=0)
            vv = jnp.concatenate([prev[:, KV_WIDTH:], cur[:, KV_WIDTH:]], axis=0)
            ctx[j]["k"] = _half_split(kk)
            ctx[j]["v"] = _half_split(vv)
            lo_col = first_col if j == 0 else 0
            ctx[j]["mask"] = band_cur | ((col < WINDOW) & (col >= row) & (col >= lo_col))

    def qk(g):
        for j, kh, rows, tiles in stacks(g):
            qs = jnp.concatenate([q_ref[rows, LANES * t:LANES * (t + 1)] for t in tiles], axis=0)
            s = [_dot_nt(qs, ctx[j]["k"][kh][par]) for par in range(2)]
            for n, t in enumerate(tiles):
                st[j, t] = {"s": [sp[WINDOW * n:WINDOW * (n + 1), :] for sp in s]}

    def softmax(g):
        for j, kh, rows, tiles in stacks(g):
            for t in tiles:
                p, inv = [], []
                for par in range(2):
                    sink = sink_ref[2 * t + par]
                    s = jnp.where(ctx[j]["mask"], st[j, t]["s"][par], -jnp.inf)
                    mx = jnp.maximum(jnp.max(s, axis=-1, keepdims=True), sink)
                    e = jnp.exp(s - mx)
                    den = jnp.sum(e, axis=-1, keepdims=True) + jnp.exp(sink - mx)
                    p.append(e.astype(BF16))
                    inv.append(1.0 / den)
                st[j, t] = {"p": p, "inv": jnp.where(lane_lo, inv[0], inv[1])}

    def pv(g):
        for j, kh, rows, tiles in stacks(g):
            got = [st.pop((j, t)) for t in tiles]
            acc = sum(_dot(jnp.concatenate([u["p"][par] for u in got], axis=0),
                           ctx[j]["v"][kh][par]) for par in range(2))
            for n, t in enumerate(tiles):
                o_ref[rows, LANES * t:LANES * (t + 1)] = (
                    acc[WINDOW * n:WINDOW * (n + 1), :] * got[n]["inv"]).astype(BF16)

    ngroup = nsub * ATT_KV_HEADS // SWA_STACKS
    return ngroup, prep, qk, softmax, pv


def _emit_pipelined(main, ngroup, prep, first, middle, last, finish=None):
    nstage = ngroup + 2
    done = 0
    prep(0)
    for k in range(nstage):
        if k < ngroup:
            first(k)
        if 0 <= k - 2 < ngroup:
            last(k - 2)
        if 0 <= k - 1 < ngroup:
            middle(k - 1)
        if k + 1 < ngroup:
            prep(k + 1)
        upto = min(len(main), k + 1) if k + 1 < nstage else len(main)
        for piece in main[done:upto]:
            piece()
        done = upto
    if finish is not None:
        finish()


def _ret_phases(rq_ref, rk_ref, rv_ref, gt_ref, g_ref, o_ref, s_scr):
    c = RET_CHUNK
    nsub = rq_ref.shape[0] // c
    ri = lax.broadcasted_iota(jnp.int32, (c, c), 0)
    ci = lax.broadcasted_iota(jnp.int32, (c, c), 1)
    rel = (ri - ci).astype(F32)
    idx = lax.broadcasted_iota(jnp.int32, (c, 1), 0).astype(F32)
    head = []
    for hh in range(RET_HEADS):
        lg = LOG_G[hh]
        head.append(dict(
            dmask=jnp.where(rel >= 0, jnp.exp(lg * jnp.maximum(rel, 0.0)), 0.0),
            qdec=jnp.exp(lg * (idx + 1.0)), kdec=jnp.exp(lg * (c - 1.0 - idx)),
            cdec=math.exp(lg * c), cols=slice(RET_DK * hh, RET_DK * (hh + 1))))
    st = {}

    def update_state(g):
        for hh, hd in enumerate(head):
            u = st[g, hh]
            s_scr[hh] = hd["cdec"] * u.pop("s_prev") + u.pop("kv")

    def prep(g):
        if g > 0:
            update_state(g - 1)
        rows = slice(c * g, c * (g + 1))
        for hh, hd in enumerate(head):
            q = rq_ref[rows, hd["cols"]]
            k = rk_ref[rows, hd["cols"]]
            s_prev = s_scr[hh]
            st[g, hh] = dict(q=q, k=k, v=rv_ref[rows, hd["cols"]], s_prev=s_prev,
                             qd=(q.astype(F32) * hd["qdec"]).astype(BF16),
                             kd=(k.astype(F32) * hd["kdec"]).astype(BF16),
                             s_bf=s_prev.astype(BF16))

    def first(g):
        for hh in range(RET_HEADS):
            u = st[g, hh]
            u["att"] = _dot_nt(u.pop("q"), u.pop("k"))
            u["inter"] = _dot(u.pop("qd"), u.pop("s_bf"))
            u["kv"] = _dot_tn(u.pop("kd"), u["v"])

    def middle(g):
        for hh, hd in enumerate(head):
            u = st[g, hh]
            u["att"] = (u["att"] * hd["dmask"]).astype(BF16)

    def last(g):
        rows = slice(c * g, c * (g + 1))
        for hh, hd in enumerate(head):
            u = st[g, hh]
            o = _dot(u.pop("att"), u.pop("v")) + u.pop("inter")
            var = jnp.mean(o * o, axis=-1, keepdims=True)
            on = o * lax.rsqrt(var + EPS) * g_ref[:, hd["cols"]]
            o_ref[rows, hd["cols"]] = (on * gt_ref[rows, hd["cols"]].astype(F32)).astype(BF16)

    return nsub, prep, first, middle, last, functools.partial(update_state, nsub - 1)


def _proj_swa_body(sink_ref, x_ref, g_ref, w_ref, qg_ref, kg_ref, inv_ref, ones_ref,
                   wo_ref, wu_ref,
                   a_out, kv_out, rq_out, wo_bf, wu_bf,
                   q_scr, kv_scr, kvp_scr, cr_scr, sr_scr, *, nblk):
    i = pl.program_id(0)
    tm = x_ref.shape[0]

    @pl.when(i == 0)
    def _():
        _rope_init(inv_ref, cr_scr, sr_scr, 1)
        q_scr[...] = jnp.zeros_like(q_scr)
        kv_scr[...] = jnp.zeros_like(kv_scr)
        kvp_scr[...] = jnp.zeros_like(kvp_scr)

    wo_bf[...] = wo_ref[...].astype(BF16)
    wu_bf[0] = wu_ref[...].astype(BF16)

    cur = lax.rem(i, 2)
    prv = 1 - cur
    blk = jnp.minimum(i, nblk - 1)

    h = _norm_rows(x_ref, g_ref)
    ones = ones_ref[...]
    inv_hd = 1.0 / ATT_HEAD_DIM
    cos, sin = _rope_tables(inv_ref, cr_scr, sr_scr, blk * tm)


    def q_piece(n):
        y = _dot(h, w_ref[:, 512 * n:512 * (n + 1)])
        for t in range(2):
            yt = y[:, 256 * t:256 * (t + 1)]
            ssq = _dot((yt * yt).astype(BF16), ones)
            c0 = 512 * n + 256 * t
            q_scr[cur, :, c0:c0 + 256] = (
                yt * lax.rsqrt(ssq * inv_hd + EPS) * qg_ref[...]).astype(BF16)

    kv0 = ATT_WIDTH
    rq0 = ATT_WIDTH + 2 * KV_WIDTH

    def kv_rq_piece():
        y = _dot(h, w_ref[:, kv0:rq0 + RET_DK])
        k = y[:, :KV_WIDTH]
        ssq = _dot((k * k).astype(BF16), ones[:KV_WIDTH, :KV_WIDTH])
        kn = k * lax.rsqrt(ssq * inv_hd + EPS) * kg_ref[...]
        v = y[:, KV_WIDTH:2 * KV_WIDTH]
        kv_scr[cur, :, :KV_WIDTH] = kn
        kv_scr[cur, :, KV_WIDTH:] = v
        kv_out[:, :KV_WIDTH] = kn
        kv_out[:, KV_WIDTH:] = v
        _rope_head(y[:, 2 * KV_WIDTH:], cos, sin, 1.0, rq_out, 0)

    def rq_piece():
        y = _dot(h, w_ref[:, rq0 + RET_DK:])
        for hh in range(1, RET_HEADS):
            _rope_head(y[:, RET_DK * (hh - 1):RET_DK * hh], cos, sin, 1.0, rq_out, hh)

    main = [functools.partial(q_piece, 0), functools.partial(q_piece, 1), kv_rq_piece, rq_piece]
    _emit_pipelined(main, *_swa_phases(sink_ref, q_scr.at[prv], kv_scr.at[prv], kvp_scr, a_out,
                                       has_prev=i > 1))
    kvp_scr[...] = kv_scr[prv, tm - WINDOW:, :]


def _proj_swa(x, ln_g, w_in, qg, kg, inv, ones, sinks, w_out, w_up):
    m = x.shape[0]
    nblk = m // TM
    cl = lambda i: jnp.minimum(i, nblk - 1)
    row = lambda w: pl.BlockSpec((TM, w), lambda i: (cl(i), 0))
    wo_spec = pl.BlockSpec((w_out.shape[0] // nblk, w_out.shape[1]), lambda i: (cl(i), 0))
    wcast = w_up.shape[1] // nblk
    per_slab = TF_RET // wcast
    assert w_up.shape[1] % nblk == 0 and TF_RET % wcast == 0
    wu_spec = pl.BlockSpec((w_up.shape[0], wcast), lambda i: (0, cl(i)))
    wu_out_spec = pl.BlockSpec((1, w_up.shape[0], wcast),
                               lambda i: (cl(i) // per_slab, 0, cl(i) % per_slab))
    return pl.pallas_call(
        functools.partial(_proj_swa_body, nblk=nblk),
        grid=(nblk + 1,),
        in_specs=[pl.BlockSpec(memory_space=pltpu.SMEM),
                  row(D_MODEL), _const_spec((1, D_MODEL)), _w_in_spec(0),
                  _const_spec((1, 256)), _const_spec((1, KV_WIDTH)),
                  _const_spec((1, HALF)), _const_spec((256, 256)), wo_spec, wu_spec],
        out_specs=[pl.BlockSpec((TM, ATT_WIDTH), lambda i: (jnp.maximum(i - 1, 0), 0)),
                   row(2 * KV_WIDTH), row(RET_WIDTH), wo_spec, wu_out_spec],
        out_shape=(jax.ShapeDtypeStruct((m, ATT_WIDTH), BF16),
                   jax.ShapeDtypeStruct((m, 2 * KV_WIDTH), F32),
                   jax.ShapeDtypeStruct((m, RET_WIDTH), BF16),
                   jax.ShapeDtypeStruct(w_out.shape, BF16),
                   jax.ShapeDtypeStruct((w_up.shape[1] // TF_RET, w_up.shape[0], TF_RET), BF16)),
        scratch_shapes=[pltpu.VMEM((2, TM, ATT_WIDTH), BF16),
                        pltpu.VMEM((2, TM, 2 * KV_WIDTH), F32),
                        pltpu.VMEM((WINDOW, 2 * KV_WIDTH), F32),
                        pltpu.VMEM((TM, HALF), F32), pltpu.VMEM((TM, HALF), F32)],
        compiler_params=_params(),
        name="proj_swa",
    )(sinks, x, ln_g, w_in, qg, kg, inv, ones, w_out, w_up)


def _proj_ret_body(x_ref, g_ref, w_ref, inv_ref, rq_ref, rg_ref, wd_ref,
                   r_out, s_out, wd_bf,
                   rk_scr, rv_scr, gt_scr, s_scr, cr_scr, sr_scr, *, nblk):
    i = pl.program_id(0)
    tm = x_ref.shape[0]

    @pl.when(i == 0)
    def _():
        _rope_init(inv_ref, cr_scr, sr_scr, 1)
        rk_scr[...] = jnp.zeros_like(rk_scr)
        rv_scr[...] = jnp.zeros_like(rv_scr)
        gt_scr[...] = jnp.zeros_like(gt_scr)
        s_scr[...] = jnp.zeros_like(s_scr)

    wd_bf[...] = wd_ref[...].astype(BF16)

    cur = lax.rem(i, 2)
    prv = 1 - cur
    blk = jnp.minimum(i, nblk - 1)

    h = _norm_rows(x_ref, g_ref)
    cos, sin = _rope_tables(inv_ref, cr_scr, sr_scr, blk * tm)

    def rk_piece(n):
        y = _dot(h, w_ref[:, 512 * n:512 * (n + 1)])
        for t in range(2):
            _rope_head(y[:, RET_DK * t:RET_DK * (t + 1)], cos, sin, RET_DK ** -0.5,
                       rk_scr.at[cur], 2 * n + t)

    def rv_piece(n):
        c = slice(512 * n, 512 * (n + 1))
        rv_scr[cur, :, c] = _dot(h, w_ref[:, RET_WIDTH + 512 * n:RET_WIDTH + 512 * (n + 1)]
                                 ).astype(BF16)

    def gate_piece(n):
        c0 = 2 * RET_WIDTH + 512 * n
        rg = _dot(h, w_ref[:, c0:c0 + 512])
        gt_scr[cur, :, 512 * n:512 * (n + 1)] = (rg / (1.0 + jnp.exp(-rg))).astype(BF16)

    main = [lambda f=f: (f(0), f(1)) for f in (rk_piece, rv_piece, gate_piece)]
    _emit_pipelined(main, *_ret_phases(rq_ref, rk_scr.at[prv], rv_scr.at[prv], gt_scr.at[prv],
                                       rg_ref, r_out, s_scr))

    @pl.when(i == nblk)
    def _():
        s_out[...] = s_scr[...]


def _proj_ret(x, ln_g, w_in, inv, rq, rg_g, w_down):
    m = x.shape[0]
    nblk = m // TM
    cl = lambda i: jnp.minimum(i, nblk - 1)
    prev = lambda w: pl.BlockSpec((TM, w), lambda i: (jnp.maximum(i - 1, 0), 0))
    wd_spec = pl.BlockSpec((w_down.shape[0] // nblk, w_down.shape[1]), lambda i: (cl(i), 0))
    state = (RET_HEADS, RET_DK, RET_DV)
    slot = pltpu.VMEM((2, TM, RET_WIDTH), BF16)
    return pl.pallas_call(
        functools.partial(_proj_ret_body, nblk=nblk),
        grid=(nblk + 1,),
        in_specs=[pl.BlockSpec((TM, D_MODEL), lambda i: (cl(i), 0)),
                  _const_spec((1, D_MODEL)), _w_in_spec(1), _const_spec((1, HALF)),
                  prev(RET_WIDTH), _const_spec((1, RET_WIDTH)), wd_spec],
        out_specs=[prev(RET_WIDTH), pl.BlockSpec(state, lambda i: (0, 0, 0)), wd_spec],
        out_shape=(jax.ShapeDtypeStruct((m, RET_WIDTH), BF16),
                   jax.ShapeDtypeStruct(state, F32),
                   jax.ShapeDtypeStruct(w_down.shape, BF16)),
        scratch_shapes=[slot, slot, slot, pltpu.VMEM(state, F32),
                        pltpu.VMEM((TM, HALF), F32), pltpu.VMEM((TM, HALF), F32)],
        compiler_params=_params(),
        name="proj_ret",
    )(x, ln_g, w_in, inv, rq, rg_g, w_down)


TN_IN = 1792
Z_TILE = 256


def _in_proj_body(x_ref, g_ref, w_ref, qg_ref, kg_ref, inv_ref, ones_ref,
                  w_bf, q_out, kv_out, rq_out, rk_out, rv_out, gt_out, kvt_out, rqt_out, rkt_out,
                  h_scr, z_scr, *, pos):
    c = pl.program_id(0)

    @pl.when(c == 0)
    def _():
        h_scr[...] = _norm_rows(x_ref.at[:, 0], g_ref)

    wb = w_ref[...].astype(BF16)
    w_bf[...] = wb
    z = _dot(h_scr[...], wb)
    per_step = TN_IN // Z_TILE
    for t in range(per_step):
        z_scr[c * per_step + t] = z[:, Z_TILE * t:Z_TILE * (t + 1)]

    @pl.when(c == pl.num_programs(0) - 1)
    def _():
        tiles = lambda col0, n: [z_scr[col0 // Z_TILE + t] for t in range(n)]
        ones = ones_ref[...]
        inv_hd = 1.0 / ATT_HEAD_DIM
        for t, y in enumerate(tiles(0, ATT_WIDTH // Z_TILE)):
            ssq = _dot((y * y).astype(BF16), ones)
            qn = _bf16_into(q_out, y * lax.rsqrt(ssq * inv_hd + EPS) * qg_ref[...])
            for half in range(2):
                q_out[:, 2 * t + half, :] = qn[:, LANES * half:LANES * (half + 1)]
        (kvr,) = tiles(ATT_WIDTH, 1)
        k = kvr[:, :KV_WIDTH]
        ssq = _dot((k * k).astype(BF16), ones[:KV_WIDTH, :KV_WIDTH])
        kv_out[:, :KV_WIDTH] = k * lax.rsqrt(ssq * inv_hd + EPS) * kg_ref[...]
        kv_out[:, KV_WIDTH:] = kvr[:, KV_WIDTH:]
        ang = jnp.full((8, 1), pos, jnp.int32).astype(F32) * inv_ref[...]
        cos = jnp.cos(ang)[:1]
        sin = jnp.sin(ang)[:1]
        rq0 = ATT_WIDTH + 2 * KV_WIDTH
        for hh, y in enumerate(tiles(rq0, RET_HEADS)):
            _rope_head(y, cos, sin, 1.0, rq_out, hh)
        for hh, y in enumerate(tiles(rq0 + RET_WIDTH, RET_HEADS)):
            _rope_head(y, cos, sin, RET_DK ** -0.5, rk_out, hh)
        for hh, y in enumerate(tiles(rq0 + 2 * RET_WIDTH, RET_HEADS)):
            rv_out[:, RET_DV * hh:RET_DV * (hh + 1)] = _bf16_into(rv_out, y)
        for hh, y in enumerate(tiles(rq0 + 3 * RET_WIDTH, RET_HEADS)):
            gt_out[:, RET_DV * hh:RET_DV * (hh + 1)] = _bf16_into(gt_out, y / (1.0 + jnp.exp(-y)))
        kvt_out[...] = kv_out[...].T
        rqt_out[...] = rq_out[...].T
        rkt_out[...] = rk_out[...].T


def _in_proj(x, ln_g, w_in, qg, kg, inv, ones, pos):
    m = x.shape[0]
    assert m == LANES
    assert RET_DK == Z_TILE and TN_IN % Z_TILE == 0 and w_in.shape[1] % TN_IN == 0
    nstep = w_in.shape[1] // TN_IN
    full = lambda w: pl.BlockSpec((m, w), lambda c: (0, 0))
    colm = lambda w: pl.BlockSpec((w, m), lambda c: (0, 0))
    wcol = pl.BlockSpec((D_MODEL, TN_IN), lambda c: (0, c))
    ret = jax.ShapeDtypeStruct((m, RET_WIDTH), F32)
    ret_t = jax.ShapeDtypeStruct((RET_WIDTH, m), F32)
    return pl.pallas_call(
        functools.partial(_in_proj_body, pos=pos),
        grid=(nstep,),
        in_specs=[_const_spec((m, 1, D_MODEL)), _const_spec((1, D_MODEL)), wcol,
                  _const_spec((1, 256)), _const_spec((1, KV_WIDTH)),
                  _const_spec((1, HALF)), _const_spec((256, 256))],
        out_specs=[wcol, pl.BlockSpec((m, ATT_HEADS // 2, LANES), lambda c: (0, 0, 0)),
                   full(2 * KV_WIDTH)] + [full(RET_WIDTH)] * 4
        + [colm(2 * KV_WIDTH), colm(RET_WIDTH), colm(RET_WIDTH)],
        out_shape=(jax.ShapeDtypeStruct(w_in.shape, BF16),
                   jax.ShapeDtypeStruct((m, ATT_HEADS // 2, LANES), F32),
                   jax.ShapeDtypeStruct((m, 2 * KV_WIDTH), F32), ret, ret, ret, ret,
                   jax.ShapeDtypeStruct((2 * KV_WIDTH, m), F32), ret_t, ret_t),
        scratch_shapes=[pltpu.VMEM((m, D_MODEL), BF16),
                        pltpu.VMEM((w_in.shape[1] // Z_TILE, m, Z_TILE), F32)],
        compiler_params=_params(),
        name="in_proj",
    )(x, ln_g, w_in, qg, kg, inv, ones)


def _this_steps_columns(t_ref, rows, bb):
    shift = lax.rem(LANES - bb * pl.program_id(0), LANES)
    return pltpu.roll(t_ref[rows, :], shift, 1)


def _swa_dec_body(q_ref, kvn_ref, kvt_ref, ck_ref, cv_ref, sink_ref, o_ref, nk_ref, nv_ref):
    bb = q_ref.shape[0]
    npair = ATT_HEADS // 2
    q8 = q_ref[...]
    q8r = pltpu.roll(q8, ATT_HEAD_DIM, 2)
    lane = lax.broadcasted_iota(jnp.int32, q8.shape, 2)
    pair = lax.broadcasted_iota(jnp.int32, q8.shape, 1)
    lo = lane < ATT_HEAD_DIM
    kv0 = pair < npair // ATT_KV_HEADS
    own = lo == kv0
    zero = jnp.zeros_like(q8)
    qe = jnp.where(own, jnp.where(kv0, q8, q8r), zero)
    qo = jnp.where(own, jnp.where(kv0, q8r, q8), zero)
    qb = jnp.concatenate([qe, qo], axis=1)

    ck = ck_ref[...]
    cv = cv_ref[...]
    kn = kvn_ref[:, :KV_WIDTH]
    vn = kvn_ref[:, KV_WIDTH:]
    s = lax.dot_general(qb.astype(BF16), ck.astype(BF16), (((2,), (1,)), ((0,), (0,))),
                        preferred_element_type=F32)
    s_new = jnp.sum(qb * kn[:, None, :], axis=-1, keepdims=True)
    sink = sink_ref[...][None, :, :]
    mx = jnp.maximum(jnp.maximum(jnp.max(s, axis=-1, keepdims=True), s_new), sink)
    p = jnp.exp(s - mx)
    p_new = jnp.exp(s_new - mx)
    den = jnp.sum(p, axis=-1, keepdims=True) + p_new + jnp.exp(sink - mx)
    o = lax.dot_general(p.astype(BF16), cv.astype(BF16), (((2,), (2,)), ((0,), (0,))),
                        preferred_element_type=F32)
    o = (o + p_new * vn[:, None, :]) / den
    oe = o[:, :npair, :]
    oo = o[:, npair:, :]
    oer = pltpu.roll(oe, ATT_HEAD_DIM, 2)
    oor = pltpu.roll(oo, ATT_HEAD_DIM, 2)
    o_ref[...] = jnp.where(lo, jnp.where(kv0, oe, oer), jnp.where(kv0, oor, oo)).astype(BF16)

    newcol = _this_steps_columns(kvt_ref, slice(None), bb)
    last = lax.broadcasted_iota(jnp.int32, (KV_WIDTH, WINDOW), 1) == WINDOW - 1
    for jb in range(bb):
        nk_ref[jb] = jnp.where(last, newcol[:KV_WIDTH, jb:jb + 1],
                               pltpu.roll(ck[jb], WINDOW - 1, 1))
        nv_ref[jb] = jnp.where(last, newcol[KV_WIDTH:, jb:jb + 1],
                               pltpu.roll(cv[jb], WINDOW - 1, 1))


def _swa_dec(q8, kvn, kvt, ck, cv, sink_col):
    b = q8.shape[0]
    bb = BB_ATT
    cache = pl.BlockSpec((bb, KV_WIDTH, WINDOW), lambda i: (i, 0, 0))
    return pl.pallas_call(
        _swa_dec_body,
        grid=(b // bb,),
        in_specs=[pl.BlockSpec((bb, ATT_HEADS // 2, LANES), lambda i: (i, 0, 0)),
                  pl.BlockSpec((bb, 2 * KV_WIDTH), lambda i: (i, 0)),
                  _const_spec(kvt.shape), cache, cache, _const_spec((ATT_HEADS, 1))],
        out_specs=[pl.BlockSpec((bb, ATT_HEADS // 2, LANES), lambda i: (i, 0, 0)), cache, cache],
        out_shape=(jax.ShapeDtypeStruct((b, ATT_HEADS // 2, LANES), BF16),
                   jax.ShapeDtypeStruct(ck.shape, F32),
                   jax.ShapeDtypeStruct(cv.shape, F32)),
        compiler_params=_params(),
        name="swa_dec",
    )(q8, kvn, kvt, ck, cv, sink_col)


def _out_proj_body(x_ref, a_ref, r_ref, w_ref, g_ref, x1_ref, h2_ref):
    if len(x_ref.shape) == 3:
        x_ref = x_ref.at[:, 0]
    x1 = (x_ref[...] + _dot(a_ref[...].astype(BF16), w_ref[:ATT_WIDTH, :])
          + _dot(r_ref[...].astype(BF16), w_ref[ATT_WIDTH:, :]))
    x1_ref[...] = x1
    var = jnp.mean(x1 * x1, axis=-1, keepdims=True)
    h2_ref[...] = (x1 * lax.rsqrt(var + EPS) * g_ref[...]).astype(BF16)


def _out_proj(x, a, r, w, g, tm):
    m = x.shape[0]
    row = lambda w: pl.BlockSpec((tm, w), lambda i: (i, 0))
    x_spec = row(D_MODEL) if x.ndim == 2 else pl.BlockSpec((tm, 1, D_MODEL), lambda i: (i, 0, 0))
    return pl.pallas_call(
        _out_proj_body,
        grid=(m // tm,),
        in_specs=[x_spec, row(ATT_WIDTH), row(RET_WIDTH),
                  _const_spec(w.shape), _const_spec((1, D_MODEL))],
        out_specs=[row(D_MODEL), row(D_MODEL)],
        out_shape=(jax.ShapeDtypeStruct((m, D_MODEL), F32),
                   jax.ShapeDtypeStruct((m, D_MODEL), BF16)),
        compiler_params=_params(),
        name="out_proj",
    )(x, a, r, w, g)


def _mlp_acc(h2_ref, wu_ref, wd_ref, o_ref):
    u = jnp.maximum(_dot(h2_ref[...], wu_ref[0]), 0.0)
    o_ref[...] += _dot((u * u).astype(BF16), wd_ref[...])


def _mlp_body(x1_ref, h2_ref, wu_ref, wd_ref, o_ref, acc):
    @pl.when(pl.program_id(1) == 0)
    def _():
        acc[...] = x1_ref[...]

    _mlp_acc(h2_ref, wu_ref, wd_ref, acc)

    @pl.when(pl.program_id(1) == pl.num_programs(1) - 1)
    def _():
        o_ref[:, 0, :] = acc[...]


def _mlp_weight_specs():
    return [pl.BlockSpec((1, D_MODEL, TF_RET), lambda i, f: (f, 0, 0)),
            pl.BlockSpec((TF_RET, D_MODEL), lambda i, f: (f, 0))]


def _mlp(x1, h2, wu, wd, tm):
    m = x1.shape[0]
    row = pl.BlockSpec((tm, D_MODEL), lambda i, f: (i, 0))
    return pl.pallas_call(
        _mlp_body,
        grid=(m // tm, D_FF // TF_RET),
        in_specs=[row, row] + _mlp_weight_specs(),
        out_specs=pl.BlockSpec((tm, 1, D_MODEL), lambda i, f: (i, 0, 0)),
        out_shape=jax.ShapeDtypeStruct((m, 1, D_MODEL), F32),
        scratch_shapes=[pltpu.VMEM((tm, D_MODEL), F32)],
        compiler_params=_params(2),
        name="mlp",
    )(x1, h2, wu, wd)


RING = 3


def _mlp_ret_body(x1_ref, h2_ref, wu_hbm, wd_hbm,
                  qt_ref, kt_ref, rq_ref, rk_ref, rv_ref, gt_ref, g_ref, s_ref,
                  o_ref, r_ref, ns_ref, wu_buf, wd_buf, sem):
    nf = pl.num_programs(1)
    b = pl.program_id(0) * nf + pl.program_id(1)
    nstep = pl.num_programs(0) * nf

    def slab_copies(step):
        slab = lax.rem(step, nf)
        slot = lax.rem(step, RING)
        rows = pl.ds(pl.multiple_of(slab * TF_RET, TF_RET), TF_RET)
        return (pltpu.make_async_copy(wu_hbm.at[slab], wu_buf.at[slot], sem.at[0, slot]),
                pltpu.make_async_copy(wd_hbm.at[rows, :], wd_buf.at[slot], sem.at[1, slot]))

    @pl.when(b == 0)
    def _():
        for step in range(RING - 1):
            for copy in slab_copies(step):
                copy.start()

    @pl.when(b + RING - 1 < nstep)
    def _():
        for copy in slab_copies(b + RING - 1):
            copy.start()

    for copy in slab_copies(b):
        copy.wait()
    slot = lax.rem(b, RING)

    def step(first):
        u = jnp.maximum(_dot(h2_ref[...], wu_buf[slot]), 0.0)
        base = x1_ref[...] if first else o_ref[...]
        o_ref[...] = base + _dot((u * u).astype(BF16), wd_buf[slot])

        row = pl.ds(b, 1)
        shift = lax.rem(LANES - b, LANES)
        for hh in range(RET_HEADS):
            g1 = math.exp(LOG_G[hh])
            cols = slice(RET_DK * hh, RET_DK * (hh + 1))
            qc = pltpu.roll(qt_ref[cols, :], shift, 1)[:, :1] * g1
            kc = pltpu.roll(kt_ref[cols, :], shift, 1)[:, :1]
            v = rv_ref[row, cols]
            s0 = s_ref[0, hh]
            ns_ref[0, hh] = g1 * s0 + kc * v
            qk = jnp.sum(rq_ref[row, cols] * rk_ref[row, cols], axis=-1, keepdims=True)
            o = jnp.sum(qc * s0, axis=0, keepdims=True) + qk * v
            var = jnp.mean(o * o, axis=-1, keepdims=True)
            r_ref[row, cols] = o * lax.rsqrt(var + EPS) * g_ref[:, cols] * gt_ref[row, cols]

    pl.when(pl.program_id(1) == 0)(functools.partial(step, True))
    pl.when(pl.program_id(1) != 0)(functools.partial(step, False))


def _mlp_ret(x1, h2, wu, wd, qt, kt, rq, rk, rv, gt, g, state):
    m = x1.shape[0]
    nb = state.shape[0]
    nf = D_FF // TF_RET
    assert m // TM * nf == nb and nb == LANES
    assert nb >= RING
    row = pl.BlockSpec((TM, D_MODEL), lambda i, f: (i, 0))
    hbm = pl.BlockSpec(memory_space=pl.ANY)
    st = pl.BlockSpec((1, RET_HEADS, RET_DK, RET_DV), lambda i, f: (i * nf + f, 0, 0, 0))
    rows = _const_spec((nb, RET_WIDTH))
    return pl.pallas_call(
        _mlp_ret_body,
        grid=(m // TM, nf),
        in_specs=[row, row, hbm, hbm,
                  _const_spec(qt.shape), _const_spec(kt.shape), rows, rows, rows, rows,
                  _const_spec((1, RET_WIDTH)), st],
        out_specs=[row, pl.BlockSpec((nb, RET_WIDTH), lambda i, f: (0, 0)), st],
        out_shape=(jax.ShapeDtypeStruct((m, D_MODEL), F32),
                   jax.ShapeDtypeStruct((nb, RET_WIDTH), F32),
                   jax.ShapeDtypeStruct(state.shape, F32)),
        scratch_shapes=[pltpu.VMEM((RING, D_MODEL, TF_RET), BF16),
                        pltpu.VMEM((RING, TF_RET, D_MODEL), BF16),
                        pltpu.SemaphoreType.DMA((2, RING))],
        compiler_params=_params(2),
        name="mlp_ret",
    )(x1, h2, wu, wd, qt, kt, rq, rk, rv, gt, g, state)


def kernel(x_prompt, x_sample, cache_k_win, cache_v_win, state_ret, ln1_g, w_in, q_norm_g,
           k_norm_g, attn_sinks, ret_norm_g, w_out, ln2_g, w_up, w_down):
    seq = x_prompt.shape[1]
    nb = x_sample.shape[0]
    assert x_prompt.shape[0] == 1 and x_sample.shape[1] == 1 and w_in.shape[0] == 1
    assert seq % TM == 0 and w_in.shape[2] == SPLIT_A + SPLIT_B

    inv = (ROPE_BASE ** (-jnp.arange(HALF, dtype=F32) / HALF)).reshape(1, HALF)

    ln1 = ln1_g.reshape(1, D_MODEL)
    ln2 = ln2_g.reshape(1, D_MODEL)
    qg = jnp.tile(q_norm_g.reshape(1, ATT_HEAD_DIM) * (ATT_HEAD_DIM ** -0.5), (1, 256 // ATT_HEAD_DIM))
    kg = jnp.tile(k_norm_g.reshape(1, ATT_HEAD_DIM), (1, KV_WIDTH // ATT_HEAD_DIM))
    rg_g = ret_norm_g.reshape(1, RET_WIDTH)
    sinks = attn_sinks.reshape(ATT_HEADS)
    blk = jnp.arange(256) // ATT_HEAD_DIM
    ones = (blk[:, None] == blk[None, :]).astype(BF16)

    wi, qs, kvs, rqs, rks, rvs, rgs, kvt, rqt, rkt = _in_proj(x_sample, ln1, w_in[0], qg, kg, inv,
                                                              ones, pos=PAST_LEN)
    sink_col = jnp.concatenate([sinks[0::2], sinks[1::2]]).reshape(ATT_HEADS, 1)
    to_fm = lambda c: c[0].transpose(0, 2, 3, 1).reshape(nb, KV_WIDTH, WINDOW)
    from_fm = lambda c: c.reshape(nb, ATT_KV_HEADS, ATT_HEAD_DIM, WINDOW).transpose(0, 3, 1, 2)[None]
    a8, nk, nv = _swa_dec(qs, kvs, kvt,
                          to_fm(cache_k_win), to_fm(cache_v_win), sink_col)
    a_s = a8.reshape(nb, ATT_WIDTH)

    xp = x_prompt[0]
    a_out, kv, rq, wo, wu = _proj_swa(xp, ln1, wi, qg, kg, inv, ones, sinks, w_out[0], w_up[0])
    r_out, s_fin, wd = _proj_ret(xp, ln1, wi, inv, rq, rg_g, w_down[0])
    x1, h2 = _out_proj(xp, a_out, r_out, wo, ln2, tm=TM)
    yp, r_s, ns = _mlp_ret(x1, h2, wu, wd, rqt, rkt, rqs, rks, rvs, rgs, rg_g, state_ret[0])

    wb = min(WINDOW, seq)
    kp = kv[seq - wb:, :KV_WIDTH].reshape(1, 1, wb, ATT_KV_HEADS, ATT_HEAD_DIM)
    vp = kv[seq - wb:, KV_WIDTH:].reshape(1, 1, wb, ATT_KV_HEADS, ATT_HEAD_DIM)
    sp = s_fin.reshape(1, 1, RET_HEADS, RET_DK, RET_DV)

    x1s, h2s = _out_proj(x_sample, a_s, r_s, wo, ln2, tm=nb)
    ys = _mlp(x1s, h2s, wu, wd, tm=nb)

    return (yp[None], ys, kp, vp, sp, from_fm(nk), from_fm(nv), ns[None])
```

```python
import functools
import math

import jax
import jax.numpy as jnp
from jax import lax
from jax.experimental import pallas as pl
from jax.experimental.pallas import tpu as pltpu

D_MODEL = 2048
ATT_HEADS = 16
ATT_KV_HEADS = 2
ATT_HEAD_DIM = 64
WINDOW = 128
RET_HEADS = 4
RET_DK = 256
RET_DV = 256
RET_CHUNK = 128
ROPE_BASE = 10000.0
D_FF = 4 * D_MODEL
EPS = 1e-6
PAST_LEN = 8192

ATT_WIDTH = ATT_HEADS * ATT_HEAD_DIM
KV_WIDTH = ATT_KV_HEADS * ATT_HEAD_DIM
RET_WIDTH = RET_HEADS * RET_DK
LANES = 128
HALF = RET_DK // 2
SPLIT_A = ATT_WIDTH + 2 * KV_WIDTH + RET_WIDTH
SPLIT_B = 3 * RET_WIDTH

F32 = jnp.float32
BF16 = jnp.bfloat16
VMEM_LIMIT = 60 * 1024 * 1024

TM = 512
TF_RET = 1024
BB_ATT = 32

LOG_G = tuple(math.log1p(-(2.0 ** (-5.0 - h))) for h in range(RET_HEADS))


def _dot(a, b):
    return jnp.dot(a, b, preferred_element_type=F32)


def _dot_nt(a, b):
    return lax.dot_general(a, b, (((1,), (1,)), ((), ())), preferred_element_type=F32)


def _dot_tn(a, b):
    return lax.dot_general(a, b, (((0,), (0,)), ((), ())), preferred_element_type=F32)


def _const_spec(shape):
    n = len(shape)
    return pl.BlockSpec(shape, lambda *_: (0,) * n, pipeline_mode=pl.Buffered(1))


def _w_in_spec(part):
    width, start = ((SPLIT_A, 0), (SPLIT_B, SPLIT_A))[part]
    return pl.BlockSpec((pl.Element(D_MODEL), pl.Element(width)), lambda *_: (0, start),
                        pipeline_mode=pl.Buffered(1))


def _params(n_axes=1):
    return pltpu.CompilerParams(dimension_semantics=("arbitrary",) * n_axes,
                                vmem_limit_bytes=VMEM_LIMIT)


def _norm_rows(x_ref, g_ref):
    x = x_ref[...]
    var = jnp.mean(x * x, axis=-1, keepdims=True)
    return (x * lax.rsqrt(var + EPS) * g_ref[...]).astype(BF16)


def _bf16_into(o_ref, v):
    return v.astype(BF16).astype(o_ref.dtype)


def _rope_init(inv_ref, cr_scr, sr_scr, pos_step):
    tm = cr_scr.shape[0]
    row = lax.broadcasted_iota(jnp.int32, (tm, 1), 0)
    ang_r = (pos_step * row).astype(F32) * inv_ref[...]
    cr_scr[...] = jnp.cos(ang_r)
    sr_scr[...] = jnp.sin(ang_r)


def _rope_tables(inv_ref, cr_scr, sr_scr, base_pos):
    base = jnp.zeros((8, 1), jnp.int32) + base_pos
    ang_b = base.astype(F32) * inv_ref[...]
    cb = jnp.cos(ang_b)[:1]
    sb = jnp.sin(ang_b)[:1]
    cr = cr_scr[...]
    sr = sr_scr[...]
    return cb * cr - sb * sr, sb * cr + cb * sr


def _rope_head(r, cos, sin, scale, o_ref, hh):
    x1 = r[:, :HALF]
    x2 = r[:, HALF:]
    o1 = x1 * cos - x2 * sin
    o2 = x2 * cos + x1 * sin
    if scale != 1.0:
        o1 = o1 * scale
        o2 = o2 * scale
    o_ref[:, RET_DK * hh:RET_DK * hh + HALF] = _bf16_into(o_ref, o1)
    o_ref[:, RET_DK * hh + HALF:RET_DK * (hh + 1)] = _bf16_into(o_ref, o2)


def _half_split(a):
    lane = lax.broadcasted_iota(jnp.int32, a.shape, 1)
    lo = lane < ATT_HEAD_DIM
    sw = pltpu.roll(a, ATT_HEAD_DIM, 1)
    zero = jnp.zeros_like(a)
    h0 = (jnp.where(lo, a, zero).astype(BF16), jnp.where(lo, zero, sw).astype(BF16))
    h1 = (jnp.where(lo, sw, zero).astype(BF16), jnp.where(lo, zero, a).astype(BF16))
    return h0, h1


SWA_STACKS = 1


def _swa_phases(sink_ref, q_ref, kvc_ref, kvp_ref, o_ref, has_prev):
    nsub = q_ref.shape[0] // WINDOW
    npair = ATT_HEADS // 2
    row = lax.broadcasted_iota(jnp.int32, (WINDOW, 2 * WINDOW), 0)
    col = lax.broadcasted_iota(jnp.int32, (WINDOW, 2 * WINDOW), 1)
    band_cur = (col >= WINDOW) & (col - WINDOW <= row)
    first_col = jnp.where(has_prev, 0, WINDOW)
    lane_lo = lax.broadcasted_iota(jnp.int32, (WINDOW, LANES), 1) < ATT_HEAD_DIM
    ctx = [{} for _ in range(nsub)]
    st = {}

    ppk = npair // ATT_KV_HEADS

    def stacks(g):
        for m in range(g * SWA_STACKS, (g + 1) * SWA_STACKS):
            j, kh = divmod(m, ATT_KV_HEADS)
            yield j, kh, slice(WINDOW * j, WINDOW * (j + 1)), range(ppk * kh, ppk * (kh + 1))

    def prep(g):
        for j in sorted({j for j, _, _, _ in stacks(g)} - {j for j in range(nsub) if ctx[j]}):
            prev = kvp_ref[...] if j == 0 else kvc_ref[WINDOW * (j - 1):WINDOW * j, :]
            cur = kvc_ref[WINDOW * j:WINDOW * (j + 1), :]
            kk = jnp.concatenate([prev[:, :KV_WIDTH], cur[:, :KV_WIDTH]], axis=0)
            vv = jnp.concatenate([prev[:, KV_WIDTH:], cur[:, KV_WIDTH:]], axis=0)
            ctx[j]["k"] = _half_split(kk)
            ctx[j]["v"] = _half_split(vv)
            lo_col = first_col if j == 0 else 0
            ctx[j]["mask"] = band_cur | ((col < WINDOW) & (col >= row) & (col >= lo_col))

    def qk(g):
        for j, kh, rows, tiles in stacks(g):
            qs = jnp.concatenate([q_ref[rows, LANES * t:LANES * (t + 1)] for t in tiles], axis=0)
            s = [_dot_nt(qs, ctx[j]["k"][kh][par]) for par in range(2)]
            for n, t in enumerate(tiles):
                st[j, t] = {"s": [sp[WINDOW * n:WINDOW * (n + 1), :] for sp in s]}

    def softmax(g):
        for j, kh, rows, tiles in stacks(g):
            for t in tiles:
                p, inv = [], []
                for par in range(2):
                    sink = sink_ref[2 * t + par]
                    s = jnp.where(ctx[j]["mask"], st[j, t]["s"][par], -jnp.inf)
                    mx = jnp.maximum(jnp.max(s, axis=-1, keepdims=True), sink)
                    e = jnp.exp(s - mx)
                    den = jnp.sum(e, axis=-1, keepdims=True) + jnp.exp(sink - mx)
                    p.append(e.astype(BF16))
                    inv.append(1.0 / den)
                st[j, t] = {"p": p, "inv": jnp.where(lane_lo, inv[0], inv[1])}

    def pv(g):
        for j, kh, rows, tiles in stacks(g):
            got = [st.pop((j, t)) for t in tiles]
            acc = sum(_dot(jnp.concatenate([u["p"][par] for u in got], axis=0),
                           ctx[j]["v"][kh][par]) for par in range(2))
            for n, t in enumerate(tiles):
                o_ref[rows, LANES * t:LANES * (t + 1)] = (
                    acc[WINDOW * n:WINDOW * (n + 1), :] * got[n]["inv"]).astype(BF16)

    ngroup = nsub * ATT_KV_HEADS // SWA_STACKS
    return ngroup, prep, qk, softmax, pv


def _emit_pipelined(main, ngroup, prep, first, middle, last, finish=None):
    nstage = ngroup + 2
    done = 0
    prep(0)
    for k in range(nstage):
        if k < ngroup:
            first(k)
        if 0 <= k - 2 < ngroup:
            last(k - 2)
        if 0 <= k - 1 < ngroup:
            middle(k - 1)
        if k + 1 < ngroup:
            prep(k + 1)
        upto = min(len(main), k + 1) if k + 1 < nstage else len(main)
        for piece in main[done:upto]:
            piece()
        done = upto
    if finish is not None:
        finish()


def _ret_phases(rq_ref, rk_ref, rv_ref, gt_ref, g_ref, o_ref, s_scr):
    c = RET_CHUNK
    nsub = rq_ref.shape[0] // c
    ri = lax.broadcasted_iota(jnp.int32, (c, c), 0)
    ci = lax.broadcasted_iota(jnp.int32, (c, c), 1)
    rel = (ri - ci).astype(F32)
    idx = lax.broadcasted_iota(jnp.int32, (c, 1), 0).astype(F32)
    head = []
    for hh in range(RET_HEADS):
        lg = LOG_G[hh]
        head.append(dict(
            dmask=jnp.where(rel >= 0, jnp.exp(lg * jnp.maximum(rel, 0.0)), 0.0),
            qdec=jnp.exp(lg * (idx + 1.0)), kdec=jnp.exp(lg * (c - 1.0 - idx)),
            cdec=math.exp(lg * c), cols=slice(RET_DK * hh, RET_DK * (hh + 1))))
    st = {}

    def update_state(g):
        for hh, hd in enumerate(head):
            u = st[g, hh]
            s_scr[hh] = hd["cdec"] * u.pop("s_prev") + u.pop("kv")

    def prep(g):
        if g > 0:
            update_state(g - 1)
        rows = slice(c * g, c * (g + 1))
        for hh, hd in enumerate(head):
            q = rq_ref[rows, hd["cols"]]
            k = rk_ref[rows, hd["cols"]]
            s_prev = s_scr[hh]
            st[g, hh] = dict(q=q, k=k, v=rv_ref[rows, hd["cols"]], s_prev=s_prev,
                             qd=(q.astype(F32) * hd["qdec"]).astype(BF16),
                             kd=(k.astype(F32) * hd["kdec"]).astype(BF16),
                             s_bf=s_prev.astype(BF16))

    def first(g):
        for hh in range(RET_HEADS):
            u = st[g, hh]
            u["att"] = _dot_nt(u.pop("q"), u.pop("k"))
            u["inter"] = _dot(u.pop("qd"), u.pop("s_bf"))
            u["kv"] = _dot_tn(u.pop("kd"), u["v"])

    def middle(g):
        for hh, hd in enumerate(head):
            u = st[g, hh]
            u["att"] = (u["att"] * hd["dmask"]).astype(BF16)

    def last(g):
        rows = slice(c * g, c * (g + 1))
        for hh, hd in enumerate(head):
            u = st[g, hh]
            o = _dot(u.pop("att"), u.pop("v")) + u.pop("inter")
            var = jnp.mean(o * o, axis=-1, keepdims=True)
            on = o * lax.rsqrt(var + EPS) * g_ref[:, hd["cols"]]
            o_ref[rows, hd["cols"]] = (on * gt_ref[rows, hd["cols"]].astype(F32)).astype(BF16)

    return nsub, prep, first, middle, last, functools.partial(update_state, nsub - 1)


def _proj_swa_body(sink_ref, x_ref, g_ref, w_ref, qg_ref, kg_ref, inv_ref, ones_ref,
                   wo_ref, wu_ref,
                   a_out, kv_out, rq_out, wo_bf, wu_bf,
                   q_scr, kv_scr, kvp_scr, cr_scr, sr_scr, *, nblk):
    i = pl.program_id(0)
    tm = x_ref.shape[0]

    @pl.when(i == 0)
    def _():
        _rope_init(inv_ref, cr_scr, sr_scr, 1)
        q_scr[...] = jnp.zeros_like(q_scr)
        kv_scr[...] = jnp.zeros_like(kv_scr)
        kvp_scr[...] = jnp.zeros_like(kvp_scr)

    wo_bf[...] = wo_ref[...].astype(BF16)
    wu_bf[0] = wu_ref[...].astype(BF16)

    cur = lax.rem(i, 2)
    prv = 1 - cur
    blk = jnp.minimum(i, nblk - 1)

    h = _norm_rows(x_ref, g_ref)
    ones = ones_ref[...]
    inv_hd = 1.0 / ATT_HEAD_DIM
    cos, sin = _rope_tables(inv_ref, cr_scr, sr_scr, blk * tm)


    def q_piece(n):
        y = _dot(h, w_ref[:, 512 * n:512 * (n + 1)])
        for t in range(2):
            yt = y[:, 256 * t:256 * (t + 1)]
            ssq = _dot((yt * yt).astype(BF16), ones)
            c0 = 512 * n + 256 * t
            q_scr[cur, :, c0:c0 + 256] = (
                yt * lax.rsqrt(ssq * inv_hd + EPS) * qg_ref[...]).astype(BF16)

    kv0 = ATT_WIDTH
    rq0 = ATT_WIDTH + 2 * KV_WIDTH

    def kv_rq_piece():
        y = _dot(h, w_ref[:, kv0:rq0 + RET_DK])
        k = y[:, :KV_WIDTH]
        ssq = _dot((k * k).astype(BF16), ones[:KV_WIDTH, :KV_WIDTH])
        kn = k * lax.rsqrt(ssq * inv_hd + EPS) * kg_ref[...]
        v = y[:, KV_WIDTH:2 * KV_WIDTH]
        kv_scr[cur, :, :KV_WIDTH] = kn
        kv_scr[cur, :, KV_WIDTH:] = v
        kv_out[:, :KV_WIDTH] = kn
        kv_out[:, KV_WIDTH:] = v
        _rope_head(y[:, 2 * KV_WIDTH:], cos, sin, 1.0, rq_out, 0)

    def rq_piece():
        y = _dot(h, w_ref[:, rq0 + RET_DK:])
        for hh in range(1, RET_HEADS):
            _rope_head(y[:, RET_DK * (hh - 1):RET_DK * hh], cos, sin, 1.0, rq_out, hh)

    main = [functools.partial(q_piece, 0), functools.partial(q_piece, 1), kv_rq_piece, rq_piece]
    _emit_pipelined(main, *_swa_phases(sink_ref, q_scr.at[prv], kv_scr.at[prv], kvp_scr, a_out,
                                       has_prev=i > 1))
    kvp_scr[...] = kv_scr[prv, tm - WINDOW:, :]


def _proj_swa(x, ln_g, w_in, qg, kg, inv, ones, sinks, w_out, w_up):
    m = x.shape[0]
    nblk = m // TM
    cl = lambda i: jnp.minimum(i, nblk - 1)
    row = lambda w: pl.BlockSpec((TM, w), lambda i: (cl(i), 0))
    wo_spec = pl.BlockSpec((w_out.shape[0] // nblk, w_out.shape[1]), lambda i: (cl(i), 0))
    wcast = w_up.shape[1] // nblk
    per_slab = TF_RET // wcast
    assert w_up.shape[1] % nblk == 0 and TF_RET % wcast == 0
    wu_spec = pl.BlockSpec((w_up.shape[0], wcast), lambda i: (0, cl(i)))
    wu_out_spec = pl.BlockSpec((1, w_up.shape[0], wcast),
                               lambda i: (cl(i) // per_slab, 0, cl(i) % per_slab))
    return pl.pallas_call(
        functools.partial(_proj_swa_body, nblk=nblk),
        grid=(nblk + 1,),
        in_specs=[pl.BlockSpec(memory_space=pltpu.SMEM),
                  row(D_MODEL), _const_spec((1, D_MODEL)), _w_in_spec(0),
                  _const_spec((1, 256)), _const_spec((1, KV_WIDTH)),
                  _const_spec((1, HALF)), _const_spec((256, 256)), wo_spec, wu_spec],
        out_specs=[pl.BlockSpec((TM, ATT_WIDTH), lambda i: (jnp.maximum(i - 1, 0), 0)),
                   row(2 * KV_WIDTH), row(RET_WIDTH), wo_spec, wu_out_spec],
        out_shape=(jax.ShapeDtypeStruct((m, ATT_WIDTH), BF16),
                   jax.ShapeDtypeStruct((m, 2 * KV_WIDTH), F32),
                   jax.ShapeDtypeStruct((m, RET_WIDTH), BF16),
                   jax.ShapeDtypeStruct(w_out.shape, BF16),
                   jax.ShapeDtypeStruct((w_up.shape[1] // TF_RET, w_up.shape[0], TF_RET), BF16)),
        scratch_shapes=[pltpu.VMEM((2, TM, ATT_WIDTH), BF16),
                        pltpu.VMEM((2, TM, 2 * KV_WIDTH), F32),
                        pltpu.VMEM((WINDOW, 2 * KV_WIDTH), F32),
                        pltpu.VMEM((TM, HALF), F32), pltpu.VMEM((TM, HALF), F32)],
        compiler_params=_params(),
        name="proj_swa",
    )(sinks, x, ln_g, w_in, qg, kg, inv, ones, w_out, w_up)


def _proj_ret_body(x_ref, g_ref, w_ref, inv_ref, rq_ref, rg_ref, wd_ref,
                   r_out, s_out, wd_bf,
                   rk_scr, rv_scr, gt_scr, s_scr, cr_scr, sr_scr, *, nblk):
    i = pl.program_id(0)
    tm = x_ref.shape[0]

    @pl.when(i == 0)
    def _():
        _rope_init(inv_ref, cr_scr, sr_scr, 1)
        rk_scr[...] = jnp.zeros_like(rk_scr)
        rv_scr[...] = jnp.zeros_like(rv_scr)
        gt_scr[...] = jnp.zeros_like(gt_scr)
        s_scr[...] = jnp.zeros_like(s_scr)

    wd_bf[...] = wd_ref[...].astype(BF16)

    cur = lax.rem(i, 2)
    prv = 1 - cur
    blk = jnp.minimum(i, nblk - 1)

    h = _norm_rows(x_ref, g_ref)
    cos, sin = _rope_tables(inv_ref, cr_scr, sr_scr, blk * tm)

    def rk_piece(n):
        y = _dot(h, w_ref[:, 512 * n:512 * (n + 1)])
        for t in range(2):
            _rope_head(y[:, RET_DK * t:RET_DK * (t + 1)], cos, sin, RET_DK ** -0.5,
                       rk_scr.at[cur], 2 * n + t)

    def rv_piece(n):
        c = slice(512 * n, 512 * (n + 1))
        rv_scr[cur, :, c] = _dot(h, w_ref[:, RET_WIDTH + 512 * n:RET_WIDTH + 512 * (n + 1)]
                                 ).astype(BF16)

    def gate_piece(n):
        c0 = 2 * RET_WIDTH + 512 * n
        rg = _dot(h, w_ref[:, c0:c0 + 512])
        gt_scr[cur, :, 512 * n:512 * (n + 1)] = (rg / (1.0 + jnp.exp(-rg))).astype(BF16)

    main = [lambda f=f: (f(0), f(1)) for f in (rk_piece, rv_piece, gate_piece)]
    _emit_pipelined(main, *_ret_phases(rq_ref, rk_scr.at[prv], rv_scr.at[prv], gt_scr.at[prv],
                                       rg_ref, r_out, s_scr))

    @pl.when(i == nblk)
    def _():
        s_out[...] = s_scr[...]


def _proj_ret(x, ln_g, w_in, inv, rq, rg_g, w_down):
    m = x.shape[0]
    nblk = m // TM
    cl = lambda i: jnp.minimum(i, nblk - 1)
    prev = lambda w: pl.BlockSpec((TM, w), lambda i: (jnp.maximum(i - 1, 0), 0))
    wd_spec = pl.BlockSpec((w_down.shape[0] // nblk, w_down.shape[1]), lambda i: (cl(i), 0))
    state = (RET_HEADS, RET_DK, RET_DV)
    slot = pltpu.VMEM((2, TM, RET_WIDTH), BF16)
    return pl.pallas_call(
        functools.partial(_proj_ret_body, nblk=nblk),
        grid=(nblk + 1,),
        in_specs=[pl.BlockSpec((TM, D_MODEL), lambda i: (cl(i), 0)),
                  _const_spec((1, D_MODEL)), _w_in_spec(1), _const_spec((1, HALF)),
                  prev(RET_WIDTH), _const_spec((1, RET_WIDTH)), wd_spec],
        out_specs=[prev(RET_WIDTH), pl.BlockSpec(state, lambda i: (0, 0, 0)), wd_spec],
        out_shape=(jax.ShapeDtypeStruct((m, RET_WIDTH), BF16),
                   jax.ShapeDtypeStruct(state, F32),
                   jax.ShapeDtypeStruct(w_down.shape, BF16)),
        scratch_shapes=[slot, slot, slot, pltpu.VMEM(state, F32),
                        pltpu.VMEM((TM, HALF), F32), pltpu.VMEM((TM, HALF), F32)],
        compiler_params=_params(),
        name="proj_ret",
    )(x, ln_g, w_in, inv, rq, rg_g, w_down)


TN_IN = 1792
Z_TILE = 256


def _in_proj_body(x_ref, g_ref, w_ref, qg_ref, kg_ref, inv_ref, ones_ref,
                  w_bf, q_out, kv_out, rq_out, rk_out, rv_out, gt_out, kvt_out, rqt_out, rkt_out,
                  h_scr, z_scr, *, pos):
    c = pl.program_id(0)

    @pl.when(c == 0)
    def _():
        h_scr[...] = _norm_rows(x_ref.at[:, 0], g_ref)

    wb = w_ref[...].astype(BF16)
    w_bf[...] = wb
    z = _dot(h_scr[...], wb)
    per_step = TN_IN // Z_TILE
    for t in range(per_step):
        z_scr[c * per_step + t] = z[:, Z_TILE * t:Z_TILE * (t + 1)]

    @pl.when(c == pl.num_programs(0) - 1)
    def _():
        tiles = lambda col0, n: [z_scr[col0 // Z_TILE + t] for t in range(n)]
        ones = ones_ref[...]
        inv_hd = 1.0 / ATT_HEAD_DIM
        for t, y in enumerate(tiles(0, ATT_WIDTH // Z_TILE)):
            ssq = _dot((y * y).astype(BF16), ones)
            qn = _bf16_into(q_out, y * lax.rsqrt(ssq * inv_hd + EPS) * qg_ref[...])
            for half in range(2):
                q_out[:, 2 * t + half, :] = qn[:, LANES * half:LANES * (half + 1)]
        (kvr,) = tiles(ATT_WIDTH, 1)
        k = kvr[:, :KV_WIDTH]
        ssq = _dot((k * k).astype(BF16), ones[:KV_WIDTH, :KV_WIDTH])
        kv_out[:, :KV_WIDTH] = k * lax.rsqrt(ssq * inv_hd + EPS) * kg_ref[...]
        kv_out[:, KV_WIDTH:] = kvr[:, KV_WIDTH:]
        ang = jnp.full((8, 1), pos, jnp.int32).astype(F32) * inv_ref[...]
        cos = jnp.cos(ang)[:1]
        sin = jnp.sin(ang)[:1]
        rq0 = ATT_WIDTH + 2 * KV_WIDTH
        for hh, y in enumerate(tiles(rq0, RET_HEADS)):
            _rope_head(y, cos, sin, 1.0, rq_out, hh)
        for hh, y in enumerate(tiles(rq0 + RET_WIDTH, RET_HEADS)):
            _rope_head(y, cos, sin, RET_DK ** -0.5, rk_out, hh)
        for hh, y in enumerate(tiles(rq0 + 2 * RET_WIDTH, RET_HEADS)):
            rv_out[:, RET_DV * hh:RET_DV * (hh + 1)] = _bf16_into(rv_out, y)
        for hh, y in enumerate(tiles(rq0 + 3 * RET_WIDTH, RET_HEADS)):
            gt_out[:, RET_DV * hh:RET_DV * (hh + 1)] = _bf16_into(gt_out, y / (1.0 + jnp.exp(-y)))
        kvt_out[...] = kv_out[...].T
        rqt_out[...] = rq_out[...].T
        rkt_out[...] = rk_out[...].T


def _in_proj(x, ln_g, w_in, qg, kg, inv, ones, pos):
    m = x.shape[0]
    assert m == LANES
    assert RET_DK == Z_TILE and TN_IN % Z_TILE == 0 and w_in.shape[1] % TN_IN == 0
    nstep = w_in.shape[1] // TN_IN
    full = lambda w: pl.BlockSpec((m, w), lambda c: (0, 0))
    colm = lambda w: pl.BlockSpec((w, m), lambda c: (0, 0))
    wcol = pl.BlockSpec((D_MODEL, TN_IN), lambda c: (0, c))
    ret = jax.ShapeDtypeStruct((m, RET_WIDTH), F32)
    ret_t = jax.ShapeDtypeStruct((RET_WIDTH, m), F32)
    return pl.pallas_call(
        functools.partial(_in_proj_body, pos=pos),
        grid=(nstep,),
        in_specs=[_const_spec((m, 1, D_MODEL)), _const_spec((1, D_MODEL)), wcol,
                  _const_spec((1, 256)), _const_spec((1, KV_WIDTH)),
                  _const_spec((1, HALF)), _const_spec((256, 256))],
        out_specs=[wcol, pl.BlockSpec((m, ATT_HEADS // 2, LANES), lambda c: (0, 0, 0)),
                   full(2 * KV_WIDTH)] + [full(RET_WIDTH)] * 4
        + [colm(2 * KV_WIDTH), colm(RET_WIDTH), colm(RET_WIDTH)],
        out_shape=(jax.ShapeDtypeStruct(w_in.shape, BF16),
                   jax.ShapeDtypeStruct((m, ATT_HEADS // 2, LANES), F32),
                   jax.ShapeDtypeStruct((m, 2 * KV_WIDTH), F32), ret, ret, ret, ret,
                   jax.ShapeDtypeStruct((2 * KV_WIDTH, m), F32), ret_t, ret_t),
        scratch_shapes=[pltpu.VMEM((m, D_MODEL), BF16),
                        pltpu.VMEM((w_in.shape[1] // Z_TILE, m, Z_TILE), F32)],
        compiler_params=_params(),
        name="in_proj",
    )(x, ln_g, w_in, qg, kg, inv, ones)


def _this_steps_columns(t_ref, rows, bb):
    shift = lax.rem(LANES - bb * pl.program_id(0), LANES)
    return pltpu.roll(t_ref[rows, :], shift, 1)


def _swa_dec_body(q_ref, kvn_ref, kvt_ref, ck_ref, cv_ref, sink_ref, o_ref, nk_ref, nv_ref):
    bb = q_ref.shape[0]
    npair = ATT_HEADS // 2
    q8 = q_ref[...]
    q8r = pltpu.roll(q8, ATT_HEAD_DIM, 2)
    lane = lax.broadcasted_iota(jnp.int32, q8.shape, 2)
    pair = lax.broadcasted_iota(jnp.int32, q8.shape, 1)
    lo = lane < ATT_HEAD_DIM
    kv0 = pair < npair // ATT_KV_HEADS
    own = lo == kv0
    zero = jnp.zeros_like(q8)
    qe = jnp.where(own, jnp.where(kv0, q8, q8r), zero)
    qo = jnp.where(own, jnp.where(kv0, q8r, q8), zero)
    qb = jnp.concatenate([qe, qo], axis=1)

    ck = ck_ref[...]
    cv = cv_ref[...]
    kn = kvn_ref[:, :KV_WIDTH]
    vn = kvn_ref[:, KV_WIDTH:]
    s = lax.dot_general(qb.astype(BF16), ck.astype(BF16), (((2,), (1,)), ((0,), (0,))),
                        preferred_element_type=F32)
    s_new = jnp.sum(qb * kn[:, None, :], axis=-1, keepdims=True)
    sink = sink_ref[...][None, :, :]
    mx = jnp.maximum(jnp.maximum(jnp.max(s, axis=-1, keepdims=True), s_new), sink)
    p = jnp.exp(s - mx)
    p_new = jnp.exp(s_new - mx)
    den = jnp.sum(p, axis=-1, keepdims=True) + p_new + jnp.exp(sink - mx)
    o = lax.dot_general(p.astype(BF16), cv.astype(BF16), (((2,), (2,)), ((0,), (0,))),
                        preferred_element_type=F32)
    o = (o + p_new * vn[:, None, :]) / den
    oe = o[:, :npair, :]
    oo = o[:, npair:, :]
    oer = pltpu.roll(oe, ATT_HEAD_DIM, 2)
    oor = pltpu.roll(oo, ATT_HEAD_DIM, 2)
    o_ref[...] = jnp.where(lo, jnp.where(kv0, oe, oer), jnp.where(kv0, oor, oo)).astype(BF16)

    newcol = _this_steps_columns(kvt_ref, slice(None), bb)
    last = lax.broadcasted_iota(jnp.int32, (KV_WIDTH, WINDOW), 1) == WINDOW - 1
    for jb in range(bb):
        nk_ref[jb] = jnp.where(last, newcol[:KV_WIDTH, jb:jb + 1],
                               pltpu.roll(ck[jb], WINDOW - 1, 1))
        nv_ref[jb] = jnp.where(last, newcol[KV_WIDTH:, jb:jb + 1],
                               pltpu.roll(cv[jb], WINDOW - 1, 1))


def _swa_dec(q8, kvn, kvt, ck, cv, sink_col):
    b = q8.shape[0]
    bb = BB_ATT
    cache = pl.BlockSpec((bb, KV_WIDTH, WINDOW), lambda i: (i, 0, 0))
    return pl.pallas_call(
        _swa_dec_body,
        grid=(b // bb,),
        in_specs=[pl.BlockSpec((bb, ATT_HEADS // 2, LANES), lambda i: (i, 0, 0)),
                  pl.BlockSpec((bb, 2 * KV_WIDTH), lambda i: (i, 0)),
                  _const_spec(kvt.shape), cache, cache, _const_spec((ATT_HEADS, 1))],
        out_specs=[pl.BlockSpec((bb, ATT_HEADS // 2, LANES), lambda i: (i, 0, 0)), cache, cache],
        out_shape=(jax.ShapeDtypeStruct((b, ATT_HEADS // 2, LANES), BF16),
                   jax.ShapeDtypeStruct(ck.shape, F32),
                   jax.ShapeDtypeStruct(cv.shape, F32)),
        compiler_params=_params(),
        name="swa_dec",
    )(q8, kvn, kvt, ck, cv, sink_col)


def _out_proj_body(x_ref, a_ref, r_ref, w_ref, g_ref, x1_ref, h2_ref):
    if len(x_ref.shape) == 3:
        x_ref = x_ref.at[:, 0]
    x1 = (x_ref[...] + _dot(a_ref[...].astype(BF16), w_ref[:ATT_WIDTH, :])
          + _dot(r_ref[...].astype(BF16), w_ref[ATT_WIDTH:, :]))
    x1_ref[...] = x1
    var = jnp.mean(x1 * x1, axis=-1, keepdims=True)
    h2_ref[...] = (x1 * lax.rsqrt(var + EPS) * g_ref[...]).astype(BF16)


def _out_proj(x, a, r, w, g, tm):
    m = x.shape[0]
    row = lambda w: pl.BlockSpec((tm, w), lambda i: (i, 0))
    return pl.pallas_call(
        _out_proj_body,
        grid=(m // tm,),
        in_specs=[row(D_MODEL), row(ATT_WIDTH), row(RET_WIDTH),
                  _const_spec(w.shape), _const_spec((1, D_MODEL))],
        out_specs=[row(D_MODEL), row(D_MODEL)],
        out_shape=(jax.ShapeDtypeStruct((m, D_MODEL), F32),
                   jax.ShapeDtypeStruct((m, D_MODEL), BF16)),
        compiler_params=_params(),
        name="out_proj",
    )(x, a, r, w, g)


def _mlp_acc(h2_ref, wu_ref, wd_ref, o_ref):
    u = jnp.maximum(_dot(h2_ref[...], wu_ref[0]), 0.0)
    o_ref[...] += _dot((u * u).astype(BF16), wd_ref[...])


def _dec_tail_body(x_ref, a_ref, r_ref, wo_ref, g2_ref, wu_ref, wd_ref, o_ref, acc, h2_scr):
    @pl.when(pl.program_id(0) == 0)
    def _():
        _out_proj_body(x_ref, a_ref, r_ref, wo_ref, g2_ref, acc, h2_scr)

    _mlp_acc(h2_scr, wu_ref, wd_ref, acc)

    @pl.when(pl.program_id(0) == pl.num_programs(0) - 1)
    def _():
        o_ref[:, 0, :] = acc[...]


def _dec_tail(x, a, r, wo, g2, wu, wd):
    m = x.shape[0]
    full = lambda s: pl.BlockSpec(s, lambda f: (0,) * len(s), pipeline_mode=pl.Buffered(1))
    return pl.pallas_call(
        _dec_tail_body,
        grid=(D_FF // TF_RET,),
        in_specs=[full((m, 1, D_MODEL)), full((m, ATT_WIDTH)), full((m, RET_WIDTH)),
                  full(wo.shape), full((1, D_MODEL)),
                  pl.BlockSpec((1, D_MODEL, TF_RET), lambda f: (f, 0, 0)),
                  pl.BlockSpec((TF_RET, D_MODEL), lambda f: (f, 0))],
        out_specs=pl.BlockSpec((m, 1, D_MODEL), lambda f: (0, 0, 0)),
        out_shape=jax.ShapeDtypeStruct((m, 1, D_MODEL), F32),
        scratch_shapes=[pltpu.VMEM((m, D_MODEL), F32), pltpu.VMEM((m, D_MODEL), BF16)],
        compiler_params=_params(),
        name="dec_tail",
    )(x, a, r, wo, g2, wu, wd)


RING = 3


def _mlp_ret_body(x1_ref, h2_ref, wu_hbm, wd_hbm,
                  qt_ref, kt_ref, rq_ref, rk_ref, rv_ref, gt_ref, g_ref, s_ref,
                  o_ref, r_ref, ns_ref, wu_buf, wd_buf, sem):
    nf = pl.num_programs(1)
    b = pl.program_id(0) * nf + pl.program_id(1)
    nstep = pl.num_programs(0) * nf

    def slab_copies(step):
        slab = lax.rem(step, nf)
        slot = lax.rem(step, RING)
        rows = pl.ds(pl.multiple_of(slab * TF_RET, TF_RET), TF_RET)
        return (pltpu.make_async_copy(wu_hbm.at[slab], wu_buf.at[slot], sem.at[0, slot]),
                pltpu.make_async_copy(wd_hbm.at[rows, :], wd_buf.at[slot], sem.at[1, slot]))

    @pl.when(b == 0)
    def _():
        for step in range(RING - 1):
            for copy in slab_copies(step):
                copy.start()

    @pl.when(b + RING - 1 < nstep)
    def _():
        for copy in slab_copies(b + RING - 1):
            copy.start()

    for copy in slab_copies(b):
        copy.wait()
    slot = lax.rem(b, RING)

    def step(first):
        u = jnp.maximum(_dot(h2_ref[...], wu_buf[slot]), 0.0)
        base = x1_ref[...] if first else o_ref[...]
        o_ref[...] = base + _dot((u * u).astype(BF16), wd_buf[slot])

        row = pl.ds(b, 1)
        shift = lax.rem(LANES - b, LANES)
        for hh in range(RET_HEADS):
            g1 = math.exp(LOG_G[hh])
            cols = slice(RET_DK * hh, RET_DK * (hh + 1))
            qc = pltpu.roll(qt_ref[cols, :], shift, 1)[:, :1] * g1
            kc = pltpu.roll(kt_ref[cols, :], shift, 1)[:, :1]
            v = rv_ref[row, cols]
            s0 = s_ref[0, hh]
            ns_ref[0, hh] = g1 * s0 + kc * v
            qk = jnp.sum(rq_ref[row, cols] * rk_ref[row, cols], axis=-1, keepdims=True)
            o = jnp.sum(qc * s0, axis=0, keepdims=True) + qk * v
            var = jnp.mean(o * o, axis=-1, keepdims=True)
            r_ref[row, cols] = o * lax.rsqrt(var + EPS) * g_ref[:, cols] * gt_ref[row, cols]

    pl.when(pl.program_id(1) == 0)(functools.partial(step, True))
    pl.when(pl.program_id(1) != 0)(functools.partial(step, False))


def _mlp_ret(x1, h2, wu, wd, qt, kt, rq, rk, rv, gt, g, state):
    m = x1.shape[0]
    nb = state.shape[0]
    nf = D_FF // TF_RET
    assert m // TM * nf == nb and nb == LANES
    assert nb >= RING
    row = pl.BlockSpec((TM, D_MODEL), lambda i, f: (i, 0))
    hbm = pl.BlockSpec(memory_space=pl.ANY)
    st = pl.BlockSpec((1, RET_HEADS, RET_DK, RET_DV), lambda i, f: (i * nf + f, 0, 0, 0))
    rows = _const_spec((nb, RET_WIDTH))
    return pl.pallas_call(
        _mlp_ret_body,
        grid=(m // TM, nf),
        in_specs=[row, row, hbm, hbm,
                  _const_spec(qt.shape), _const_spec(kt.shape), rows, rows, rows, rows,
                  _const_spec((1, RET_WIDTH)), st],
        out_specs=[row, pl.BlockSpec((nb, RET_WIDTH), lambda i, f: (0, 0)), st],
        out_shape=(jax.ShapeDtypeStruct((m, D_MODEL), F32),
                   jax.ShapeDtypeStruct((nb, RET_WIDTH), F32),
                   jax.ShapeDtypeStruct(state.shape, F32)),
        scratch_shapes=[pltpu.VMEM((RING, D_MODEL, TF_RET), BF16),
                        pltpu.VMEM((RING, TF_RET, D_MODEL), BF16),
                        pltpu.SemaphoreType.DMA((2, RING))],
        compiler_params=_params(2),
        name="mlp_ret",
    )(x1, h2, wu, wd, qt, kt, rq, rk, rv, gt, g, state)


def kernel(x_prompt, x_sample, cache_k_win, cache_v_win, state_ret, ln1_g, w_in, q_norm_g,
           k_norm_g, attn_sinks, ret_norm_g, w_out, ln2_g, w_up, w_down):
    seq = x_prompt.shape[1]
    nb = x_sample.shape[0]
    assert x_prompt.shape[0] == 1 and x_sample.shape[1] == 1 and w_in.shape[0] == 1
    assert seq % TM == 0 and w_in.shape[2] == SPLIT_A + SPLIT_B

    inv = (ROPE_BASE ** (-jnp.arange(HALF, dtype=F32) / HALF)).reshape(1, HALF)

    ln1 = ln1_g.reshape(1, D_MODEL)
    ln2 = ln2_g.reshape(1, D_MODEL)
    qg = jnp.tile(q_norm_g.reshape(1, ATT_HEAD_DIM) * (ATT_HEAD_DIM ** -0.5), (1, 256 // ATT_HEAD_DIM))
    kg = jnp.tile(k_norm_g.reshape(1, ATT_HEAD_DIM), (1, KV_WIDTH // ATT_HEAD_DIM))
    rg_g = ret_norm_g.reshape(1, RET_WIDTH)
    sinks = attn_sinks.reshape(ATT_HEADS)
    blk = jnp.arange(256) // ATT_HEAD_DIM
    ones = (blk[:, None] == blk[None, :]).astype(BF16)

    wi, qs, kvs, rqs, rks, rvs, rgs, kvt, rqt, rkt = _in_proj(x_sample, ln1, w_in[0], qg, kg, inv,
                                                              ones, pos=PAST_LEN)
    sink_col = jnp.concatenate([sinks[0::2], sinks[1::2]]).reshape(ATT_HEADS, 1)
    to_fm = lambda c: c[0].transpose(0, 2, 3, 1).reshape(nb, KV_WIDTH, WINDOW)
    from_fm = lambda c: c.reshape(nb, ATT_KV_HEADS, ATT_HEAD_DIM, WINDOW).transpose(0, 3, 1, 2)[None]
    a8, nk, nv = _swa_dec(qs, kvs, kvt,
                          to_fm(cache_k_win), to_fm(cache_v_win), sink_col)
    a_s = a8.reshape(nb, ATT_WIDTH)

    xp = x_prompt[0]
    a_out, kv, rq, wo, wu = _proj_swa(xp, ln1, wi, qg, kg, inv, ones, sinks, w_out[0], w_up[0])
    r_out, s_fin, wd = _proj_ret(xp, ln1, wi, inv, rq, rg_g, w_down[0])
    x1, h2 = _out_proj(xp, a_out, r_out, wo, ln2, tm=TM)
    yp, r_s, ns = _mlp_ret(x1, h2, wu, wd, rqt, rkt, rqs, rks, rvs, rgs, rg_g, state_ret[0])

    wb = min(WINDOW, seq)
    kp = kv[seq - wb:, :KV_WIDTH].reshape(1, 1, wb, ATT_KV_HEADS, ATT_HEAD_DIM)
    vp = kv[seq - wb:, KV_WIDTH:].reshape(1, 1, wb, ATT_KV_HEADS, ATT_HEAD_DIM)
    sp = s_fin.reshape(1, 1, RET_HEADS, RET_DK, RET_DV)

    ys = _dec_tail(x_sample, a_s, r_s, wo, ln2, wu, wd)

    return (yp[None], ys, kp, vp, sp, from_fm(nk), from_fm(nv), ns[None])
```

```python
import functools
import math

import jax
import jax.numpy as jnp
from jax import lax
from jax.experimental import pallas as pl
from jax.experimental.pallas import tpu as pltpu

D_MODEL = 2048
ATT_HEADS = 16
ATT_KV_HEADS = 2
ATT_HEAD_DIM = 64
WINDOW = 128
RET_HEADS = 4
RET_DK = 256
RET_DV = 256
RET_CHUNK = 128
ROPE_BASE = 10000.0
D_FF = 4 * D_MODEL
EPS = 1e-6
PAST_LEN = 8192

ATT_WIDTH = ATT_HEADS * ATT_HEAD_DIM
KV_WIDTH = ATT_KV_HEADS * ATT_HEAD_DIM
RET_WIDTH = RET_HEADS * RET_DK
LANES = 128
HALF = RET_DK // 2
SPLIT_A = ATT_WIDTH + 2 * KV_WIDTH + RET_WIDTH
SPLIT_B = 3 * RET_WIDTH

F32 = jnp.float32
BF16 = jnp.bfloat16
VMEM_LIMIT = 60 * 1024 * 1024

TM = 512
TF_RET = 1024
BB_ATT = 32

LOG_G = tuple(math.log1p(-(2.0 ** (-5.0 - h))) for h in range(RET_HEADS))


def _dot(a, b):
    return jnp.dot(a, b, preferred_element_type=F32)


def _dot_nt(a, b):
    return lax.dot_general(a, b, (((1,), (1,)), ((), ())), preferred_element_type=F32)


def _dot_tn(a, b):
    return lax.dot_general(a, b, (((0,), (0,)), ((), ())), preferred_element_type=F32)


def _const_spec(shape):
    n = len(shape)
    return pl.BlockSpec(shape, lambda *_: (0,) * n, pipeline_mode=pl.Buffered(1))


def _w_in_spec(part):
    width, start = ((SPLIT_A, 0), (SPLIT_B, SPLIT_A))[part]
    return pl.BlockSpec((pl.Element(D_MODEL), pl.Element(width)), lambda *_: (0, start),
                        pipeline_mode=pl.Buffered(1))


def _params(n_axes=1):
    return pltpu.CompilerParams(dimension_semantics=("arbitrary",) * n_axes,
                                vmem_limit_bytes=VMEM_LIMIT)


def _norm_rows(x_ref, g_ref):
    x = x_ref[...]
    var = jnp.mean(x * x, axis=-1, keepdims=True)
    return (x * lax.rsqrt(var + EPS) * g_ref[...]).astype(BF16)


def _bf16_into(o_ref, v):
    return v.astype(BF16).astype(o_ref.dtype)


def _rope_init(inv_ref, cr_scr, sr_scr, pos_step):
    tm = cr_scr.shape[0]
    row = lax.broadcasted_iota(jnp.int32, (tm, 1), 0)
    ang_r = (pos_step * row).astype(F32) * inv_ref[...]
    cr_scr[...] = jnp.cos(ang_r)
    sr_scr[...] = jnp.sin(ang_r)


def _rope_tables(inv_ref, cr_scr, sr_scr, base_pos):
    base = jnp.zeros((8, 1), jnp.int32) + base_pos
    ang_b = base.astype(F32) * inv_ref[...]
    cb = jnp.cos(ang_b)[:1]
    sb = jnp.sin(ang_b)[:1]
    cr = cr_scr[...]
    sr = sr_scr[...]
    return cb * cr - sb * sr, sb * cr + cb * sr


def _rope_head(r, cos, sin, scale, o_ref, hh):
    x1 = r[:, :HALF]
    x2 = r[:, HALF:]
    o1 = x1 * cos - x2 * sin
    o2 = x2 * cos + x1 * sin
    if scale != 1.0:
        o1 = o1 * scale
        o2 = o2 * scale
    o_ref[:, RET_DK * hh:RET_DK * hh + HALF] = _bf16_into(o_ref, o1)
    o_ref[:, RET_DK * hh + HALF:RET_DK * (hh + 1)] = _bf16_into(o_ref, o2)


def _half_split(a):
    lane = lax.broadcasted_iota(jnp.int32, a.shape, 1)
    lo = lane < ATT_HEAD_DIM
    sw = pltpu.roll(a, ATT_HEAD_DIM, 1)
    zero = jnp.zeros_like(a)
    h0 = (jnp.where(lo, a, zero).astype(BF16), jnp.where(lo, zero, sw).astype(BF16))
    h1 = (jnp.where(lo, sw, zero).astype(BF16), jnp.where(lo, zero, a).astype(BF16))
    return h0, h1


SWA_STACKS = 1


def _swa_phases(sink_ref, q_ref, kvc_ref, kvp_ref, o_ref, has_prev):
    nsub = q_ref.shape[0] // WINDOW
    npair = ATT_HEADS // 2
    row = lax.broadcasted_iota(jnp.int32, (WINDOW, 2 * WINDOW), 0)
    col = lax.broadcasted_iota(jnp.int32, (WINDOW, 2 * WINDOW), 1)
    band_cur = (col >= WINDOW) & (col - WINDOW <= row)
    first_col = jnp.where(has_prev, 0, WINDOW)
    lane_lo = lax.broadcasted_iota(jnp.int32, (WINDOW, LANES), 1) < ATT_HEAD_DIM
    ctx = [{} for _ in range(nsub)]
    st = {}

    ppk = npair // ATT_KV_HEADS

    def stacks(g):
        for m in range(g * SWA_STACKS, (g + 1) * SWA_STACKS):
            j, kh = divmod(m, ATT_KV_HEADS)
            yield j, kh, slice(WINDOW * j, WINDOW * (j + 1)), range(ppk * kh, ppk * (kh + 1))

    def prep(g):
        for j in sorted({j for j, _, _, _ in stacks(g)} - {j for j in range(nsub) if ctx[j]}):
            prev = kvp_ref[...] if j == 0 else kvc_ref[WINDOW * (j - 1):WINDOW * j, :]
            cur = kvc_ref[WINDOW * j:WINDOW * (j + 1), :]
            kk = jnp.concatenate([prev[:, :KV_WIDTH], cur[:, :KV_WIDTH]], axis=0)
            vv = jnp.concatenate([prev[:, KV_WIDTH:], cur[:, KV_WIDTH:]], axis=0)
            ctx[j]["k"] = _half_split(kk)
            ctx[j]["v"] = _half_split(vv)
            lo_col = first_col if j == 0 else 0
            ctx[j]["mask"] = band_cur | ((col < WINDOW) & (col >= row) & (col >= lo_col))

    def qk(g):
        for j, kh, rows, tiles in stacks(g):
            qs = jnp.concatenate([q_ref[rows, LANES * t:LANES * (t + 1)] for t in tiles], axis=0)
            s = [_dot_nt(qs, ctx[j]["k"][kh][par]) for par in range(2)]
            for n, t in enumerate(tiles):
                st[j, t] = {"s": [sp[WINDOW * n:WINDOW * (n + 1), :] for sp in s]}

    def softmax(g):
        for j, kh, rows, tiles in stacks(g):
            for t in tiles:
                p, inv = [], []
                for par in range(2):
                    sink = sink_ref[2 * t + par]
                    s = jnp.where(ctx[j]["mask"], st[j, t]["s"][par], -jnp.inf)
                    mx = jnp.maximum(jnp.max(s, axis=-1, keepdims=True), sink)
                    e = jnp.exp(s - mx)
                    den = jnp.sum(e, axis=-1, keepdims=True) + jnp.exp(sink - mx)
                    p.append(e.astype(BF16))
                    inv.append(1.0 / den)
                st[j, t] = {"p": p, "inv": jnp.where(lane_lo, inv[0], inv[1])}

    def pv(g):
        for j, kh, rows, tiles in stacks(g):
            got = [st.pop((j, t)) for t in tiles]
            acc = sum(_dot(jnp.concatenate([u["p"][par] for u in got], axis=0),
                           ctx[j]["v"][kh][par]) for par in range(2))
            for n, t in enumerate(tiles):
                o_ref[rows, LANES * t:LANES * (t + 1)] = (
                    acc[WINDOW * n:WINDOW * (n + 1), :] * got[n]["inv"]).astype(BF16)

    ngroup = nsub * ATT_KV_HEADS // SWA_STACKS
    return ngroup, prep, qk, softmax, pv


def _emit_pipelined(main, ngroup, prep, first, middle, last, finish=None):
    nstage = ngroup + 2
    done = 0
    prep(0)
    for k in range(nstage):
        if k < ngroup:
            first(k)
        if 0 <= k - 2 < ngroup:
            last(k - 2)
        if 0 <= k - 1 < ngroup:
            middle(k - 1)
        if k + 1 < ngroup:
            prep(k + 1)
        upto = min(len(main), k + 1) if k + 1 < nstage else len(main)
        for piece in main[done:upto]:
            piece()
        done = upto
    if finish is not None:
        finish()


def _ret_phases(rq_ref, rk_ref, rv_ref, gt_ref, g_ref, o_ref, s_scr):
    c = RET_CHUNK
    nsub = rq_ref.shape[0] // c
    ri = lax.broadcasted_iota(jnp.int32, (c, c), 0)
    ci = lax.broadcasted_iota(jnp.int32, (c, c), 1)
    rel = (ri - ci).astype(F32)
    idx = lax.broadcasted_iota(jnp.int32, (c, 1), 0).astype(F32)
    head = []
    for hh in range(RET_HEADS):
        lg = LOG_G[hh]
        head.append(dict(
            dmask=jnp.where(rel >= 0, jnp.exp(lg * jnp.maximum(rel, 0.0)), 0.0),
            qdec=jnp.exp(lg * (idx + 1.0)), kdec=jnp.exp(lg * (c - 1.0 - idx)),
            cdec=math.exp(lg * c), cols=slice(RET_DK * hh, RET_DK * (hh + 1))))
    st = {}

    def update_state(g):
        for hh, hd in enumerate(head):
            u = st[g, hh]
            s_scr[hh] = hd["cdec"] * u.pop("s_prev") + u.pop("kv")

    def prep(g):
        if g > 0:
            update_state(g - 1)
        rows = slice(c * g, c * (g + 1))
        for hh, hd in enumerate(head):
            q = rq_ref[rows, hd["cols"]]
            k = rk_ref[rows, hd["cols"]]
            s_prev = s_scr[hh]
            st[g, hh] = dict(q=q, k=k, v=rv_ref[rows, hd["cols"]], s_prev=s_prev,
                             qd=(q.astype(F32) * hd["qdec"]).astype(BF16),
                             kd=(k.astype(F32) * hd["kdec"]).astype(BF16),
                             s_bf=s_prev.astype(BF16))

    def first(g):
        for hh in range(RET_HEADS):
            u = st[g, hh]
            u["att"] = _dot_nt(u.pop("q"), u.pop("k"))
            u["inter"] = _dot(u.pop("qd"), u.pop("s_bf"))
            u["kv"] = _dot_tn(u.pop("kd"), u["v"])

    def middle(g):
        for hh, hd in enumerate(head):
            u = st[g, hh]
            u["att"] = (u["att"] * hd["dmask"]).astype(BF16)

    def last(g):
        rows = slice(c * g, c * (g + 1))
        for hh, hd in enumerate(head):
            u = st[g, hh]
            o = _dot(u.pop("att"), u.pop("v")) + u.pop("inter")
            var = jnp.mean(o * o, axis=-1, keepdims=True)
            on = o * lax.rsqrt(var + EPS) * g_ref[:, hd["cols"]]
            o_ref[rows, hd["cols"]] = (on * gt_ref[rows, hd["cols"]].astype(F32)).astype(BF16)

    return nsub, prep, first, middle, last, functools.partial(update_state, nsub - 1)


def _proj_swa_body(sink_ref, x_ref, g_ref, w_ref, qg_ref, kg_ref, inv_ref, ones_ref,
                   wo_ref, wu_ref,
                   a_out, kv_out, rq_out, wo_bf, wu_bf,
                   q_scr, kv_scr, kvp_scr, cr_scr, sr_scr, *, nblk):
    i = pl.program_id(0)
    tm = x_ref.shape[0]

    @pl.when(i == 0)
    def _():
        _rope_init(inv_ref, cr_scr, sr_scr, 1)
        kvp_scr[...] = jnp.zeros_like(kvp_scr)

    cur = lax.rem(i, 2)
    prv = 1 - cur
    kv0 = ATT_WIDTH
    rq0 = ATT_WIDTH + 2 * KV_WIDTH

    def main_pieces():
        wo_bf[...] = wo_ref[...].astype(BF16)
        wu_bf[0] = wu_ref[...].astype(BF16)

        h = _norm_rows(x_ref, g_ref)
        ones = ones_ref[...]
        inv_hd = 1.0 / ATT_HEAD_DIM
        cos, sin = _rope_tables(inv_ref, cr_scr, sr_scr, i * tm)


        def q_piece(n):
            y = _dot(h, w_ref[:, 512 * n:512 * (n + 1)])
            for t in range(2):
                yt = y[:, 256 * t:256 * (t + 1)]
                ssq = _dot((yt * yt).astype(BF16), ones)
                c0 = 512 * n + 256 * t
                q_scr[cur, :, c0:c0 + 256] = (
                    yt * lax.rsqrt(ssq * inv_hd + EPS) * qg_ref[...]).astype(BF16)

        def kv_rq_piece():
            y = _dot(h, w_ref[:, kv0:rq0 + RET_DK])
            k = y[:, :KV_WIDTH]
            ssq = _dot((k * k).astype(BF16), ones[:KV_WIDTH, :KV_WIDTH])
            kn = k * lax.rsqrt(ssq * inv_hd + EPS) * kg_ref[...]
            v = y[:, KV_WIDTH:2 * KV_WIDTH]
            kv_scr[cur, :, :KV_WIDTH] = kn
            kv_scr[cur, :, KV_WIDTH:] = v
            kv_out[:, :KV_WIDTH] = kn
            kv_out[:, KV_WIDTH:] = v
            _rope_head(y[:, 2 * KV_WIDTH:], cos, sin, 1.0, rq_out, 0)

        def rq_piece():
            y = _dot(h, w_ref[:, rq0 + RET_DK:])
            for hh in range(1, RET_HEADS):
                _rope_head(y[:, RET_DK * (hh - 1):RET_DK * hh], cos, sin, 1.0, rq_out, hh)

        return [functools.partial(q_piece, 0), functools.partial(q_piece, 1), kv_rq_piece,
                rq_piece]

    def attention_phases():
        return _swa_phases(sink_ref, q_scr.at[prv], kv_scr.at[prv], kvp_scr, a_out,
                           has_prev=i > 1)

    @pl.when(i == 0)
    def _():
        for piece in main_pieces():
            piece()

    @pl.when((i > 0) & (i < nblk))
    def _():
        _emit_pipelined(main_pieces(), *attention_phases())
        kvp_scr[...] = kv_scr[prv, tm - WINDOW:, :]

    @pl.when(i == nblk)
    def _():
        _emit_pipelined([], *attention_phases())


def _proj_swa(x, ln_g, w_in, qg, kg, inv, ones, sinks, w_out, w_up):
    m = x.shape[0]
    nblk = m // TM
    cl = lambda i: jnp.minimum(i, nblk - 1)
    row = lambda w: pl.BlockSpec((TM, w), lambda i: (cl(i), 0))
    wo_spec = pl.BlockSpec((w_out.shape[0] // nblk, w_out.shape[1]), lambda i: (cl(i), 0))
    wcast = w_up.shape[1] // nblk
    per_slab = TF_RET // wcast
    assert w_up.shape[1] % nblk == 0 and TF_RET % wcast == 0
    wu_spec = pl.BlockSpec((w_up.shape[0], wcast), lambda i: (0, cl(i)))
    wu_out_spec = pl.BlockSpec((1, w_up.shape[0], wcast),
                               lambda i: (cl(i) // per_slab, 0, cl(i) % per_slab))
    return pl.pallas_call(
        functools.partial(_proj_swa_body, nblk=nblk),
        grid=(nblk + 1,),
        in_specs=[pl.BlockSpec(memory_space=pltpu.SMEM),
                  row(D_MODEL), _const_spec((1, D_MODEL)), _w_in_spec(0),
                  _const_spec((1, 256)), _const_spec((1, KV_WIDTH)),
                  _const_spec((1, HALF)), _const_spec((256, 256)), wo_spec, wu_spec],
        out_specs=[pl.BlockSpec((TM, ATT_WIDTH), lambda i: (jnp.maximum(i - 1, 0), 0)),
                   row(2 * KV_WIDTH), row(RET_WIDTH), wo_spec, wu_out_spec],
        out_shape=(jax.ShapeDtypeStruct((m, ATT_WIDTH), BF16),
                   jax.ShapeDtypeStruct((m, 2 * KV_WIDTH), F32),
                   jax.ShapeDtypeStruct((m, RET_WIDTH), BF16),
                   jax.ShapeDtypeStruct(w_out.shape, BF16),
                   jax.ShapeDtypeStruct((w_up.shape[1] // TF_RET, w_up.shape[0], TF_RET), BF16)),
        scratch_shapes=[pltpu.VMEM((2, TM, ATT_WIDTH), BF16),
                        pltpu.VMEM((2, TM, 2 * KV_WIDTH), F32),
                        pltpu.VMEM((WINDOW, 2 * KV_WIDTH), F32),
                        pltpu.VMEM((TM, HALF), F32), pltpu.VMEM((TM, HALF), F32)],
        compiler_params=_params(),
        name="proj_swa",
    )(sinks, x, ln_g, w_in, qg, kg, inv, ones, w_out, w_up)


def _proj_ret_body(x_ref, g_ref, w_ref, inv_ref, rq_ref, rg_ref, wd_ref,
                   r_out, s_out, wd_bf,
                   rk_scr, rv_scr, gt_scr, s_scr, cr_scr, sr_scr, *, nblk):
    i = pl.program_id(0)
    tm = x_ref.shape[0]

    @pl.when(i == 0)
    def _():
        _rope_init(inv_ref, cr_scr, sr_scr, 1)
        s_scr[...] = jnp.zeros_like(s_scr)

    cur = lax.rem(i, 2)
    prv = 1 - cur

    def main_pieces():
        wd_bf[...] = wd_ref[...].astype(BF16)

        h = _norm_rows(x_ref, g_ref)
        cos, sin = _rope_tables(inv_ref, cr_scr, sr_scr, i * tm)

        def rk_piece(n):
            y = _dot(h, w_ref[:, 512 * n:512 * (n + 1)])
            for t in range(2):
                _rope_head(y[:, RET_DK * t:RET_DK * (t + 1)], cos, sin, RET_DK ** -0.5,
                           rk_scr.at[cur], 2 * n + t)

        def rv_piece(n):
            c = slice(512 * n, 512 * (n + 1))
            rv_scr[cur, :, c] = _dot(h, w_ref[:, RET_WIDTH + 512 * n:RET_WIDTH + 512 * (n + 1)]
                                     ).astype(BF16)

        def gate_piece(n):
            c0 = 2 * RET_WIDTH + 512 * n
            rg = _dot(h, w_ref[:, c0:c0 + 512])
            gt_scr[cur, :, 512 * n:512 * (n + 1)] = (
                rg / (1.0 + jnp.exp(-rg))).astype(BF16)

        return [lambda f=f: (f(0), f(1)) for f in (rk_piece, rv_piece, gate_piece)]

    def retention_phases():
        return _ret_phases(rq_ref, rk_scr.at[prv], rv_scr.at[prv], gt_scr.at[prv],
                           rg_ref, r_out, s_scr)

    @pl.when(i == 0)
    def _():
        for piece in main_pieces():
            piece()

    @pl.when((i > 0) & (i < nblk))
    def _():
        _emit_pipelined(main_pieces(), *retention_phases())

    @pl.when(i == nblk)
    def _():
        _emit_pipelined([], *retention_phases())
        s_out[...] = s_scr[...]


def _proj_ret(x, ln_g, w_in, inv, rq, rg_g, w_down):
    m = x.shape[0]
    nblk = m // TM
    cl = lambda i: jnp.minimum(i, nblk - 1)
    prev = lambda w: pl.BlockSpec((TM, w), lambda i: (jnp.maximum(i - 1, 0), 0))
    wd_spec = pl.BlockSpec((w_down.shape[0] // nblk, w_down.shape[1]), lambda i: (cl(i), 0))
    state = (RET_HEADS, RET_DK, RET_DV)
    slot = pltpu.VMEM((2, TM, RET_WIDTH), BF16)
    return pl.pallas_call(
        functools.partial(_proj_ret_body, nblk=nblk),
        grid=(nblk + 1,),
        in_specs=[pl.BlockSpec((TM, D_MODEL), lambda i: (cl(i), 0)),
                  _const_spec((1, D_MODEL)), _w_in_spec(1), _const_spec((1, HALF)),
                  prev(RET_WIDTH), _const_spec((1, RET_WIDTH)), wd_spec],
        out_specs=[prev(RET_WIDTH), pl.BlockSpec(state, lambda i: (0, 0, 0)), wd_spec],
        out_shape=(jax.ShapeDtypeStruct((m, RET_WIDTH), BF16),
                   jax.ShapeDtypeStruct(state, F32),
                   jax.ShapeDtypeStruct(w_down.shape, BF16)),
        scratch_shapes=[slot, slot, slot, pltpu.VMEM(state, F32),
                        pltpu.VMEM((TM, HALF), F32), pltpu.VMEM((TM, HALF), F32)],
        compiler_params=_params(),
        name="proj_ret",
    )(x, ln_g, w_in, inv, rq, rg_g, w_down)


TN_IN = 1792
Z_TILE = 256


def _in_proj_body(x_ref, g_ref, w_ref, qg_ref, kg_ref, inv_ref, ones_ref,
                  w_bf, q_out, kv_out, rq_out, rk_out, rv_out, gt_out, kvt_out, rqt_out, rkt_out,
                  h_scr, z_scr, xd_scr, *, pos):
    c = pl.program_id(0)

    @pl.when(c == 0)
    def _():
        xd_scr[...] = x_ref[:, 0, :]
        h_scr[...] = _norm_rows(xd_scr, g_ref)

    wb = w_ref[...].astype(BF16)
    w_bf[...] = wb
    z = _dot(h_scr[...], wb)
    per_step = TN_IN // Z_TILE
    for t in range(per_step):
        z_scr[c * per_step + t] = z[:, Z_TILE * t:Z_TILE * (t + 1)]

    @pl.when(c == pl.num_programs(0) - 1)
    def _():
        tiles = lambda col0, n: [z_scr[col0 // Z_TILE + t] for t in range(n)]
        ones = ones_ref[...]
        inv_hd = 1.0 / ATT_HEAD_DIM
        for t, y in enumerate(tiles(0, ATT_WIDTH // Z_TILE)):
            ssq = _dot((y * y).astype(BF16), ones)
            qn = _bf16_into(q_out, y * lax.rsqrt(ssq * inv_hd + EPS) * qg_ref[...])
            for half in range(2):
                q_out[:, 2 * t + half, :] = qn[:, LANES * half:LANES * (half + 1)]
        (kvr,) = tiles(ATT_WIDTH, 1)
        k = kvr[:, :KV_WIDTH]
        ssq = _dot((k * k).astype(BF16), ones[:KV_WIDTH, :KV_WIDTH])
        kv_out[:, :KV_WIDTH] = k * lax.rsqrt(ssq * inv_hd + EPS) * kg_ref[...]
        kv_out[:, KV_WIDTH:] = kvr[:, KV_WIDTH:]
        ang = jnp.full((8, 1), pos, jnp.int32).astype(F32) * inv_ref[...]
        cos = jnp.cos(ang)[:1]
        sin = jnp.sin(ang)[:1]
        rq0 = ATT_WIDTH + 2 * KV_WIDTH
        for hh, y in enumerate(tiles(rq0, RET_HEADS)):
            _rope_head(y, cos, sin, 1.0, rq_out, hh)
        for hh, y in enumerate(tiles(rq0 + RET_WIDTH, RET_HEADS)):
            _rope_head(y, cos, sin, RET_DK ** -0.5, rk_out, hh)
        for hh, y in enumerate(tiles(rq0 + 2 * RET_WIDTH, RET_HEADS)):
            rv_out[:, RET_DV * hh:RET_DV * (hh + 1)] = _bf16_into(rv_out, y)
        for hh, y in enumerate(tiles(rq0 + 3 * RET_WIDTH, RET_HEADS)):
            gt_out[:, RET_DV * hh:RET_DV * (hh + 1)] = _bf16_into(gt_out, y / (1.0 + jnp.exp(-y)))
        kvt_out[...] = kv_out[...].T
        rqt_out[...] = rq_out[...].T
        rkt_out[...] = rk_out[...].T


def _in_proj(x, ln_g, w_in, qg, kg, inv, ones, pos):
    m = x.shape[0]
    assert m == LANES
    assert RET_DK == Z_TILE and TN_IN % Z_TILE == 0 and w_in.shape[1] % TN_IN == 0
    nstep = w_in.shape[1] // TN_IN
    full = lambda w: pl.BlockSpec((m, w), lambda c: (0, 0))
    colm = lambda w: pl.BlockSpec((w, m), lambda c: (0, 0))
    wcol = pl.BlockSpec((D_MODEL, TN_IN), lambda c: (0, c))
    ret = jax.ShapeDtypeStruct((m, RET_WIDTH), F32)
    ret_t = jax.ShapeDtypeStruct((RET_WIDTH, m), F32)
    return pl.pallas_call(
        functools.partial(_in_proj_body, pos=pos),
        grid=(nstep,),
        in_specs=[_const_spec((m, 1, D_MODEL)), _const_spec((1, D_MODEL)), wcol,
                  _const_spec((1, 256)), _const_spec((1, KV_WIDTH)),
                  _const_spec((1, HALF)), _const_spec((256, 256))],
        out_specs=[wcol, pl.BlockSpec((m, ATT_HEADS // 2, LANES), lambda c: (0, 0, 0)),
                   full(2 * KV_WIDTH)] + [full(RET_WIDTH)] * 4
        + [colm(2 * KV_WIDTH), colm(RET_WIDTH), colm(RET_WIDTH)],
        out_shape=(jax.ShapeDtypeStruct(w_in.shape, BF16),
                   jax.ShapeDtypeStruct((m, ATT_HEADS // 2, LANES), F32),
                   jax.ShapeDtypeStruct((m, 2 * KV_WIDTH), F32), ret, ret, ret, ret,
                   jax.ShapeDtypeStruct((2 * KV_WIDTH, m), F32), ret_t, ret_t),
        scratch_shapes=[pltpu.VMEM((m, D_MODEL), BF16),
                        pltpu.VMEM((w_in.shape[1] // Z_TILE, m, Z_TILE), F32),
                        pltpu.VMEM((m, D_MODEL), F32)],
        compiler_params=_params(),
        name="in_proj",
    )(x, ln_g, w_in, qg, kg, inv, ones)


def _this_steps_columns(t_ref, rows, bb):
    shift = lax.rem(LANES - bb * pl.program_id(0), LANES)
    return pltpu.roll(t_ref[rows, :], shift, 1)


def _swa_dec_body(q_ref, kvn_ref, kvt_ref, ck_ref, cv_ref, sink_ref, o_ref, nk_ref, nv_ref):
    bb = q_ref.shape[0]
    npair = ATT_HEADS // 2
    q8 = q_ref[...]
    q8r = pltpu.roll(q8, ATT_HEAD_DIM, 2)
    lane = lax.broadcasted_iota(jnp.int32, q8.shape, 2)
    pair = lax.broadcasted_iota(jnp.int32, q8.shape, 1)
    lo = lane < ATT_HEAD_DIM
    kv0 = pair < npair // ATT_KV_HEADS
    own = lo == kv0
    zero = jnp.zeros_like(q8)
    qe = jnp.where(own, jnp.where(kv0, q8, q8r), zero)
    qo = jnp.where(own, jnp.where(kv0, q8r, q8), zero)
    qb = jnp.concatenate([qe, qo], axis=1)

    ck = ck_ref[...]
    cv = cv_ref[...]
    kn = kvn_ref[:, :KV_WIDTH]
    vn = kvn_ref[:, KV_WIDTH:]
    s = lax.dot_general(qb.astype(BF16), ck.astype(BF16), (((2,), (1,)), ((0,), (0,))),
                        preferred_element_type=F32)
    s_new = jnp.sum(qb * kn[:, None, :], axis=-1, keepdims=True)
    sink = sink_ref[...][None, :, :]
    mx = jnp.maximum(jnp.maximum(jnp.max(s, axis=-1, keepdims=True), s_new), sink)
    p = jnp.exp(s - mx)
    p_new = jnp.exp(s_new - mx)
    den = jnp.sum(p, axis=-1, keepdims=True) + p_new + jnp.exp(sink - mx)
    o = lax.dot_general(p.astype(BF16), cv.astype(BF16), (((2,), (2,)), ((0,), (0,))),
                        preferred_element_type=F32)
    o = (o + p_new * vn[:, None, :]) / den
    oe = o[:, :npair, :]
    oo = o[:, npair:, :]
    oer = pltpu.roll(oe, ATT_HEAD_DIM, 2)
    oor = pltpu.roll(oo, ATT_HEAD_DIM, 2)
    o_ref[...] = jnp.where(lo, jnp.where(kv0, oe, oer), jnp.where(kv0, oor, oo)).astype(BF16)

    newcol = _this_steps_columns(kvt_ref, slice(None), bb)
    last = lax.broadcasted_iota(jnp.int32, (KV_WIDTH, WINDOW), 1) == WINDOW - 1
    for jb in range(bb):
        nk_ref[jb] = jnp.where(last, newcol[:KV_WIDTH, jb:jb + 1],
                               pltpu.roll(ck[jb], WINDOW - 1, 1))
        nv_ref[jb] = jnp.where(last, newcol[KV_WIDTH:, jb:jb + 1],
                               pltpu.roll(cv[jb], WINDOW - 1, 1))


def _swa_dec(q8, kvn, kvt, ck, cv, sink_col):
    b = q8.shape[0]
    bb = BB_ATT
    cache = pl.BlockSpec((bb, KV_WIDTH, WINDOW), lambda i: (i, 0, 0))
    return pl.pallas_call(
        _swa_dec_body,
        grid=(b // bb,),
        in_specs=[pl.BlockSpec((bb, ATT_HEADS // 2, LANES), lambda i: (i, 0, 0)),
                  pl.BlockSpec((bb, 2 * KV_WIDTH), lambda i: (i, 0)),
                  _const_spec(kvt.shape), cache, cache, _const_spec((ATT_HEADS, 1))],
        out_specs=[pl.BlockSpec((bb, ATT_HEADS // 2, LANES), lambda i: (i, 0, 0)), cache, cache],
        out_shape=(jax.ShapeDtypeStruct((b, ATT_HEADS // 2, LANES), BF16),
                   jax.ShapeDtypeStruct(ck.shape, F32),
                   jax.ShapeDtypeStruct(cv.shape, F32)),
        compiler_params=_params(),
        name="swa_dec",
    )(q8, kvn, kvt, ck, cv, sink_col)


def _out_proj_body(x_ref, a_ref, r_ref, w_ref, g_ref, x1_ref, h2_ref):
    if len(x_ref.shape) == 3:
        x1_ref[...] = x_ref[:, 0, :]
        x_ref = x1_ref
    x1 = (x_ref[...] + _dot(a_ref[...].astype(BF16), w_ref[:ATT_WIDTH, :])
          + _dot(r_ref[...].astype(BF16), w_ref[ATT_WIDTH:, :]))
    x1_ref[...] = x1
    var = jnp.mean(x1 * x1, axis=-1, keepdims=True)
    h2_ref[...] = (x1 * lax.rsqrt(var + EPS) * g_ref[...]).astype(BF16)


def _out_proj(x, a, r, w, g, tm):
    m = x.shape[0]
    row = lambda w: pl.BlockSpec((tm, w), lambda i: (i, 0))
    return pl.pallas_call(
        _out_proj_body,
        grid=(m // tm,),
        in_specs=[row(D_MODEL), row(ATT_WIDTH), row(RET_WIDTH),
                  _const_spec(w.shape), _const_spec((1, D_MODEL))],
        out_specs=[row(D_MODEL), row(D_MODEL)],
        out_shape=(jax.ShapeDtypeStruct((m, D_MODEL), F32),
                   jax.ShapeDtypeStruct((m, D_MODEL), BF16)),
        compiler_params=_params(),
        name="out_proj",
    )(x, a, r, w, g)


def _mlp_acc(h2_ref, wu_ref, wd_ref, o_ref):
    u = jnp.maximum(_dot(h2_ref[...], wu_ref[0]), 0.0)
    o_ref[...] += _dot((u * u).astype(BF16), wd_ref[...])


def _dec_tail_body(x_ref, a_ref, r_ref, wo_ref, g2_ref, wu_ref, wd_ref, o_ref, acc, h2_scr):
    @pl.when(pl.program_id(0) == 0)
    def _():
        _out_proj_body(x_ref, a_ref, r_ref, wo_ref, g2_ref, acc, h2_scr)

    _mlp_acc(h2_scr, wu_ref, wd_ref, acc)

    @pl.when(pl.program_id(0) == pl.num_programs(0) - 1)
    def _():
        o_ref[:, 0, :] = acc[...]


def _dec_tail(x, a, r, wo, g2, wu, wd):
    m = x.shape[0]
    full = lambda s: pl.BlockSpec(s, lambda f: (0,) * len(s), pipeline_mode=pl.Buffered(1))
    return pl.pallas_call(
        _dec_tail_body,
        grid=(D_FF // TF_RET,),
        in_specs=[full((m, 1, D_MODEL)), full((m, ATT_WIDTH)), full((m, RET_WIDTH)),
                  full(wo.shape), full((1, D_MODEL)),
                  pl.BlockSpec((1, D_MODEL, TF_RET), lambda f: (f, 0, 0)),
                  pl.BlockSpec((TF_RET, D_MODEL), lambda f: (f, 0))],
        out_specs=pl.BlockSpec((m, 1, D_MODEL), lambda f: (0, 0, 0)),
        out_shape=jax.ShapeDtypeStruct((m, 1, D_MODEL), F32),
        scratch_shapes=[pltpu.VMEM((m, D_MODEL), F32), pltpu.VMEM((m, D_MODEL), BF16)],
        compiler_params=_params(),
        name="dec_tail",
    )(x, a, r, wo, g2, wu, wd)


RING = 3


def _mlp_ret_body(x1_ref, h2_ref, wu_hbm, wd_hbm,
                  qt_ref, kt_ref, rq_ref, rk_ref, rv_ref, gt_ref, g_ref, s_ref,
                  o_ref, r_ref, ns_ref, wu_buf, wd_buf, sem):
    nf = pl.num_programs(1)
    b = pl.program_id(0) * nf + pl.program_id(1)
    nstep = pl.num_programs(0) * nf

    def slab_copies(step):
        slab = lax.rem(step, nf)
        slot = lax.rem(step, RING)
        rows = pl.ds(pl.multiple_of(slab * TF_RET, TF_RET), TF_RET)
        return (pltpu.make_async_copy(wu_hbm.at[slab], wu_buf.at[slot], sem.at[0, slot]),
                pltpu.make_async_copy(wd_hbm.at[rows, :], wd_buf.at[slot], sem.at[1, slot]))

    @pl.when(b == 0)
    def _():
        for step in range(RING - 1):
            for copy in slab_copies(step):
                copy.start()

    @pl.when(b + RING - 1 < nstep)
    def _():
        for copy in slab_copies(b + RING - 1):
            copy.start()

    for copy in slab_copies(b):
        copy.wait()
    slot = lax.rem(b, RING)

    def step(first):
        u = jnp.maximum(_dot(h2_ref[...], wu_buf[slot]), 0.0)
        base = x1_ref[...] if first else o_ref[...]
        o_ref[...] = base + _dot((u * u).astype(BF16), wd_buf[slot])

        row = pl.ds(b, 1)
        shift = lax.rem(LANES - b, LANES)
        for hh in range(RET_HEADS):
            g1 = math.exp(LOG_G[hh])
            cols = slice(RET_DK * hh, RET_DK * (hh + 1))
            qc = pltpu.roll(qt_ref[cols, :], shift, 1)[:, :1] * g1
            kc = pltpu.roll(kt_ref[cols, :], shift, 1)[:, :1]
            v = rv_ref[row, cols]
            s0 = s_ref[0, hh]
            ns_ref[0, hh] = g1 * s0 + kc * v
            qk = jnp.sum(rq_ref[row, cols] * rk_ref[row, cols], axis=-1, keepdims=True)
            o = jnp.sum(qc * s0, axis=0, keepdims=True) + qk * v
            var = jnp.mean(o * o, axis=-1, keepdims=True)
            r_ref[row, cols] = o * lax.rsqrt(var + EPS) * g_ref[:, cols] * gt_ref[row, cols]

    pl.when(pl.program_id(1) == 0)(functools.partial(step, True))
    pl.when(pl.program_id(1) != 0)(functools.partial(step, False))


def _mlp_ret(x1, h2, wu, wd, qt, kt, rq, rk, rv, gt, g, state):
    m = x1.shape[0]
    nb = state.shape[0]
    nf = D_FF // TF_RET
    assert m // TM * nf == nb and nb == LANES
    assert nb >= RING
    row = pl.BlockSpec((TM, D_MODEL), lambda i, f: (i, 0))
    hbm = pl.BlockSpec(memory_space=pl.ANY)
    st = pl.BlockSpec((1, RET_HEADS, RET_DK, RET_DV), lambda i, f: (i * nf + f, 0, 0, 0))
    rows = _const_spec((nb, RET_WIDTH))
    return pl.pallas_call(
        _mlp_ret_body,
        grid=(m // TM, nf),
        in_specs=[row, row, hbm, hbm,
                  _const_spec(qt.shape), _const_spec(kt.shape), rows, rows, rows, rows,
                  _const_spec((1, RET_WIDTH)), st],
        out_specs=[row, pl.BlockSpec((nb, RET_WIDTH), lambda i, f: (0, 0)), st],
        out_shape=(jax.ShapeDtypeStruct((m, D_MODEL), F32),
                   jax.ShapeDtypeStruct((nb, RET_WIDTH), F32),
                   jax.ShapeDtypeStruct(state.shape, F32)),
        scratch_shapes=[pltpu.VMEM((RING, D_MODEL, TF_RET), BF16),
                        pltpu.VMEM((RING, TF_RET, D_MODEL), BF16),
                        pltpu.SemaphoreType.DMA((2, RING))],
        compiler_params=_params(2),
        name="mlp_ret",
    )(x1, h2, wu, wd, qt, kt, rq, rk, rv, gt, g, state)


def kernel(x_prompt, x_sample, cache_k_win, cache_v_win, state_ret, ln1_g, w_in, q_norm_g,
           k_norm_g, attn_sinks, ret_norm_g, w_out, ln2_g, w_up, w_down):
    seq = x_prompt.shape[1]
    nb = x_sample.shape[0]
    assert x_prompt.shape[0] == 1 and x_sample.shape[1] == 1 and w_in.shape[0] == 1
    assert seq % TM == 0 and w_in.shape[2] == SPLIT_A + SPLIT_B

    inv = (ROPE_BASE ** (-jnp.arange(HALF, dtype=F32) / HALF)).reshape(1, HALF)

    ln1 = ln1_g.reshape(1, D_MODEL)
    ln2 = ln2_g.reshape(1, D_MODEL)
    qg = jnp.tile(q_norm_g.reshape(1, ATT_HEAD_DIM) * (ATT_HEAD_DIM ** -0.5), (1, 256 // ATT_HEAD_DIM))
    kg = jnp.tile(k_norm_g.reshape(1, ATT_HEAD_DIM), (1, KV_WIDTH // ATT_HEAD_DIM))
    rg_g = ret_norm_g.reshape(1, RET_WIDTH)
    sinks = attn_sinks.reshape(ATT_HEADS)
    blk = jnp.arange(256) // ATT_HEAD_DIM
    ones = (blk[:, None] == blk[None, :]).astype(BF16)

    wi, qs, kvs, rqs, rks, rvs, rgs, kvt, rqt, rkt = _in_proj(x_sample, ln1, w_in[0], qg, kg, inv,
                                                              ones, pos=PAST_LEN)
    sink_col = jnp.concatenate([sinks[0::2], sinks[1::2]]).reshape(ATT_HEADS, 1)
    to_fm = lambda c: c[0].transpose(0, 2, 3, 1).reshape(nb, KV_WIDTH, WINDOW)
    from_fm = lambda c: c.reshape(nb, ATT_KV_HEADS, ATT_HEAD_DIM, WINDOW).transpose(0, 3, 1, 2)[None]
    a8, nk, nv = _swa_dec(qs, kvs, kvt,
                          to_fm(cache_k_win), to_fm(cache_v_win), sink_col)
    a_s = a8.reshape(nb, ATT_WIDTH)

    xp = x_prompt[0]
    a_out, kv, rq, wo, wu = _proj_swa(xp, ln1, wi, qg, kg, inv, ones, sinks, w_out[0], w_up[0])
    r_out, s_fin, wd = _proj_ret(xp, ln1, wi, inv, rq, rg_g, w_down[0])
    x1, h2 = _out_proj(xp, a_out, r_out, wo, ln2, tm=TM)
    yp, r_s, ns = _mlp_ret(x1, h2, wu, wd, rqt, rkt, rqs, rks, rvs, rgs, rg_g, state_ret[0])

    wb = min(WINDOW, seq)
    kp = kv[seq - wb:, :KV_WIDTH].reshape(1, 1, wb, ATT_KV_HEADS, ATT_HEAD_DIM)
    vp = kv[seq - wb:, KV_WIDTH:].reshape(1, 1, wb, ATT_KV_HEADS, ATT_HEAD_DIM)
    sp = s_fin.reshape(1, 1, RET_HEADS, RET_DK, RET_DV)

    ys = _dec_tail(x_sample, a_s, r_s, wo, ln2, wu, wd)

    return (yp[None], ys, kp, vp, sp, from_fm(nk), from_fm(nv), ns[None])
```

```python
import functools
import math

import jax
import jax.numpy as jnp
from jax import lax
from jax.experimental import pallas as pl
from jax.experimental.pallas import tpu as pltpu

D_MODEL = 2048
ATT_HEADS = 16
ATT_KV_HEADS = 2
ATT_HEAD_DIM = 64
WINDOW = 128
RET_HEADS = 4
RET_DK = 256
RET_DV = 256
RET_CHUNK = 128
ROPE_BASE = 10000.0
D_FF = 4 * D_MODEL
EPS = 1e-6
PAST_LEN = 8192

ATT_WIDTH = ATT_HEADS * ATT_HEAD_DIM
KV_WIDTH = ATT_KV_HEADS * ATT_HEAD_DIM
RET_WIDTH = RET_HEADS * RET_DK
LANES = 128
HALF = RET_DK // 2
SPLIT_A = ATT_WIDTH + 2 * KV_WIDTH + RET_WIDTH
SPLIT_B = 3 * RET_WIDTH

F32 = jnp.float32
BF16 = jnp.bfloat16
VMEM_LIMIT = 60 * 1024 * 1024

TM = 512
TF_RET = 1024
BB_ATT = 32

LOG_G = tuple(math.log1p(-(2.0 ** (-5.0 - h))) for h in range(RET_HEADS))


def _dot(a, b):
    return jnp.dot(a, b, preferred_element_type=F32)


def _dot_nt(a, b):
    return lax.dot_general(a, b, (((1,), (1,)), ((), ())), preferred_element_type=F32)


def _dot_tn(a, b):
    return lax.dot_general(a, b, (((0,), (0,)), ((), ())), preferred_element_type=F32)


def _const_spec(shape):
    n = len(shape)
    return pl.BlockSpec(shape, lambda *_: (0,) * n, pipeline_mode=pl.Buffered(1))


def _w_in_spec(part):
    width, start = ((SPLIT_A, 0), (SPLIT_B, SPLIT_A))[part]
    return pl.BlockSpec((pl.Element(D_MODEL), pl.Element(width)), lambda *_: (0, start),
                        pipeline_mode=pl.Buffered(1))


def _params(n_axes=1):
    return pltpu.CompilerParams(dimension_semantics=("arbitrary",) * n_axes,
                                vmem_limit_bytes=VMEM_LIMIT)


def _norm_rows(x_ref, g_ref):
    x = x_ref[...]
    var = jnp.mean(x * x, axis=-1, keepdims=True)
    return (x * lax.rsqrt(var + EPS) * g_ref[...]).astype(BF16)


def _bf16_into(o_ref, v):
    return v.astype(BF16).astype(o_ref.dtype)


def _rope_init(inv_ref, cr_scr, sr_scr, pos_step):
    tm = cr_scr.shape[0]
    row = lax.broadcasted_iota(jnp.int32, (tm, 1), 0)
    ang_r = (pos_step * row).astype(F32) * inv_ref[...]
    cr_scr[...] = jnp.cos(ang_r)
    sr_scr[...] = jnp.sin(ang_r)


def _rope_tables(inv_ref, cr_scr, sr_scr, base_pos):
    base = jnp.zeros((8, 1), jnp.int32) + base_pos
    ang_b = base.astype(F32) * inv_ref[...]
    cb = jnp.cos(ang_b)[:1]
    sb = jnp.sin(ang_b)[:1]
    cr = cr_scr[...]
    sr = sr_scr[...]
    return cb * cr - sb * sr, sb * cr + cb * sr


def _rope_head(r, cos, sin, scale, o_ref, hh):
    x1 = r[:, :HALF]
    x2 = r[:, HALF:]
    o1 = x1 * cos - x2 * sin
    o2 = x2 * cos + x1 * sin
    if scale != 1.0:
        o1 = o1 * scale
        o2 = o2 * scale
    o_ref[:, RET_DK * hh:RET_DK * hh + HALF] = _bf16_into(o_ref, o1)
    o_ref[:, RET_DK * hh + HALF:RET_DK * (hh + 1)] = _bf16_into(o_ref, o2)


def _half_split(a):
    lane = lax.broadcasted_iota(jnp.int32, a.shape, 1)
    lo = lane < ATT_HEAD_DIM
    sw = pltpu.roll(a, ATT_HEAD_DIM, 1)
    zero = jnp.zeros_like(a)
    h0 = (jnp.where(lo, a, zero).astype(BF16), jnp.where(lo, zero, sw).astype(BF16))
    h1 = (jnp.where(lo, sw, zero).astype(BF16), jnp.where(lo, zero, a).astype(BF16))
    return h0, h1


SWA_STACKS = 1


def _swa_phases(sink_ref, q_ref, kvc_ref, kvp_ref, o_ref, has_prev):
    nsub = q_ref.shape[0] // WINDOW
    npair = ATT_HEADS // 2
    row = lax.broadcasted_iota(jnp.int32, (WINDOW, 2 * WINDOW), 0)
    col = lax.broadcasted_iota(jnp.int32, (WINDOW, 2 * WINDOW), 1)
    band_cur = (col >= WINDOW) & (col - WINDOW <= row)
    first_col = jnp.where(has_prev, 0, WINDOW)
    lane_lo = lax.broadcasted_iota(jnp.int32, (WINDOW, LANES), 1) < ATT_HEAD_DIM
    ctx = [{} for _ in range(nsub)]
    st = {}

    ppk = npair // ATT_KV_HEADS

    def stacks(g):
        for m in range(g * SWA_STACKS, (g + 1) * SWA_STACKS):
            j, kh = divmod(m, ATT_KV_HEADS)
            yield j, kh, slice(WINDOW * j, WINDOW * (j + 1)), range(ppk * kh, ppk * (kh + 1))

    def prep(g):
        for j in sorted({j for j, _, _, _ in stacks(g)} - {j for j in range(nsub) if ctx[j]}):
            prev = kvp_ref[...] if j == 0 else kvc_ref[WINDOW * (j - 1):WINDOW * j, :]
            cur = kvc_ref[WINDOW * j:WINDOW * (j + 1), :]
            kk = jnp.concatenate([prev[:, :KV_WIDTH], cur[:, :KV_WIDTH]], axis=0)
            vv = jnp.concatenate([prev[:, KV_WIDTH:], cur[:, KV_WIDTH:]], axis=0)
            ctx[j]["k"] = _half_split(kk)
            ctx[j]["v"] = _half_split(vv)
            lo_col = first_col if j == 0 else 0
            ctx[j]["mask"] = band_cur | ((col < WINDOW) & (col >= row) & (col >= lo_col))

    def qk(g):
        for j, kh, rows, tiles in stacks(g):
            qs = jnp.concatenate([q_ref[rows, LANES * t:LANES * (t + 1)] for t in tiles], axis=0)
            s = [_dot_nt(qs, ctx[j]["k"][kh][par]) for par in range(2)]
            for n, t in enumerate(tiles):
                st[j, t] = {"s": [sp[WINDOW * n:WINDOW * (n + 1), :] for sp in s]}

    def softmax(g):
        for j, kh, rows, tiles in stacks(g):
            for t in tiles:
                p, inv = [], []
                for par in range(2):
                    sink = sink_ref[2 * t + par]
                    s = jnp.where(ctx[j]["mask"], st[j, t]["s"][par], -jnp.inf)
                    mx = jnp.maximum(jnp.max(s, axis=-1, keepdims=True), sink)
                    e = jnp.exp(s - mx)
                    den = jnp.sum(e, axis=-1, keepdims=True) + jnp.exp(sink - mx)
                    p.append(e.astype(BF16))
                    inv.append(1.0 / den)
                st[j, t] = {"p": p, "inv": jnp.where(lane_lo, inv[0], inv[1])}

    def pv(g):
        for j, kh, rows, tiles in stacks(g):
            got = [st.pop((j, t)) for t in tiles]
            acc = sum(_dot(jnp.concatenate([u["p"][par] for u in got], axis=0),
                           ctx[j]["v"][kh][par]) for par in range(2))
            for n, t in enumerate(tiles):
                o_ref[rows, LANES * t:LANES * (t + 1)] = (
                    acc[WINDOW * n:WINDOW * (n + 1), :] * got[n]["inv"]).astype(BF16)

    ngroup = nsub * ATT_KV_HEADS // SWA_STACKS
    return ngroup, prep, qk, softmax, pv


def _emit_pipelined(main, ngroup, prep, first, middle, last, finish=None):
    nstage = ngroup + 2
    done = 0
    prep(0)
    for k in range(nstage):
        if k < ngroup:
            first(k)
        if 0 <= k - 2 < ngroup:
            last(k - 2)
        if 0 <= k - 1 < ngroup:
            middle(k - 1)
        if k + 1 < ngroup:
            prep(k + 1)
        upto = min(len(main), k + 1) if k + 1 < nstage else len(main)
        for piece in main[done:upto]:
            piece()
        done = upto
    if finish is not None:
        finish()


def _ret_phases(rq_ref, rk_ref, rv_ref, gt_ref, g_ref, o_ref, s_scr):
    c = RET_CHUNK
    nsub = rq_ref.shape[0] // c
    ri = lax.broadcasted_iota(jnp.int32, (c, c), 0)
    ci = lax.broadcasted_iota(jnp.int32, (c, c), 1)
    rel = (ri - ci).astype(F32)
    idx = lax.broadcasted_iota(jnp.int32, (c, 1), 0).astype(F32)
    head = []
    for hh in range(RET_HEADS):
        lg = LOG_G[hh]
        head.append(dict(
            dmask=jnp.where(rel >= 0, jnp.exp(lg * jnp.maximum(rel, 0.0)), 0.0),
            qdec=jnp.exp(lg * (idx + 1.0)), kdec=jnp.exp(lg * (c - 1.0 - idx)),
            cdec=math.exp(lg * c), cols=slice(RET_DK * hh, RET_DK * (hh + 1))))
    st = {}

    def update_state(g):
        for hh, hd in enumerate(head):
            u = st[g, hh]
            s_scr[hh] = hd["cdec"] * u.pop("s_prev") + u.pop("kv")

    def prep(g):
        if g > 0:
            update_state(g - 1)
        rows = slice(c * g, c * (g + 1))
        for hh, hd in enumerate(head):
            q = rq_ref[rows, hd["cols"]]
            k = rk_ref[rows, hd["cols"]]
            s_prev = s_scr[hh]
            st[g, hh] = dict(q=q, k=k, v=rv_ref[rows, hd["cols"]], s_prev=s_prev,
                             qd=(q.astype(F32) * hd["qdec"]).astype(BF16),
                             kd=(k.astype(F32) * hd["kdec"]).astype(BF16),
                             s_bf=s_prev.astype(BF16))

    def first(g):
        for hh in range(RET_HEADS):
            u = st[g, hh]
            u["att"] = _dot_nt(u.pop("q"), u.pop("k"))
            u["inter"] = _dot(u.pop("qd"), u.pop("s_bf"))
            u["kv"] = _dot_tn(u.pop("kd"), u["v"])

    def middle(g):
        for hh, hd in enumerate(head):
            u = st[g, hh]
            u["att"] = (u["att"] * hd["dmask"]).astype(BF16)

    def last(g):
        rows = slice(c * g, c * (g + 1))
        for hh, hd in enumerate(head):
            u = st[g, hh]
            o = _dot(u.pop("att"), u.pop("v")) + u.pop("inter")
            var = jnp.mean(o * o, axis=-1, keepdims=True)
            on = o * lax.rsqrt(var + EPS) * g_ref[:, hd["cols"]]
            o_ref[rows, hd["cols"]] = (on * gt_ref[rows, hd["cols"]].astype(F32)).astype(BF16)

    return nsub, prep, first, middle, last, functools.partial(update_state, nsub - 1)


def _proj_swa_body(sink_ref, x_ref, g_ref, w_ref, qg_ref, kg_ref, inv_ref, ones_ref,
                   wo_ref, wu_ref,
                   a_out, kv_out, rq_out, wo_bf, wu_bf,
                   q_scr, kv_scr, kvp_scr, cr_scr, sr_scr, *, nblk):
    i = pl.program_id(0)
    tm = x_ref.shape[0]

    @pl.when(i == 0)
    def _():
        _rope_init(inv_ref, cr_scr, sr_scr, 1)
        q_scr[...] = jnp.zeros_like(q_scr)
        kv_scr[...] = jnp.zeros_like(kv_scr)
        kvp_scr[...] = jnp.zeros_like(kvp_scr)

    wo_bf[...] = wo_ref[...].astype(BF16)
    wu_bf[0] = wu_ref[...].astype(BF16)

    cur = lax.rem(i, 2)
    prv = 1 - cur
    blk = jnp.minimum(i, nblk - 1)

    h = _norm_rows(x_ref, g_ref)
    ones = ones_ref[...]
    inv_hd = 1.0 / ATT_HEAD_DIM
    cos, sin = _rope_tables(inv_ref, cr_scr, sr_scr, blk * tm)


    def q_piece(n):
        y = _dot(h, w_ref[:, 512 * n:512 * (n + 1)])
        for t in range(2):
            yt = y[:, 256 * t:256 * (t + 1)]
            ssq = _dot((yt * yt).astype(BF16), ones)
            c0 = 512 * n + 256 * t
            q_scr[cur, :, c0:c0 + 256] = (
                yt * lax.rsqrt(ssq * inv_hd + EPS) * qg_ref[...]).astype(BF16)

    kv0 = ATT_WIDTH
    rq0 = ATT_WIDTH + 2 * KV_WIDTH

    def kv_rq_piece():
        y = _dot(h, w_ref[:, kv0:rq0 + RET_DK])
        k = y[:, :KV_WIDTH]
        ssq = _dot((k * k).astype(BF16), ones[:KV_WIDTH, :KV_WIDTH])
        kn = k * lax.rsqrt(ssq * inv_hd + EPS) * kg_ref[...]
        v = y[:, KV_WIDTH:2 * KV_WIDTH]
        kv_scr[cur, :, :KV_WIDTH] = kn
        kv_scr[cur, :, KV_WIDTH:] = v
        kv_out[:, :KV_WIDTH] = kn
        kv_out[:, KV_WIDTH:] = v
        _rope_head(y[:, 2 * KV_WIDTH:], cos, sin, 1.0, rq_out, 0)

    def rq_piece():
        y = _dot(h, w_ref[:, rq0 + RET_DK:])
        for hh in range(1, RET_HEADS):
            _rope_head(y[:, RET_DK * (hh - 1):RET_DK * hh], cos, sin, 1.0, rq_out, hh)

    main = [functools.partial(q_piece, 0), functools.partial(q_piece, 1), kv_rq_piece, rq_piece]
    _emit_pipelined(main, *_swa_phases(sink_ref, q_scr.at[prv], kv_scr.at[prv], kvp_scr, a_out,
                                       has_prev=i > 1))
    kvp_scr[...] = kv_scr[prv, tm - WINDOW:, :]


def _proj_swa(x, ln_g, w_in, qg, kg, inv, ones, sinks, w_out, w_up):
    m = x.shape[0]
    nblk = m // TM
    cl = lambda i: jnp.minimum(i, nblk - 1)
    row = lambda w: pl.BlockSpec((TM, w), lambda i: (cl(i), 0))
    wo_spec = pl.BlockSpec((w_out.shape[0] // nblk, w_out.shape[1]), lambda i: (cl(i), 0))
    wcast = w_up.shape[1] // nblk
    per_slab = TF_RET // wcast
    assert w_up.shape[1] % nblk == 0 and TF_RET % wcast == 0
    wu_spec = pl.BlockSpec((w_up.shape[0], wcast), lambda i: (0, cl(i)))
    wu_out_spec = pl.BlockSpec((1, w_up.shape[0], wcast),
                               lambda i: (cl(i) // per_slab, 0, cl(i) % per_slab))
    return pl.pallas_call(
        functools.partial(_proj_swa_body, nblk=nblk),
        grid=(nblk + 1,),
        in_specs=[pl.BlockSpec(memory_space=pltpu.SMEM),
                  row(D_MODEL), _const_spec((1, D_MODEL)), _w_in_spec(0),
                  _const_spec((1, 256)), _const_spec((1, KV_WIDTH)),
                  _const_spec((1, HALF)), _const_spec((256, 256)), wo_spec, wu_spec],
        out_specs=[pl.BlockSpec((TM, ATT_WIDTH), lambda i: (jnp.maximum(i - 1, 0), 0)),
                   row(2 * KV_WIDTH), row(RET_WIDTH), wo_spec, wu_out_spec],
        out_shape=(jax.ShapeDtypeStruct((m, ATT_WIDTH), BF16),
                   jax.ShapeDtypeStruct((m, 2 * KV_WIDTH), F32),
                   jax.ShapeDtypeStruct((m, RET_WIDTH), BF16),
                   jax.ShapeDtypeStruct(w_out.shape, BF16),
                   jax.ShapeDtypeStruct((w_up.shape[1] // TF_RET, w_up.shape[0], TF_RET), BF16)),
        scratch_shapes=[pltpu.VMEM((2, TM, ATT_WIDTH), BF16),
                        pltpu.VMEM((2, TM, 2 * KV_WIDTH), F32),
                        pltpu.VMEM((WINDOW, 2 * KV_WIDTH), F32),
                        pltpu.VMEM((TM, HALF), F32), pltpu.VMEM((TM, HALF), F32)],
        compiler_params=_params(),
        name="proj_swa",
    )(sinks, x, ln_g, w_in, qg, kg, inv, ones, w_out, w_up)


def _proj_ret_body(x_ref, g_ref, w_ref, inv_ref, rq_ref, rg_ref, wd_ref,
                   r_out, s_out, wd_bf,
                   rk_scr, rv_scr, gt_scr, s_scr, cr_scr, sr_scr, *, nblk):
    i = pl.program_id(0)
    tm = x_ref.shape[0]

    @pl.when(i == 0)
    def _():
        _rope_init(inv_ref, cr_scr, sr_scr, 1)
        rk_scr[...] = jnp.zeros_like(rk_scr)
        rv_scr[...] = jnp.zeros_like(rv_scr)
        gt_scr[...] = jnp.zeros_like(gt_scr)
        s_scr[...] = jnp.zeros_like(s_scr)

    wd_bf[...] = wd_ref[...].astype(BF16)

    cur = lax.rem(i, 2)
    prv = 1 - cur
    blk = jnp.minimum(i, nblk - 1)

    h = _norm_rows(x_ref, g_ref)
    cos, sin = _rope_tables(inv_ref, cr_scr, sr_scr, blk * tm)

    def rk_piece(n):
        y = _dot(h, w_ref[:, 512 * n:512 * (n + 1)])
        for t in range(2):
            _rope_head(y[:, RET_DK * t:RET_DK * (t + 1)], cos, sin, RET_DK ** -0.5,
                       rk_scr.at[cur], 2 * n + t)

    def rv_piece(n):
        c = slice(512 * n, 512 * (n + 1))
        rv_scr[cur, :, c] = _dot(h, w_ref[:, RET_WIDTH + 512 * n:RET_WIDTH + 512 * (n + 1)]
                                 ).astype(BF16)

    def gate_piece(n):
        c0 = 2 * RET_WIDTH + 512 * n
        rg = _dot(h, w_ref[:, c0:c0 + 512])
        gt_scr[cur, :, 512 * n:512 * (n + 1)] = (rg / (1.0 + jnp.exp(-rg))).astype(BF16)

    main = [lambda f=f: (f(0), f(1)) for f in (rk_piece, rv_piece, gate_piece)]
    _emit_pipelined(main, *_ret_phases(rq_ref, rk_scr.at[prv], rv_scr.at[prv], gt_scr.at[prv],
                                       rg_ref, r_out, s_scr))

    @pl.when(i == nblk)
    def _():
        s_out[...] = s_scr[...]


def _proj_ret(x, ln_g, w_in, inv, rq, rg_g, w_down):
    m = x.shape[0]
    nblk = m // TM
    cl = lambda i: jnp.minimum(i, nblk - 1)
    prev = lambda w: pl.BlockSpec((TM, w), lambda i: (jnp.maximum(i - 1, 0), 0))
    wd_spec = pl.BlockSpec((w_down.shape[0] // nblk, w_down.shape[1]), lambda i: (cl(i), 0))
    state = (RET_HEADS, RET_DK, RET_DV)
    slot = pltpu.VMEM((2, TM, RET_WIDTH), BF16)
    return pl.pallas_call(
        functools.partial(_proj_ret_body, nblk=nblk),
        grid=(nblk + 1,),
        in_specs=[pl.BlockSpec((TM, D_MODEL), lambda i: (cl(i), 0)),
                  _const_spec((1, D_MODEL)), _w_in_spec(1), _const_spec((1, HALF)),
                  prev(RET_WIDTH), _const_spec((1, RET_WIDTH)), wd_spec],
        out_specs=[prev(RET_WIDTH), pl.BlockSpec(state, lambda i: (0, 0, 0)), wd_spec],
        out_shape=(jax.ShapeDtypeStruct((m, RET_WIDTH), BF16),
                   jax.ShapeDtypeStruct(state, F32),
                   jax.ShapeDtypeStruct(w_down.shape, BF16)),
        scratch_shapes=[slot, slot, slot, pltpu.VMEM(state, F32),
                        pltpu.VMEM((TM, HALF), F32), pltpu.VMEM((TM, HALF), F32)],
        compiler_params=_params(),
        name="proj_ret",
    )(x, ln_g, w_in, inv, rq, rg_g, w_down)


TN_IN = 1792
Z_TILE = 256


def _in_proj_body(x_ref, g_ref, w_ref, qg_ref, kg_ref, inv_ref, ones_ref,
                  w_bf, q_out, kv_out, rq_out, rk_out, rv_out, gt_out, kvt_out, rqt_out, rkt_out,
                  h_scr, z_scr, xd_scr, *, pos):
    c = pl.program_id(0)

    @pl.when(c == 0)
    def _():
        xd_scr[...] = x_ref[:, 0, :]
        h_scr[...] = _norm_rows(xd_scr, g_ref)

    wb = w_ref[...].astype(BF16)
    w_bf[...] = wb
    z = _dot(h_scr[...], wb)
    per_step = TN_IN // Z_TILE
    for t in range(per_step):
        z_scr[c * per_step + t] = z[:, Z_TILE * t:Z_TILE * (t + 1)]

    @pl.when(c == pl.num_programs(0) - 1)
    def _():
        tiles = lambda col0, n: [z_scr[col0 // Z_TILE + t] for t in range(n)]
        ones = ones_ref[...]
        inv_hd = 1.0 / ATT_HEAD_DIM
        for t, y in enumerate(tiles(0, ATT_WIDTH // Z_TILE)):
            ssq = _dot((y * y).astype(BF16), ones)
            qn = _bf16_into(q_out, y * lax.rsqrt(ssq * inv_hd + EPS) * qg_ref[...])
            for half in range(2):
                q_out[:, 2 * t + half, :] = qn[:, LANES * half:LANES * (half + 1)]
        (kvr,) = tiles(ATT_WIDTH, 1)
        k = kvr[:, :KV_WIDTH]
        ssq = _dot((k * k).astype(BF16), ones[:KV_WIDTH, :KV_WIDTH])
        kv_out[:, :KV_WIDTH] = k * lax.rsqrt(ssq * inv_hd + EPS) * kg_ref[...]
        kv_out[:, KV_WIDTH:] = kvr[:, KV_WIDTH:]
        ang = jnp.full((8, 1), pos, jnp.int32).astype(F32) * inv_ref[...]
        cos = jnp.cos(ang)[:1]
        sin = jnp.sin(ang)[:1]
        rq0 = ATT_WIDTH + 2 * KV_WIDTH
        for hh, y in enumerate(tiles(rq0, RET_HEADS)):
            _rope_head(y, cos, sin, 1.0, rq_out, hh)
        for hh, y in enumerate(tiles(rq0 + RET_WIDTH, RET_HEADS)):
            _rope_head(y, cos, sin, RET_DK ** -0.5, rk_out, hh)
        for hh, y in enumerate(tiles(rq0 + 2 * RET_WIDTH, RET_HEADS)):
            rv_out[:, RET_DV * hh:RET_DV * (hh + 1)] = _bf16_into(rv_out, y)
        for hh, y in enumerate(tiles(rq0 + 3 * RET_WIDTH, RET_HEADS)):
            gt_out[:, RET_DV * hh:RET_DV * (hh + 1)] = _bf16_into(gt_out, y / (1.0 + jnp.exp(-y)))
        kvt_out[...] = kv_out[...].T
        rqt_out[...] = rq_out[...].T
        rkt_out[...] = rk_out[...].T


def _in_proj(x, ln_g, w_in, qg, kg, inv, ones, pos):
    m = x.shape[0]
    assert m == LANES
    assert RET_DK == Z_TILE and TN_IN % Z_TILE == 0 and w_in.shape[1] % TN_IN == 0
    nstep = w_in.shape[1] // TN_IN
    full = lambda w: pl.BlockSpec((m, w), lambda c: (0, 0))
    colm = lambda w: pl.BlockSpec((w, m), lambda c: (0, 0))
    wcol = pl.BlockSpec((D_MODEL, TN_IN), lambda c: (0, c))
    ret = jax.ShapeDtypeStruct((m, RET_WIDTH), F32)
    ret_t = jax.ShapeDtypeStruct((RET_WIDTH, m), F32)
    return pl.pallas_call(
        functools.partial(_in_proj_body, pos=pos),
        grid=(nstep,),
        in_specs=[_const_spec((m, 1, D_MODEL)), _const_spec((1, D_MODEL)), wcol,
                  _const_spec((1, 256)), _const_spec((1, KV_WIDTH)),
                  _const_spec((1, HALF)), _const_spec((256, 256))],
        out_specs=[wcol, pl.BlockSpec((m, ATT_HEADS // 2, LANES), lambda c: (0, 0, 0)),
                   full(2 * KV_WIDTH)] + [full(RET_WIDTH)] * 4
        + [colm(2 * KV_WIDTH), colm(RET_WIDTH), colm(RET_WIDTH)],
        out_shape=(jax.ShapeDtypeStruct(w_in.shape, BF16),
                   jax.ShapeDtypeStruct((m, ATT_HEADS // 2, LANES), F32),
                   jax.ShapeDtypeStruct((m, 2 * KV_WIDTH), F32), ret, ret, ret, ret,
                   jax.ShapeDtypeStruct((2 * KV_WIDTH, m), F32), ret_t, ret_t),
        scratch_shapes=[pltpu.VMEM((m, D_MODEL), BF16),
                        pltpu.VMEM((w_in.shape[1] // Z_TILE, m, Z_TILE), F32),
                        pltpu.VMEM((m, D_MODEL), F32)],
        compiler_params=_params(),
        name="in_proj",
    )(x, ln_g, w_in, qg, kg, inv, ones)


def _this_steps_columns(t_ref, rows, bb):
    shift = lax.rem(LANES - bb * pl.program_id(0), LANES)
    return pltpu.roll(t_ref[rows, :], shift, 1)


def _swa_dec_body(q_ref, kvn_ref, kvt_ref, ck_ref, cv_ref, sink_ref, o_ref, nk_ref, nv_ref):
    bb = q_ref.shape[0]
    npair = ATT_HEADS // 2
    q8 = q_ref[...]
    q8r = pltpu.roll(q8, ATT_HEAD_DIM, 2)
    lane = lax.broadcasted_iota(jnp.int32, q8.shape, 2)
    pair = lax.broadcasted_iota(jnp.int32, q8.shape, 1)
    lo = lane < ATT_HEAD_DIM
    kv0 = pair < npair // ATT_KV_HEADS
    own = lo == kv0
    zero = jnp.zeros_like(q8)
    qe = jnp.where(own, jnp.where(kv0, q8, q8r), zero)
    qo = jnp.where(own, jnp.where(kv0, q8r, q8), zero)
    qb = jnp.concatenate([qe, qo], axis=1)

    ck = ck_ref[...]
    cv = cv_ref[...]
    kn = kvn_ref[:, :KV_WIDTH]
    vn = kvn_ref[:, KV_WIDTH:]
    s = lax.dot_general(qb.astype(BF16), ck.astype(BF16), (((2,), (1,)), ((0,), (0,))),
                        preferred_element_type=F32)
    s_new = jnp.sum(qb * kn[:, None, :], axis=-1, keepdims=True)
    sink = sink_ref[...][None, :, :]
    mx = jnp.maximum(jnp.maximum(jnp.max(s, axis=-1, keepdims=True), s_new), sink)
    p = jnp.exp(s - mx)
    p_new = jnp.exp(s_new - mx)
    den = jnp.sum(p, axis=-1, keepdims=True) + p_new + jnp.exp(sink - mx)
    o = lax.dot_general(p.astype(BF16), cv.astype(BF16), (((2,), (2,)), ((0,), (0,))),
                        preferred_element_type=F32)
    o = (o + p_new * vn[:, None, :]) / den
    oe = o[:, :npair, :]
    oo = o[:, npair:, :]
    oer = pltpu.roll(oe, ATT_HEAD_DIM, 2)
    oor = pltpu.roll(oo, ATT_HEAD_DIM, 2)
    o_ref[...] = jnp.where(lo, jnp.where(kv0, oe, oer), jnp.where(kv0, oor, oo)).astype(BF16)

    newcol = _this_steps_columns(kvt_ref, slice(None), bb)
    last = lax.broadcasted_iota(jnp.int32, (KV_WIDTH, WINDOW), 1) == WINDOW - 1
    for jb in range(bb):
        nk_ref[jb] = jnp.where(last, newcol[:KV_WIDTH, jb:jb + 1],
                               pltpu.roll(ck[jb], WINDOW - 1, 1))
        nv_ref[jb] = jnp.where(last, newcol[KV_WIDTH:, jb:jb + 1],
                               pltpu.roll(cv[jb], WINDOW - 1, 1))


def _swa_dec(q8, kvn, kvt, ck, cv, sink_col):
    b = q8.shape[0]
    bb = BB_ATT
    cache = pl.BlockSpec((bb, KV_WIDTH, WINDOW), lambda i: (i, 0, 0))
    return pl.pallas_call(
        _swa_dec_body,
        grid=(b // bb,),
        in_specs=[pl.BlockSpec((bb, ATT_HEADS // 2, LANES), lambda i: (i, 0, 0)),
                  pl.BlockSpec((bb, 2 * KV_WIDTH), lambda i: (i, 0)),
                  _const_spec(kvt.shape), cache, cache, _const_spec((ATT_HEADS, 1))],
        out_specs=[pl.BlockSpec((bb, ATT_HEADS // 2, LANES), lambda i: (i, 0, 0)), cache, cache],
        out_shape=(jax.ShapeDtypeStruct((b, ATT_HEADS // 2, LANES), BF16),
                   jax.ShapeDtypeStruct(ck.shape, F32),
                   jax.ShapeDtypeStruct(cv.shape, F32)),
        compiler_params=_params(),
        name="swa_dec",
    )(q8, kvn, kvt, ck, cv, sink_col)


def _out_proj_body(x_ref, a_ref, r_ref, w_ref, g_ref, x1_ref, h2_ref):
    if len(x_ref.shape) == 3:
        x1_ref[...] = x_ref[:, 0, :]
        x_ref = x1_ref
    x1 = (x_ref[...] + _dot(a_ref[...].astype(BF16), w_ref[:ATT_WIDTH, :])
          + _dot(r_ref[...].astype(BF16), w_ref[ATT_WIDTH:, :]))
    x1_ref[...] = x1
    var = jnp.mean(x1 * x1, axis=-1, keepdims=True)
    h2_ref[...] = (x1 * lax.rsqrt(var + EPS) * g_ref[...]).astype(BF16)


def _out_proj(x, a, r, w, g, tm):
    m = x.shape[0]
    row = lambda w: pl.BlockSpec((tm, w), lambda i: (i, 0))
    return pl.pallas_call(
        _out_proj_body,
        grid=(m // tm,),
        in_specs=[row(D_MODEL), row(ATT_WIDTH), row(RET_WIDTH),
                  _const_spec(w.shape), _const_spec((1, D_MODEL))],
        out_specs=[row(D_MODEL), row(D_MODEL)],
        out_shape=(jax.ShapeDtypeStruct((m, D_MODEL), F32),
                   jax.ShapeDtypeStruct((m, D_MODEL), BF16)),
        compiler_params=_params(),
        name="out_proj",
    )(x, a, r, w, g)


def _mlp_acc(h2_ref, wu_ref, wd_ref, o_ref):
    u = jnp.maximum(_dot(h2_ref[...], wu_ref[0]), 0.0)
    o_ref[...] += _dot((u * u).astype(BF16), wd_ref[...])


def _dec_tail_body(x_ref, a_ref, r_ref, wo_ref, g2_ref, wu_ref, wd_ref, o_ref, acc, h2_scr):
    @pl.when(pl.program_id(0) == 0)
    def _():
        _out_proj_body(x_ref, a_ref, r_ref, wo_ref, g2_ref, acc, h2_scr)

    _mlp_acc(h2_scr, wu_ref, wd_ref, acc)

    @pl.when(pl.program_id(0) == pl.num_programs(0) - 1)
    def _():
        o_ref[:, 0, :] = acc[...]


def _dec_tail(x, a, r, wo, g2, wu, wd):
    m = x.shape[0]
    full = lambda s: pl.BlockSpec(s, lambda f: (0,) * len(s), pipeline_mode=pl.Buffered(1))
    return pl.pallas_call(
        _dec_tail_body,
        grid=(D_FF // TF_RET,),
        in_specs=[full((m, 1, D_MODEL)), full((m, ATT_WIDTH)), full((m, RET_WIDTH)),
                  full(wo.shape), full((1, D_MODEL)),
                  pl.BlockSpec((1, D_MODEL, TF_RET), lambda f: (f, 0, 0)),
                  pl.BlockSpec((TF_RET, D_MODEL), lambda f: (f, 0))],
        out_specs=pl.BlockSpec((m, 1, D_MODEL), lambda f: (0, 0, 0)),
        out_shape=jax.ShapeDtypeStruct((m, 1, D_MODEL), F32),
        scratch_shapes=[pltpu.VMEM((m, D_MODEL), F32), pltpu.VMEM((m, D_MODEL), BF16)],
        compiler_params=_params(),
        name="dec_tail",
    )(x, a, r, wo, g2, wu, wd)


RING = 3


def _mlp_ret_body(x1_ref, h2_ref, wu_hbm, wd_hbm,
                  qt_ref, kt_ref, rq_ref, rk_ref, rv_ref, gt_ref, g_ref, s_ref,
                  o_ref, r_ref, ns_ref, wu_buf, wd_buf, sem):
    nf = pl.num_programs(1)
    b = pl.program_id(0) * nf + pl.program_id(1)
    nstep = pl.num_programs(0) * nf

    def slab_copies(step):
        slab = lax.rem(step, nf)
        slot = lax.rem(step, RING)
        rows = pl.ds(pl.multiple_of(slab * TF_RET, TF_RET), TF_RET)
        return (pltpu.make_async_copy(wu_hbm.at[slab], wu_buf.at[slot], sem.at[0, slot]),
                pltpu.make_async_copy(wd_hbm.at[rows, :], wd_buf.at[slot], sem.at[1, slot]))

    @pl.when(b == 0)
    def _():
        for step in range(RING - 1):
            for copy in slab_copies(step):
                copy.start()

    @pl.when(b + RING - 1 < nstep)
    def _():
        for copy in slab_copies(b + RING - 1):
            copy.start(priority=1)

    for copy in slab_copies(b):
        copy.wait()
    slot = lax.rem(b, RING)

    def step(first):
        u = jnp.maximum(_dot(h2_ref[...], wu_buf[slot]), 0.0)
        base = x1_ref[...] if first else o_ref[...]
        o_ref[...] = base + _dot((u * u).astype(BF16), wd_buf[slot])

        row = pl.ds(b, 1)
        shift = lax.rem(LANES - b, LANES)
        for hh in range(RET_HEADS):
            g1 = math.exp(LOG_G[hh])
            cols = slice(RET_DK * hh, RET_DK * (hh + 1))
            qc = pltpu.roll(qt_ref[cols, :], shift, 1)[:, :1] * g1
            kc = pltpu.roll(kt_ref[cols, :], shift, 1)[:, :1]
            v = rv_ref[row, cols]
            s0 = s_ref[0, hh]
            ns_ref[0, hh] = g1 * s0 + kc * v
            qk = jnp.sum(rq_ref[row, cols] * rk_ref[row, cols], axis=-1, keepdims=True)
            o = jnp.sum(qc * s0, axis=0, keepdims=True) + qk * v
            var = jnp.mean(o * o, axis=-1, keepdims=True)
            r_ref[row, cols] = o * lax.rsqrt(var + EPS) * g_ref[:, cols] * gt_ref[row, cols]

    pl.when(pl.program_id(1) == 0)(functools.partial(step, True))
    pl.when(pl.program_id(1) != 0)(functools.partial(step, False))


def _mlp_ret(x1, h2, wu, wd, qt, kt, rq, rk, rv, gt, g, state):
    m = x1.shape[0]
    nb = state.shape[0]
    nf = D_FF // TF_RET
    assert m // TM * nf == nb and nb == LANES
    assert nb >= RING
    row = pl.BlockSpec((TM, D_MODEL), lambda i, f: (i, 0))
    hbm = pl.BlockSpec(memory_space=pl.ANY)
    st = pl.BlockSpec((1, RET_HEADS, RET_DK, RET_DV), lambda i, f: (i * nf + f, 0, 0, 0))
    rows = _const_spec((nb, RET_WIDTH))
    return pl.pallas_call(
        _mlp_ret_body,
        grid=(m // TM, nf),
        in_specs=[row, row, hbm, hbm,
                  _const_spec(qt.shape), _const_spec(kt.shape), rows, rows, rows, rows,
                  _const_spec((1, RET_WIDTH)), st],
        out_specs=[row, pl.BlockSpec((nb, RET_WIDTH), lambda i, f: (0, 0)), st],
        out_shape=(jax.ShapeDtypeStruct((m, D_MODEL), F32),
                   jax.ShapeDtypeStruct((nb, RET_WIDTH), F32),
                   jax.ShapeDtypeStruct(state.shape, F32)),
        scratch_shapes=[pltpu.VMEM((RING, D_MODEL, TF_RET), BF16),
                        pltpu.VMEM((RING, TF_RET, D_MODEL), BF16),
                        pltpu.SemaphoreType.DMA((2, RING))],
        compiler_params=_params(2),
        name="mlp_ret",
    )(x1, h2, wu, wd, qt, kt, rq, rk, rv, gt, g, state)


def kernel(x_prompt, x_sample, cache_k_win, cache_v_win, state_ret, ln1_g, w_in, q_norm_g,
           k_norm_g, attn_sinks, ret_norm_g, w_out, ln2_g, w_up, w_down):
    seq = x_prompt.shape[1]
    nb = x_sample.shape[0]
    assert x_prompt.shape[0] == 1 and x_sample.shape[1] == 1 and w_in.shape[0] == 1
    assert seq % TM == 0 and w_in.shape[2] == SPLIT_A + SPLIT_B

    inv = (ROPE_BASE ** (-jnp.arange(HALF, dtype=F32) / HALF)).reshape(1, HALF)

    ln1 = ln1_g.reshape(1, D_MODEL)
    ln2 = ln2_g.reshape(1, D_MODEL)
    qg = jnp.tile(q_norm_g.reshape(1, ATT_HEAD_DIM) * (ATT_HEAD_DIM ** -0.5), (1, 256 // ATT_HEAD_DIM))
    kg = jnp.tile(k_norm_g.reshape(1, ATT_HEAD_DIM), (1, KV_WIDTH // ATT_HEAD_DIM))
    rg_g = ret_norm_g.reshape(1, RET_WIDTH)
    sinks = attn_sinks.reshape(ATT_HEADS)
    blk = jnp.arange(256) // ATT_HEAD_DIM
    ones = (blk[:, None] == blk[None, :]).astype(BF16)

    wi, qs, kvs, rqs, rks, rvs, rgs, kvt, rqt, rkt = _in_proj(x_sample, ln1, w_in[0], qg, kg, inv,
                                                              ones, pos=PAST_LEN)
    sink_col = jnp.concatenate([sinks[0::2], sinks[1::2]]).reshape(ATT_HEADS, 1)
    to_fm = lambda c: c[0].transpose(0, 2, 3, 1).reshape(nb, KV_WIDTH, WINDOW)
    from_fm = lambda c: c.reshape(nb, ATT_KV_HEADS, ATT_HEAD_DIM, WINDOW).transpose(0, 3, 1, 2)[None]
    a8, nk, nv = _swa_dec(qs, kvs, kvt,
                          to_fm(cache_k_win), to_fm(cache_v_win), sink_col)
    a_s = a8.reshape(nb, ATT_WIDTH)

    xp = x_prompt[0]
    a_out, kv, rq, wo, wu = _proj_swa(xp, ln1, wi, qg, kg, inv, ones, sinks, w_out[0], w_up[0])
    r_out, s_fin, wd = _proj_ret(xp, ln1, wi, inv, rq, rg_g, w_down[0])
    x1, h2 = _out_proj(xp, a_out, r_out, wo, ln2, tm=TM)
    yp, r_s, ns = _mlp_ret(x1, h2, wu, wd, rqt, rkt, rqs, rks, rvs, rgs, rg_g, state_ret[0])

    wb = min(WINDOW, seq)
    kp = kv[seq - wb:, :KV_WIDTH].reshape(1, 1, wb, ATT_KV_HEADS, ATT_HEAD_DIM)
    vp = kv[seq - wb:, KV_WIDTH:].reshape(1, 1, wb, ATT_KV_HEADS, ATT_HEAD_DIM)
    sp = s_fin.reshape(1, 1, RET_HEADS, RET_DK, RET_DV)

    ys = _dec_tail(x_sample, a_s, r_s, wo, ln2, wu, wd)

    return (yp[None], ys, kp, vp, sp, from_fm(nk), from_fm(nv), ns[None])
```
